```python
import math
import jax
import jax.numpy as jnp
from jax import lax
import numpy as np

D_MODEL = 1024
BATCH = 8
SEQ = 4096
DEPTH = 2

GRID_W = 64
CTX_LEN = 256
N_MOD = 6
HEAD_DIM = 64
N_HEADS_A = 8
KV_HEADS_A = 2
WINDOW = 128
ATTN_BLOCK = 128
N_HEADS_B = 4
N_HEADS_C = 4
HEAD_DIM_C = 128
MLSTM_CHUNK = 128
N_BRANCHES = 3
A_Q = N_HEADS_A * HEAD_DIM
A_KV = KV_HEADS_A * HEAD_DIM
B_QK = N_HEADS_B * 2 * HEAD_DIM
B_V = N_HEADS_B * 2 * HEAD_DIM
C_WIDTH = N_HEADS_C * HEAD_DIM_C
C_GATES = 4 * N_HEADS_C
SPLIT_SIZES = (A_Q, A_KV, A_KV, B_QK, B_QK, B_V, C_WIDTH, C_WIDTH, C_WIDTH, C_WIDTH, C_GATES, N_BRANCHES * D_MODEL)
N_IN = sum(SPLIT_SIZES)
N_EXPERTS = 64
N_GROUPS = 8
TOPK_GROUPS = 4
TOP_K = 8
D_EXPERT = 256
D_SHARED = 256
ROUTED_SCALE = 2.5
EXPERT_BLOCK = 128
ROPE_BASE = 10000.0
EPS = 1e-6

kernel_name = "hybrid_diffusion_mixer_moe_block"


def rms_norm(x, g):
    xf = x.astype(jnp.float32)
    y = xf * lax.rsqrt(jnp.mean(xf * xf, axis=-1, keepdims=True) + EPS)
    return (y * g.astype(jnp.float32)).astype(x.dtype)


def modulate(h, shift, scale):
    return h * (1 + scale) + shift


def to_heads(a, n_heads):
    return a.reshape(*a.shape[:-1], n_heads, a.shape[-1] // n_heads)


def from_heads(a):
    return a.reshape(*a.shape[:-2], a.shape[-2] * a.shape[-1])


def split_proj(p):
    parts, start = [], 0
    for size in SPLIT_SIZES:
        parts.append(p[..., start:start + size])
        start += size
    return parts


def diff_pair(a):
    a = a.reshape(*a.shape[:-2], a.shape[-2] // 2, 2, a.shape[-1])
    return a[..., 0, :], a[..., 1, :]


def axial_rope_tables(row, col):
    quarter = HEAD_DIM // 4
    inv = 1.0 / (ROPE_BASE ** (jnp.arange(quarter, dtype=jnp.float32) / quarter))
    ang = jnp.stack([row.astype(jnp.float32)[:, None] * inv, col.astype(jnp.float32)[:, None] * inv], axis=1)
    return jnp.cos(ang), jnp.sin(ang)


def apply_rope(x, cos, sin):
    b, n, h, dh = x.shape
    q = dh // 4
    xr = x.astype(jnp.float32).reshape(b, n, h, 2, 2, q)
    a, bb = xr[..., 0, :], xr[..., 1, :]
    cs, sn = cos[None, :, None], sin[None, :, None]
    out = jnp.stack([a * cs - bb * sn, bb * cs + a * sn], axis=-2)
    return out.reshape(b, n, h, dh).astype(x.dtype)


def softmax_with_sink(s, sink):
    m = jnp.maximum(jnp.max(s, axis=-1, keepdims=True), sink)
    p = jnp.exp(s - m)
    return p / (jnp.sum(p, axis=-1, keepdims=True) + jnp.exp(sink - m))


def window_sink_attention(q, k, v, kc, vc, sink):
    b, s_len, h, dh = q.shape
    g = h // KV_HEADS_A
    nb = s_len // ATTN_BLOCK
    scale = dh ** -0.5
    pad = ((0, 0), (ATTN_BLOCK, ATTN_BLOCK), (0, 0), (0, 0))
    kp, vp = jnp.pad(k, pad), jnp.pad(v, pad)
    qb = q.reshape(b, nb, ATTN_BLOCK, KV_HEADS_A, g, dh).swapaxes(0, 1)
    sk = sink.astype(jnp.float32).reshape(KV_HEADS_A, g)[:, :, None, None]
    offs_q = jnp.arange(ATTN_BLOCK)
    offs_k = jnp.arange(3 * ATTN_BLOCK) - ATTN_BLOCK
    band = jnp.abs(offs_q[:, None] - offs_k[None, :]) <= WINDOW

    def one_block(args):
        qi, bi = args
        start = bi * ATTN_BLOCK
        kb = lax.dynamic_slice_in_dim(kp, start, 3 * ATTN_BLOCK, axis=1)
        vb = lax.dynamic_slice_in_dim(vp, start, 3 * ATTN_BLOCK, axis=1)
        kpos = start + offs_k
        valid = band & ((kpos >= 0) & (kpos < s_len))[None, :]
        s_lat = jnp.einsum('bqkgd,bskd->bkgqs', qi, kb).astype(jnp.float32) * scale
        s_lat = jnp.where(valid, s_lat, -jnp.inf)
        s_ctx = jnp.einsum('bqkgd,bckd->bkgqc', qi, kc).astype(jnp.float32) * scale
        p = softmax_with_sink(jnp.concatenate([s_lat, s_ctx], axis=-1), sk).astype(v.dtype)
        return (jnp.einsum('bkgqs,bskd->bqkgd', p[..., :3 * ATTN_BLOCK], vb)
                + jnp.einsum('bkgqc,bckd->bqkgd', p[..., 3 * ATTN_BLOCK:], vc))

    out = lax.map(one_block, (qb, jnp.arange(nb)))
    return out.swapaxes(0, 1).reshape(b, s_len, h, dh)


def ctx_sink_attention(q, k, v, sink):
    b, n, h, dh = q.shape
    g = h // KV_HEADS_A
    s = jnp.einsum('bqkgd,bckd->bkgqc', q.reshape(b, n, KV_HEADS_A, g, dh), k).astype(jnp.float32) * dh ** -0.5
    p = softmax_with_sink(s, sink.astype(jnp.float32).reshape(KV_HEADS_A, g)[:, :, None, None]).astype(v.dtype)
    return jnp.einsum('bkgqc,bckd->bqkgd', p, v).reshape(b, n, h, dh)


def diff_attend(q1, q2, k1, k2, v, lam):
    b, sq, h, dh = q1.shape
    nb = sq // ATTN_BLOCK
    scale = dh ** -0.5

    def blocks(a):
        return a.reshape(b, nb, ATTN_BLOCK, h, dh).swapaxes(0, 1)

    def one_block(args):
        a1, a2 = args
        p1 = jax.nn.softmax(jnp.einsum('bqhd,bkhd->bhqk', a1, k1).astype(jnp.float32) * scale, axis=-1)
        p2 = jax.nn.softmax(jnp.einsum('bqhd,bkhd->bhqk', a2, k2).astype(jnp.float32) * scale, axis=-1)
        att = (p1 - lam * p2).astype(v.dtype)
        return jnp.einsum('bhqk,bkhe->bqhe', att, v)

    out = lax.map(one_block, (blocks(q1), blocks(q2)))
    return out.swapaxes(0, 1).reshape(b, sq, h, v.shape[-1])


def mlstm_scan(q, k, v, log_i, log_f, state):
    b, n_tok, h, _ = q.shape
    dv = v.shape[-1]
    t = MLSTM_CHUNK
    nc = n_tok // t

    def chunks(a):
        a = jnp.moveaxis(a, 2, 1)
        a = a.reshape(b, h, nc, t, *a.shape[3:])
        return jnp.moveaxis(a, 2, 0)

    tril = jnp.tril(jnp.ones((t, t), dtype=bool))

    def step(carry, inp):
        c_st, n_st, m_st = carry
        qc, kc, vc, ic, fc = inp
        bcum = jnp.cumsum(fc, axis=-1)
        dmat = jnp.where(tril, bcum[..., :, None] - bcum[..., None, :] + ic[..., None, :], -jnp.inf)
        inter = bcum + m_st[..., None]
        m_t = jnp.maximum(inter, jnp.max(dmat, axis=-1))
        s = jnp.einsum('bhtd,bhsd->bhts', qc, kc) * jnp.exp(dmat - m_t[..., None])
        a_inter = jnp.exp(inter - m_t)
        num = jnp.einsum('bhts,bhse->bhte', s, vc) + a_inter[..., None] * jnp.einsum('bhtd,bhde->bhte', qc, c_st)
        den = jnp.sum(s, axis=-1) + a_inter * jnp.einsum('bhtd,bhd->bht', qc, n_st)
        h_out = num / jnp.maximum(jnp.abs(den), jnp.exp(-m_t))[..., None]
        b_last = bcum[..., -1]
        gsum = b_last[..., None] - bcum + ic
        m_new = jnp.maximum(b_last + m_st, jnp.max(gsum, axis=-1))
        decay = jnp.exp(b_last + m_st - m_new)
        wk = jnp.exp(gsum - m_new[..., None])[..., None] * kc
        c_new = decay[..., None, None] * c_st + jnp.einsum('bhtd,bhte->bhde', wk, vc)
        n_new = decay[..., None] * n_st + jnp.sum(wk, axis=-2)
        return (c_new, n_new, m_new), h_out

    state, hs = lax.scan(step, state, (chunks(q), chunks(k), chunks(v), chunks(log_i), chunks(log_f)))
    hs = jnp.moveaxis(hs, 0, 2).reshape(b, h, n_tok, dv)
    return jnp.moveaxis(hs, 1, 2), state


def mlstm_inputs(q, k, v, g, b_gate):
    gates = g.reshape(*g.shape[:-1], 4, N_HEADS_C).astype(jnp.float32) + b_gate.astype(jnp.float32)
    return (to_heads(q, N_HEADS_C).astype(jnp.float32),
            to_heads(k, N_HEADS_C).astype(jnp.float32) * HEAD_DIM_C ** -0.5,
            to_heads(v, N_HEADS_C).astype(jnp.float32), gates)


def bidir_mlstm(q, k, v, gates, qc, kc, vc, gates_c):
    b, _, h, dk = q.shape
    dv = v.shape[-1]
    zero = (jnp.zeros((b, h, dk, dv), jnp.float32), jnp.zeros((b, h, dk), jnp.float32), jnp.zeros((b, h), jnp.float32))
    flip = lambda a: jnp.flip(a, axis=1)
    logsig = jax.nn.log_sigmoid
    hc_f, st_f = mlstm_scan(qc, kc, vc, gates_c[..., 0, :], logsig(gates_c[..., 1, :]), zero)
    hl_f, _ = mlstm_scan(q, k, v, gates[..., 0, :], logsig(gates[..., 1, :]), st_f)
    hc_b, st_b = mlstm_scan(flip(qc), flip(kc), flip(vc), flip(gates_c[..., 2, :]), flip(logsig(gates_c[..., 3, :])), zero)
    hl_b, _ = mlstm_scan(flip(q), flip(k), flip(v), flip(gates[..., 2, :]), flip(logsig(gates[..., 3, :])), st_b)
    return hl_f + flip(hl_b), hc_f + flip(hc_b)


def token_mixers(hl, hc, cos, sin, lam_init, need_ctx, w_in, b_gate, sink, lam_q1, lam_k1,
                 lam_q2, lam_k2, g_diff, g_mlstm, w_a, w_b, w_c, w_out):
    (aq, ak, av, bq, bk, bv, cq, ck, cv, co, cg, gt) = split_proj(hl @ w_in)
    (aq_c, ak_c, av_c, bq_c, bk_c, bv_c, cq_c, ck_c, cv_c, co_c, cg_c, gt_c) = split_proj(hc @ w_in)
    f32 = jnp.float32

    ka_c, va_c = to_heads(ak_c, KV_HEADS_A), to_heads(av_c, KV_HEADS_A)
    oa = window_sink_attention(apply_rope(to_heads(aq, N_HEADS_A), cos, sin),
                               apply_rope(to_heads(ak, KV_HEADS_A), cos, sin),
                               to_heads(av, KV_HEADS_A), ka_c, va_c, sink)

    lam = (jnp.exp(jnp.sum(lam_q1.astype(f32) * lam_k1.astype(f32)))
           - jnp.exp(jnp.sum(lam_q2.astype(f32) * lam_k2.astype(f32))) + lam_init)
    q1, q2 = diff_pair(apply_rope(to_heads(bq, 2 * N_HEADS_B), cos, sin))
    k1, k2 = diff_pair(apply_rope(to_heads(bk, 2 * N_HEADS_B), cos, sin))
    k1_c, k2_c = diff_pair(to_heads(bk_c, 2 * N_HEADS_B))
    vb_c = to_heads(bv_c, N_HEADS_B)
    ob = diff_attend(q1, q2, jnp.concatenate([k1, k1_c], axis=1), jnp.concatenate([k2, k2_c], axis=1),
                     jnp.concatenate([to_heads(bv, N_HEADS_B), vb_c], axis=1), lam)

    h_lat, h_ctx = bidir_mlstm(*mlstm_inputs(cq, ck, cv, cg, b_gate), *mlstm_inputs(cq_c, ck_c, cv_c, cg_c, b_gate))

    def diff_out(o):
        return from_heads(rms_norm(o, g_diff) * (1.0 - lam_init))

    def mlstm_out(hh, o):
        normed = rms_norm(hh, g_mlstm.reshape(N_HEADS_C, HEAD_DIM_C))
        return from_heads(normed * jax.nn.sigmoid(to_heads(o, N_HEADS_C).astype(f32))).astype(o.dtype)

    def merge(oa_, ob_, oc_, gt_):
        g = jax.nn.sigmoid(gt_.reshape(*gt_.shape[:-1], N_BRANCHES, D_MODEL))
        y = (g[..., 0, :] * (from_heads(oa_) @ w_a) + g[..., 1, :] * (diff_out(ob_) @ w_b)
             + g[..., 2, :] * (oc_ @ w_c))
        return y @ w_out

    out_l = merge(oa, ob, mlstm_out(h_lat, co), gt)
    if not need_ctx:
        return out_l, None
    oa_c = ctx_sink_attention(to_heads(aq_c, N_HEADS_A), ka_c, va_c, sink)
    q1_c, q2_c = diff_pair(to_heads(bq_c, 2 * N_HEADS_B))
    ob_c = diff_attend(q1_c, q2_c, k1_c, k2_c, vb_c, lam)
    out_c = merge(oa_c, ob_c, mlstm_out(h_ctx, co_c), gt_c)
    return out_l, out_c


def swiglu(h, wg, wu, wd):
    return (jax.nn.silu(h @ wg) * (h @ wu)) @ wd


def route(h, w_router, b_router):
    s = jax.nn.sigmoid((h @ w_router).astype(jnp.float32))
    sel = s + b_router.astype(jnp.float32)
    per = N_EXPERTS // N_GROUPS
    grp = sel.reshape(*sel.shape[:-1], N_GROUPS, per)
    grp_score = jnp.sum(lax.top_k(grp, 2)[0], axis=-1)
    top_g = lax.top_k(grp_score, TOPK_GROUPS)[1]
    g_mask = jnp.sum(jax.nn.one_hot(top_g, N_GROUPS, dtype=jnp.float32), axis=-2)
    e_mask = jnp.repeat(g_mask, per, axis=-1) > 0
    idx = lax.top_k(jnp.where(e_mask, sel, -jnp.inf), TOP_K)[1]
    w = jnp.take_along_axis(s, idx, axis=-1)
    w = w / jnp.sum(w, axis=-1, keepdims=True) * ROUTED_SCALE
    return idx, w.astype(h.dtype)


def grouped_experts(hs, idx, wts, w_eg, w_eu, w_ed):
    n_tok, d = hs.shape
    n_assign = n_tok * TOP_K
    flat_e = idx.reshape(n_assign)
    order = jnp.argsort(flat_e)
    sorted_e = flat_e[order]
    tok = order // TOP_K
    counts = jnp.bincount(flat_e, length=N_EXPERTS)
    padded = (counts + EXPERT_BLOCK - 1) // EXPERT_BLOCK * EXPERT_BLOCK
    pad_end = jnp.cumsum(padded)
    pad_start = pad_end - padded
    grp_start = jnp.cumsum(counts) - counts
    dest = pad_start[sorted_e] + jnp.arange(n_assign) - grp_start[sorted_e]
    n_blocks = -(-n_assign // EXPERT_BLOCK) + N_EXPERTS
    rows = jnp.full((n_blocks * EXPERT_BLOCK,), n_tok, dtype=jnp.int32).at[dest].set(tok)
    block_e = jnp.minimum(jnp.searchsorted(pad_end, jnp.arange(n_blocks) * EXPERT_BLOCK, side='right'), N_EXPERTS - 1)
    xs = jnp.concatenate([hs, jnp.zeros((1, d), hs.dtype)], axis=0)[rows].reshape(n_blocks, EXPERT_BLOCK, d)

    def run_block(args):
        xb, e = args
        return swiglu(xb, w_eg[e], w_eu[e], w_ed[e])

    ys = lax.map(run_block, (xs, block_e)).reshape(n_blocks * EXPERT_BLOCK, d)
    contrib = ys[dest] * wts.reshape(n_assign)[order][:, None]
    return jnp.zeros((n_tok, d), ys.dtype).at[tok].add(contrib)


def moe_ffn(h, w_router, b_router, w_eg, w_eu, w_ed, w_sg, w_su, w_sd):
    idx, wts = route(h, w_router, b_router)
    routed = lax.map(lambda a: grouped_experts(a[0], a[1], a[2], w_eg, w_eu, w_ed), (h, idx, wts))
    return routed + swiglu(h, w_sg, w_su, w_sd)


def setup_inputs(seed: int = 0) -> dict:
    key = jax.random.key(seed)
    ks = jax.random.split(key, 32)
    f32 = jnp.float32
    d = D_MODEL

    def nrm(k, shape, scale):
        return jax.random.normal(k, shape, f32) * scale

    b_i = nrm(ks[9], (DEPTH, 2, N_HEADS_C), 0.1)
    b_f = 3.0 + 3.0 * jax.random.uniform(ks[10], (DEPTH, 2, N_HEADS_C), f32)
    return {
        'x': nrm(ks[0], (BATCH, SEQ, d), 1.0),
        'c': nrm(ks[1], (BATCH, d), 1.0),
        'ctx': nrm(ks[2], (BATCH, CTX_LEN, d), 1.0),
        'c_ctx': nrm(ks[3], (d,), 1.0),
        'w_mod': nrm(ks[4], (DEPTH, d, N_MOD * d), 0.5 * d ** -0.5),
        'b_mod': nrm(ks[5], (DEPTH, N_MOD * d), 0.02),
        'g_mix': 1.0 + nrm(ks[6], (DEPTH, d), 0.05),
        'g_ffn': 1.0 + nrm(ks[7], (DEPTH, d), 0.05),
        'w_in': nrm(ks[8], (DEPTH, d, N_IN), d ** -0.5),
        'b_gate': jnp.stack([b_i[:, 0], b_f[:, 0], b_i[:, 1], b_f[:, 1]], axis=1),
        'sink': nrm(ks[11], (DEPTH, N_HEADS_A), 0.5),
        'lam_q1': nrm(ks[12], (DEPTH, HEAD_DIM), 0.1),
        'lam_k1': nrm(ks[13], (DEPTH, HEAD_DIM), 0.1),
        'lam_q2': nrm(ks[14], (DEPTH, HEAD_DIM), 0.1),
        'lam_k2': nrm(ks[15], (DEPTH, HEAD_DIM), 0.1),
        'g_diff': 1.0 + nrm(ks[16], (DEPTH, 2 * HEAD_DIM), 0.05),
        'g_mlstm': 1.0 + nrm(ks[17], (DEPTH, C_WIDTH), 0.05),
        'w_a': nrm(ks[18], (DEPTH, A_Q, d), A_Q ** -0.5),
        'w_b': nrm(ks[19], (DEPTH, B_V, d), B_V ** -0.5),
        'w_c': nrm(ks[20], (DEPTH, C_WIDTH, d), C_WIDTH ** -0.5),
        'w_out': nrm(ks[21], (DEPTH, d, d), d ** -0.5),
        'w_router': nrm(ks[22], (DEPTH, d, N_EXPERTS), d ** -0.5),
        'b_router': nrm(ks[23], (DEPTH, N_EXPERTS), 0.01),
        'w_exp_gate': nrm(ks[24], (DEPTH, N_EXPERTS, d, D_EXPERT), d ** -0.5),
        'w_exp_up': nrm(ks[25], (DEPTH, N_EXPERTS, d, D_EXPERT), d ** -0.5),
        'w_exp_down': nrm(ks[26], (DEPTH, N_EXPERTS, D_EXPERT, d), D_EXPERT ** -0.5),
        'w_sh_gate': nrm(ks[27], (DEPTH, d, D_SHARED), d ** -0.5),
        'w_sh_up': nrm(ks[28], (DEPTH, d, D_SHARED), d ** -0.5),
        'w_sh_down': nrm(ks[29], (DEPTH, D_SHARED, d), D_SHARED ** -0.5),
        'g_final': 1.0 + nrm(ks[30], (d,), 0.05),
    }


def reference(x, c, ctx, c_ctx, w_mod, b_mod, g_mix, g_ffn, w_in, b_gate, sink, lam_q1, lam_k1,
              lam_q2, lam_k2, g_diff, g_mlstm, w_a, w_b, w_c, w_out, w_router, b_router,
              w_exp_gate, w_exp_up, w_exp_down, w_sh_gate, w_sh_up, w_sh_down, g_final):
    b, s_len, d = x.shape
    ctx_len = ctx.shape[1]
    rows = s_len // GRID_W
    row = jnp.repeat(jnp.arange(rows), GRID_W)
    col = jnp.tile(jnp.arange(GRID_W), rows)
    cos, sin = axial_rope_tables(row, col)
    xc = ctx
    for layer in range(DEPTH):
        need_ctx = layer < DEPTH - 1
        lam_init = 0.8 - 0.6 * math.exp(-0.3 * layer)
        mod = (jax.nn.silu(c) @ w_mod[layer] + b_mod[layer]).reshape(b, N_MOD, 1, d)
        mod_c = (jax.nn.silu(c_ctx) @ w_mod[layer] + b_mod[layer]).reshape(N_MOD, d)
        hl = modulate(rms_norm(x, g_mix[layer]), mod[:, 0], mod[:, 1])
        hc = modulate(rms_norm(xc, g_mix[layer]), mod_c[0], mod_c[1])
        out_l, out_c = token_mixers(hl, hc, cos, sin, lam_init, need_ctx, w_in[layer], b_gate[layer],
                                    sink[layer], lam_q1[layer], lam_k1[layer], lam_q2[layer], lam_k2[layer],
                                    g_diff[layer], g_mlstm[layer], w_a[layer], w_b[layer], w_c[layer], w_out[layer])
        x = x + mod[:, 2] * out_l
        hl = modulate(rms_norm(x, g_ffn[layer]), mod[:, 3], mod[:, 4])
        moe = (w_router[layer], b_router[layer], w_exp_gate[layer], w_exp_up[layer], w_exp_down[layer],
               w_sh_gate[layer], w_sh_up[layer], w_sh_down[layer])
        if need_ctx:
            xc = xc + mod_c[2] * out_c
            hc = modulate(rms_norm(xc, g_ffn[layer]), mod_c[3], mod_c[4])
            y = moe_ffn(jnp.concatenate([hc, hl], axis=1), *moe)
            xc = xc + mod_c[5] * y[:, :ctx_len]
            x = x + mod[:, 5] * y[:, ctx_len:]
        else:
            x = x + mod[:, 5] * moe_ffn(hl, *moe)
    return rms_norm(x, g_final)
```

```python
import functools
import math

import jax
import jax.numpy as jnp
from jax import lax
from jax.experimental import pallas as pl
from jax.experimental.pallas import tpu as pltpu

F32 = jnp.float32
BF16 = jnp.bfloat16
HIGHEST = lax.Precision.HIGHEST

GRID_W = 64
N_MOD = 6
HEAD_DIM = 64
N_HEADS_A = 8
KV_HEADS_A = 2
WINDOW = 128
N_HEADS_B = 4
N_HEADS_C = 4
HEAD_DIM_C = 128
N_EXPERTS = 64
N_GROUPS = 8
TOPK_GROUPS = 4
TOP_K = 8
ROUTED_SCALE = 2.5
ROPE_BASE = 10000.0
EPS = 1e-6

LANES = 128
CHUNK = 128
TILE_N = 512
TM = 256
EXPERT_ROWS = 512
COMBINE_TM = 128
NEG = -1e30
VMEM_LIMIT = 56 * 1024 * 1024

T_AQ, T_BQ, T_BK, T_KVA, T_BV, T_CO, T_GT, T_CQ, T_CV, N_TILES = 0, 1, 2, 3, 4, 5, 6, 12, 13, 14

NT_DIMS = (((1,), (1,)), ((), ()))


def _params(sem):
    return pltpu.CompilerParams(dimension_semantics=sem, vmem_limit_bytes=VMEM_LIMIT)


def _rms(x, g):
    return x * lax.rsqrt(jnp.mean(x * x, axis=-1, keepdims=True) + EPS) * g


def _sigmoid(x):
    return jax.nn.sigmoid(x)


def _mod_kernel(c_ref, w_ref, b_ref, o_ref):
    c = c_ref[...]
    s = c * _sigmoid(c)
    o_ref[...] = jnp.dot(s, w_ref[...], precision=HIGHEST, preferred_element_type=F32) + b_ref[...]


def _mod_vectors(cs, w_mod, b_mod):
    depth, d, n = w_mod.shape
    r = cs.shape[0]
    tn = 3 * LANES
    return pl.pallas_call(
        _mod_kernel,
        grid=(depth, n // tn),
        in_specs=[pl.BlockSpec((r, d), lambda l, j: (0, 0)),
                  pl.BlockSpec((None, d, tn), lambda l, j: (l, 0, j)),
                  pl.BlockSpec((None, 1, tn), lambda l, j: (l, 0, j))],
        out_specs=pl.BlockSpec((None, r, tn), lambda l, j: (l, 0, j)),
        out_shape=jax.ShapeDtypeStruct((depth, r, n), F32),
        compiler_params=_params(("parallel", "parallel")),
        name="mod_vectors",
    )(cs, w_mod, b_mod.reshape(depth, 1, n))


def _inproj_kernel(x_ref, mod_ref, g_ref, w_ref, wkt_ref, cos_ref, sa_ref, sb_ref, p_ref, gate_ref, kt_ref):
    x = x_ref[...]
    h = _rms(x, g_ref[...]) * (1.0 + mod_ref[1:2, :]) + mod_ref[0:1, :]
    hb = h.astype(BF16)
    cos, sa, sb = cos_ref[...], sa_ref[...], sb_ref[...]

    def rope(t):
        return t * cos + pltpu.roll(t, 16, 1) * sa + pltpu.roll(t, LANES - 16, 1) * sb

    q_scale = HEAD_DIM ** -0.5
    for j in range(N_TILES):
        acc = jnp.dot(hb, w_ref[:, j * TILE_N:(j + 1) * TILE_N], preferred_element_type=F32)
        parts = [acc[:, s * LANES:(s + 1) * LANES] for s in range(TILE_N // LANES)]
        if j in (T_AQ, T_BQ):
            parts = [rope(t) * q_scale for t in parts]
        elif j == T_BK:
            parts = [rope(t) for t in parts]
        elif j == T_KVA:
            gate_ref[...] = parts[2]
            parts[0] = rope(parts[0])
        for s, t in enumerate(parts):
            p_ref[:, j * TILE_N + s * LANES:j * TILE_N + (s + 1) * LANES] = t.astype(BF16)
    kt = lax.dot_general(wkt_ref[...], hb, NT_DIMS, preferred_element_type=F32)
    kt_ref[...] = (kt * (HEAD_DIM_C ** -0.5)).astype(BF16)


def _inproj(xs, mods, g_mix, w_big, w_kt, cos, sa, sb, n_ctx_tiles):
    b, l, d = xs.shape
    npad = w_big.shape[1]
    ck = w_kt.shape[0]
    grid = (b, l // TM)
    tok = lambda bi, ti: (bi, ti, 0)
    return pl.pallas_call(
        _inproj_kernel,
        grid=grid,
        in_specs=[pl.BlockSpec((None, TM, d), tok),
                  pl.BlockSpec((None, None, 8, d), lambda bi, ti: (bi, jnp.where(ti >= n_ctx_tiles, 1, 0), 0, 0)),
                  pl.BlockSpec((1, d), lambda bi, ti: (0, 0)),
                  pl.BlockSpec((d, npad), lambda bi, ti: (0, 0), pipeline_mode=pl.Buffered(1)),
                  pl.BlockSpec((ck, d), lambda bi, ti: (0, 0), pipeline_mode=pl.Buffered(1)),
                  pl.BlockSpec((TM, LANES), lambda bi, ti: (ti, 0)),
                  pl.BlockSpec((TM, LANES), lambda bi, ti: (ti, 0)),
                  pl.BlockSpec((TM, LANES), lambda bi, ti: (ti, 0))],
        out_specs=[pl.BlockSpec((None, TM, npad), tok),
                   pl.BlockSpec((None, TM, LANES), tok),
                   pl.BlockSpec((None, ck, TM), lambda bi, ti: (bi, 0, ti))],
        out_shape=[jax.ShapeDtypeStruct((b, l, npad), BF16),
                   jax.ShapeDtypeStruct((b, l, LANES), F32),
                   jax.ShapeDtypeStruct((b, ck, l), BF16)],
        compiler_params=_params(("parallel", "parallel")),
        name="inproj",
    )(xs, mods, g_mix, w_big, w_kt, cos, sa, sb)


def _mixa_kernel(sink_ref, q_ref, kp_ref, kc_ref, kn_ref, kx_ref, o_ref, *, n_ctx_blocks, n_blocks):
    i = pl.program_id(1)
    lat = i >= n_ctx_blocks
    has_prev = jnp.logical_and(lat, i > n_ctx_blocks)
    has_next = jnp.logical_and(lat, i < n_blocks - 1)
    r = lax.broadcasted_iota(jnp.int32, (CHUNK, CHUNK), 0)
    c = lax.broadcasted_iota(jnp.int32, (CHUNK, CHUNK), 1)
    n_ctx = kx_ref.shape[0]
    valid = jnp.concatenate([
        jnp.logical_and(c >= r, has_prev),
        jnp.broadcast_to(lat, (CHUNK, CHUNK)),
        jnp.logical_and(c <= r, has_next),
        jnp.ones((CHUNK, n_ctx), jnp.bool_)], axis=1)
    kcat = jnp.concatenate([kp_ref[:, :LANES], kc_ref[:, :LANES], kn_ref[:, :LANES], kx_ref[:, :LANES]], axis=0)
    vcat = jnp.concatenate([kp_ref[:, LANES:], kc_ref[:, LANES:], kn_ref[:, LANES:], kx_ref[:, LANES:]], axis=0)
    lane = lax.broadcasted_iota(jnp.int32, (CHUNK, LANES), 1)
    low = lane < HEAD_DIM
    n_pairs = N_HEADS_A // KV_HEADS_A
    outs = []
    for gk in range(KV_HEADS_A):
        keep = low if gk == 0 else jnp.logical_not(low)
        zero = jnp.zeros((CHUNK, LANES), BF16)
        lhs = jnp.concatenate([jnp.where(keep, q_ref[:, t * LANES:(t + 1) * LANES], zero) for t in range(n_pairs)],
                              axis=0)
        s = lax.dot_general(lhs, kcat, NT_DIMS, preferred_element_type=F32)
        o_g = []
        for t in range(n_pairs):
            st = jnp.where(valid, s[t * CHUNK:(t + 1) * CHUNK], NEG)
            sk = sink_ref[gk * n_pairs + t]
            m = jnp.maximum(jnp.max(st, axis=-1, keepdims=True), sk)
            p = jnp.exp(st - m)
            den = jnp.sum(p, axis=-1, keepdims=True) + jnp.exp(sk - m)
            o_g.append(jnp.dot(p.astype(BF16), vcat, preferred_element_type=F32) / den)
        outs.append(o_g)
    for t in range(n_pairs):
        o_ref[:, t * LANES:(t + 1) * LANES] = jnp.where(low, outs[0][t], outs[1][t]).astype(BF16)


def _mixer_a(p, sink, n_ctx):
    b, l, _ = p.shape
    nb = l // CHUNK
    ncb = n_ctx // CHUNK
    kvw = 2 * LANES
    kv_col = T_KVA * TILE_N // kvw
    aq_w = N_HEADS_A * HEAD_DIM
    kern = functools.partial(_mixa_kernel, n_ctx_blocks=ncb, n_blocks=nb)
    return pl.pallas_call(
        kern,
        grid=(b, nb),
        in_specs=[pl.BlockSpec(memory_space=pltpu.SMEM),
                  pl.BlockSpec((None, CHUNK, aq_w), lambda bi, i: (bi, i, T_AQ)),
                  pl.BlockSpec((None, CHUNK, kvw), lambda bi, i: (bi, jnp.maximum(i - 1, 0), kv_col)),
                  pl.BlockSpec((None, CHUNK, kvw), lambda bi, i: (bi, i, kv_col)),
                  pl.BlockSpec((None, CHUNK, kvw), lambda bi, i: (bi, jnp.minimum(i + 1, nb - 1), kv_col)),
                  pl.BlockSpec((None, n_ctx, kvw), lambda bi, i: (bi, 0, kv_col))],
        out_specs=pl.BlockSpec((None, CHUNK, aq_w), lambda bi, i: (bi, i, 0)),
        out_shape=jax.ShapeDtypeStruct((b, l, aq_w), BF16),
        compiler_params=_params(("parallel", "parallel")),
        name="mixer_a",
    )(sink, p, p, p, p, p)


def _mixb_kernel(lam_ref, gd_ref, q_ref, k_ref, v_ref, o_ref, *, lam_init, n_ctx_tiles, kc):
    qi = pl.program_id(2)
    lp = lam_ref[...]
    lam = (jnp.exp(jnp.sum(lp[0:1] * lp[1:2], axis=-1, keepdims=True))
           - jnp.exp(jnp.sum(lp[2:3] * lp[3:4], axis=-1, keepdims=True)) + lam_init)
    q = q_ref[...]
    tq = q.shape[0]
    lane = lax.broadcasted_iota(jnp.int32, (tq, LANES), 1)
    zero = jnp.zeros_like(q)
    q1 = jnp.where(lane < HEAD_DIM, q, zero)
    q2 = jnp.where(lane >= HEAD_DIM, q, zero)
    n_all = k_ref.shape[0] // kc
    nk = jnp.where(qi < n_ctx_tiles, n_ctx_tiles * (tq // kc), n_all)

    def update(s, vv, m, l, a):
        mn = jnp.maximum(m, jnp.max(s, axis=-1, keepdims=True))
        al = jnp.exp(m - mn)
        pr = jnp.exp(s - mn)
        l = al * l + jnp.sum(pr, axis=-1, keepdims=True)
        a = al * a + jnp.dot(pr.astype(BF16), vv, preferred_element_type=F32)
        return mn, l, a

    def body(ci, carry):
        m1, l1, a1, m2, l2, a2 = carry
        off = pl.multiple_of(ci * kc, kc)
        kk = k_ref[pl.ds(off, kc), :]
        vv = v_ref[pl.ds(off, kc), :]
        s1 = lax.dot_general(q1, kk, NT_DIMS, preferred_element_type=F32)
        s2 = lax.dot_general(q2, kk, NT_DIMS, preferred_element_type=F32)
        m1, l1, a1 = update(s1, vv, m1, l1, a1)
        m2, l2, a2 = update(s2, vv, m2, l2, a2)
        return m1, l1, a1, m2, l2, a2

    m0 = jnp.full((tq, 1), NEG, F32)
    l0 = jnp.zeros((tq, 1), F32)
    a0 = jnp.zeros((tq, LANES), F32)
    m1, l1, a1, m2, l2, a2 = lax.fori_loop(0, nk, body, (m0, l0, a0, m0, l0, a0))
    o = a1 / l1 - lam * (a2 / l2)
    o_ref[...] = (_rms(o, gd_ref[...]) * (1.0 - lam_init)).astype(BF16)


def _mixer_b(p, lam_params, g_diff, lam_init, n_ctx):
    b, l, _ = p.shape
    tq = TM
    kern = functools.partial(_mixb_kernel, lam_init=lam_init, n_ctx_tiles=n_ctx // tq, kc=tq)
    q0 = T_BQ * TILE_N // LANES
    k0 = T_BK * TILE_N // LANES
    v0 = T_BV * TILE_N // LANES
    return pl.pallas_call(
        kern,
        grid=(b, N_HEADS_B, l // tq),
        in_specs=[pl.BlockSpec((4, HEAD_DIM), lambda bi, h, qi: (0, 0)),
                  pl.BlockSpec((1, LANES), lambda bi, h, qi: (0, 0)),
                  pl.BlockSpec((None, tq, LANES), lambda bi, h, qi: (bi, qi, q0 + h)),
                  pl.BlockSpec((None, l, LANES), lambda bi, h, qi: (bi, 0, k0 + h)),
                  pl.BlockSpec((None, l, LANES), lambda bi, h, qi: (bi, 0, v0 + h))],
        out_specs=pl.BlockSpec((None, tq, LANES), lambda bi, h, qi: (bi, qi, h)),
        out_shape=jax.ShapeDtypeStruct((b, l, N_HEADS_B * LANES), BF16),
        compiler_params=_params(("parallel", "parallel", "parallel")),
        name="mixer_b",
    )(lam_params, g_diff, p, p, p)


def _log_sigmoid(x):
    return jnp.minimum(x, 0.0) - jnp.log1p(jnp.exp(-jnp.abs(x)))


def _mlstm_kernel(q_ref, kt_ref, v_ref, gc_ref, gr_ref, bc_ref, br_ref, o_ref, s_scr, m_scr):
    d = pl.program_id(1)
    c = pl.program_id(2)

    @pl.when(c == 0)
    def _():
        s_scr[...] = jnp.zeros_like(s_scr)
        m_scr[...] = jnp.zeros_like(m_scr)

    fwd = d == 0
    r = lax.broadcasted_iota(jnp.int32, (CHUNK, CHUNK), 0)
    cc = lax.broadcasted_iota(jnp.int32, (CHUNK, CHUNK), 1)
    tri = jnp.where(fwd, r - cc, cc - r) >= 0
    trif = tri.astype(F32)
    gcol = gc_ref[...] + bc_ref[...]
    grow = gr_ref[...] + br_ref[...]
    lf_col = _log_sigmoid(gcol)
    lf_row = _log_sigmoid(grow)
    bcum_col = jnp.dot(trif, lf_col, precision=HIGHEST, preferred_element_type=F32)
    bcum_row = lax.dot_general(lf_row, trif, NT_DIMS, precision=HIGHEST, preferred_element_type=F32)
    tot_row = jnp.sum(lf_row, axis=-1, keepdims=True)
    lane = lax.broadcasted_iota(jnp.int32, (CHUNK, LANES), 1)
    ones_col = (lane == 0).astype(BF16)
    nh = N_HEADS_C

    for h in range(nh):
        def pick_col(a, kind):
            return jnp.where(fwd, a[:, kind * nh + h:kind * nh + h + 1],
                             a[:, (kind + 2) * nh + h:(kind + 2) * nh + h + 1])

        def pick_row(a, kind):
            return jnp.where(fwd, a[kind * nh + h:kind * nh + h + 1, :],
                             a[(kind + 2) * nh + h:(kind + 2) * nh + h + 1, :])

        ic_row = pick_row(grow, 0)
        b_col = pick_col(bcum_col, 1)
        b_row = pick_row(bcum_row, 1)
        total = pick_row(tot_row, 1)
        m_st = m_scr[h, 0:1, 0:1]
        dm = jnp.where(tri, b_col - b_row + ic_row, NEG)
        inter = b_col + m_st
        m_t = jnp.maximum(inter, jnp.max(dm, axis=-1, keepdims=True))
        e = jnp.exp(dm - m_t)
        qh = q_ref[:, h * LANES:(h + 1) * LANES]
        kth = kt_ref[h * LANES:(h + 1) * LANES, :]
        vaug = jnp.concatenate([v_ref[:, h * LANES:(h + 1) * LANES], ones_col], axis=1)
        s = jnp.dot(qh, kth, preferred_element_type=F32) * e
        st = s_scr[h]
        intra = jnp.dot(s.astype(BF16), vaug, preferred_element_type=F32)
        cross = jnp.dot(qh, st.astype(BF16), preferred_element_type=F32)
        nd = intra + jnp.exp(inter - m_t) * cross
        den = nd[:, LANES:LANES + 1]
        o_ref[:, h * LANES:(h + 1) * LANES] = nd[:, :LANES] / jnp.maximum(jnp.abs(den), jnp.exp(-m_t))
        gs_row = total - b_row + ic_row
        m_new = jnp.maximum(total + m_st, jnp.max(gs_row, axis=-1, keepdims=True))
        decay = jnp.exp(total + m_st - m_new)
        wkt = (kth.astype(F32) * jnp.exp(gs_row - m_new)).astype(BF16)
        s_scr[h] = decay * st + jnp.dot(wkt, vaug, preferred_element_type=F32)
        m_scr[h] = jnp.broadcast_to(m_new, m_scr.shape[1:])


def _mlstm(p, kt, gates, gates_t, bias_row, bias_col, n_ctx):
    b, l, _ = p.shape
    nc = l // CHUNK
    ncc = n_ctx // CHUNK
    cw = N_HEADS_C * HEAD_DIM_C

    def chunk(d, c):
        rev = jnp.where(c < ncc, ncc - 1 - c, nc + ncc - 1 - c)
        return jnp.where(d == 0, c, rev)

    return pl.pallas_call(
        _mlstm_kernel,
        grid=(b, 2, nc),
        in_specs=[pl.BlockSpec((None, CHUNK, cw), lambda bi, d, c: (bi, chunk(d, c), T_CQ)),
                  pl.BlockSpec((None, cw, CHUNK), lambda bi, d, c: (bi, 0, chunk(d, c))),
                  pl.BlockSpec((None, CHUNK, cw), lambda bi, d, c: (bi, chunk(d, c), T_CV)),
                  pl.BlockSpec((None, CHUNK, LANES), lambda bi, d, c: (bi, chunk(d, c), 0)),
                  pl.BlockSpec((None, 16, CHUNK), lambda bi, d, c: (bi, 0, chunk(d, c))),
                  pl.BlockSpec((1, LANES), lambda bi, d, c: (0, 0)),
                  pl.BlockSpec((16, LANES), lambda bi, d, c: (0, 0))],
        out_specs=pl.BlockSpec((None, None, CHUNK, cw), lambda bi, d, c: (d, bi, chunk(d, c), 0)),
        out_shape=jax.ShapeDtypeStruct((2, b, l, cw), F32),
        scratch_shapes=[pltpu.VMEM((N_HEADS_C, HEAD_DIM_C, 2 * LANES), F32),
                        pltpu.VMEM((N_HEADS_C, 8, LANES), F32)],
        compiler_params=_params(("parallel", "parallel", "arbitrary")),
        name="mlstm",
    )(p, kt, p, gates, gates_t, bias_row, bias_col)


def _merge_kernel(x_ref, mod_ref, oa_ref, ob_ref, hf_ref, hb_ref, co_ref, gt_ref, gm_ref,
                  wa_ref, wb_ref, wc_ref, wo_ref, xo_ref):
    d = x_ref.shape[-1]
    hs = hf_ref[...] + hb_ref[...]
    co = co_ref[...].astype(F32)
    gm = gm_ref[...]
    oc = []
    for h in range(N_HEADS_C):
        sl = slice(h * LANES, (h + 1) * LANES)
        oc.append((_rms(hs[:, sl], gm[:, sl]) * _sigmoid(co[:, sl])).astype(BF16))
    oc = jnp.concatenate(oc, axis=1)
    y = (_sigmoid(gt_ref[:, 0:d].astype(F32)) * jnp.dot(oa_ref[...], wa_ref[...], preferred_element_type=F32)
         + _sigmoid(gt_ref[:, d:2 * d].astype(F32)) * jnp.dot(ob_ref[...], wb_ref[...], preferred_element_type=F32)
         + _sigmoid(gt_ref[:, 2 * d:3 * d].astype(F32)) * jnp.dot(oc, wc_ref[...], preferred_element_type=F32))
    out = jnp.dot(y.astype(BF16), wo_ref[...], preferred_element_type=F32)
    xo_ref[...] = x_ref[...] + mod_ref[2:3, :] * out


def _merge(xs, mods, oa, ob, hm, p, g_mlstm, wa, wb, wc, wo, n_ctx_tiles):
    b, l, d = xs.shape
    tok = lambda bi, ti: (bi, ti, 0)
    cw = N_HEADS_C * HEAD_DIM_C
    const = lambda bi, ti: (0, 0)
    return pl.pallas_call(
        _merge_kernel,
        grid=(b, l // TM),
        in_specs=[pl.BlockSpec((None, TM, d), tok),
                  pl.BlockSpec((None, None, 8, d), lambda bi, ti: (bi, jnp.where(ti >= n_ctx_tiles, 1, 0), 0, 0)),
                  pl.BlockSpec((None, TM, oa.shape[-1]), tok),
                  pl.BlockSpec((None, TM, ob.shape[-1]), tok),
                  pl.BlockSpec((None, None, TM, cw), lambda bi, ti: (0, bi, ti, 0)),
                  pl.BlockSpec((None, None, TM, cw), lambda bi, ti: (1, bi, ti, 0)),
                  pl.BlockSpec((None, TM, cw), lambda bi, ti: (bi, ti, T_CO)),
                  pl.BlockSpec((None, TM, 3 * d), lambda bi, ti: (bi, ti, T_GT * TILE_N // (3 * d))),
                  pl.BlockSpec((1, cw), const),
                  pl.BlockSpec(wa.shape, const), pl.BlockSpec(wb.shape, const),
                  pl.BlockSpec(wc.shape, const), pl.BlockSpec(wo.shape, const)],
        out_specs=pl.BlockSpec((None, TM, d), tok),
        out_shape=jax.ShapeDtypeStruct((b, l, d), F32),
        compiler_params=_params(("parallel", "parallel")),
        name="merge",
    )(xs, mods, oa, ob, hm, hm, p, p, g_mlstm, wa, wb, wc, wo)


def _router_kernel(x_ref, mod_ref, g_ref, wrt_ref, br_ref, h_ref, idx_ref, wt_ref):
    h = _rms(x_ref[...], g_ref[...]) * (1.0 + mod_ref[4:5, :]) + mod_ref[3:4, :]
    h_ref[...] = h
    tm = h.shape[0]
    per = N_EXPERTS // N_GROUPS
    lt = lax.dot_general(wrt_ref[...], h, NT_DIMS, precision=HIGHEST, preferred_element_type=F32)
    s = _sigmoid(lt)
    sel = s + br_ref[...]
    ninf = -jnp.inf
    sel3 = sel.reshape(N_GROUPS, per, tm)
    eidx = lax.broadcasted_iota(jnp.int32, (N_GROUPS, per, tm), 1)
    m1 = jnp.max(sel3, axis=1, keepdims=True)
    first = jnp.min(jnp.where(sel3 == m1, eidx, per), axis=1, keepdims=True)
    m2 = jnp.max(jnp.where(eidx == first, ninf, sel3), axis=1, keepdims=True)
    gscore = (m1 + m2).reshape(N_GROUPS, tm)
    gidx = lax.broadcasted_iota(jnp.int32, (N_GROUPS, tm), 0)
    gmask = jnp.zeros((N_GROUPS, tm), jnp.bool_)
    cur = gscore
    for _ in range(TOPK_GROUPS):
        mx = jnp.max(cur, axis=0, keepdims=True)
        hit = gidx == jnp.min(jnp.where(cur == mx, gidx, N_GROUPS), axis=0, keepdims=True)
        gmask = jnp.logical_or(gmask, hit)
        cur = jnp.where(hit, ninf, cur)
    cur = jnp.where(gmask.reshape(N_GROUPS, 1, tm), sel3, ninf).reshape(N_EXPERTS, tm)
    eid = lax.broadcasted_iota(jnp.int32, (N_EXPERTS, tm), 0)
    ids, ws = [], []
    for _ in range(TOP_K):
        mx = jnp.max(cur, axis=0, keepdims=True)
        pick = jnp.min(jnp.where(cur == mx, eid, N_EXPERTS), axis=0, keepdims=True)
        hit = eid == pick
        ids.append(pick)
        ws.append(jnp.sum(jnp.where(hit, s, 0.0), axis=0, keepdims=True))
        cur = jnp.where(hit, ninf, cur)
    wsum = ws[0]
    for w in ws[1:]:
        wsum = wsum + w
    idx_ref[...] = jnp.concatenate(ids, axis=0)
    wt_ref[...] = jnp.concatenate([w / wsum * ROUTED_SCALE for w in ws], axis=0)


def _router(xs, mods, g_ffn, w_router_t, b_router, n_ctx_tiles):
    b, l, d = xs.shape
    tok = lambda bi, ti: (bi, ti, 0)
    const = lambda bi, ti: (0, 0)
    return pl.pallas_call(
        _router_kernel,
        grid=(b, l // TM),
        in_specs=[pl.BlockSpec((None, TM, d), tok),
                  pl.BlockSpec((None, None, 8, d), lambda bi, ti: (bi, jnp.where(ti >= n_ctx_tiles, 1, 0), 0, 0)),
                  pl.BlockSpec((1, d), const),
                  pl.BlockSpec((N_EXPERTS, d), const),
                  pl.BlockSpec((N_EXPERTS, 1), const)],
        out_specs=[pl.BlockSpec((None, TM, d), tok),
                   pl.BlockSpec((None, TOP_K, TM), lambda bi, ti: (bi, 0, ti)),
                   pl.BlockSpec((None, TOP_K, TM), lambda bi, ti: (bi, 0, ti))],
        out_shape=[jax.ShapeDtypeStruct((b, l, d), F32),
                   jax.ShapeDtypeStruct((b, TOP_K, l), jnp.int32),
                   jax.ShapeDtypeStruct((b, TOP_K, l), F32)],
        compiler_params=_params(("parallel", "parallel")),
        name="router",
    )(xs, mods, g_ffn, w_router_t, b_router)


def _row_copy(src_hbm, dst_vmem, src_row, dst_row, sem):
    return pltpu.make_async_copy(src_hbm.at[pl.ds(src_row, 1)], dst_vmem.at[pl.ds(dst_row, 1)], sem)


def _expert_kernel(be_ref, nu_ref, rows_hbm, h_hbm, wg_ref, wu_ref, wd_ref, y_ref, idx_smem, xbuf, sem_idx, sem_rows):
    i = pl.program_id(0)
    rows = xbuf.shape[0]

    @pl.when(i < nu_ref[0])
    def _():
        cp = pltpu.make_async_copy(rows_hbm.at[i], idx_smem, sem_idx)
        cp.start()
        cp.wait()

        def issue(j, carry):
            _row_copy(h_hbm, xbuf, idx_smem[j], j, sem_rows).start()
            return carry

        lax.fori_loop(0, rows, issue, 0, unroll=8)
        pltpu.make_async_copy(h_hbm.at[pl.ds(0, rows)], xbuf, sem_rows).wait()
        x = xbuf[...].astype(BF16)
        g = jnp.dot(x, wg_ref[...], preferred_element_type=F32)
        u = jnp.dot(x, wu_ref[...], preferred_element_type=F32)
        a = (g * _sigmoid(g) * u).astype(BF16)
        y_ref[...] = jnp.dot(a, wd_ref[...], preferred_element_type=F32)

    @pl.when(i >= nu_ref[0])
    def _():
        y_ref[...] = jnp.zeros_like(y_ref)


def _experts(h_flat, rows, block_e, n_used, wg, wu, wd):
    n_blocks, blk = rows.shape
    d = h_flat.shape[1]
    de = wg.shape[-1]
    grid_spec = pltpu.PrefetchScalarGridSpec(
        num_scalar_prefetch=2,
        grid=(n_blocks,),
        in_specs=[pl.BlockSpec(memory_space=pl.ANY),
                  pl.BlockSpec(memory_space=pl.ANY),
                  pl.BlockSpec((None, d, de), lambda i, be, nu: (be[i], 0, 0)),
                  pl.BlockSpec((None, d, de), lambda i, be, nu: (be[i], 0, 0)),
                  pl.BlockSpec((None, de, d), lambda i, be, nu: (be[i], 0, 0))],
        out_specs=pl.BlockSpec((blk, d), lambda i, be, nu: (i, 0)),
        scratch_shapes=[pltpu.SMEM((blk,), jnp.int32),
                        pltpu.VMEM((blk, d), F32),
                        pltpu.SemaphoreType.DMA,
                        pltpu.SemaphoreType.DMA])
    return pl.pallas_call(
        _expert_kernel,
        grid_spec=grid_spec,
        out_shape=jax.ShapeDtypeStruct((n_blocks * blk, d), F32),
        compiler_params=_params(("arbitrary",)),
        name="experts",
    )(block_e, n_used, rows, h_flat, wg, wu, wd)


def _combine_kernel(pos_hbm, ys_hbm, x_ref, h_ref, w_ref, mod_ref, wsg_ref, wsu_ref, wsd_ref, o_ref,
                    idx_smem, buf, sem_idx, sem_rows):
    i = pl.program_id(0)
    tm = x_ref.shape[0]
    cp = pltpu.make_async_copy(pos_hbm.at[i], idx_smem, sem_idx)
    cp.start()
    cp.wait()

    def issue(j, carry):
        for k in range(TOP_K):
            _row_copy(ys_hbm, buf.at[k], idx_smem[j * TOP_K + k], j, sem_rows).start()
        return carry

    lax.fori_loop(0, tm, issue, 0, unroll=2)
    hb = h_ref[...].astype(BF16)
    g = jnp.dot(hb, wsg_ref[...], preferred_element_type=F32)
    u = jnp.dot(hb, wsu_ref[...], preferred_element_type=F32)
    acc = jnp.dot((g * _sigmoid(g) * u).astype(BF16), wsd_ref[...], preferred_element_type=F32)
    for k in range(TOP_K):
        pltpu.make_async_copy(ys_hbm.at[pl.ds(0, tm)], buf.at[k], sem_rows).wait()
    w = w_ref[...]
    for k in range(TOP_K):
        acc = acc + w[:, k:k + 1] * buf[k]
    o_ref[...] = x_ref[...] + mod_ref[5:6, :] * acc


def _combine(x_flat, h_flat, pos, ys, wts, mods, wsg, wsu, wsd, tiles_per_sample, n_ctx_tiles):
    n, d = x_flat.shape
    tm = COMBINE_TM
    ds_ = wsg.shape[-1]
    tok = lambda i: (i, 0)
    const = lambda i: (0, 0)

    def mod_idx(i):
        return (i // tiles_per_sample, jnp.where(i % tiles_per_sample >= n_ctx_tiles, 1, 0), 0, 0)

    return pl.pallas_call(
        _combine_kernel,
        grid=(n // tm,),
        in_specs=[pl.BlockSpec(memory_space=pl.ANY),
                  pl.BlockSpec(memory_space=pl.ANY),
                  pl.BlockSpec((tm, d), tok),
                  pl.BlockSpec((tm, d), tok),
                  pl.BlockSpec((tm, TOP_K), tok),
                  pl.BlockSpec((None, None, 8, d), mod_idx),
                  pl.BlockSpec((d, ds_), const), pl.BlockSpec((d, ds_), const), pl.BlockSpec((ds_, d), const)],
        out_specs=pl.BlockSpec((tm, d), tok),
        out_shape=jax.ShapeDtypeStruct((n, d), F32),
        scratch_shapes=[pltpu.SMEM((tm * TOP_K,), jnp.int32),
                        pltpu.VMEM((TOP_K, tm, d), F32),
                        pltpu.SemaphoreType.DMA,
                        pltpu.SemaphoreType.DMA],
        compiler_params=_params(("arbitrary",)),
        name="combine",
    )(pos, ys, x_flat, h_flat, wts, mods, wsg, wsu, wsd)


def _final_kernel(x_ref, g_ref, o_ref):
    o_ref[...] = _rms(x_ref[...], g_ref[...])


def _final_norm(xs, g_final, n_ctx_tiles):
    b, l, d = xs.shape
    s_len = l - n_ctx_tiles * TM
    return pl.pallas_call(
        _final_kernel,
        grid=(b, s_len // TM),
        in_specs=[pl.BlockSpec((None, TM, d), lambda bi, ti: (bi, ti + n_ctx_tiles, 0)),
                  pl.BlockSpec((1, d), lambda bi, ti: (0, 0))],
        out_specs=pl.BlockSpec((None, TM, d), lambda bi, ti: (bi, ti, 0)),
        out_shape=jax.ShapeDtypeStruct((b, s_len, d), F32),
        compiler_params=_params(("parallel", "parallel")),
        name="final_norm",
    )(xs, g_final)


def _dispatch_plan(idx_t, blk):
    b, k, l = idx_t.shape
    e = jnp.transpose(idx_t, (0, 2, 1)).reshape(-1)
    na = e.shape[0]
    order = jnp.argsort(e)
    se = e[order]
    counts = jnp.bincount(e, length=N_EXPERTS)
    padded = (counts + blk - 1) // blk * blk
    pad_end = jnp.cumsum(padded)
    pad_start = pad_end - padded
    grp_start = jnp.cumsum(counts) - counts
    dest = (pad_start[se] + jnp.arange(na) - grp_start[se]).astype(jnp.int32)
    n_blocks = -(-na // blk) + N_EXPERTS
    rows = jnp.zeros((n_blocks * blk,), jnp.int32).at[dest].set((order // k).astype(jnp.int32))
    pos = jnp.zeros((na,), jnp.int32).at[order].set(dest)
    block_e = jnp.minimum(jnp.searchsorted(pad_end, jnp.arange(n_blocks) * blk, side='right'), N_EXPERTS - 1)
    n_used = (pad_end[-1] // blk).astype(jnp.int32).reshape(1)
    return rows.reshape(n_blocks, blk), pos, block_e.astype(jnp.int32), n_used


def _rope_tables(s_len, n_ctx):
    rows = s_len // GRID_W
    row = jnp.repeat(jnp.arange(rows), GRID_W).astype(F32)
    col = jnp.tile(jnp.arange(GRID_W), rows).astype(F32)
    quarter = HEAD_DIM // 4
    inv = 1.0 / (ROPE_BASE ** (jnp.arange(quarter, dtype=F32) / quarter))
    ar, ac = row[:, None] * inv, col[:, None] * inv
    cr, sr, cc, sc = jnp.cos(ar), jnp.sin(ar), jnp.cos(ac), jnp.sin(ac)
    z = jnp.zeros_like(sr)
    cos = jnp.concatenate([cr, cr, cc, cc], axis=1)
    sa = jnp.concatenate([z, sr, z, sc], axis=1)
    sb = jnp.concatenate([-sr, z, -sc, z], axis=1)
    rep = LANES // HEAD_DIM

    def full(t, fill):
        t = jnp.tile(t, (1, rep))
        return jnp.concatenate([jnp.full((n_ctx, LANES), fill, F32), t], axis=0)

    return full(cos, 1.0), full(sa, 0.0), full(sb, 0.0)


def _pair_perm():
    g = N_HEADS_A // KV_HEADS_A
    heads = [h for t in range(g) for h in (t, t + g)]
    return jnp.concatenate([jnp.arange(h * HEAD_DIM, (h + 1) * HEAD_DIM) for h in heads])


def _split_w_in(w):
    a_q, a_kv = N_HEADS_A * HEAD_DIM, KV_HEADS_A * HEAD_DIM
    b_w = N_HEADS_B * 2 * HEAD_DIM
    c_w = N_HEADS_C * HEAD_DIM_C
    sizes = (a_q, a_kv, a_kv, b_w, b_w, b_w, c_w, c_w, c_w, c_w, 4 * N_HEADS_C, w.shape[1])
    parts, start = [], 0
    for sz in sizes[:-1]:
        parts.append(w[:, start:start + sz])
        start += sz
    parts.append(w[:, start:])
    return parts


def _pack_w_in(w):
    d = w.shape[0]
    aq, ak, av, bq, bk, bv, cq, ck, cv, co, cg, gt = _split_w_in(w)
    pad = lambda n: jnp.zeros((d, n), w.dtype)
    kva = jnp.concatenate([ak, av, cg, pad(TILE_N - ak.shape[1] - av.shape[1] - cg.shape[1])], axis=1)
    big = jnp.concatenate([aq[:, _pair_perm()], bq, bk, kva, bv, co, gt, cq, cv], axis=1)
    return big.astype(BF16), ck.T.astype(BF16)


def kernel(x, c, ctx, c_ctx, w_mod, b_mod, g_mix, g_ffn, w_in, b_gate, sink, lam_q1, lam_k1, lam_q2, lam_k2,
           g_diff, g_mlstm, w_a, w_b, w_c, w_out, w_router, b_router, w_exp_gate, w_exp_up, w_exp_down,
           w_sh_gate, w_sh_up, w_sh_down, g_final):
    b, s_len, d = x.shape
    n_ctx = ctx.shape[1]
    l = n_ctx + s_len
    depth = w_mod.shape[0]
    n_ctx_tiles = n_ctx // TM
    assert n_ctx % TM == 0 and s_len % TM == 0 and d % LANES == 0 and s_len % GRID_W == 0

    xs = jnp.concatenate([ctx, x], axis=1)
    cos, sa, sb = _rope_tables(s_len, n_ctx)

    rows_c = 16
    cs = jnp.concatenate([c, c_ctx[None], jnp.zeros((rows_c - b - 1, d), F32)], axis=0)
    mod_all = _mod_vectors(cs, w_mod, b_mod).reshape(depth, rows_c, N_MOD, d)
    mod_all = jnp.pad(mod_all, ((0, 0), (0, 0), (0, 8 - N_MOD), (0, 0)))

    perm = _pair_perm()
    for layer in range(depth):
        lam_init = 0.8 - 0.6 * math.exp(-0.3 * layer)
        mods = jnp.stack([jnp.broadcast_to(mod_all[layer, b], (b, 8, d)), mod_all[layer, :b]], axis=1)
        w_big, w_kt = _pack_w_in(w_in[layer])
        p, gates, kt = _inproj(xs, mods, g_mix[layer][None], w_big, w_kt, cos, sa, sb, n_ctx_tiles)

        oa = _mixer_a(p, sink[layer], n_ctx)
        lam_params = jnp.stack([lam_q1[layer], lam_k1[layer], lam_q2[layer], lam_k2[layer]])
        ob = _mixer_b(p, lam_params, g_diff[layer][None], lam_init, n_ctx)

        bias = b_gate[layer].reshape(-1)
        bias_row = jnp.pad(bias, (0, LANES - bias.shape[0]))[None]
        bias_col = jnp.broadcast_to(bias[:, None], (bias.shape[0], LANES))
        gates_t = jnp.transpose(gates[:, :, :bias.shape[0]], (0, 2, 1))
        hm = _mlstm(p, kt, gates, gates_t, bias_row, bias_col, n_ctx)

        xs = _merge(xs, mods, oa, ob, hm, p, g_mlstm[layer][None],
                    w_a[layer][perm].astype(BF16), w_b[layer].astype(BF16), w_c[layer].astype(BF16),
                    w_out[layer].astype(BF16), n_ctx_tiles)

        h, idx_t, wt_t = _router(xs, mods, g_ffn[layer][None], w_router[layer].T, b_router[layer][:, None],
                                 n_ctx_tiles)
        rows, pos, block_e, n_used = _dispatch_plan(idx_t, EXPERT_ROWS)
        h_flat = h.reshape(b * l, d)
        ys = _experts(h_flat, rows, block_e, n_used, w_exp_gate[layer].astype(BF16),
                      w_exp_up[layer].astype(BF16), w_exp_down[layer].astype(BF16))
        wts = jnp.transpose(wt_t, (0, 2, 1)).reshape(b * l, TOP_K)
        xs = _combine(xs.reshape(b * l, d), h_flat, pos.reshape(-1, COMBINE_TM * TOP_K), ys, wts, mods,
                      w_sh_gate[layer].astype(BF16), w_sh_up[layer].astype(BF16), w_sh_down[layer].astype(BF16),
                      l // COMBINE_TM, n_ctx // COMBINE_TM).reshape(b, l, d)
    return _final_norm(xs, g_final[None], n_ctx_tiles)
```

```python
import functools
import math

import jax
import jax.numpy as jnp
from jax import lax
from jax.experimental import pallas as pl
from jax.experimental.pallas import tpu as pltpu

F32 = jnp.float32
BF16 = jnp.bfloat16
HIGHEST = lax.Precision.HIGHEST

GRID_W = 64
N_MOD = 6
HEAD_DIM = 64
N_HEADS_A = 8
KV_HEADS_A = 2
WINDOW = 128
N_HEADS_B = 4
N_HEADS_C = 4
HEAD_DIM_C = 128
N_EXPERTS = 64
N_GROUPS = 8
TOPK_GROUPS = 4
TOP_K = 8
ROUTED_SCALE = 2.5
ROPE_BASE = 10000.0
EPS = 1e-6

LANES = 128
CHUNK = 128
TILE_N = 512
TM = 256
MIXB_KEYS = 512
EXPERT_ROWS = 512
COMBINE_TM = 128
NEG = -1e30
VMEM_LIMIT = 56 * 1024 * 1024

T_AQ, T_BQ, T_BK, T_KVA, T_BV, T_CO, T_GT, T_CQ, T_CV, N_TILES = 0, 1, 2, 3, 4, 5, 6, 12, 13, 14

NT_DIMS = (((1,), (1,)), ((), ()))


def _params(sem):
    return pltpu.CompilerParams(dimension_semantics=sem, vmem_limit_bytes=VMEM_LIMIT)


def _rms(x, g):
    return x * lax.rsqrt(jnp.mean(x * x, axis=-1, keepdims=True) + EPS) * g


def _sigmoid(x):
    return jax.nn.sigmoid(x)


def _mod_kernel(c_ref, w_ref, b_ref, o_ref):
    c = c_ref[...]
    s = c * _sigmoid(c)
    o_ref[...] = jnp.dot(s, w_ref[...], precision=HIGHEST, preferred_element_type=F32) + b_ref[...]


def _mod_vectors(cs, w_mod, b_mod):
    depth, d, n = w_mod.shape
    r = cs.shape[0]
    tn = 3 * LANES
    return pl.pallas_call(
        _mod_kernel,
        grid=(depth, n // tn),
        in_specs=[pl.BlockSpec((r, d), lambda l, j: (0, 0)),
                  pl.BlockSpec((None, d, tn), lambda l, j: (l, 0, j)),
                  pl.BlockSpec((None, 1, tn), lambda l, j: (l, 0, j))],
        out_specs=pl.BlockSpec((None, r, tn), lambda l, j: (l, 0, j)),
        out_shape=jax.ShapeDtypeStruct((depth, r, n), F32),
        compiler_params=_params(("parallel", "parallel")),
        name="mod_vectors",
    )(cs, w_mod, b_mod.reshape(depth, 1, n))


def _inproj_kernel(x_ref, mod_ref, g_ref, w_ref, wkt_ref, cos_ref, sa_ref, sb_ref, p_ref, gate_ref, kt_ref):
    x = x_ref[...]
    h = _rms(x, g_ref[...]) * (1.0 + mod_ref[1:2, :]) + mod_ref[0:1, :]
    hb = h.astype(BF16)
    cos, sa, sb = cos_ref[...], sa_ref[...], sb_ref[...]

    def rope(t):
        return t * cos + pltpu.roll(t, 16, 1) * sa + pltpu.roll(t, LANES - 16, 1) * sb

    q_scale = HEAD_DIM ** -0.5
    for j in range(N_TILES):
        acc = jnp.dot(hb, w_ref[:, j * TILE_N:(j + 1) * TILE_N], preferred_element_type=F32)
        parts = [acc[:, s * LANES:(s + 1) * LANES] for s in range(TILE_N // LANES)]
        if j == T_AQ:
            parts = [rope(t) * q_scale for t in parts]
        elif j == T_BQ:
            parts = [rope(t) * (q_scale * math.log2(math.e)) for t in parts]
        elif j == T_BK:
            parts = [rope(t) for t in parts]
        elif j == T_KVA:
            gate_ref[...] = parts[2]
            parts[0] = rope(parts[0])
        for s, t in enumerate(parts):
            p_ref[:, j * TILE_N + s * LANES:j * TILE_N + (s + 1) * LANES] = t.astype(BF16)
    kt = lax.dot_general(wkt_ref[...], hb, NT_DIMS, preferred_element_type=F32)
    kt_ref[...] = (kt * (HEAD_DIM_C ** -0.5)).astype(BF16)


def _inproj(xs, mods, g_mix, w_big, w_kt, cos, sa, sb, n_ctx_tiles):
    b, l, d = xs.shape
    npad = w_big.shape[1]
    ck = w_kt.shape[0]
    grid = (b, l // TM)
    tok = lambda bi, ti: (bi, ti, 0)
    return pl.pallas_call(
        _inproj_kernel,
        grid=grid,
        in_specs=[pl.BlockSpec((None, TM, d), tok),
                  pl.BlockSpec((None, None, 8, d), lambda bi, ti: (bi, jnp.where(ti >= n_ctx_tiles, 1, 0), 0, 0)),
                  pl.BlockSpec((1, d), lambda bi, ti: (0, 0)),
                  pl.BlockSpec((d, npad), lambda bi, ti: (0, 0), pipeline_mode=pl.Buffered(1)),
                  pl.BlockSpec((ck, d), lambda bi, ti: (0, 0), pipeline_mode=pl.Buffered(1)),
                  pl.BlockSpec((TM, LANES), lambda bi, ti: (ti, 0)),
                  pl.BlockSpec((TM, LANES), lambda bi, ti: (ti, 0)),
                  pl.BlockSpec((TM, LANES), lambda bi, ti: (ti, 0))],
        out_specs=[pl.BlockSpec((None, TM, npad), tok),
                   pl.BlockSpec((None, TM, LANES), tok),
                   pl.BlockSpec((None, ck, TM), lambda bi, ti: (bi, 0, ti))],
        out_shape=[jax.ShapeDtypeStruct((b, l, npad), BF16),
                   jax.ShapeDtypeStruct((b, l, LANES), F32),
                   jax.ShapeDtypeStruct((b, ck, l), BF16)],
        compiler_params=_params(("parallel", "parallel")),
        name="inproj",
    )(xs, mods, g_mix, w_big, w_kt, cos, sa, sb)


def _mixa_kernel(sink_ref, q_ref, kp_ref, kc_ref, kn_ref, kx_ref, o_ref, *, n_ctx_blocks, n_blocks):
    i = pl.program_id(1)
    lat = i >= n_ctx_blocks
    has_prev = jnp.logical_and(lat, i > n_ctx_blocks)
    has_next = jnp.logical_and(lat, i < n_blocks - 1)
    r = lax.broadcasted_iota(jnp.int32, (CHUNK, CHUNK), 0)
    c = lax.broadcasted_iota(jnp.int32, (CHUNK, CHUNK), 1)
    n_ctx = kx_ref.shape[0]
    valid = jnp.concatenate([
        jnp.logical_and(c >= r, has_prev),
        jnp.broadcast_to(lat, (CHUNK, CHUNK)),
        jnp.logical_and(c <= r, has_next),
        jnp.ones((CHUNK, n_ctx), jnp.bool_)], axis=1)
    kcat = jnp.concatenate([kp_ref[:, :LANES], kc_ref[:, :LANES], kn_ref[:, :LANES], kx_ref[:, :LANES]], axis=0)
    vcat = jnp.concatenate([kp_ref[:, LANES:], kc_ref[:, LANES:], kn_ref[:, LANES:], kx_ref[:, LANES:]], axis=0)
    lane = lax.broadcasted_iota(jnp.int32, (CHUNK, LANES), 1)
    low = lane < HEAD_DIM
    n_pairs = N_HEADS_A // KV_HEADS_A
    outs = []
    for gk in range(KV_HEADS_A):
        keep = low if gk == 0 else jnp.logical_not(low)
        zero = jnp.zeros((CHUNK, LANES), BF16)
        lhs = jnp.concatenate([jnp.where(keep, q_ref[:, t * LANES:(t + 1) * LANES], zero) for t in range(n_pairs)],
                              axis=0)
        s = lax.dot_general(lhs, kcat, NT_DIMS, preferred_element_type=F32)
        o_g = []
        for t in range(n_pairs):
            st = jnp.where(valid, s[t * CHUNK:(t + 1) * CHUNK], NEG)
            sk = sink_ref[gk * n_pairs + t]
            m = jnp.maximum(jnp.max(st, axis=-1, keepdims=True), sk)
            p = jnp.exp(st - m)
            den = jnp.sum(p, axis=-1, keepdims=True) + jnp.exp(sk - m)
            o_g.append(jnp.dot(p.astype(BF16), vcat, preferred_element_type=F32) / den)
        outs.append(o_g)
    for t in range(n_pairs):
        o_ref[:, t * LANES:(t + 1) * LANES] = jnp.where(low, outs[0][t], outs[1][t]).astype(BF16)


def _mixer_a(p, sink, n_ctx):
    b, l, _ = p.shape
    nb = l // CHUNK
    ncb = n_ctx // CHUNK
    kvw = 2 * LANES
    kv_col = T_KVA * TILE_N // kvw
    aq_w = N_HEADS_A * HEAD_DIM
    kern = functools.partial(_mixa_kernel, n_ctx_blocks=ncb, n_blocks=nb)
    return pl.pallas_call(
        kern,
        grid=(b, nb),
        in_specs=[pl.BlockSpec(memory_space=pltpu.SMEM),
                  pl.BlockSpec((None, CHUNK, aq_w), lambda bi, i: (bi, i, T_AQ)),
                  pl.BlockSpec((None, CHUNK, kvw), lambda bi, i: (bi, jnp.maximum(i - 1, 0), kv_col)),
                  pl.BlockSpec((None, CHUNK, kvw), lambda bi, i: (bi, i, kv_col)),
                  pl.BlockSpec((None, CHUNK, kvw), lambda bi, i: (bi, jnp.minimum(i + 1, nb - 1), kv_col)),
                  pl.BlockSpec((None, n_ctx, kvw), lambda bi, i: (bi, 0, kv_col))],
        out_specs=pl.BlockSpec((None, CHUNK, aq_w), lambda bi, i: (bi, i, 0)),
        out_shape=jax.ShapeDtypeStruct((b, l, aq_w), BF16),
        compiler_params=_params(("parallel", "parallel")),
        name="mixer_a",
    )(sink, p, p, p, p, p)


def _fold_lanes(op, acc, s):
    for t in range(s.shape[1] // LANES):
        acc = op(acc, s[:, t * LANES:(t + 1) * LANES])
    return acc


def _mixb_kernel(lam_ref, gd_ref, q_ref, k_ref, v_ref, o_ref, s_scr, *, lam_init, n_ctx_tiles, n_ctx, kl):
    qi = pl.program_id(2)
    lp = lam_ref[...]
    lam = (jnp.exp(jnp.sum(lp[0:1] * lp[1:2], axis=-1, keepdims=True))
           - jnp.exp(jnp.sum(lp[2:3] * lp[3:4], axis=-1, keepdims=True)) + lam_init)
    q = q_ref[...]
    tq = q.shape[0]
    lane = lax.broadcasted_iota(jnp.int32, (tq, LANES), 1)
    zero = jnp.zeros_like(q)
    lhs = jnp.concatenate([jnp.where(lane < HEAD_DIM, q, zero), jnp.where(lane >= HEAD_DIM, q, zero)], axis=0)
    n_lat = jnp.where(qi >= n_ctx_tiles, (k_ref.shape[0] - n_ctx) // kl, 0)

    def chunk(ci):
        return pl.ds(pl.multiple_of(n_ctx + ci * kl, LANES), kl)

    s_c = lax.dot_general(lhs, k_ref[0:n_ctx, :], NT_DIMS, preferred_element_type=F32)
    mrun = _fold_lanes(jnp.maximum, jnp.full((2 * tq, LANES), NEG, F32), s_c)

    def scores(ci, mrun):
        s = lax.dot_general(lhs, k_ref[chunk(ci), :], NT_DIMS, preferred_element_type=F32)
        s_scr[ci] = s
        return _fold_lanes(jnp.maximum, mrun, s)

    mrun = lax.fori_loop(0, n_lat, scores, mrun)
    m = jnp.max(mrun, axis=-1, keepdims=True)
    p_c = jnp.exp2(s_c - m)
    lrun = _fold_lanes(jnp.add, jnp.zeros((2 * tq, LANES), F32), p_c)
    acc = jnp.dot(p_c.astype(BF16), v_ref[0:n_ctx, :], preferred_element_type=F32)

    def values(ci, carry):
        lrun, acc = carry
        pr = jnp.exp2(s_scr[ci] - m)
        acc = acc + jnp.dot(pr.astype(BF16), v_ref[chunk(ci), :], preferred_element_type=F32)
        return _fold_lanes(jnp.add, lrun, pr), acc

    lrun, acc = lax.fori_loop(0, n_lat, values, (lrun, acc))
    o2 = acc / jnp.sum(lrun, axis=-1, keepdims=True)
    o = o2[:tq] - lam * o2[tq:]
    o_ref[...] = (_rms(o, gd_ref[...]) * (1.0 - lam_init)).astype(BF16)


def _mixer_b(p, lam_params, g_diff, lam_init, n_ctx):
    b, l, _ = p.shape
    tq = TM
    kl = min(MIXB_KEYS, l - n_ctx)
    assert (l - n_ctx) % kl == 0
    kern = functools.partial(_mixb_kernel, lam_init=lam_init, n_ctx_tiles=n_ctx // tq, n_ctx=n_ctx, kl=kl)
    q0 = T_BQ * TILE_N // LANES
    k0 = T_BK * TILE_N // LANES
    v0 = T_BV * TILE_N // LANES
    return pl.pallas_call(
        kern,
        grid=(b, N_HEADS_B, l // tq),
        in_specs=[pl.BlockSpec((4, HEAD_DIM), lambda bi, h, qi: (0, 0)),
                  pl.BlockSpec((1, LANES), lambda bi, h, qi: (0, 0)),
                  pl.BlockSpec((None, tq, LANES), lambda bi, h, qi: (bi, qi, q0 + h)),
                  pl.BlockSpec((None, l, LANES), lambda bi, h, qi: (bi, 0, k0 + h)),
                  pl.BlockSpec((None, l, LANES), lambda bi, h, qi: (bi, 0, v0 + h))],
        out_specs=pl.BlockSpec((None, tq, LANES), lambda bi, h, qi: (bi, qi, h)),
        out_shape=jax.ShapeDtypeStruct((b, l, N_HEADS_B * LANES), BF16),
        scratch_shapes=[pltpu.VMEM(((l - n_ctx) // kl, 2 * tq, kl), F32)],
        compiler_params=_params(("parallel", "parallel", "parallel")),
        name="mixer_b",
    )(lam_params, g_diff, p, p, p)


def _log_sigmoid(x):
    return jnp.minimum(x, 0.0) - jnp.log1p(jnp.exp(-jnp.abs(x)))


def _mlstm_kernel(q_ref, kt_ref, v_ref, gc_ref, gr_ref, bc_ref, br_ref, o_ref, s_scr, m_scr):
    d = pl.program_id(1)
    c = pl.program_id(2)

    @pl.when(c == 0)
    def _():
        s_scr[...] = jnp.zeros_like(s_scr)
        m_scr[...] = jnp.zeros_like(m_scr)

    fwd = d == 0
    r = lax.broadcasted_iota(jnp.int32, (CHUNK, CHUNK), 0)
    cc = lax.broadcasted_iota(jnp.int32, (CHUNK, CHUNK), 1)
    tri = jnp.where(fwd, r - cc, cc - r) >= 0
    trif = tri.astype(F32)
    gcol = gc_ref[...] + bc_ref[...]
    grow = gr_ref[...] + br_ref[...]
    lf_col = _log_sigmoid(gcol)
    lf_row = _log_sigmoid(grow)
    bcum_col = jnp.dot(trif, lf_col, precision=HIGHEST, preferred_element_type=F32)
    bcum_row = lax.dot_general(lf_row, trif, NT_DIMS, precision=HIGHEST, preferred_element_type=F32)
    tot_row = jnp.sum(lf_row, axis=-1, keepdims=True)
    lane = lax.broadcasted_iota(jnp.int32, (CHUNK, LANES), 1)
    ones_col = (lane == 0).astype(BF16)
    nh = N_HEADS_C

    for h in range(nh):
        def pick_col(a, kind):
            return jnp.where(fwd, a[:, kind * nh + h:kind * nh + h + 1],
                             a[:, (kind + 2) * nh + h:(kind + 2) * nh + h + 1])

        def pick_row(a, kind):
            return jnp.where(fwd, a[kind * nh + h:kind * nh + h + 1, :],
                             a[(kind + 2) * nh + h:(kind + 2) * nh + h + 1, :])

        ic_row = pick_row(grow, 0)
        b_col = pick_col(bcum_col, 1)
        b_row = pick_row(bcum_row, 1)
        total = pick_row(tot_row, 1)
        m_st = m_scr[h, 0:1, 0:1]
        dm = jnp.where(tri, b_col - b_row + ic_row, NEG)
        inter = b_col + m_st
        m_t = jnp.maximum(inter, jnp.max(dm, axis=-1, keepdims=True))
        e = jnp.exp(dm - m_t)
        qh = q_ref[:, h * LANES:(h + 1) * LANES]
        kth = kt_ref[h * LANES:(h + 1) * LANES, :]
        vaug = jnp.concatenate([v_ref[:, h * LANES:(h + 1) * LANES], ones_col], axis=1)
        s = jnp.dot(qh, kth, preferred_element_type=F32) * e
        st = s_scr[h]
        intra = jnp.dot(s.astype(BF16), vaug, preferred_element_type=F32)
        cross = jnp.dot(qh, st.astype(BF16), preferred_element_type=F32)
        nd = intra + jnp.exp(inter - m_t) * cross
        den = nd[:, LANES:LANES + 1]
        o_ref[:, h * LANES:(h + 1) * LANES] = nd[:, :LANES] / jnp.maximum(jnp.abs(den), jnp.exp(-m_t))
        gs_row = total - b_row + ic_row
        m_new = jnp.maximum(total + m_st, jnp.max(gs_row, axis=-1, keepdims=True))
        decay = jnp.exp(total + m_st - m_new)
        wkt = (kth.astype(F32) * jnp.exp(gs_row - m_new)).astype(BF16)
        s_scr[h] = decay * st + jnp.dot(wkt, vaug, preferred_element_type=F32)
        m_scr[h] = jnp.broadcast_to(m_new, m_scr.shape[1:])


def _mlstm(p, kt, gates, gates_t, bias_row, bias_col, n_ctx):
    b, l, _ = p.shape
    nc = l // CHUNK
    ncc = n_ctx // CHUNK
    cw = N_HEADS_C * HEAD_DIM_C

    def chunk(d, c):
        rev = jnp.where(c < ncc, ncc - 1 - c, nc + ncc - 1 - c)
        return jnp.where(d == 0, c, rev)

    return pl.pallas_call(
        _mlstm_kernel,
        grid=(b, 2, nc),
        in_specs=[pl.BlockSpec((None, CHUNK, cw), lambda bi, d, c: (bi, chunk(d, c), T_CQ)),
                  pl.BlockSpec((None, cw, CHUNK), lambda bi, d, c: (bi, 0, chunk(d, c))),
                  pl.BlockSpec((None, CHUNK, cw), lambda bi, d, c: (bi, chunk(d, c), T_CV)),
                  pl.BlockSpec((None, CHUNK, LANES), lambda bi, d, c: (bi, chunk(d, c), 0)),
                  pl.BlockSpec((None, 16, CHUNK), lambda bi, d, c: (bi, 0, chunk(d, c))),
                  pl.BlockSpec((1, LANES), lambda bi, d, c: (0, 0)),
                  pl.BlockSpec((16, LANES), lambda bi, d, c: (0, 0))],
        out_specs=pl.BlockSpec((None, None, CHUNK, cw), lambda bi, d, c: (d, bi, chunk(d, c), 0)),
        out_shape=jax.ShapeDtypeStruct((2, b, l, cw), F32),
        scratch_shapes=[pltpu.VMEM((N_HEADS_C, HEAD_DIM_C, 2 * LANES), F32),
                        pltpu.VMEM((N_HEADS_C, 8, LANES), F32)],
        compiler_params=_params(("parallel", "parallel", "arbitrary")),
        name="mlstm",
    )(p, kt, p, gates, gates_t, bias_row, bias_col)


def _merge_kernel(x_ref, mod_ref, oa_ref, ob_ref, hf_ref, hb_ref, co_ref, gt_ref, gm_ref,
                  wa_ref, wb_ref, wc_ref, wo_ref, xo_ref):
    d = x_ref.shape[-1]
    hs = hf_ref[...] + hb_ref[...]
    co = co_ref[...].astype(F32)
    gm = gm_ref[...]
    oc = []
    for h in range(N_HEADS_C):
        sl = slice(h * LANES, (h + 1) * LANES)
        oc.append((_rms(hs[:, sl], gm[:, sl]) * _sigmoid(co[:, sl])).astype(BF16))
    oc = jnp.concatenate(oc, axis=1)
    y = (_sigmoid(gt_ref[:, 0:d].astype(F32)) * jnp.dot(oa_ref[...], wa_ref[...], preferred_element_type=F32)
         + _sigmoid(gt_ref[:, d:2 * d].astype(F32)) * jnp.dot(ob_ref[...], wb_ref[...], preferred_element_type=F32)
         + _sigmoid(gt_ref[:, 2 * d:3 * d].astype(F32)) * jnp.dot(oc, wc_ref[...], preferred_element_type=F32))
    out = jnp.dot(y.astype(BF16), wo_ref[...], preferred_element_type=F32)
    xo_ref[...] = x_ref[...] + mod_ref[2:3, :] * out


def _merge(xs, mods, oa, ob, hm, p, g_mlstm, wa, wb, wc, wo, n_ctx_tiles):
    b, l, d = xs.shape
    tok = lambda bi, ti: (bi, ti, 0)
    cw = N_HEADS_C * HEAD_DIM_C
    const = lambda bi, ti: (0, 0)
    return pl.pallas_call(
        _merge_kernel,
        grid=(b, l // TM),
        in_specs=[pl.BlockSpec((None, TM, d), tok),
                  pl.BlockSpec((None, None, 8, d), lambda bi, ti: (bi, jnp.where(ti >= n_ctx_tiles, 1, 0), 0, 0)),
                  pl.BlockSpec((None, TM, oa.shape[-1]), tok),
                  pl.BlockSpec((None, TM, ob.shape[-1]), tok),
                  pl.BlockSpec((None, None, TM, cw), lambda bi, ti: (0, bi, ti, 0)),
                  pl.BlockSpec((None, None, TM, cw), lambda bi, ti: (1, bi, ti, 0)),
                  pl.BlockSpec((None, TM, cw), lambda bi, ti: (bi, ti, T_CO)),
                  pl.BlockSpec((None, TM, 3 * d), lambda bi, ti: (bi, ti, T_GT * TILE_N // (3 * d))),
                  pl.BlockSpec((1, cw), const),
                  pl.BlockSpec(wa.shape, const), pl.BlockSpec(wb.shape, const),
                  pl.BlockSpec(wc.shape, const), pl.BlockSpec(wo.shape, const)],
        out_specs=pl.BlockSpec((None, TM, d), tok),
        out_shape=jax.ShapeDtypeStruct((b, l, d), F32),
        compiler_params=_params(("parallel", "parallel")),
        name="merge",
    )(xs, mods, oa, ob, hm, hm, p, p, g_mlstm, wa, wb, wc, wo)


def _router_kernel(x_ref, mod_ref, g_ref, wrt_ref, br_ref, h_ref, idx_ref, wt_ref, cnt_ref):
    h = _rms(x_ref[...], g_ref[...]) * (1.0 + mod_ref[4:5, :]) + mod_ref[3:4, :]
    h_ref[...] = h
    tm = h.shape[0]
    per = N_EXPERTS // N_GROUPS
    lt = lax.dot_general(wrt_ref[...], h, NT_DIMS, precision=HIGHEST, preferred_element_type=F32)
    s = _sigmoid(lt)
    sel = s + br_ref[...]
    ninf = -jnp.inf
    sel3 = sel.reshape(N_GROUPS, per, tm)
    eidx = lax.broadcasted_iota(jnp.int32, (N_GROUPS, per, tm), 1)
    m1 = jnp.max(sel3, axis=1, keepdims=True)
    first = jnp.min(jnp.where(sel3 == m1, eidx, per), axis=1, keepdims=True)
    m2 = jnp.max(jnp.where(eidx == first, ninf, sel3), axis=1, keepdims=True)
    gscore = (m1 + m2).reshape(N_GROUPS, tm)
    gidx = lax.broadcasted_iota(jnp.int32, (N_GROUPS, tm), 0)
    gmask = jnp.zeros((N_GROUPS, tm), jnp.bool_)
    cur = gscore
    for _ in range(TOPK_GROUPS):
        mx = jnp.max(cur, axis=0, keepdims=True)
        hit = gidx == jnp.min(jnp.where(cur == mx, gidx, N_GROUPS), axis=0, keepdims=True)
        gmask = jnp.logical_or(gmask, hit)
        cur = jnp.where(hit, ninf, cur)
    cur = jnp.where(gmask.reshape(N_GROUPS, 1, tm), sel3, ninf).reshape(N_EXPERTS, tm)
    eid = lax.broadcasted_iota(jnp.int32, (N_EXPERTS, tm), 0)
    ids, ws = [], []
    chosen = jnp.zeros((N_EXPERTS, tm), F32)
    for _ in range(TOP_K):
        mx = jnp.max(cur, axis=0, keepdims=True)
        pick = jnp.min(jnp.where(cur == mx, eid, N_EXPERTS), axis=0, keepdims=True)
        hit = eid == pick
        ids.append(pick)
        ws.append(jnp.sum(jnp.where(hit, s, 0.0), axis=0, keepdims=True))
        cur = jnp.where(hit, ninf, cur)
        chosen = chosen + hit.astype(F32)
    wsum = ws[0]
    for w in ws[1:]:
        wsum = wsum + w
    idx_ref[...] = jnp.concatenate(ids, axis=0)
    wt_ref[...] = jnp.concatenate([w / wsum * ROUTED_SCALE for w in ws], axis=0)
    cnt_ref[...] = jnp.sum(chosen, axis=1, keepdims=True).astype(jnp.int32)


def _router(xs, mods, g_ffn, w_router_t, b_router, n_ctx_tiles):
    b, l, d = xs.shape
    tok = lambda bi, ti: (bi, ti, 0)
    const = lambda bi, ti: (0, 0)
    return pl.pallas_call(
        _router_kernel,
        grid=(b, l // TM),
        in_specs=[pl.BlockSpec((None, TM, d), tok),
                  pl.BlockSpec((None, None, 8, d), lambda bi, ti: (bi, jnp.where(ti >= n_ctx_tiles, 1, 0), 0, 0)),
                  pl.BlockSpec((1, d), const),
                  pl.BlockSpec((N_EXPERTS, d), const),
                  pl.BlockSpec((N_EXPERTS, 1), const)],
        out_specs=[pl.BlockSpec((None, TM, d), tok),
                   pl.BlockSpec((None, TOP_K, TM), lambda bi, ti: (bi, 0, ti)),
                   pl.BlockSpec((None, TOP_K, TM), lambda bi, ti: (bi, 0, ti)),
                   pl.BlockSpec((None, None, N_EXPERTS, 1), lambda bi, ti: (bi, ti, 0, 0))],
        out_shape=[jax.ShapeDtypeStruct((b, l, d), F32),
                   jax.ShapeDtypeStruct((b, TOP_K, l), jnp.int32),
                   jax.ShapeDtypeStruct((b, TOP_K, l), F32),
                   jax.ShapeDtypeStruct((b, l // TM, N_EXPERTS, 1), jnp.int32)],
        compiler_params=_params(("parallel", "parallel")),
        name="router",
    )(xs, mods, g_ffn, w_router_t, b_router)


def _dispatch_kernel(fs_ref, fl_ref, idx_ref, base_ref, h_ref, pos_ref, xs_hbm,
                     pos_vmem, pos_smem, zeros, sem_pos, sem_rows, sem_fill):
    i = pl.program_id(0)
    tm = h_ref.shape[0]
    idx = idx_ref[...]
    eid = lax.broadcasted_iota(jnp.int32, (N_EXPERTS, tm), 0)
    hits = [eid == idx[k:k + 1, :] for k in range(TOP_K)]
    chosen = hits[0].astype(BF16)
    for hk in hits[1:]:
        chosen = chosen + hk.astype(BF16)
    r = lax.broadcasted_iota(jnp.int32, (tm, tm), 0)
    c = lax.broadcasted_iota(jnp.int32, (tm, tm), 1)
    before = (r < c).astype(BF16)
    rank = jnp.dot(chosen, before, preferred_element_type=F32)
    row_of = rank.astype(jnp.int32) + base_ref[...]
    pos = jnp.concatenate([jnp.sum(jnp.where(hk, row_of, 0), axis=0, keepdims=True) for hk in hits], axis=0)
    pos_ref[...] = pos
    pos_vmem[...] = pos
    cp = pltpu.make_async_copy(pos_vmem, pos_smem, sem_pos)
    cp.start()
    cp.wait()

    def issue(j, carry):
        for k in range(TOP_K):
            pltpu.make_async_copy(h_ref.at[pl.ds(j, 1)], xs_hbm.at[pl.ds(pos_smem[k, j], 1)], sem_rows).start()
        return carry

    lax.fori_loop(0, tm, issue, 0, unroll=2)

    sublanes = 8
    fill_sizes = [1 << s for s in range(zeros.shape[0].bit_length())]

    def fill(wait):
        def per_expert(e, carry):
            start, left = fs_ref[e], fl_ref[e]
            for sz in fill_sizes:
                take = (left & sz) != 0

                @pl.when(take)
                def _():
                    if sz < sublanes:
                        cps = [pltpu.make_async_copy(zeros.at[pl.ds(0, 1)], xs_hbm.at[pl.ds(start + r, 1)], sem_fill)
                               for r in range(sz)]
                    else:
                        cps = [pltpu.make_async_copy(zeros.at[pl.ds(0, sz)],
                                                     xs_hbm.at[pl.ds(pl.multiple_of(start, sublanes), sz)], sem_fill)]
                    for cpz in cps:
                        cpz.wait() if wait else cpz.start()

                start = start + jnp.where(take, sz, 0)
            return carry

        lax.fori_loop(0, N_EXPERTS, per_expert, 0)

        def tail(t, carry):
            row = pl.multiple_of(fs_ref[N_EXPERTS] + t * zeros.shape[0], sublanes)
            cpz = pltpu.make_async_copy(zeros, xs_hbm.at[pl.ds(row, zeros.shape[0])], sem_fill)
            cpz.wait() if wait else cpz.start()
            return carry

        lax.fori_loop(0, fl_ref[N_EXPERTS], tail, 0)

    @pl.when(i == 0)
    def _():
        zeros[...] = jnp.zeros_like(zeros)
        fill(False)
        fill(True)

    for k in range(TOP_K):
        pltpu.make_async_copy(h_ref, xs_hbm.at[pl.ds(0, tm)], sem_rows).wait()


def _dispatch(idx_t, base, h_flat, fill_start, fill_len, n_rows):
    b, k, l = idx_t.shape
    n, d = h_flat.shape
    nt = l // TM
    grid_spec = pltpu.PrefetchScalarGridSpec(
        num_scalar_prefetch=2,
        grid=(b * nt,),
        in_specs=[pl.BlockSpec((None, k, TM), lambda i, fs, fl: (i // nt, 0, i % nt)),
                  pl.BlockSpec((None, N_EXPERTS, 1), lambda i, fs, fl: (i, 0, 0)),
                  pl.BlockSpec((TM, d), lambda i, fs, fl: (i, 0))],
        out_specs=[pl.BlockSpec((None, k, TM), lambda i, fs, fl: (i // nt, 0, i % nt)),
                   pl.BlockSpec(memory_space=pl.ANY)],
        scratch_shapes=[pltpu.VMEM((k, TM), jnp.int32),
                        pltpu.SMEM((k, TM), jnp.int32),
                        pltpu.VMEM((EXPERT_ROWS // 2, d), F32),
                        pltpu.SemaphoreType.DMA,
                        pltpu.SemaphoreType.DMA,
                        pltpu.SemaphoreType.DMA])
    return pl.pallas_call(
        _dispatch_kernel,
        grid_spec=grid_spec,
        out_shape=[jax.ShapeDtypeStruct((b, k, l), jnp.int32),
                   jax.ShapeDtypeStruct((n_rows, d), F32)],
        compiler_params=_params(("arbitrary",)),
        name="dispatch",
    )(fill_start, fill_len, idx_t, base, h_flat)


def _expert_kernel(be_ref, nu_ref, x_ref, wg_ref, wu_ref, wd_ref, y_ref):
    i = pl.program_id(0)

    @pl.when(i < nu_ref[0])
    def _():
        x = x_ref[...].astype(BF16)
        g = jnp.dot(x, wg_ref[...], preferred_element_type=F32)
        u = jnp.dot(x, wu_ref[...], preferred_element_type=F32)
        a = (g * _sigmoid(g) * u).astype(BF16)
        y_ref[...] = jnp.dot(a, wd_ref[...], preferred_element_type=F32)

    @pl.when(i >= nu_ref[0])
    def _():
        y_ref[...] = jnp.zeros_like(y_ref)


def _experts(xs, block_e, n_used, wg, wu, wd):
    n_rows, d = xs.shape
    blk = EXPERT_ROWS
    de = wg.shape[-1]
    grid_spec = pltpu.PrefetchScalarGridSpec(
        num_scalar_prefetch=2,
        grid=(n_rows // blk,),
        in_specs=[pl.BlockSpec((blk, d), lambda i, be, nu: (jnp.minimum(i, nu[0] - 1), 0)),
                  pl.BlockSpec((None, d, de), lambda i, be, nu: (be[i], 0, 0)),
                  pl.BlockSpec((None, d, de), lambda i, be, nu: (be[i], 0, 0)),
                  pl.BlockSpec((None, de, d), lambda i, be, nu: (be[i], 0, 0))],
        out_specs=pl.BlockSpec((blk, d), lambda i, be, nu: (i, 0)))
    return pl.pallas_call(
        _expert_kernel,
        grid_spec=grid_spec,
        out_shape=jax.ShapeDtypeStruct((n_rows, d), F32),
        compiler_params=_params(("arbitrary",)),
        name="experts",
    )(block_e, n_used, xs, wg, wu, wd)


def _combine_kernel(pos_hbm, ys_hbm, x_ref, h_ref, w_ref, mod_ref, wsg_ref, wsu_ref, wsd_ref, o_ref,
                    pos_smem, buf, sem_idx, sem_rows, *, tiles_per_sample):
    i = pl.program_id(0)
    tm = x_ref.shape[0]
    t0 = pl.multiple_of((i % tiles_per_sample) * tm, tm)
    cp = pltpu.make_async_copy(pos_hbm.at[i // tiles_per_sample, :, pl.ds(t0, tm)], pos_smem, sem_idx)
    cp.start()
    cp.wait()

    def issue(j, carry):
        for k in range(TOP_K):
            pltpu.make_async_copy(ys_hbm.at[pl.ds(pos_smem[k, j], 1)], buf.at[k, pl.ds(j, 1)], sem_rows).start()
        return carry

    lax.fori_loop(0, tm, issue, 0, unroll=2)
    hb = h_ref[...].astype(BF16)
    g = jnp.dot(hb, wsg_ref[...], preferred_element_type=F32)
    u = jnp.dot(hb, wsu_ref[...], preferred_element_type=F32)
    acc = jnp.dot((g * _sigmoid(g) * u).astype(BF16), wsd_ref[...], preferred_element_type=F32)
    for k in range(TOP_K):
        pltpu.make_async_copy(ys_hbm.at[pl.ds(0, tm)], buf.at[k], sem_rows).wait()
    w = w_ref[...]
    for k in range(TOP_K):
        acc = acc + w[:, k:k + 1] * buf[k]
    o_ref[...] = x_ref[...] + mod_ref[5:6, :] * acc


def _combine(x_flat, h_flat, pos, ys, wts, mods, wsg, wsu, wsd, tiles_per_sample, n_ctx_tiles):
    n, d = x_flat.shape
    tm = COMBINE_TM
    ds_ = wsg.shape[-1]
    tok = lambda i: (i, 0)
    const = lambda i: (0, 0)

    def mod_idx(i):
        return (i // tiles_per_sample, jnp.where(i % tiles_per_sample >= n_ctx_tiles, 1, 0), 0, 0)

    return pl.pallas_call(
        functools.partial(_combine_kernel, tiles_per_sample=tiles_per_sample),
        grid=(n // tm,),
        in_specs=[pl.BlockSpec(memory_space=pl.ANY),
                  pl.BlockSpec(memory_space=pl.ANY),
                  pl.BlockSpec((tm, d), tok),
                  pl.BlockSpec((tm, d), tok),
                  pl.BlockSpec((tm, TOP_K), tok),
                  pl.BlockSpec((None, None, 8, d), mod_idx),
                  pl.BlockSpec((d, ds_), const), pl.BlockSpec((d, ds_), const), pl.BlockSpec((ds_, d), const)],
        out_specs=pl.BlockSpec((tm, d), tok),
        out_shape=jax.ShapeDtypeStruct((n, d), F32),
        scratch_shapes=[pltpu.SMEM((TOP_K, tm), jnp.int32),
                        pltpu.VMEM((TOP_K, tm, d), F32),
                        pltpu.SemaphoreType.DMA,
                        pltpu.SemaphoreType.DMA],
        compiler_params=_params(("arbitrary",)),
        name="combine",
    )(pos, ys, x_flat, h_flat, wts, mods, wsg, wsu, wsd)


def _final_kernel(x_ref, g_ref, o_ref):
    o_ref[...] = _rms(x_ref[...], g_ref[...])


def _final_norm(xs, g_final, n_ctx_tiles):
    b, l, d = xs.shape
    s_len = l - n_ctx_tiles * TM
    return pl.pallas_call(
        _final_kernel,
        grid=(b, s_len // TM),
        in_specs=[pl.BlockSpec((None, TM, d), lambda bi, ti: (bi, ti + n_ctx_tiles, 0)),
                  pl.BlockSpec((1, d), lambda bi, ti: (0, 0))],
        out_specs=pl.BlockSpec((None, TM, d), lambda bi, ti: (bi, ti, 0)),
        out_shape=jax.ShapeDtypeStruct((b, s_len, d), F32),
        compiler_params=_params(("parallel", "parallel")),
        name="final_norm",
    )(xs, g_final)


def _dispatch_plan(cnt, n_assign, blk):
    tot = jnp.sum(cnt, axis=0)
    padded = (tot + blk - 1) // blk * blk
    pad_end = jnp.cumsum(padded)
    pad_start = pad_end - padded
    base = pad_start[None, :] + jnp.cumsum(cnt, axis=0) - cnt
    n_blocks = -(-n_assign // blk) + N_EXPERTS
    first_row = jnp.arange(n_blocks, dtype=jnp.int32) * blk
    block_e = jnp.minimum(jnp.sum((pad_end[None, :] <= first_row[:, None]).astype(jnp.int32), axis=1), N_EXPERTS - 1)
    n_used = (pad_end[-1] // blk).astype(jnp.int32).reshape(1)
    slab = blk // 2
    fill_start = jnp.concatenate([pad_start + tot, pad_end[-1:]]).astype(jnp.int32)
    fill_len = jnp.concatenate([padded - tot, (n_blocks * blk - pad_end[-1:]) // slab]).astype(jnp.int32)
    return base.astype(jnp.int32)[:, :, None], block_e, n_used, fill_start, fill_len, n_blocks * blk


def _rope_tables(s_len, n_ctx):
    rows = s_len // GRID_W
    row = jnp.repeat(jnp.arange(rows), GRID_W).astype(F32)
    col = jnp.tile(jnp.arange(GRID_W), rows).astype(F32)
    quarter = HEAD_DIM // 4
    inv = 1.0 / (ROPE_BASE ** (jnp.arange(quarter, dtype=F32) / quarter))
    ar, ac = row[:, None] * inv, col[:, None] * inv
    cr, sr, cc, sc = jnp.cos(ar), jnp.sin(ar), jnp.cos(ac), jnp.sin(ac)
    z = jnp.zeros_like(sr)
    cos = jnp.concatenate([cr, cr, cc, cc], axis=1)
    sa = jnp.concatenate([z, sr, z, sc], axis=1)
    sb = jnp.concatenate([-sr, z, -sc, z], axis=1)
    rep = LANES // HEAD_DIM

    def full(t, fill):
        t = jnp.tile(t, (1, rep))
        return jnp.concatenate([jnp.full((n_ctx, LANES), fill, F32), t], axis=0)

    return full(cos, 1.0), full(sa, 0.0), full(sb, 0.0)


def _pair_perm():
    g = N_HEADS_A // KV_HEADS_A
    heads = [h for t in range(g) for h in (t, t + g)]
    return jnp.concatenate([jnp.arange(h * HEAD_DIM, (h + 1) * HEAD_DIM) for h in heads])


def _split_w_in(w):
    a_q, a_kv = N_HEADS_A * HEAD_DIM, KV_HEADS_A * HEAD_DIM
    b_w = N_HEADS_B * 2 * HEAD_DIM
    c_w = N_HEADS_C * HEAD_DIM_C
    sizes = (a_q, a_kv, a_kv, b_w, b_w, b_w, c_w, c_w, c_w, c_w, 4 * N_HEADS_C, w.shape[1])
    parts, start = [], 0
    for sz in sizes[:-1]:
        parts.append(w[:, start:start + sz])
        start += sz
    parts.append(w[:, start:])
    return parts


def _pack_w_in(w):
    d = w.shape[0]
    aq, ak, av, bq, bk, bv, cq, ck, cv, co, cg, gt = _split_w_in(w)
    pad = lambda n: jnp.zeros((d, n), w.dtype)
    kva = jnp.concatenate([ak, av, cg, pad(TILE_N - ak.shape[1] - av.shape[1] - cg.shape[1])], axis=1)
    big = jnp.concatenate([aq[:, _pair_perm()], bq, bk, kva, bv, co, gt, cq, cv], axis=1)
    return big.astype(BF16), ck.T.astype(BF16)


def kernel(x, c, ctx, c_ctx, w_mod, b_mod, g_mix, g_ffn, w_in, b_gate, sink, lam_q1, lam_k1, lam_q2, lam_k2,
           g_diff, g_mlstm, w_a, w_b, w_c, w_out, w_router, b_router, w_exp_gate, w_exp_up, w_exp_down,
           w_sh_gate, w_sh_up, w_sh_down, g_final):
    b, s_len, d = x.shape
    n_ctx = ctx.shape[1]
    l = n_ctx + s_len
    depth = w_mod.shape[0]
    n_ctx_tiles = n_ctx // TM
    assert n_ctx % TM == 0 and s_len % TM == 0 and d % LANES == 0 and s_len % GRID_W == 0

    xs = jnp.concatenate([ctx, x], axis=1)
    cos, sa, sb = _rope_tables(s_len, n_ctx)

    rows_c = 16
    cs = jnp.concatenate([c, c_ctx[None], jnp.zeros((rows_c - b - 1, d), F32)], axis=0)
    mod_all = _mod_vectors(cs, w_mod, b_mod).reshape(depth, rows_c, N_MOD, d)
    mod_all = jnp.pad(mod_all, ((0, 0), (0, 0), (0, 8 - N_MOD), (0, 0)))

    perm = _pair_perm()
    for layer in range(depth):
        lam_init = 0.8 - 0.6 * math.exp(-0.3 * layer)
        mods = jnp.stack([jnp.broadcast_to(mod_all[layer, b], (b, 8, d)), mod_all[layer, :b]], axis=1)
        w_big, w_kt = _pack_w_in(w_in[layer])
        p, gates, kt = _inproj(xs, mods, g_mix[layer][None], w_big, w_kt, cos, sa, sb, n_ctx_tiles)

        oa = _mixer_a(p, sink[layer], n_ctx)
        lam_params = jnp.stack([lam_q1[layer], lam_k1[layer], lam_q2[layer], lam_k2[layer]])
        ob = _mixer_b(p, lam_params, g_diff[layer][None], lam_init, n_ctx)

        bias = b_gate[layer].reshape(-1)
        bias_row = jnp.pad(bias, (0, LANES - bias.shape[0]))[None]
        bias_col = jnp.broadcast_to(bias[:, None], (bias.shape[0], LANES))
        gates_t = jnp.transpose(gates[:, :, :bias.shape[0]], (0, 2, 1))
        hm = _mlstm(p, kt, gates, gates_t, bias_row, bias_col, n_ctx)

        xs = _merge(xs, mods, oa, ob, hm, p, g_mlstm[layer][None],
                    w_a[layer][perm].astype(BF16), w_b[layer].astype(BF16), w_c[layer].astype(BF16),
                    w_out[layer].astype(BF16), n_ctx_tiles)

        h, idx_t, wt_t, cnt = _router(xs, mods, g_ffn[layer][None], w_router[layer].T, b_router[layer][:, None],
                                      n_ctx_tiles)
        base, block_e, n_used, fill_start, fill_len, n_rows = _dispatch_plan(
            cnt.reshape(-1, N_EXPERTS), b * l * TOP_K, EXPERT_ROWS)
        h_flat = h.reshape(b * l, d)
        pos, xrows = _dispatch(idx_t, base, h_flat, fill_start, fill_len, n_rows)
        ys = _experts(xrows, block_e, n_used, w_exp_gate[layer].astype(BF16),
                      w_exp_up[layer].astype(BF16), w_exp_down[layer].astype(BF16))
        wts = jnp.transpose(wt_t, (0, 2, 1)).reshape(b * l, TOP_K)
        xs = _combine(xs.reshape(b * l, d), h_flat, pos, ys, wts, mods,
                      w_sh_gate[layer].astype(BF16), w_sh_up[layer].astype(BF16), w_sh_down[layer].astype(BF16),
                      l // COMBINE_TM, n_ctx // COMBINE_TM).reshape(b, l, d)
    return _final_norm(xs, g_final[None], n_ctx_tiles)
```

```python
import functools
import math

import jax
import jax.numpy as jnp
from jax import lax
from jax.experimental import pallas as pl
from jax.experimental.pallas import tpu as pltpu

F32 = jnp.float32
BF16 = jnp.bfloat16
HIGHEST = lax.Precision.HIGHEST

GRID_W = 64
N_MOD = 6
HEAD_DIM = 64
N_HEADS_A = 8
KV_HEADS_A = 2
WINDOW = 128
N_HEADS_B = 4
N_HEADS_C = 4
HEAD_DIM_C = 128
N_EXPERTS = 64
N_GROUPS = 8
TOPK_GROUPS = 4
TOP_K = 8
ROUTED_SCALE = 2.5
ROPE_BASE = 10000.0
EPS = 1e-6

LANES = 128
CHUNK = 128
TILE_N = 512
TM = 256
MIXB_KEYS = 1024
EXPERT_ROWS = 512
COMBINE_TM = 128
NEG = -1e30
VMEM_LIMIT = 56 * 1024 * 1024

T_AQ, T_BQ, T_BK, T_KVA, T_BV, T_CO, T_GT, T_CQ, T_CV, N_TILES = 0, 1, 2, 3, 4, 5, 6, 12, 13, 14

NT_DIMS = (((1,), (1,)), ((), ()))


def _params(sem):
    return pltpu.CompilerParams(dimension_semantics=sem, vmem_limit_bytes=VMEM_LIMIT)


def _rms(x, g):
    return x * lax.rsqrt(jnp.mean(x * x, axis=-1, keepdims=True) + EPS) * g


def _sigmoid(x):
    return jax.nn.sigmoid(x)


def _mod_kernel(c_ref, w_ref, b_ref, o_ref):
    c = c_ref[...]
    s = c * _sigmoid(c)
    o_ref[...] = jnp.dot(s, w_ref[...], precision=HIGHEST, preferred_element_type=F32) + b_ref[...]


def _mod_vectors(cs, w_mod, b_mod):
    depth, d, n = w_mod.shape
    r = cs.shape[0]
    tn = 3 * LANES
    return pl.pallas_call(
        _mod_kernel,
        grid=(depth, n // tn),
        in_specs=[pl.BlockSpec((r, d), lambda l, j: (0, 0)),
                  pl.BlockSpec((None, d, tn), lambda l, j: (l, 0, j)),
                  pl.BlockSpec((None, 1, tn), lambda l, j: (l, 0, j))],
        out_specs=pl.BlockSpec((None, r, tn), lambda l, j: (l, 0, j)),
        out_shape=jax.ShapeDtypeStruct((depth, r, n), F32),
        compiler_params=_params(("parallel", "parallel")),
        name="mod_vectors",
    )(cs, w_mod, b_mod.reshape(depth, 1, n))


def _inproj_kernel(x_ref, mod_ref, g_ref, w_ref, wkt_ref, cos_ref, sa_ref, sb_ref, p_ref, gate_ref, kt_ref):
    x = x_ref[...]
    h = _rms(x, g_ref[...]) * (1.0 + mod_ref[1:2, :]) + mod_ref[0:1, :]
    hb = h.astype(BF16)
    cos, sa, sb = cos_ref[...], sa_ref[...], sb_ref[...]

    def rope(t):
        return t * cos + pltpu.roll(t, 16, 1) * sa + pltpu.roll(t, LANES - 16, 1) * sb

    q_scale = HEAD_DIM ** -0.5
    for j in range(N_TILES):
        acc = jnp.dot(hb, w_ref[:, j * TILE_N:(j + 1) * TILE_N], preferred_element_type=F32)
        parts = [acc[:, s * LANES:(s + 1) * LANES] for s in range(TILE_N // LANES)]
        if j == T_AQ:
            parts = [rope(t) * q_scale for t in parts]
        elif j == T_BQ:
            parts = [rope(t) * (q_scale * math.log2(math.e)) for t in parts]
        elif j == T_BK:
            parts = [rope(t) for t in parts]
        elif j == T_KVA:
            gate_ref[...] = parts[2]
            parts[0] = rope(parts[0])
        for s, t in enumerate(parts):
            p_ref[:, j * TILE_N + s * LANES:j * TILE_N + (s + 1) * LANES] = t.astype(BF16)
    kt = lax.dot_general(wkt_ref[...], hb, NT_DIMS, preferred_element_type=F32)
    kt_ref[...] = (kt * (HEAD_DIM_C ** -0.5)).astype(BF16)


def _inproj(xs, mods, g_mix, w_big, w_kt, cos, sa, sb, n_ctx_tiles):
    b, l, d = xs.shape
    npad = w_big.shape[1]
    ck = w_kt.shape[0]
    grid = (b, l // TM)
    tok = lambda bi, ti: (bi, ti, 0)
    return pl.pallas_call(
        _inproj_kernel,
        grid=grid,
        in_specs=[pl.BlockSpec((None, TM, d), tok),
                  pl.BlockSpec((None, None, 8, d), lambda bi, ti: (bi, jnp.where(ti >= n_ctx_tiles, 1, 0), 0, 0)),
                  pl.BlockSpec((1, d), lambda bi, ti: (0, 0)),
                  pl.BlockSpec((d, npad), lambda bi, ti: (0, 0), pipeline_mode=pl.Buffered(1)),
                  pl.BlockSpec((ck, d), lambda bi, ti: (0, 0), pipeline_mode=pl.Buffered(1)),
                  pl.BlockSpec((TM, LANES), lambda bi, ti: (ti, 0)),
                  pl.BlockSpec((TM, LANES), lambda bi, ti: (ti, 0)),
                  pl.BlockSpec((TM, LANES), lambda bi, ti: (ti, 0))],
        out_specs=[pl.BlockSpec((None, TM, npad), tok),
                   pl.BlockSpec((None, TM, LANES), tok),
                   pl.BlockSpec((None, ck, TM), lambda bi, ti: (bi, 0, ti))],
        out_shape=[jax.ShapeDtypeStruct((b, l, npad), BF16),
                   jax.ShapeDtypeStruct((b, l, LANES), F32),
                   jax.ShapeDtypeStruct((b, ck, l), BF16)],
        compiler_params=_params(("parallel", "parallel")),
        name="inproj",
    )(xs, mods, g_mix, w_big, w_kt, cos, sa, sb)


def _mixa_kernel(sink_ref, q_ref, kp_ref, kc_ref, kn_ref, kx_ref, o_ref, *, n_ctx_blocks, n_blocks):
    i = pl.program_id(1)
    lat = i >= n_ctx_blocks
    has_prev = jnp.logical_and(lat, i > n_ctx_blocks)
    has_next = jnp.logical_and(lat, i < n_blocks - 1)
    r = lax.broadcasted_iota(jnp.int32, (CHUNK, CHUNK), 0)
    c = lax.broadcasted_iota(jnp.int32, (CHUNK, CHUNK), 1)
    n_ctx = kx_ref.shape[0]
    valid = jnp.concatenate([
        jnp.logical_and(c >= r, has_prev),
        jnp.broadcast_to(lat, (CHUNK, CHUNK)),
        jnp.logical_and(c <= r, has_next),
        jnp.ones((CHUNK, n_ctx), jnp.bool_)], axis=1)
    kcat = jnp.concatenate([kp_ref[:, :LANES], kc_ref[:, :LANES], kn_ref[:, :LANES], kx_ref[:, :LANES]], axis=0)
    vcat = jnp.concatenate([kp_ref[:, LANES:], kc_ref[:, LANES:], kn_ref[:, LANES:], kx_ref[:, LANES:]], axis=0)
    lane = lax.broadcasted_iota(jnp.int32, (CHUNK, LANES), 1)
    low = lane < HEAD_DIM
    n_pairs = N_HEADS_A // KV_HEADS_A
    outs = []
    for gk in range(KV_HEADS_A):
        keep = low if gk == 0 else jnp.logical_not(low)
        zero = jnp.zeros((CHUNK, LANES), BF16)
        lhs = jnp.concatenate([jnp.where(keep, q_ref[:, t * LANES:(t + 1) * LANES], zero) for t in range(n_pairs)],
                              axis=0)
        s = lax.dot_general(lhs, kcat, NT_DIMS, preferred_element_type=F32)
        o_g = []
        for t in range(n_pairs):
            st = jnp.where(valid, s[t * CHUNK:(t + 1) * CHUNK], NEG)
            sk = sink_ref[gk * n_pairs + t]
            m = jnp.maximum(jnp.max(st, axis=-1, keepdims=True), sk)
            p = jnp.exp(st - m)
            den = jnp.sum(p, axis=-1, keepdims=True) + jnp.exp(sk - m)
            o_g.append(jnp.dot(p.astype(BF16), vcat, preferred_element_type=F32) / den)
        outs.append(o_g)
    for t in range(n_pairs):
        o_ref[:, t * LANES:(t + 1) * LANES] = jnp.where(low, outs[0][t], outs[1][t]).astype(BF16)


def _mixer_a(p, sink, n_ctx):
    b, l, _ = p.shape
    nb = l // CHUNK
    ncb = n_ctx // CHUNK
    kvw = 2 * LANES
    kv_col = T_KVA * TILE_N // kvw
    aq_w = N_HEADS_A * HEAD_DIM
    kern = functools.partial(_mixa_kernel, n_ctx_blocks=ncb, n_blocks=nb)
    return pl.pallas_call(
        kern,
        grid=(b, nb),
        in_specs=[pl.BlockSpec(memory_space=pltpu.SMEM),
                  pl.BlockSpec((None, CHUNK, aq_w), lambda bi, i: (bi, i, T_AQ)),
                  pl.BlockSpec((None, CHUNK, kvw), lambda bi, i: (bi, jnp.maximum(i - 1, 0), kv_col)),
                  pl.BlockSpec((None, CHUNK, kvw), lambda bi, i: (bi, i, kv_col)),
                  pl.BlockSpec((None, CHUNK, kvw), lambda bi, i: (bi, jnp.minimum(i + 1, nb - 1), kv_col)),
                  pl.BlockSpec((None, n_ctx, kvw), lambda bi, i: (bi, 0, kv_col))],
        out_specs=pl.BlockSpec((None, CHUNK, aq_w), lambda bi, i: (bi, i, 0)),
        out_shape=jax.ShapeDtypeStruct((b, l, aq_w), BF16),
        compiler_params=_params(("parallel", "parallel")),
        name="mixer_a",
    )(sink, p, p, p, p, p)


def _fold_lanes(op, acc, s):
    for t in range(s.shape[1] // LANES):
        acc = op(acc, s[:, t * LANES:(t + 1) * LANES])
    return acc


def _mixb_kernel(lam_ref, gd_ref, q_ref, k_ref, v_ref, *rest, lam_init, chunks):
    o_ref, s_scr, va_scr = rest[-3], rest[-2], rest[-1]

    @pl.when(pl.program_id(2) == 0)
    def _():
        n_keys = v_ref.shape[0]
        va_scr[:, :LANES] = v_ref[...]
        va_scr[:, LANES:] = (lax.broadcasted_iota(jnp.int32, (n_keys, LANES), 1) == 0).astype(BF16)

    lp = lam_ref[...]
    lam = (jnp.exp(jnp.sum(lp[0:1] * lp[1:2], axis=-1, keepdims=True))
           - jnp.exp(jnp.sum(lp[2:3] * lp[3:4], axis=-1, keepdims=True)) + lam_init)
    q = q_ref[...]
    tq = q.shape[0]
    lane = lax.broadcasted_iota(jnp.int32, (tq, LANES), 1)
    zero = jnp.zeros_like(q)
    qs = (jnp.where(lane < HEAD_DIM, q, zero), jnp.where(lane >= HEAD_DIM, q, zero))
    rows = [slice(mi * tq, (mi + 1) * tq) for mi in range(2)]
    mrun = [jnp.full((tq, LANES), NEG, F32) for _ in range(2)]
    for off, sz in chunks:
        for mi in range(2):
            s_scr[rows[mi], off:off + sz] = lax.dot_general(qs[mi], k_ref[off:off + sz, :], NT_DIMS,
                                                            preferred_element_type=F32)
            mrun[mi] = _fold_lanes(jnp.maximum, mrun[mi], s_scr[rows[mi], off:off + sz])
    m = [jnp.max(mr, axis=-1, keepdims=True) for mr in mrun]
    acc = [jnp.zeros((tq, 2 * LANES), F32) for _ in range(2)]
    for off, sz in chunks:
        for mi in range(2):
            pr = jnp.exp2(s_scr[rows[mi], off:off + sz] - m[mi])
            acc[mi] = acc[mi] + jnp.dot(pr.astype(BF16), va_scr[off:off + sz, :], preferred_element_type=F32)
    outs = [a[:, :LANES] / a[:, LANES:LANES + 1] for a in acc]
    o = outs[0] - lam * outs[1]
    o_ref[...] = (_rms(o, gd_ref[...]) * (1.0 - lam_init)).astype(BF16)


def _mixer_b(p, lam_params, g_diff, lam_init, n_ctx):
    b, l, _ = p.shape
    tq = TM
    kl = min(MIXB_KEYS, l - n_ctx)
    assert (l - n_ctx) % kl == 0 and n_ctx % tq == 0
    q0 = T_BQ * TILE_N // LANES
    k0 = T_BK * TILE_N // LANES
    v0 = T_BV * TILE_N // LANES
    n_ctx_tiles = n_ctx // tq
    ctx_chunks = ((0, n_ctx),)
    all_chunks = ctx_chunks + tuple((n_ctx + c * kl, kl) for c in range((l - n_ctx) // kl))

    def call(chunks, n_keys, q_tiles, q_first, prev):
        kern = functools.partial(_mixb_kernel, lam_init=lam_init, chunks=chunks)
        in_specs = [pl.BlockSpec((4, HEAD_DIM), lambda bi, h, qi: (0, 0)),
                    pl.BlockSpec((1, LANES), lambda bi, h, qi: (0, 0)),
                    pl.BlockSpec((None, tq, LANES), lambda bi, h, qi: (bi, qi + q_first, q0 + h)),
                    pl.BlockSpec((None, n_keys, LANES), lambda bi, h, qi: (bi, 0, k0 + h)),
                    pl.BlockSpec((None, n_keys, LANES), lambda bi, h, qi: (bi, 0, v0 + h))]
        args = [lam_params, g_diff, p, p, p]
        aliases = {}
        if prev is not None:
            in_specs.append(pl.BlockSpec(memory_space=pl.ANY))
            args.append(prev)
            aliases = {len(args) - 1: 0}
        return pl.pallas_call(
            kern,
            grid=(b, N_HEADS_B, q_tiles),
            in_specs=in_specs,
            out_specs=pl.BlockSpec((None, tq, LANES), lambda bi, h, qi: (bi, qi + q_first, h)),
            out_shape=jax.ShapeDtypeStruct((b, l, N_HEADS_B * LANES), BF16),
            scratch_shapes=[pltpu.VMEM((2 * tq, n_keys), F32), pltpu.VMEM((n_keys, 2 * LANES), BF16)],
            input_output_aliases=aliases,
            compiler_params=_params(("parallel", "parallel", "arbitrary")),
            name="mixer_b",
        )(*args)

    ob = call(all_chunks, l, (l - n_ctx) // tq, n_ctx_tiles, None)
    return call(ctx_chunks, n_ctx, n_ctx_tiles, 0, ob)


def _log_sigmoid(x):
    return jnp.minimum(x, 0.0) - jnp.log1p(jnp.exp(-jnp.abs(x)))


def _mlstm_kernel(q_ref, kt_ref, v_ref, gc_ref, gr_ref, bc_ref, br_ref, o_ref, s_scr, m_scr):
    d = pl.program_id(1)
    c = pl.program_id(2)

    @pl.when(c == 0)
    def _():
        s_scr[...] = jnp.zeros_like(s_scr)
        m_scr[...] = jnp.zeros_like(m_scr)

    fwd = d == 0
    r = lax.broadcasted_iota(jnp.int32, (CHUNK, CHUNK), 0)
    cc = lax.broadcasted_iota(jnp.int32, (CHUNK, CHUNK), 1)
    tri = jnp.where(fwd, r - cc, cc - r) >= 0
    trif = tri.astype(F32)
    gcol = gc_ref[...] + bc_ref[...]
    grow = gr_ref[...] + br_ref[...]
    lf_col = _log_sigmoid(gcol)
    lf_row = _log_sigmoid(grow)
    bcum_col = jnp.dot(trif, lf_col, precision=HIGHEST, preferred_element_type=F32)
    bcum_row = lax.dot_general(lf_row, trif, NT_DIMS, precision=HIGHEST, preferred_element_type=F32)
    tot_row = jnp.sum(lf_row, axis=-1, keepdims=True)
    lane = lax.broadcasted_iota(jnp.int32, (CHUNK, LANES), 1)
    ones_col = (lane == 0).astype(BF16)
    nh = N_HEADS_C

    for h in range(nh):
        def pick_col(a, kind):
            return jnp.where(fwd, a[:, kind * nh + h:kind * nh + h + 1],
                             a[:, (kind + 2) * nh + h:(kind + 2) * nh + h + 1])

        def pick_row(a, kind):
            return jnp.where(fwd, a[kind * nh + h:kind * nh + h + 1, :],
                             a[(kind + 2) * nh + h:(kind + 2) * nh + h + 1, :])

        ic_row = pick_row(grow, 0)
        b_col = pick_col(bcum_col, 1)
        b_row = pick_row(bcum_row, 1)
        total = pick_row(tot_row, 1)
        m_st = m_scr[h, 0:1, 0:1]
        dm = jnp.where(tri, b_col - b_row + ic_row, NEG)
        inter = b_col + m_st
        m_t = jnp.maximum(inter, jnp.max(dm, axis=-1, keepdims=True))
        e = jnp.exp(dm - m_t)
        qh = q_ref[:, h * LANES:(h + 1) * LANES]
        kth = kt_ref[h * LANES:(h + 1) * LANES, :]
        vaug = jnp.concatenate([v_ref[:, h * LANES:(h + 1) * LANES], ones_col], axis=1)
        s = jnp.dot(qh, kth, preferred_element_type=F32) * e
        st = s_scr[h]
        intra = jnp.dot(s.astype(BF16), vaug, preferred_element_type=F32)
        cross = jnp.dot(qh, st.astype(BF16), preferred_element_type=F32)
        nd = intra + jnp.exp(inter - m_t) * cross
        den = nd[:, LANES:LANES + 1]
        o_ref[:, h * LANES:(h + 1) * LANES] = nd[:, :LANES] / jnp.maximum(jnp.abs(den), jnp.exp(-m_t))
        gs_row = total - b_row + ic_row
        m_new = jnp.maximum(total + m_st, jnp.max(gs_row, axis=-1, keepdims=True))
        decay = jnp.exp(total + m_st - m_new)
        wkt = (kth.astype(F32) * jnp.exp(gs_row - m_new)).astype(BF16)
        s_scr[h] = decay * st + jnp.dot(wkt, vaug, preferred_element_type=F32)
        m_scr[h] = jnp.broadcast_to(m_new, m_scr.shape[1:])


def _mlstm(p, kt, gates, gates_t, bias_row, bias_col, n_ctx):
    b, l, _ = p.shape
    nc = l // CHUNK
    ncc = n_ctx // CHUNK
    cw = N_HEADS_C * HEAD_DIM_C

    def chunk(d, c):
        rev = jnp.where(c < ncc, ncc - 1 - c, nc + ncc - 1 - c)
        return jnp.where(d == 0, c, rev)

    return pl.pallas_call(
        _mlstm_kernel,
        grid=(b, 2, nc),
        in_specs=[pl.BlockSpec((None, CHUNK, cw), lambda bi, d, c: (bi, chunk(d, c), T_CQ)),
                  pl.BlockSpec((None, cw, CHUNK), lambda bi, d, c: (bi, 0, chunk(d, c))),
                  pl.BlockSpec((None, CHUNK, cw), lambda bi, d, c: (bi, chunk(d, c), T_CV)),
                  pl.BlockSpec((None, CHUNK, LANES), lambda bi, d, c: (bi, chunk(d, c), 0)),
                  pl.BlockSpec((None, 16, CHUNK), lambda bi, d, c: (bi, 0, chunk(d, c))),
                  pl.BlockSpec((1, LANES), lambda bi, d, c: (0, 0)),
                  pl.BlockSpec((16, LANES), lambda bi, d, c: (0, 0))],
        out_specs=pl.BlockSpec((None, None, CHUNK, cw), lambda bi, d, c: (d, bi, chunk(d, c), 0)),
        out_shape=jax.ShapeDtypeStruct((2, b, l, cw), F32),
        scratch_shapes=[pltpu.VMEM((N_HEADS_C, HEAD_DIM_C, 2 * LANES), F32),
                        pltpu.VMEM((N_HEADS_C, 8, LANES), F32)],
        compiler_params=_params(("parallel", "parallel", "arbitrary")),
        name="mlstm",
    )(p, kt, p, gates, gates_t, bias_row, bias_col)


def _merge_kernel(x_ref, mod_ref, oa_ref, ob_ref, hf_ref, hb_ref, co_ref, gt_ref, gm_ref,
                  wa_ref, wb_ref, wc_ref, wo_ref, xo_ref):
    d = x_ref.shape[-1]
    hs = hf_ref[...] + hb_ref[...]
    co = co_ref[...].astype(F32)
    gm = gm_ref[...]
    oc = []
    for h in range(N_HEADS_C):
        sl = slice(h * LANES, (h + 1) * LANES)
        oc.append((_rms(hs[:, sl], gm[:, sl]) * _sigmoid(co[:, sl])).astype(BF16))
    oc = jnp.concatenate(oc, axis=1)
    y = (_sigmoid(gt_ref[:, 0:d].astype(F32)) * jnp.dot(oa_ref[...], wa_ref[...], preferred_element_type=F32)
         + _sigmoid(gt_ref[:, d:2 * d].astype(F32)) * jnp.dot(ob_ref[...], wb_ref[...], preferred_element_type=F32)
         + _sigmoid(gt_ref[:, 2 * d:3 * d].astype(F32)) * jnp.dot(oc, wc_ref[...], preferred_element_type=F32))
    out = jnp.dot(y.astype(BF16), wo_ref[...], preferred_element_type=F32)
    xo_ref[...] = x_ref[...] + mod_ref[2:3, :] * out


def _merge(xs, mods, oa, ob, hm, p, g_mlstm, wa, wb, wc, wo, n_ctx_tiles):
    b, l, d = xs.shape
    tok = lambda bi, ti: (bi, ti, 0)
    cw = N_HEADS_C * HEAD_DIM_C
    const = lambda bi, ti: (0, 0)
    return pl.pallas_call(
        _merge_kernel,
        grid=(b, l // TM),
        in_specs=[pl.BlockSpec((None, TM, d), tok),
                  pl.BlockSpec((None, None, 8, d), lambda bi, ti: (bi, jnp.where(ti >= n_ctx_tiles, 1, 0), 0, 0)),
                  pl.BlockSpec((None, TM, oa.shape[-1]), tok),
                  pl.BlockSpec((None, TM, ob.shape[-1]), tok),
                  pl.BlockSpec((None, None, TM, cw), lambda bi, ti: (0, bi, ti, 0)),
                  pl.BlockSpec((None, None, TM, cw), lambda bi, ti: (1, bi, ti, 0)),
                  pl.BlockSpec((None, TM, cw), lambda bi, ti: (bi, ti, T_CO)),
                  pl.BlockSpec((None, TM, 3 * d), lambda bi, ti: (bi, ti, T_GT * TILE_N // (3 * d))),
                  pl.BlockSpec((1, cw), const),
                  pl.BlockSpec(wa.shape, const), pl.BlockSpec(wb.shape, const),
                  pl.BlockSpec(wc.shape, const), pl.BlockSpec(wo.shape, const)],
        out_specs=pl.BlockSpec((None, TM, d), tok),
        out_shape=jax.ShapeDtypeStruct((b, l, d), F32),
        compiler_params=_params(("parallel", "parallel")),
        name="merge",
    )(xs, mods, oa, ob, hm, hm, p, p, g_mlstm, wa, wb, wc, wo)


def _router_kernel(x_ref, mod_ref, g_ref, wrt_ref, br_ref, h_ref, idx_ref, wt_ref, cnt_ref):
    h = _rms(x_ref[...], g_ref[...]) * (1.0 + mod_ref[4:5, :]) + mod_ref[3:4, :]
    h_ref[...] = h
    tm = h.shape[0]
    per = N_EXPERTS // N_GROUPS
    lt = lax.dot_general(wrt_ref[...], h, NT_DIMS, precision=HIGHEST, preferred_element_type=F32)
    s = _sigmoid(lt)
    sel = s + br_ref[...]
    ninf = -jnp.inf
    sel3 = sel.reshape(N_GROUPS, per, tm)
    eidx = lax.broadcasted_iota(jnp.int32, (N_GROUPS, per, tm), 1)
    m1 = jnp.max(sel3, axis=1, keepdims=True)
    first = jnp.min(jnp.where(sel3 == m1, eidx, per), axis=1, keepdims=True)
    m2 = jnp.max(jnp.where(eidx == first, ninf, sel3), axis=1, keepdims=True)
    gscore = (m1 + m2).reshape(N_GROUPS, tm)
    gidx = lax.broadcasted_iota(jnp.int32, (N_GROUPS, tm), 0)
    gmask = jnp.zeros((N_GROUPS, tm), jnp.bool_)
    cur = gscore
    for _ in range(TOPK_GROUPS):
        mx = jnp.max(cur, axis=0, keepdims=True)
        hit = gidx == jnp.min(jnp.where(cur == mx, gidx, N_GROUPS), axis=0, keepdims=True)
        gmask = jnp.logical_or(gmask, hit)
        cur = jnp.where(hit, ninf, cur)
    cur = jnp.where(gmask.reshape(N_GROUPS, 1, tm), sel3, ninf).reshape(N_EXPERTS, tm)
    eid = lax.broadcasted_iota(jnp.int32, (N_EXPERTS, tm), 0)
    ids, ws = [], []
    chosen = jnp.zeros((N_EXPERTS, tm), F32)
    for _ in range(TOP_K):
        mx = jnp.max(cur, axis=0, keepdims=True)
        pick = jnp.min(jnp.where(cur == mx, eid, N_EXPERTS), axis=0, keepdims=True)
        hit = eid == pick
        ids.append(pick)
        ws.append(jnp.sum(jnp.where(hit, s, 0.0), axis=0, keepdims=True))
        cur = jnp.where(hit, ninf, cur)
        chosen = chosen + hit.astype(F32)
    wsum = ws[0]
    for w in ws[1:]:
        wsum = wsum + w
    idx_ref[...] = jnp.concatenate(ids, axis=0)
    wt_ref[...] = jnp.concatenate([w / wsum * ROUTED_SCALE for w in ws], axis=0)
    cnt_ref[...] = jnp.sum(chosen, axis=1, keepdims=True).astype(jnp.int32)


def _router(xs, mods, g_ffn, w_router_t, b_router, n_ctx_tiles):
    b, l, d = xs.shape
    tok = lambda bi, ti: (bi, ti, 0)
    const = lambda bi, ti: (0, 0)
    return pl.pallas_call(
        _router_kernel,
        grid=(b, l // TM),
        in_specs=[pl.BlockSpec((None, TM, d), tok),
                  pl.BlockSpec((None, None, 8, d), lambda bi, ti: (bi, jnp.where(ti >= n_ctx_tiles, 1, 0), 0, 0)),
                  pl.BlockSpec((1, d), const),
                  pl.BlockSpec((N_EXPERTS, d), const),
                  pl.BlockSpec((N_EXPERTS, 1), const)],
        out_specs=[pl.BlockSpec((None, TM, d), tok),
                   pl.BlockSpec((None, TOP_K, TM), lambda bi, ti: (bi, 0, ti)),
                   pl.BlockSpec((None, TOP_K, TM), lambda bi, ti: (bi, 0, ti)),
                   pl.BlockSpec((None, None, N_EXPERTS, 1), lambda bi, ti: (bi, ti, 0, 0))],
        out_shape=[jax.ShapeDtypeStruct((b, l, d), F32),
                   jax.ShapeDtypeStruct((b, TOP_K, l), jnp.int32),
                   jax.ShapeDtypeStruct((b, TOP_K, l), F32),
                   jax.ShapeDtypeStruct((b, l // TM, N_EXPERTS, 1), jnp.int32)],
        compiler_params=_params(("parallel", "parallel")),
        name="router",
    )(xs, mods, g_ffn, w_router_t, b_router)


def _dispatch_kernel(fs_ref, fl_ref, idx_ref, base_ref, h_ref, pos_ref, xs_hbm,
                     pos_vmem, pos_smem, zeros, sem_pos, sem_rows, sem_fill):
    i = pl.program_id(0)
    tm = h_ref.shape[0]
    idx = idx_ref[...]
    eid = lax.broadcasted_iota(jnp.int32, (N_EXPERTS, tm), 0)
    hits = [eid == idx[k:k + 1, :] for k in range(TOP_K)]
    chosen = hits[0].astype(BF16)
    for hk in hits[1:]:
        chosen = chosen + hk.astype(BF16)
    r = lax.broadcasted_iota(jnp.int32, (tm, tm), 0)
    c = lax.broadcasted_iota(jnp.int32, (tm, tm), 1)
    before = (r < c).astype(BF16)
    rank = jnp.dot(chosen, before, preferred_element_type=F32)
    row_of = rank.astype(jnp.int32) + base_ref[...]
    pos = jnp.concatenate([jnp.sum(jnp.where(hk, row_of, 0), axis=0, keepdims=True) for hk in hits], axis=0)
    pos_ref[...] = pos
    pos_vmem[...] = pos
    cp = pltpu.make_async_copy(pos_vmem, pos_smem, sem_pos)
    cp.start()
    cp.wait()

    def issue(j, carry):
        for k in range(TOP_K):
            pltpu.make_async_copy(h_ref.at[pl.ds(j, 1)], xs_hbm.at[pl.ds(pos_smem[k, j], 1)], sem_rows).start()
        return carry

    lax.fori_loop(0, tm, issue, 0, unroll=2)

    sublanes = 8
    fill_sizes = [1 << s for s in range(zeros.shape[0].bit_length())]

    def fill(wait):
        def per_expert(e, carry):
            start, left = fs_ref[e], fl_ref[e]
            for sz in fill_sizes:
                take = (left & sz) != 0

                @pl.when(take)
                def _():
                    if sz < sublanes:
                        cps = [pltpu.make_async_copy(zeros.at[pl.ds(0, 1)], xs_hbm.at[pl.ds(start + r, 1)], sem_fill)
                               for r in range(sz)]
                    else:
                        cps = [pltpu.make_async_copy(zeros.at[pl.ds(0, sz)],
                                                     xs_hbm.at[pl.ds(pl.multiple_of(start, sublanes), sz)], sem_fill)]
                    for cpz in cps:
                        cpz.wait() if wait else cpz.start()

                start = start + jnp.where(take, sz, 0)
            return carry

        lax.fori_loop(0, N_EXPERTS, per_expert, 0)

        def tail(t, carry):
            row = pl.multiple_of(fs_ref[N_EXPERTS] + t * zeros.shape[0], sublanes)
            cpz = pltpu.make_async_copy(zeros, xs_hbm.at[pl.ds(row, zeros.shape[0])], sem_fill)
            cpz.wait() if wait else cpz.start()
            return carry

        lax.fori_loop(0, fl_ref[N_EXPERTS], tail, 0)

    @pl.when(i == 0)
    def _():
        zeros[...] = jnp.zeros_like(zeros)
        fill(False)
        fill(True)

    for k in range(TOP_K):
        pltpu.make_async_copy(h_ref, xs_hbm.at[pl.ds(0, tm)], sem_rows).wait()


def _dispatch(idx_t, base, h_flat, fill_start, fill_len, n_rows):
    b, k, l = idx_t.shape
    n, d = h_flat.shape
    nt = l // TM
    grid_spec = pltpu.PrefetchScalarGridSpec(
        num_scalar_prefetch=2,
        grid=(b * nt,),
        in_specs=[pl.BlockSpec((None, k, TM), lambda i, fs, fl: (i // nt, 0, i % nt)),
                  pl.BlockSpec((None, N_EXPERTS, 1), lambda i, fs, fl: (i, 0, 0)),
                  pl.BlockSpec((TM, d), lambda i, fs, fl: (i, 0))],
        out_specs=[pl.BlockSpec((None, k, TM), lambda i, fs, fl: (i // nt, 0, i % nt)),
                   pl.BlockSpec(memory_space=pl.ANY)],
        scratch_shapes=[pltpu.VMEM((k, TM), jnp.int32),
                        pltpu.SMEM((k, TM), jnp.int32),
                        pltpu.VMEM((EXPERT_ROWS // 2, d), F32),
                        pltpu.SemaphoreType.DMA,
                        pltpu.SemaphoreType.DMA,
                        pltpu.SemaphoreType.DMA])
    return pl.pallas_call(
        _dispatch_kernel,
        grid_spec=grid_spec,
        out_shape=[jax.ShapeDtypeStruct((b, k, l), jnp.int32),
                   jax.ShapeDtypeStruct((n_rows, d), F32)],
        compiler_params=_params(("arbitrary",)),
        name="dispatch",
    )(fill_start, fill_len, idx_t, base, h_flat)


def _expert_kernel(be_ref, nu_ref, x_ref, wg_ref, wu_ref, wd_ref, y_ref):
    i = pl.program_id(0)

    @pl.when(i < nu_ref[0])
    def _():
        x = x_ref[...].astype(BF16)
        g = jnp.dot(x, wg_ref[...], preferred_element_type=F32)
        u = jnp.dot(x, wu_ref[...], preferred_element_type=F32)
        a = (g * _sigmoid(g) * u).astype(BF16)
        y_ref[...] = jnp.dot(a, wd_ref[...], preferred_element_type=F32)

    @pl.when(i >= nu_ref[0])
    def _():
        y_ref[...] = jnp.zeros_like(y_ref)


def _experts(xs, block_e, n_used, wg, wu, wd):
    n_rows, d = xs.shape
    blk = EXPERT_ROWS
    de = wg.shape[-1]
    grid_spec = pltpu.PrefetchScalarGridSpec(
        num_scalar_prefetch=2,
        grid=(n_rows // blk,),
        in_specs=[pl.BlockSpec((blk, d), lambda i, be, nu: (jnp.minimum(i, nu[0] - 1), 0)),
                  pl.BlockSpec((None, d, de), lambda i, be, nu: (be[i], 0, 0)),
                  pl.BlockSpec((None, d, de), lambda i, be, nu: (be[i], 0, 0)),
                  pl.BlockSpec((None, de, d), lambda i, be, nu: (be[i], 0, 0))],
        out_specs=pl.BlockSpec((blk, d), lambda i, be, nu: (i, 0)))
    return pl.pallas_call(
        _expert_kernel,
        grid_spec=grid_spec,
        out_shape=jax.ShapeDtypeStruct((n_rows, d), F32),
        compiler_params=_params(("arbitrary",)),
        name="experts",
    )(block_e, n_used, xs, wg, wu, wd)


def _combine_kernel(pos_hbm, ys_hbm, x_ref, h_ref, w_ref, mod_ref, wsg_ref, wsu_ref, wsd_ref, o_ref,
                    pos_smem, buf, sem_idx, sem_rows, *, tiles_per_sample):
    i = pl.program_id(0)
    tm = x_ref.shape[0]
    t0 = pl.multiple_of((i % tiles_per_sample) * tm, tm)
    cp = pltpu.make_async_copy(pos_hbm.at[i // tiles_per_sample, :, pl.ds(t0, tm)], pos_smem, sem_idx)
    cp.start()
    cp.wait()

    def issue(j, carry):
        for k in range(TOP_K):
            pltpu.make_async_copy(ys_hbm.at[pl.ds(pos_smem[k, j], 1)], buf.at[k, pl.ds(j, 1)], sem_rows).start()
        return carry

    lax.fori_loop(0, tm, issue, 0, unroll=2)
    hb = h_ref[...].astype(BF16)
    g = jnp.dot(hb, wsg_ref[...], preferred_element_type=F32)
    u = jnp.dot(hb, wsu_ref[...], preferred_element_type=F32)
    acc = jnp.dot((g * _sigmoid(g) * u).astype(BF16), wsd_ref[...], preferred_element_type=F32)
    for k in range(TOP_K):
        pltpu.make_async_copy(ys_hbm.at[pl.ds(0, tm)], buf.at[k], sem_rows).wait()
    w = w_ref[...]
    for k in range(TOP_K):
        acc = acc + w[:, k:k + 1] * buf[k]
    o_ref[...] = x_ref[...] + mod_ref[5:6, :] * acc


def _combine(x_flat, h_flat, pos, ys, wts, mods, wsg, wsu, wsd, tiles_per_sample, n_ctx_tiles):
    n, d = x_flat.shape
    tm = COMBINE_TM
    ds_ = wsg.shape[-1]
    tok = lambda i: (i, 0)
    const = lambda i: (0, 0)

    def mod_idx(i):
        return (i // tiles_per_sample, jnp.where(i % tiles_per_sample >= n_ctx_tiles, 1, 0), 0, 0)

    return pl.pallas_call(
        functools.partial(_combine_kernel, tiles_per_sample=tiles_per_sample),
        grid=(n // tm,),
        in_specs=[pl.BlockSpec(memory_space=pl.ANY),
                  pl.BlockSpec(memory_space=pl.ANY),
                  pl.BlockSpec((tm, d), tok),
                  pl.BlockSpec((tm, d), tok),
                  pl.BlockSpec((tm, TOP_K), tok),
                  pl.BlockSpec((None, None, 8, d), mod_idx),
                  pl.BlockSpec((d, ds_), const), pl.BlockSpec((d, ds_), const), pl.BlockSpec((ds_, d), const)],
        out_specs=pl.BlockSpec((tm, d), tok),
        out_shape=jax.ShapeDtypeStruct((n, d), F32),
        scratch_shapes=[pltpu.SMEM((TOP_K, tm), jnp.int32),
                        pltpu.VMEM((TOP_K, tm, d), F32),
                        pltpu.SemaphoreType.DMA,
                        pltpu.SemaphoreType.DMA],
        compiler_params=_params(("arbitrary",)),
        name="combine",
    )(pos, ys, x_flat, h_flat, wts, mods, wsg, wsu, wsd)


def _final_kernel(x_ref, g_ref, o_ref):
    o_ref[...] = _rms(x_ref[...], g_ref[...])


def _final_norm(xs, g_final, n_ctx_tiles):
    b, l, d = xs.shape
    s_len = l - n_ctx_tiles * TM
    return pl.pallas_call(
        _final_kernel,
        grid=(b, s_len // TM),
        in_specs=[pl.BlockSpec((None, TM, d), lambda bi, ti: (bi, ti + n_ctx_tiles, 0)),
                  pl.BlockSpec((1, d), lambda bi, ti: (0, 0))],
        out_specs=pl.BlockSpec((None, TM, d), lambda bi, ti: (bi, ti, 0)),
        out_shape=jax.ShapeDtypeStruct((b, s_len, d), F32),
        compiler_params=_params(("parallel", "parallel")),
        name="final_norm",
    )(xs, g_final)


def _dispatch_plan(cnt, n_assign, blk):
    tot = jnp.sum(cnt, axis=0)
    padded = (tot + blk - 1) // blk * blk
    pad_end = jnp.cumsum(padded)
    pad_start = pad_end - padded
    base = pad_start[None, :] + jnp.cumsum(cnt, axis=0) - cnt
    n_blocks = -(-n_assign // blk) + N_EXPERTS
    first_row = jnp.arange(n_blocks, dtype=jnp.int32) * blk
    block_e = jnp.minimum(jnp.sum((pad_end[None, :] <= first_row[:, None]).astype(jnp.int32), axis=1), N_EXPERTS - 1)
    n_used = (pad_end[-1] // blk).astype(jnp.int32).reshape(1)
    slab = blk // 2
    fill_start = jnp.concatenate([pad_start + tot, pad_end[-1:]]).astype(jnp.int32)
    fill_len = jnp.concatenate([padded - tot, (n_blocks * blk - pad_end[-1:]) // slab]).astype(jnp.int32)
    return base.astype(jnp.int32)[:, :, None], block_e, n_used, fill_start, fill_len, n_blocks * blk


def _rope_tables(s_len, n_ctx):
    rows = s_len // GRID_W
    row = jnp.repeat(jnp.arange(rows), GRID_W).astype(F32)
    col = jnp.tile(jnp.arange(GRID_W), rows).astype(F32)
    quarter = HEAD_DIM // 4
    inv = 1.0 / (ROPE_BASE ** (jnp.arange(quarter, dtype=F32) / quarter))
    ar, ac = row[:, None] * inv, col[:, None] * inv
    cr, sr, cc, sc = jnp.cos(ar), jnp.sin(ar), jnp.cos(ac), jnp.sin(ac)
    z = jnp.zeros_like(sr)
    cos = jnp.concatenate([cr, cr, cc, cc], axis=1)
    sa = jnp.concatenate([z, sr, z, sc], axis=1)
    sb = jnp.concatenate([-sr, z, -sc, z], axis=1)
    rep = LANES // HEAD_DIM

    def full(t, fill):
        t = jnp.tile(t, (1, rep))
        return jnp.concatenate([jnp.full((n_ctx, LANES), fill, F32), t], axis=0)

    return full(cos, 1.0), full(sa, 0.0), full(sb, 0.0)


def _pair_perm():
    g = N_HEADS_A // KV_HEADS_A
    heads = [h for t in range(g) for h in (t, t + g)]
    return jnp.concatenate([jnp.arange(h * HEAD_DIM, (h + 1) * HEAD_DIM) for h in heads])


def _split_w_in(w):
    a_q, a_kv = N_HEADS_A * HEAD_DIM, KV_HEADS_A * HEAD_DIM
    b_w = N_HEADS_B * 2 * HEAD_DIM
    c_w = N_HEADS_C * HEAD_DIM_C
    sizes = (a_q, a_kv, a_kv, b_w, b_w, b_w, c_w, c_w, c_w, c_w, 4 * N_HEADS_C, w.shape[1])
    parts, start = [], 0
    for sz in sizes[:-1]:
        parts.append(w[:, start:start + sz])
        start += sz
    parts.append(w[:, start:])
    return parts


def _pack_w_in(w):
    d = w.shape[0]
    aq, ak, av, bq, bk, bv, cq, ck, cv, co, cg, gt = _split_w_in(w)
    pad = lambda n: jnp.zeros((d, n), w.dtype)
    kva = jnp.concatenate([ak, av, cg, pad(TILE_N - ak.shape[1] - av.shape[1] - cg.shape[1])], axis=1)
    big = jnp.concatenate([aq[:, _pair_perm()], bq, bk, kva, bv, co, gt, cq, cv], axis=1)
    return big.astype(BF16), ck.T.astype(BF16)


def kernel(x, c, ctx, c_ctx, w_mod, b_mod, g_mix, g_ffn, w_in, b_gate, sink, lam_q1, lam_k1, lam_q2, lam_k2,
           g_diff, g_mlstm, w_a, w_b, w_c, w_out, w_router, b_router, w_exp_gate, w_exp_up, w_exp_down,
           w_sh_gate, w_sh_up, w_sh_down, g_final):
    b, s_len, d = x.shape
    n_ctx = ctx.shape[1]
    l = n_ctx + s_len
    depth = w_mod.shape[0]
    n_ctx_tiles = n_ctx // TM
    assert n_ctx % TM == 0 and s_len % TM == 0 and d % LANES == 0 and s_len % GRID_W == 0

    xs = jnp.concatenate([ctx, x], axis=1)
    cos, sa, sb = _rope_tables(s_len, n_ctx)

    rows_c = 16
    cs = jnp.concatenate([c, c_ctx[None], jnp.zeros((rows_c - b - 1, d), F32)], axis=0)
    mod_all = _mod_vectors(cs, w_mod, b_mod).reshape(depth, rows_c, N_MOD, d)
    mod_all = jnp.pad(mod_all, ((0, 0), (0, 0), (0, 8 - N_MOD), (0, 0)))

    perm = _pair_perm()
    for layer in range(depth):
        lam_init = 0.8 - 0.6 * math.exp(-0.3 * layer)
        mods = jnp.stack([jnp.broadcast_to(mod_all[layer, b], (b, 8, d)), mod_all[layer, :b]], axis=1)
        w_big, w_kt = _pack_w_in(w_in[layer])
        p, gates, kt = _inproj(xs, mods, g_mix[layer][None], w_big, w_kt, cos, sa, sb, n_ctx_tiles)

        oa = _mixer_a(p, sink[layer], n_ctx)
        lam_params = jnp.stack([lam_q1[layer], lam_k1[layer], lam_q2[layer], lam_k2[layer]])
        ob = _mixer_b(p, lam_params, g_diff[layer][None], lam_init, n_ctx)

        bias = b_gate[layer].reshape(-1)
        bias_row = jnp.pad(bias, (0, LANES - bias.shape[0]))[None]
        bias_col = jnp.broadcast_to(bias[:, None], (bias.shape[0], LANES))
        gates_t = jnp.transpose(gates[:, :, :bias.shape[0]], (0, 2, 1))
        hm = _mlstm(p, kt, gates, gates_t, bias_row, bias_col, n_ctx)

        xs = _merge(xs, mods, oa, ob, hm, p, g_mlstm[layer][None],
                    w_a[layer][perm].astype(BF16), w_b[layer].astype(BF16), w_c[layer].astype(BF16),
                    w_out[layer].astype(BF16), n_ctx_tiles)

        h, idx_t, wt_t, cnt = _router(xs, mods, g_ffn[layer][None], w_router[layer].T, b_router[layer][:, None],
                                      n_ctx_tiles)
        base, block_e, n_used, fill_start, fill_len, n_rows = _dispatch_plan(
            cnt.reshape(-1, N_EXPERTS), b * l * TOP_K, EXPERT_ROWS)
        h_flat = h.reshape(b * l, d)
        pos, xrows = _dispatch(idx_t, base, h_flat, fill_start, fill_len, n_rows)
        ys = _experts(xrows, block_e, n_used, w_exp_gate[layer].astype(BF16),
                      w_exp_up[layer].astype(BF16), w_exp_down[layer].astype(BF16))
        wts = jnp.transpose(wt_t, (0, 2, 1)).reshape(b * l, TOP_K)
        xs = _combine(xs.reshape(b * l, d), h_flat, pos, ys, wts, mods,
                      w_sh_gate[layer].astype(BF16), w_sh_up[layer].astype(BF16), w_sh_down[layer].astype(BF16),
                      l // COMBINE_TM, n_ctx // COMBINE_TM).reshape(b, l, d)
    return _final_norm(xs, g_final[None], n_ctx_tiles)
```

```python
import functools
import math

import jax
import jax.numpy as jnp
from jax import lax
from jax.experimental import pallas as pl
from jax.experimental.pallas import tpu as pltpu

F32 = jnp.float32
BF16 = jnp.bfloat16
HIGHEST = lax.Precision.HIGHEST

GRID_W = 64
N_MOD = 6
HEAD_DIM = 64
N_HEADS_A = 8
KV_HEADS_A = 2
WINDOW = 128
N_HEADS_B = 4
N_HEADS_C = 4
HEAD_DIM_C = 128
N_EXPERTS = 64
N_GROUPS = 8
TOPK_GROUPS = 4
TOP_K = 8
ROUTED_SCALE = 2.5
ROPE_BASE = 10000.0
EPS = 1e-6

LANES = 128
CHUNK = 128
TILE_N = 512
TM = 256
MIXB_KEYS = 1024
EXPERT_ROWS = 512
SUBLANES = 8
SORT_ROWS = TOP_K * TM + N_EXPERTS * SUBLANES
SORT_CHUNK = 512
NEG = -1e30
VMEM_LIMIT = 56 * 1024 * 1024

T_AQ, T_BQ, T_BK, T_KVA, T_BV, T_CO, T_GT, T_CQ, T_CV, N_TILES = 0, 1, 2, 3, 4, 5, 6, 12, 13, 14

NT_DIMS = (((1,), (1,)), ((), ()))


def _params(sem):
    return pltpu.CompilerParams(dimension_semantics=sem, vmem_limit_bytes=VMEM_LIMIT)


def _rms(x, g):
    return x * lax.rsqrt(jnp.mean(x * x, axis=-1, keepdims=True) + EPS) * g


def _sigmoid(x):
    return jax.nn.sigmoid(x)


def _mod_kernel(c_ref, w_ref, b_ref, o_ref):
    c = c_ref[...]
    s = c * _sigmoid(c)
    o_ref[...] = jnp.dot(s, w_ref[...], precision=HIGHEST, preferred_element_type=F32) + b_ref[...]


def _mod_vectors(cs, w_mod, b_mod):
    depth, d, n = w_mod.shape
    r = cs.shape[0]
    tn = 3 * LANES
    return pl.pallas_call(
        _mod_kernel,
        grid=(depth, n // tn),
        in_specs=[pl.BlockSpec((r, d), lambda l, j: (0, 0)),
                  pl.BlockSpec((None, d, tn), lambda l, j: (l, 0, j)),
                  pl.BlockSpec((None, 1, tn), lambda l, j: (l, 0, j))],
        out_specs=pl.BlockSpec((None, r, tn), lambda l, j: (l, 0, j)),
        out_shape=jax.ShapeDtypeStruct((depth, r, n), F32),
        compiler_params=_params(("parallel", "parallel")),
        name="mod_vectors",
    )(cs, w_mod, b_mod.reshape(depth, 1, n))


def _inproj_kernel(x_ref, mod_ref, g_ref, w_ref, wkt_ref, cos_ref, sa_ref, sb_ref, p_ref, gate_ref, kt_ref):
    x = x_ref[...]
    h = _rms(x, g_ref[...]) * (1.0 + mod_ref[1:2, :]) + mod_ref[0:1, :]
    hb = h.astype(BF16)
    cos, sa, sb = cos_ref[...], sa_ref[...], sb_ref[...]

    def rope(t):
        return t * cos + pltpu.roll(t, 16, 1) * sa + pltpu.roll(t, LANES - 16, 1) * sb

    q_scale = HEAD_DIM ** -0.5
    for j in range(N_TILES):
        acc = jnp.dot(hb, w_ref[:, j * TILE_N:(j + 1) * TILE_N], preferred_element_type=F32)
        parts = [acc[:, s * LANES:(s + 1) * LANES] for s in range(TILE_N // LANES)]
        if j == T_AQ:
            parts = [rope(t) * q_scale for t in parts]
        elif j == T_BQ:
            parts = [rope(t) * (q_scale * math.log2(math.e)) for t in parts]
        elif j == T_BK:
            parts = [rope(t) for t in parts]
        elif j == T_KVA:
            gate_ref[...] = parts[2]
            parts[0] = rope(parts[0])
        for s, t in enumerate(parts):
            p_ref[:, j * TILE_N + s * LANES:j * TILE_N + (s + 1) * LANES] = t.astype(BF16)
    kt = lax.dot_general(wkt_ref[...], hb, NT_DIMS, preferred_element_type=F32)
    kt_ref[...] = (kt * (HEAD_DIM_C ** -0.5)).astype(BF16)


def _inproj(xs, mods, g_mix, w_big, w_kt, cos, sa, sb, n_ctx_tiles):
    b, l, d = xs.shape
    npad = w_big.shape[1]
    ck = w_kt.shape[0]
    grid = (b, l // TM)
    tok = lambda bi, ti: (bi, ti, 0)
    return pl.pallas_call(
        _inproj_kernel,
        grid=grid,
        in_specs=[pl.BlockSpec((None, TM, d), tok),
                  pl.BlockSpec((None, None, 8, d), lambda bi, ti: (bi, jnp.where(ti >= n_ctx_tiles, 1, 0), 0, 0)),
                  pl.BlockSpec((1, d), lambda bi, ti: (0, 0)),
                  pl.BlockSpec((d, npad), lambda bi, ti: (0, 0), pipeline_mode=pl.Buffered(1)),
                  pl.BlockSpec((ck, d), lambda bi, ti: (0, 0), pipeline_mode=pl.Buffered(1)),
                  pl.BlockSpec((TM, LANES), lambda bi, ti: (ti, 0)),
                  pl.BlockSpec((TM, LANES), lambda bi, ti: (ti, 0)),
                  pl.BlockSpec((TM, LANES), lambda bi, ti: (ti, 0))],
        out_specs=[pl.BlockSpec((None, TM, npad), tok),
                   pl.BlockSpec((None, TM, LANES), tok),
                   pl.BlockSpec((None, ck, TM), lambda bi, ti: (bi, 0, ti))],
        out_shape=[jax.ShapeDtypeStruct((b, l, npad), BF16),
                   jax.ShapeDtypeStruct((b, l, LANES), F32),
                   jax.ShapeDtypeStruct((b, ck, l), BF16)],
        compiler_params=_params(("parallel", "parallel")),
        name="inproj",
    )(xs, mods, g_mix, w_big, w_kt, cos, sa, sb)


def _mixa_kernel(sink_ref, q_ref, kp_ref, kc_ref, kn_ref, kx_ref, o_ref, *, n_ctx_blocks, n_blocks):
    i = pl.program_id(1)
    lat = i >= n_ctx_blocks
    has_prev = jnp.logical_and(lat, i > n_ctx_blocks)
    has_next = jnp.logical_and(lat, i < n_blocks - 1)
    r = lax.broadcasted_iota(jnp.int32, (CHUNK, CHUNK), 0)
    c = lax.broadcasted_iota(jnp.int32, (CHUNK, CHUNK), 1)
    n_ctx = kx_ref.shape[0]
    valid = jnp.concatenate([
        jnp.logical_and(c >= r, has_prev),
        jnp.broadcast_to(lat, (CHUNK, CHUNK)),
        jnp.logical_and(c <= r, has_next),
        jnp.ones((CHUNK, n_ctx), jnp.bool_)], axis=1)
    kcat = jnp.concatenate([kp_ref[:, :LANES], kc_ref[:, :LANES], kn_ref[:, :LANES], kx_ref[:, :LANES]], axis=0)
    vcat = jnp.concatenate([kp_ref[:, LANES:], kc_ref[:, LANES:], kn_ref[:, LANES:], kx_ref[:, LANES:]], axis=0)
    lane = lax.broadcasted_iota(jnp.int32, (CHUNK, LANES), 1)
    low = lane < HEAD_DIM
    n_pairs = N_HEADS_A // KV_HEADS_A
    outs = []
    for gk in range(KV_HEADS_A):
        keep = low if gk == 0 else jnp.logical_not(low)
        zero = jnp.zeros((CHUNK, LANES), BF16)
        lhs = jnp.concatenate([jnp.where(keep, q_ref[:, t * LANES:(t + 1) * LANES], zero) for t in range(n_pairs)],
                              axis=0)
        s = lax.dot_general(lhs, kcat, NT_DIMS, preferred_element_type=F32)
        o_g = []
        for t in range(n_pairs):
            st = jnp.where(valid, s[t * CHUNK:(t + 1) * CHUNK], NEG)
            sk = sink_ref[gk * n_pairs + t]
            m = jnp.maximum(jnp.max(st, axis=-1, keepdims=True), sk)
            p = jnp.exp(st - m)
            den = jnp.sum(p, axis=-1, keepdims=True) + jnp.exp(sk - m)
            o_g.append(jnp.dot(p.astype(BF16), vcat, preferred_element_type=F32) / den)
        outs.append(o_g)
    for t in range(n_pairs):
        o_ref[:, t * LANES:(t + 1) * LANES] = jnp.where(low, outs[0][t], outs[1][t]).astype(BF16)


def _mixer_a(p, sink, n_ctx):
    b, l, _ = p.shape
    nb = l // CHUNK
    ncb = n_ctx // CHUNK
    kvw = 2 * LANES
    kv_col = T_KVA * TILE_N // kvw
    aq_w = N_HEADS_A * HEAD_DIM
    kern = functools.partial(_mixa_kernel, n_ctx_blocks=ncb, n_blocks=nb)
    return pl.pallas_call(
        kern,
        grid=(b, nb),
        in_specs=[pl.BlockSpec(memory_space=pltpu.SMEM),
                  pl.BlockSpec((None, CHUNK, aq_w), lambda bi, i: (bi, i, T_AQ)),
                  pl.BlockSpec((None, CHUNK, kvw), lambda bi, i: (bi, jnp.maximum(i - 1, 0), kv_col)),
                  pl.BlockSpec((None, CHUNK, kvw), lambda bi, i: (bi, i, kv_col)),
                  pl.BlockSpec((None, CHUNK, kvw), lambda bi, i: (bi, jnp.minimum(i + 1, nb - 1), kv_col)),
                  pl.BlockSpec((None, n_ctx, kvw), lambda bi, i: (bi, 0, kv_col))],
        out_specs=pl.BlockSpec((None, CHUNK, aq_w), lambda bi, i: (bi, i, 0)),
        out_shape=jax.ShapeDtypeStruct((b, l, aq_w), BF16),
        compiler_params=_params(("parallel", "parallel")),
        name="mixer_a",
    )(sink, p, p, p, p, p)


def _fold_lanes(op, acc, s):
    for t in range(s.shape[1] // LANES):
        acc = op(acc, s[:, t * LANES:(t + 1) * LANES])
    return acc


def _mixb_kernel(lam_ref, gd_ref, q_ref, k_ref, v_ref, *rest, lam_init, chunks):
    o_ref, s_scr, va_scr = rest[-3], rest[-2], rest[-1]

    @pl.when(pl.program_id(2) == 0)
    def _():
        n_keys = v_ref.shape[0]
        va_scr[:, :LANES] = v_ref[...]
        va_scr[:, LANES:] = (lax.broadcasted_iota(jnp.int32, (n_keys, LANES), 1) == 0).astype(BF16)

    lp = lam_ref[...]
    lam = (jnp.exp(jnp.sum(lp[0:1] * lp[1:2], axis=-1, keepdims=True))
           - jnp.exp(jnp.sum(lp[2:3] * lp[3:4], axis=-1, keepdims=True)) + lam_init)
    q = q_ref[...]
    tq = q.shape[0]
    lane = lax.broadcasted_iota(jnp.int32, (tq, LANES), 1)
    zero = jnp.zeros_like(q)
    qs = (jnp.where(lane < HEAD_DIM, q, zero), jnp.where(lane >= HEAD_DIM, q, zero))
    rows = [slice(mi * tq, (mi + 1) * tq) for mi in range(2)]
    mrun = [jnp.full((tq, LANES), NEG, F32) for _ in range(2)]
    for off, sz in chunks:
        for mi in range(2):
            s_scr[rows[mi], off:off + sz] = lax.dot_general(qs[mi], k_ref[off:off + sz, :], NT_DIMS,
                                                            preferred_element_type=F32)
            mrun[mi] = _fold_lanes(jnp.maximum, mrun[mi], s_scr[rows[mi], off:off + sz])
    m = [jnp.max(mr, axis=-1, keepdims=True) for mr in mrun]
    acc = [jnp.zeros((tq, 2 * LANES), F32) for _ in range(2)]
    for off, sz in chunks:
        for mi in range(2):
            pr = jnp.exp2(s_scr[rows[mi], off:off + sz] - m[mi])
            acc[mi] = acc[mi] + jnp.dot(pr.astype(BF16), va_scr[off:off + sz, :], preferred_element_type=F32)
    outs = [a[:, :LANES] / a[:, LANES:LANES + 1] for a in acc]
    o = outs[0] - lam * outs[1]
    o_ref[...] = (_rms(o, gd_ref[...]) * (1.0 - lam_init)).astype(BF16)


def _mixer_b(p, lam_params, g_diff, lam_init, n_ctx):
    b, l, _ = p.shape
    tq = TM
    kl = min(MIXB_KEYS, l - n_ctx)
    assert (l - n_ctx) % kl == 0 and n_ctx % tq == 0
    q0 = T_BQ * TILE_N // LANES
    k0 = T_BK * TILE_N // LANES
    v0 = T_BV * TILE_N // LANES
    n_ctx_tiles = n_ctx // tq
    ctx_chunks = ((0, n_ctx),)
    all_chunks = ctx_chunks + tuple((n_ctx + c * kl, kl) for c in range((l - n_ctx) // kl))

    def call(chunks, n_keys, q_tiles, q_first):
        kern = functools.partial(_mixb_kernel, lam_init=lam_init, chunks=chunks)
        return pl.pallas_call(
            kern,
            grid=(b, N_HEADS_B, q_tiles),
            in_specs=[pl.BlockSpec((4, HEAD_DIM), lambda bi, h, qi: (0, 0)),
                      pl.BlockSpec((1, LANES), lambda bi, h, qi: (0, 0)),
                      pl.BlockSpec((None, tq, LANES), lambda bi, h, qi: (bi, qi + q_first, q0 + h)),
                      pl.BlockSpec((None, n_keys, LANES), lambda bi, h, qi: (bi, 0, k0 + h)),
                      pl.BlockSpec((None, n_keys, LANES), lambda bi, h, qi: (bi, 0, v0 + h))],
            out_specs=pl.BlockSpec((None, tq, LANES), lambda bi, h, qi: (bi, qi, h)),
            out_shape=jax.ShapeDtypeStruct((b, q_tiles * tq, N_HEADS_B * LANES), BF16),
            scratch_shapes=[pltpu.VMEM((2 * tq, n_keys), F32), pltpu.VMEM((n_keys, 2 * LANES), BF16)],
            compiler_params=_params(("parallel", "parallel", "arbitrary")),
            name="mixer_b",
        )(lam_params, g_diff, p, p, p)

    return call(ctx_chunks, n_ctx, n_ctx_tiles, 0), call(all_chunks, l, (l - n_ctx) // tq, n_ctx_tiles)


def _log_sigmoid(x):
    return jnp.minimum(x, 0.0) - jnp.log1p(jnp.exp(-jnp.abs(x)))


def _mlstm_kernel(q_ref, kt_ref, v_ref, gc_ref, gr_ref, bc_ref, br_ref, o_ref, s_scr, m_scr):
    d = pl.program_id(1)
    c = pl.program_id(2)

    @pl.when(c == 0)
    def _():
        s_scr[...] = jnp.zeros_like(s_scr)
        m_scr[...] = jnp.zeros_like(m_scr)

    fwd = d == 0
    r = lax.broadcasted_iota(jnp.int32, (CHUNK, CHUNK), 0)
    cc = lax.broadcasted_iota(jnp.int32, (CHUNK, CHUNK), 1)
    tri = jnp.where(fwd, r - cc, cc - r) >= 0
    trif = tri.astype(F32)
    gcol = gc_ref[...] + bc_ref[...]
    grow = gr_ref[...] + br_ref[...]
    lf_col = _log_sigmoid(gcol)
    lf_row = _log_sigmoid(grow)
    bcum_col = jnp.dot(trif, lf_col, precision=HIGHEST, preferred_element_type=F32)
    bcum_row = lax.dot_general(lf_row, trif, NT_DIMS, precision=HIGHEST, preferred_element_type=F32)
    tot_row = jnp.sum(lf_row, axis=-1, keepdims=True)
    lane = lax.broadcasted_iota(jnp.int32, (CHUNK, LANES), 1)
    ones_col = (lane == 0).astype(BF16)
    nh = N_HEADS_C

    for h in range(nh):
        def pick_col(a, kind):
            return jnp.where(fwd, a[:, kind * nh + h:kind * nh + h + 1],
                             a[:, (kind + 2) * nh + h:(kind + 2) * nh + h + 1])

        def pick_row(a, kind):
            return jnp.where(fwd, a[kind * nh + h:kind * nh + h + 1, :],
                             a[(kind + 2) * nh + h:(kind + 2) * nh + h + 1, :])

        ic_row = pick_row(grow, 0)
        b_col = pick_col(bcum_col, 1)
        b_row = pick_row(bcum_row, 1)
        total = pick_row(tot_row, 1)
        m_st = m_scr[h, 0:1, 0:1]
        dm = jnp.where(tri, b_col - b_row + ic_row, NEG)
        inter = b_col + m_st
        m_t = jnp.maximum(inter, jnp.max(dm, axis=-1, keepdims=True))
        e = jnp.exp(dm - m_t)
        qh = q_ref[:, h * LANES:(h + 1) * LANES]
        kth = kt_ref[h * LANES:(h + 1) * LANES, :]
        vaug = jnp.concatenate([v_ref[:, h * LANES:(h + 1) * LANES], ones_col], axis=1)
        s = jnp.dot(qh, kth, preferred_element_type=F32) * e
        st = s_scr[h]
        intra = jnp.dot(s.astype(BF16), vaug, preferred_element_type=F32)
        cross = jnp.dot(qh, st.astype(BF16), preferred_element_type=F32)
        nd = intra + jnp.exp(inter - m_t) * cross
        den = nd[:, LANES:LANES + 1]
        o_ref[:, h * LANES:(h + 1) * LANES] = nd[:, :LANES] / jnp.maximum(jnp.abs(den), jnp.exp(-m_t))
        gs_row = total - b_row + ic_row
        m_new = jnp.maximum(total + m_st, jnp.max(gs_row, axis=-1, keepdims=True))
        decay = jnp.exp(total + m_st - m_new)
        wkt = (kth.astype(F32) * jnp.exp(gs_row - m_new)).astype(BF16)
        s_scr[h] = decay * st + jnp.dot(wkt, vaug, preferred_element_type=F32)
        m_scr[h] = jnp.broadcast_to(m_new, m_scr.shape[1:])


def _mlstm(p, kt, gates, gates_t, bias_row, bias_col, n_ctx):
    b, l, _ = p.shape
    nc = l // CHUNK
    ncc = n_ctx // CHUNK
    cw = N_HEADS_C * HEAD_DIM_C

    def chunk(d, c):
        rev = jnp.where(c < ncc, ncc - 1 - c, nc + ncc - 1 - c)
        return jnp.where(d == 0, c, rev)

    return pl.pallas_call(
        _mlstm_kernel,
        grid=(b, 2, nc),
        in_specs=[pl.BlockSpec((None, CHUNK, cw), lambda bi, d, c: (bi, chunk(d, c), T_CQ)),
                  pl.BlockSpec((None, cw, CHUNK), lambda bi, d, c: (bi, 0, chunk(d, c))),
                  pl.BlockSpec((None, CHUNK, cw), lambda bi, d, c: (bi, chunk(d, c), T_CV)),
                  pl.BlockSpec((None, CHUNK, LANES), lambda bi, d, c: (bi, chunk(d, c), 0)),
                  pl.BlockSpec((None, 16, CHUNK), lambda bi, d, c: (bi, 0, chunk(d, c))),
                  pl.BlockSpec((1, LANES), lambda bi, d, c: (0, 0)),
                  pl.BlockSpec((16, LANES), lambda bi, d, c: (0, 0))],
        out_specs=pl.BlockSpec((None, None, CHUNK, cw), lambda bi, d, c: (d, bi, chunk(d, c), 0)),
        out_shape=jax.ShapeDtypeStruct((2, b, l, cw), F32),
        scratch_shapes=[pltpu.VMEM((N_HEADS_C, HEAD_DIM_C, 2 * LANES), F32),
                        pltpu.VMEM((N_HEADS_C, 8, LANES), F32)],
        compiler_params=_params(("parallel", "parallel", "arbitrary")),
        name="mlstm",
    )(p, kt, p, gates, gates_t, bias_row, bias_col)


def _merge_kernel(x_ref, mod_ref, oa_ref, obc_ref, obl_ref, hf_ref, hb_ref, co_ref, gt_ref, gm_ref,
                  wa_ref, wb_ref, wc_ref, wo_ref, xo_ref, *, n_ctx_tiles):
    d = x_ref.shape[-1]
    ob = jnp.where(pl.program_id(1) < n_ctx_tiles, obc_ref[...], obl_ref[...])
    hs = hf_ref[...] + hb_ref[...]
    co = co_ref[...].astype(F32)
    gm = gm_ref[...]
    oc = []
    for h in range(N_HEADS_C):
        sl = slice(h * LANES, (h + 1) * LANES)
        oc.append((_rms(hs[:, sl], gm[:, sl]) * _sigmoid(co[:, sl])).astype(BF16))
    oc = jnp.concatenate(oc, axis=1)
    y = (_sigmoid(gt_ref[:, 0:d].astype(F32)) * jnp.dot(oa_ref[...], wa_ref[...], preferred_element_type=F32)
         + _sigmoid(gt_ref[:, d:2 * d].astype(F32)) * jnp.dot(ob, wb_ref[...], preferred_element_type=F32)
         + _sigmoid(gt_ref[:, 2 * d:3 * d].astype(F32)) * jnp.dot(oc, wc_ref[...], preferred_element_type=F32))
    out = jnp.dot(y.astype(BF16), wo_ref[...], preferred_element_type=F32)
    xo_ref[...] = x_ref[...] + mod_ref[2:3, :] * out


def _merge(xs, mods, oa, ob_ctx, ob_lat, hm, p, g_mlstm, wa, wb, wc, wo, n_ctx_tiles):
    b, l, d = xs.shape
    tok = lambda bi, ti: (bi, ti, 0)
    cw = N_HEADS_C * HEAD_DIM_C
    const = lambda bi, ti: (0, 0)
    return pl.pallas_call(
        functools.partial(_merge_kernel, n_ctx_tiles=n_ctx_tiles),
        grid=(b, l // TM),
        in_specs=[pl.BlockSpec((None, TM, d), tok),
                  pl.BlockSpec((None, None, 8, d), lambda bi, ti: (bi, jnp.where(ti >= n_ctx_tiles, 1, 0), 0, 0)),
                  pl.BlockSpec((None, TM, oa.shape[-1]), tok),
                  pl.BlockSpec((None, TM, ob_ctx.shape[-1]), lambda bi, ti: (bi, jnp.minimum(ti, n_ctx_tiles - 1), 0)),
                  pl.BlockSpec((None, TM, ob_lat.shape[-1]), lambda bi, ti: (bi, jnp.maximum(ti - n_ctx_tiles, 0), 0)),
                  pl.BlockSpec((None, None, TM, cw), lambda bi, ti: (0, bi, ti, 0)),
                  pl.BlockSpec((None, None, TM, cw), lambda bi, ti: (1, bi, ti, 0)),
                  pl.BlockSpec((None, TM, cw), lambda bi, ti: (bi, ti, T_CO)),
                  pl.BlockSpec((None, TM, 3 * d), lambda bi, ti: (bi, ti, T_GT * TILE_N // (3 * d))),
                  pl.BlockSpec((1, cw), const),
                  pl.BlockSpec(wa.shape, const), pl.BlockSpec(wb.shape, const),
                  pl.BlockSpec(wc.shape, const), pl.BlockSpec(wo.shape, const)],
        out_specs=pl.BlockSpec((None, TM, d), tok),
        out_shape=jax.ShapeDtypeStruct((b, l, d), F32),
        compiler_params=_params(("parallel", "parallel")),
        name="merge",
    )(xs, mods, oa, ob_ctx, ob_lat, hm, hm, p, p, g_mlstm, wa, wb, wc, wo)


def _router_kernel(x_ref, mod_ref, g_ref, wrt_ref, br_ref, h_ref, idx_ref, wt_ref, cnt_ref):
    h = _rms(x_ref[...], g_ref[...]) * (1.0 + mod_ref[4:5, :]) + mod_ref[3:4, :]
    h_ref[...] = h.astype(BF16)
    tm = h.shape[0]
    per = N_EXPERTS // N_GROUPS
    lt = lax.dot_general(wrt_ref[...], h, NT_DIMS, precision=HIGHEST, preferred_element_type=F32)
    s = _sigmoid(lt)
    sel = s + br_ref[...]
    ninf = -jnp.inf
    sel3 = sel.reshape(N_GROUPS, per, tm)
    eidx = lax.broadcasted_iota(jnp.int32, (N_GROUPS, per, tm), 1)
    m1 = jnp.max(sel3, axis=1, keepdims=True)
    first = jnp.min(jnp.where(sel3 == m1, eidx, per), axis=1, keepdims=True)
    m2 = jnp.max(jnp.where(eidx == first, ninf, sel3), axis=1, keepdims=True)
    gscore = (m1 + m2).reshape(N_GROUPS, tm)
    gidx = lax.broadcasted_iota(jnp.int32, (N_GROUPS, tm), 0)
    gmask = jnp.zeros((N_GROUPS, tm), jnp.bool_)
    cur = gscore
    for _ in range(TOPK_GROUPS):
        mx = jnp.max(cur, axis=0, keepdims=True)
        hit = gidx == jnp.min(jnp.where(cur == mx, gidx, N_GROUPS), axis=0, keepdims=True)
        gmask = jnp.logical_or(gmask, hit)
        cur = jnp.where(hit, ninf, cur)
    cur = jnp.where(gmask.reshape(N_GROUPS, 1, tm), sel3, ninf).reshape(N_EXPERTS, tm)
    eid = lax.broadcasted_iota(jnp.int32, (N_EXPERTS, tm), 0)
    ids, ws = [], []
    chosen = jnp.zeros((N_EXPERTS, tm), F32)
    for _ in range(TOP_K):
        mx = jnp.max(cur, axis=0, keepdims=True)
        pick = jnp.min(jnp.where(cur == mx, eid, N_EXPERTS), axis=0, keepdims=True)
        hit = eid == pick
        ids.append(pick)
        ws.append(jnp.sum(jnp.where(hit, s, 0.0), axis=0, keepdims=True))
        cur = jnp.where(hit, ninf, cur)
        chosen = chosen + hit.astype(F32)
    wsum = ws[0]
    for w in ws[1:]:
        wsum = wsum + w
    idx_ref[...] = jnp.concatenate(ids, axis=0)
    wt_ref[...] = jnp.concatenate([w / wsum * ROUTED_SCALE for w in ws], axis=0)
    cnt_ref[...] = jnp.sum(chosen, axis=1, keepdims=True).astype(jnp.int32)


def _router(xs, mods, g_ffn, w_router_t, b_router, n_ctx_tiles):
    b, l, d = xs.shape
    tok = lambda bi, ti: (bi, ti, 0)
    const = lambda bi, ti: (0, 0)
    return pl.pallas_call(
        _router_kernel,
        grid=(b, l // TM),
        in_specs=[pl.BlockSpec((None, TM, d), tok),
                  pl.BlockSpec((None, None, 8, d), lambda bi, ti: (bi, jnp.where(ti >= n_ctx_tiles, 1, 0), 0, 0)),
                  pl.BlockSpec((1, d), const),
                  pl.BlockSpec((N_EXPERTS, d), const),
                  pl.BlockSpec((N_EXPERTS, 1), const)],
        out_specs=[pl.BlockSpec((None, TM, d), tok),
                   pl.BlockSpec((None, TOP_K, TM), lambda bi, ti: (bi, 0, ti)),
                   pl.BlockSpec((None, TOP_K, TM), lambda bi, ti: (bi, 0, ti)),
                   pl.BlockSpec((None, None, N_EXPERTS, 1), lambda bi, ti: (bi, ti, 0, 0))],
        out_shape=[jax.ShapeDtypeStruct((b, l, d), BF16),
                   jax.ShapeDtypeStruct((b, TOP_K, l), jnp.int32),
                   jax.ShapeDtypeStruct((b, TOP_K, l), F32),
                   jax.ShapeDtypeStruct((b, l // TM, N_EXPERTS, 1), jnp.int32)],
        compiler_params=_params(("parallel", "parallel")),
        name="router",
    )(xs, mods, g_ffn, w_router_t, b_router)


def _pack_bf16_pairs(x):
    half = x.shape[1] // 2
    lo = lax.bitcast_convert_type(x[:, :half], jnp.uint32) >> 16
    hi = lax.bitcast_convert_type(x[:, half:], jnp.uint32) & jnp.uint32(0xFFFF0000)
    return lo | hi


def _unpack_bf16_pairs(w):
    lo = lax.bitcast_convert_type(w << 16, F32)
    hi = lax.bitcast_convert_type(w & jnp.uint32(0xFFFF0000), F32)
    return jnp.concatenate([lo, hi], axis=1).astype(BF16)


def _sort_kernel(idx_ref, off_ref, h_ref, posl_ref, ts_ref):
    tm = h_ref.shape[0]
    idx = idx_ref[...]
    eid = lax.broadcasted_iota(jnp.int32, (N_EXPERTS, tm), 0)
    hits = [eid == idx[k:k + 1, :] for k in range(TOP_K)]
    chosen = hits[0].astype(BF16)
    for hk in hits[1:]:
        chosen = chosen + hk.astype(BF16)
    r = lax.broadcasted_iota(jnp.int32, (tm, tm), 0)
    c = lax.broadcasted_iota(jnp.int32, (tm, tm), 1)
    before = (r < c).astype(BF16)
    rank = jnp.dot(chosen, before, preferred_element_type=F32)
    slot = rank.astype(jnp.int32) + off_ref[...]
    posl = jnp.concatenate([jnp.sum(jnp.where(hk, slot, 0), axis=0, keepdims=True) for hk in hits], axis=0)
    posl_ref[...] = posl
    hb = h_ref[...]
    for rb in range(ts_ref.shape[0] // SORT_CHUNK):
        rows = lax.broadcasted_iota(jnp.int32, (SORT_CHUNK, tm), 0) + rb * SORT_CHUNK
        sel = rows == posl[0:1, :]
        for k in range(1, TOP_K):
            sel = jnp.logical_or(sel, rows == posl[k:k + 1, :])
        onehot = jnp.where(sel, 1.0, 0.0).astype(BF16)
        ts = jnp.dot(onehot, hb, preferred_element_type=F32)
        ts_ref[rb * SORT_CHUNK:(rb + 1) * SORT_CHUNK, :] = _pack_bf16_pairs(ts)


def _sort_rows(idx_t, off, h_flat):
    b, k, l = idx_t.shape
    n, d = h_flat.shape
    nt = l // TM
    return pl.pallas_call(
        _sort_kernel,
        grid=(b * nt,),
        in_specs=[pl.BlockSpec((None, k, TM), lambda i: (i // nt, 0, i % nt)),
                  pl.BlockSpec((None, N_EXPERTS, 1), lambda i: (i, 0, 0)),
                  pl.BlockSpec((TM, d), lambda i: (i, 0))],
        out_specs=[pl.BlockSpec((None, k, TM), lambda i: (i // nt, 0, i % nt)),
                   pl.BlockSpec((None, SORT_ROWS, d // 2), lambda i: (i, 0, 0))],
        out_shape=[jax.ShapeDtypeStruct((b, k, l), jnp.int32),
                   jax.ShapeDtypeStruct((b * nt, SORT_ROWS, d // 2), jnp.uint32)],
        compiler_params=_params(("parallel",)),
        name="sort_rows",
    )(idx_t, off, h_flat)


RUN_SIZES = tuple(SUBLANES << s for s in range((TM // SUBLANES).bit_length()))


def _regroup_kernel(so_ref, do_ref, ln_ref, fs_ref, fl_ref, *refs, to_experts, n_tiles):
    if to_experts:
        tiles_hbm, rows_hbm, zeros, sem, sem_fill = refs
    else:
        rows_hbm, _, tiles_hbm, sem = refs
    e = pl.program_id(0)

    def runs(wait):
        def per_tile(t, carry):
            i = e * n_tiles + t
            so, do, ln = so_ref[i], do_ref[i], ln_ref[i]
            for sz in RUN_SIZES:
                take = (ln & sz) != 0

                @pl.when(take)
                def _():
                    in_tile = tiles_hbm.at[t, pl.ds(pl.multiple_of(so, SUBLANES), sz)]
                    in_rows = rows_hbm.at[pl.ds(pl.multiple_of(do, SUBLANES), sz)]
                    cp = (pltpu.make_async_copy(in_tile, in_rows, sem) if to_experts
                          else pltpu.make_async_copy(in_rows, in_tile, sem))
                    cp.wait() if wait else cp.start()

                step = jnp.where(take, sz, 0)
                so, do = so + step, do + step
            return carry

        lax.fori_loop(0, n_tiles, per_tile, 0)

    def fill(wait):
        slab = zeros.shape[0]
        start, left = fs_ref[e], fl_ref[e]
        for sz in [s for s in RUN_SIZES if s <= slab]:
            take = (left & sz) != 0

            @pl.when(take)
            def _():
                cp = pltpu.make_async_copy(zeros.at[pl.ds(0, sz)],
                                           rows_hbm.at[pl.ds(pl.multiple_of(start, SUBLANES), sz)], sem_fill)
                cp.wait() if wait else cp.start()

            start = start + jnp.where(take, sz, 0)

        @pl.when(e == N_EXPERTS - 1)
        def _():
            def tail(t, carry):
                row = pl.multiple_of(fs_ref[N_EXPERTS] + t * slab, SUBLANES)
                cp = pltpu.make_async_copy(zeros, rows_hbm.at[pl.ds(row, slab)], sem_fill)
                cp.wait() if wait else cp.start()
                return carry

            lax.fori_loop(0, fl_ref[N_EXPERTS], tail, 0)

    if to_experts:
        @pl.when(e == 0)
        def _():
            zeros[...] = jnp.zeros_like(zeros)

    runs(False)
    if to_experts:
        fill(False)
    runs(True)
    if to_experts:
        fill(True)


def _regroup(plan, tiles, rows, n_rows, to_experts):
    nt, _, w = tiles.shape
    kern = functools.partial(_regroup_kernel, to_experts=to_experts, n_tiles=nt)
    any_spec = pl.BlockSpec(memory_space=pl.ANY)
    tables = (plan["run_tile_off"], plan["run_row"], plan["run_len"], plan["fill_start"], plan["fill_len"])
    if to_experts:
        grid_spec = pltpu.PrefetchScalarGridSpec(
            num_scalar_prefetch=len(tables), grid=(N_EXPERTS,), in_specs=[any_spec], out_specs=any_spec,
            scratch_shapes=[pltpu.VMEM((EXPERT_ROWS // 2, w), tiles.dtype),
                            pltpu.SemaphoreType.DMA, pltpu.SemaphoreType.DMA])
        return pl.pallas_call(
            kern, grid_spec=grid_spec, out_shape=jax.ShapeDtypeStruct((n_rows, w), tiles.dtype),
            compiler_params=_params(("arbitrary",)), name="regroup_to_experts",
        )(*tables, tiles)
    grid_spec = pltpu.PrefetchScalarGridSpec(
        num_scalar_prefetch=len(tables), grid=(N_EXPERTS,), in_specs=[any_spec, any_spec], out_specs=any_spec,
        scratch_shapes=[pltpu.SemaphoreType.DMA])
    return pl.pallas_call(
        kern, grid_spec=grid_spec, out_shape=jax.ShapeDtypeStruct(tiles.shape, tiles.dtype),
        input_output_aliases={len(tables) + 1: 0},
        compiler_params=_params(("arbitrary",)), name="regroup_to_tiles",
    )(*tables, rows, tiles)


def _expert_kernel(be_ref, nu_ref, x_ref, wg_ref, wu_ref, wd_ref, y_ref):
    i = pl.program_id(0)

    @pl.when(i < nu_ref[0])
    def _():
        x = _unpack_bf16_pairs(x_ref[...])
        g = jnp.dot(x, wg_ref[...], preferred_element_type=F32)
        u = jnp.dot(x, wu_ref[...], preferred_element_type=F32)
        a = (g * _sigmoid(g) * u).astype(BF16)
        y = jnp.dot(a, wd_ref[...], preferred_element_type=F32)
        y_ref[...] = _pack_bf16_pairs(y.astype(BF16).astype(F32))

    @pl.when(i >= nu_ref[0])
    def _():
        y_ref[...] = jnp.zeros_like(y_ref)


def _experts(xs, block_e, n_used, wg, wu, wd):
    n_rows, w = xs.shape
    blk = EXPERT_ROWS
    d, de = wg.shape[1:]
    grid_spec = pltpu.PrefetchScalarGridSpec(
        num_scalar_prefetch=2,
        grid=(n_rows // blk,),
        in_specs=[pl.BlockSpec((blk, w), lambda i, be, nu: (jnp.minimum(i, nu[0] - 1), 0)),
                  pl.BlockSpec((None, d, de), lambda i, be, nu: (be[i], 0, 0)),
                  pl.BlockSpec((None, d, de), lambda i, be, nu: (be[i], 0, 0)),
                  pl.BlockSpec((None, de, d), lambda i, be, nu: (be[i], 0, 0))],
        out_specs=pl.BlockSpec((blk, w), lambda i, be, nu: (i, 0)))
    return pl.pallas_call(
        _expert_kernel,
        grid_spec=grid_spec,
        out_shape=jax.ShapeDtypeStruct((n_rows, w), xs.dtype),
        compiler_params=_params(("arbitrary",)),
        name="experts",
    )(block_e, n_used, xs, wg, wu, wd)


def _combine_kernel(ts_ref, posl_ref, w_ref, x_ref, h_ref, mod_ref, wsg_ref, wsu_ref, wsd_ref, o_ref):
    tm = x_ref.shape[0]
    hb = h_ref[...]
    g = jnp.dot(hb, wsg_ref[...], preferred_element_type=F32)
    u = jnp.dot(hb, wsu_ref[...], preferred_element_type=F32)
    acc = jnp.dot((g * _sigmoid(g) * u).astype(BF16), wsd_ref[...], preferred_element_type=F32)
    posl = posl_ref[...]
    w = w_ref[...]
    for rb in range(ts_ref.shape[0] // SORT_CHUNK):
        cols = lax.broadcasted_iota(jnp.int32, (tm, SORT_CHUNK), 1) + rb * SORT_CHUNK
        wm = jnp.zeros((tm, SORT_CHUNK), F32)
        for k in range(TOP_K):
            wm = jnp.where(cols == posl[:, k:k + 1], w[:, k:k + 1], wm)
        w_hi = wm.astype(BF16)
        w_lo = (wm - w_hi.astype(F32)).astype(BF16)
        ys = _unpack_bf16_pairs(ts_ref[rb * SORT_CHUNK:(rb + 1) * SORT_CHUNK, :])
        acc = acc + jnp.dot(w_hi, ys, preferred_element_type=F32) + jnp.dot(w_lo, ys, preferred_element_type=F32)
    o_ref[...] = x_ref[...] + mod_ref[5:6, :] * acc


def _combine(tiles, posl_tm, wts, x_flat, h_flat, mods, wsg, wsu, wsd, tiles_per_sample, n_ctx_tiles):
    n, d = x_flat.shape
    ds_ = wsg.shape[-1]
    tok = lambda i: (i, 0)
    const = lambda i: (0, 0)

    def mod_idx(i):
        return (i // tiles_per_sample, jnp.where(i % tiles_per_sample >= n_ctx_tiles, 1, 0), 0, 0)

    return pl.pallas_call(
        _combine_kernel,
        grid=(n // TM,),
        in_specs=[pl.BlockSpec((None,) + tiles.shape[1:], lambda i: (i, 0, 0)),
                  pl.BlockSpec((TM, TOP_K), tok),
                  pl.BlockSpec((TM, TOP_K), tok),
                  pl.BlockSpec((TM, d), tok),
                  pl.BlockSpec((TM, d), tok),
                  pl.BlockSpec((None, None, 8, d), mod_idx),
                  pl.BlockSpec((d, ds_), const), pl.BlockSpec((d, ds_), const), pl.BlockSpec((ds_, d), const)],
        out_specs=pl.BlockSpec((TM, d), tok),
        out_shape=jax.ShapeDtypeStruct((n, d), F32),
        compiler_params=_params(("parallel",)),
        name="combine",
    )(tiles, posl_tm, wts, x_flat, h_flat, mods, wsg, wsu, wsd)


def _final_kernel(x_ref, g_ref, o_ref):
    o_ref[...] = _rms(x_ref[...], g_ref[...])


def _final_norm(xs, g_final, n_ctx_tiles):
    b, l, d = xs.shape
    s_len = l - n_ctx_tiles * TM
    return pl.pallas_call(
        _final_kernel,
        grid=(b, s_len // TM),
        in_specs=[pl.BlockSpec((None, TM, d), lambda bi, ti: (bi, ti + n_ctx_tiles, 0)),
                  pl.BlockSpec((1, d), lambda bi, ti: (0, 0))],
        out_specs=pl.BlockSpec((None, TM, d), lambda bi, ti: (bi, ti, 0)),
        out_shape=jax.ShapeDtypeStruct((b, s_len, d), F32),
        compiler_params=_params(("parallel", "parallel")),
        name="final_norm",
    )(xs, g_final)


def _moe_plan(cnt, n_assign, blk):
    nt = cnt.shape[0]
    run = (cnt + SUBLANES - 1) // SUBLANES * SUBLANES
    tile_off = jnp.cumsum(run, axis=1) - run
    tot = jnp.sum(run, axis=0)
    padded = (tot + blk - 1) // blk * blk
    pad_end = jnp.cumsum(padded)
    pad_start = pad_end - padded
    row = pad_start[None, :] + jnp.cumsum(run, axis=0) - run
    n_blocks = -(-(n_assign + nt * N_EXPERTS * (SUBLANES - 1)) // blk) + N_EXPERTS
    first_row = jnp.arange(n_blocks, dtype=jnp.int32) * blk
    block_e = jnp.minimum(jnp.sum((pad_end[None, :] <= first_row[:, None]).astype(jnp.int32), axis=1), N_EXPERTS - 1)
    slab = blk // 2
    i32 = lambda a: a.astype(jnp.int32)
    return dict(
        tile_off=i32(tile_off)[:, :, None],
        run_tile_off=i32(tile_off.T.reshape(-1)), run_row=i32(row.T.reshape(-1)), run_len=i32(run.T.reshape(-1)),
        fill_start=i32(jnp.concatenate([pad_start + tot, pad_end[-1:]])),
        fill_len=i32(jnp.concatenate([padded - tot, (n_blocks * blk - pad_end[-1:]) // slab])),
        block_e=i32(block_e), n_used=i32(pad_end[-1] // blk).reshape(1), n_rows=n_blocks * blk)


def _rope_tables(s_len, n_ctx):
    rows = s_len // GRID_W
    row = jnp.repeat(jnp.arange(rows), GRID_W).astype(F32)
    col = jnp.tile(jnp.arange(GRID_W), rows).astype(F32)
    quarter = HEAD_DIM // 4
    inv = 1.0 / (ROPE_BASE ** (jnp.arange(quarter, dtype=F32) / quarter))
    ar, ac = row[:, None] * inv, col[:, None] * inv
    cr, sr, cc, sc = jnp.cos(ar), jnp.sin(ar), jnp.cos(ac), jnp.sin(ac)
    z = jnp.zeros_like(sr)
    cos = jnp.concatenate([cr, cr, cc, cc], axis=1)
    sa = jnp.concatenate([z, sr, z, sc], axis=1)
    sb = jnp.concatenate([-sr, z, -sc, z], axis=1)
    rep = LANES // HEAD_DIM

    def full(t, fill):
        t = jnp.tile(t, (1, rep))
        return jnp.concatenate([jnp.full((n_ctx, LANES), fill, F32), t], axis=0)

    return full(cos, 1.0), full(sa, 0.0), full(sb, 0.0)


def _pair_perm():
    g = N_HEADS_A // KV_HEADS_A
    heads = [h for t in range(g) for h in (t, t + g)]
    return jnp.concatenate([jnp.arange(h * HEAD_DIM, (h + 1) * HEAD_DIM) for h in heads])


def _split_w_in(w):
    a_q, a_kv = N_HEADS_A * HEAD_DIM, KV_HEADS_A * HEAD_DIM
    b_w = N_HEADS_B * 2 * HEAD_DIM
    c_w = N_HEADS_C * HEAD_DIM_C
    sizes = (a_q, a_kv, a_kv, b_w, b_w, b_w, c_w, c_w, c_w, c_w, 4 * N_HEADS_C, w.shape[1])
    parts, start = [], 0
    for sz in sizes[:-1]:
        parts.append(w[:, start:start + sz])
        start += sz
    parts.append(w[:, start:])
    return parts


def _pack_w_in(w):
    d = w.shape[0]
    aq, ak, av, bq, bk, bv, cq, ck, cv, co, cg, gt = _split_w_in(w)
    pad = lambda n: jnp.zeros((d, n), w.dtype)
    kva = jnp.concatenate([ak, av, cg, pad(TILE_N - ak.shape[1] - av.shape[1] - cg.shape[1])], axis=1)
    big = jnp.concatenate([aq[:, _pair_perm()], bq, bk, kva, bv, co, gt, cq, cv], axis=1)
    return big.astype(BF16), ck.T.astype(BF16)


def kernel(x, c, ctx, c_ctx, w_mod, b_mod, g_mix, g_ffn, w_in, b_gate, sink, lam_q1, lam_k1, lam_q2, lam_k2,
           g_diff, g_mlstm, w_a, w_b, w_c, w_out, w_router, b_router, w_exp_gate, w_exp_up, w_exp_down,
           w_sh_gate, w_sh_up, w_sh_down, g_final):
    b, s_len, d = x.shape
    n_ctx = ctx.shape[1]
    l = n_ctx + s_len
    depth = w_mod.shape[0]
    n_ctx_tiles = n_ctx // TM
    assert n_ctx % TM == 0 and s_len % TM == 0 and d % LANES == 0 and s_len % GRID_W == 0

    xs = jnp.concatenate([ctx, x], axis=1)
    cos, sa, sb = _rope_tables(s_len, n_ctx)

    rows_c = 16
    cs = jnp.concatenate([c, c_ctx[None], jnp.zeros((rows_c - b - 1, d), F32)], axis=0)
    mod_all = _mod_vectors(cs, w_mod, b_mod).reshape(depth, rows_c, N_MOD, d)
    mod_all = jnp.pad(mod_all, ((0, 0), (0, 0), (0, 8 - N_MOD), (0, 0)))

    perm = _pair_perm()
    for layer in range(depth):
        lam_init = 0.8 - 0.6 * math.exp(-0.3 * layer)
        mods = jnp.stack([jnp.broadcast_to(mod_all[layer, b], (b, 8, d)), mod_all[layer, :b]], axis=1)
        w_big, w_kt = _pack_w_in(w_in[layer])
        p, gates, kt = _inproj(xs, mods, g_mix[layer][None], w_big, w_kt, cos, sa, sb, n_ctx_tiles)

        oa = _mixer_a(p, sink[layer], n_ctx)
        lam_params = jnp.stack([lam_q1[layer], lam_k1[layer], lam_q2[layer], lam_k2[layer]])
        ob_ctx, ob_lat = _mixer_b(p, lam_params, g_diff[layer][None], lam_init, n_ctx)

        bias = b_gate[layer].reshape(-1)
        bias_row = jnp.pad(bias, (0, LANES - bias.shape[0]))[None]
        bias_col = jnp.broadcast_to(bias[:, None], (bias.shape[0], LANES))
        gates_t = jnp.transpose(gates[:, :, :bias.shape[0]], (0, 2, 1))
        hm = _mlstm(p, kt, gates, gates_t, bias_row, bias_col, n_ctx)

        xs = _merge(xs, mods, oa, ob_ctx, ob_lat, hm, p, g_mlstm[layer][None],
                    w_a[layer][perm].astype(BF16), w_b[layer].astype(BF16), w_c[layer].astype(BF16),
                    w_out[layer].astype(BF16), n_ctx_tiles)

        h, idx_t, wt_t, cnt = _router(xs, mods, g_ffn[layer][None], w_router[layer].T, b_router[layer][:, None],
                                      n_ctx_tiles)
        plan = _moe_plan(cnt.reshape(-1, N_EXPERTS), b * l * TOP_K, EXPERT_ROWS)
        h_flat = h.reshape(b * l, d)
        posl, tiles = _sort_rows(idx_t, plan["tile_off"], h_flat)
        xrows = _regroup(plan, tiles, None, plan["n_rows"], True)
        ys = _experts(xrows, plan["block_e"], plan["n_used"], w_exp_gate[layer].astype(BF16),
                      w_exp_up[layer].astype(BF16), w_exp_down[layer].astype(BF16))
        tiles = _regroup(plan, tiles, ys, plan["n_rows"], False)
        to_rows = lambda a: jnp.transpose(a, (0, 2, 1)).reshape(b * l, TOP_K)
        xs = _combine(tiles, to_rows(posl), to_rows(wt_t), xs.reshape(b * l, d), h_flat, mods,
                      w_sh_gate[layer].astype(BF16), w_sh_up[layer].astype(BF16), w_sh_down[layer].astype(BF16),
                      l // TM, n_ctx_tiles).reshape(b, l, d)
    return _final_norm(xs, g_final[None], n_ctx_tiles)
```

```python
import functools
import math

import jax
import jax.numpy as jnp
from jax import lax
from jax.experimental import pallas as pl
from jax.experimental.pallas import tpu as pltpu

F32 = jnp.float32
BF16 = jnp.bfloat16
HIGHEST = lax.Precision.HIGHEST

GRID_W = 64
N_MOD = 6
HEAD_DIM = 64
N_HEADS_A = 8
KV_HEADS_A = 2
WINDOW = 128
N_HEADS_B = 4
N_HEADS_C = 4
HEAD_DIM_C = 128
N_EXPERTS = 64
N_GROUPS = 8
TOPK_GROUPS = 4
TOP_K = 8
ROUTED_SCALE = 2.5
ROPE_BASE = 10000.0
EPS = 1e-6

LANES = 128
CHUNK = 128
TILE_N = 512
TM = 256
MIXB_KEYS = 1024
EXPERT_ROWS = 512
SUBLANES = 8
SORT_ROWS = TOP_K * TM + N_EXPERTS * SUBLANES
SORT_CHUNK = 512
NEG = -1e30
VMEM_LIMIT = 56 * 1024 * 1024

T_AQ, T_BQ, T_BK, T_KVA, T_BV, T_CO, T_GT, T_CQ, T_CV, N_TILES = 0, 1, 2, 3, 4, 5, 6, 12, 13, 14

NT_DIMS = (((1,), (1,)), ((), ()))


def _params(sem):
    return pltpu.CompilerParams(dimension_semantics=sem, vmem_limit_bytes=VMEM_LIMIT)


def _rms(x, g):
    return x * lax.rsqrt(jnp.mean(x * x, axis=-1, keepdims=True) + EPS) * g


def _sigmoid(x):
    return jax.nn.sigmoid(x)


def _mod_kernel(c_ref, w_ref, b_ref, o_ref):
    c = c_ref[...]
    s = c * _sigmoid(c)
    o_ref[...] = jnp.dot(s, w_ref[...], precision=HIGHEST, preferred_element_type=F32) + b_ref[...]


def _mod_vectors(cs, w_mod, b_mod):
    depth, d, n = w_mod.shape
    r = cs.shape[0]
    tn = 3 * LANES
    return pl.pallas_call(
        _mod_kernel,
        grid=(depth, n // tn),
        in_specs=[pl.BlockSpec((r, d), lambda l, j: (0, 0)),
                  pl.BlockSpec((None, d, tn), lambda l, j: (l, 0, j)),
                  pl.BlockSpec((None, 1, tn), lambda l, j: (l, 0, j))],
        out_specs=pl.BlockSpec((None, r, tn), lambda l, j: (l, 0, j)),
        out_shape=jax.ShapeDtypeStruct((depth, r, n), F32),
        compiler_params=_params(("parallel", "parallel")),
        name="mod_vectors",
    )(cs, w_mod, b_mod.reshape(depth, 1, n))


def _inproj_kernel(x_ref, mod_ref, g_ref, w_ref, wkt_ref, cos_ref, sa_ref, sb_ref, p_ref, gate_ref, kt_ref):
    x = x_ref[...]
    h = _rms(x, g_ref[...]) * (1.0 + mod_ref[1:2, :]) + mod_ref[0:1, :]
    hb = h.astype(BF16)
    cos, sa, sb = cos_ref[...], sa_ref[...], sb_ref[...]

    def rope(t):
        return t * cos + pltpu.roll(t, 16, 1) * sa + pltpu.roll(t, LANES - 16, 1) * sb

    q_scale = HEAD_DIM ** -0.5
    for j in range(N_TILES):
        acc = jnp.dot(hb, w_ref[:, j * TILE_N:(j + 1) * TILE_N], preferred_element_type=F32)
        parts = [acc[:, s * LANES:(s + 1) * LANES] for s in range(TILE_N // LANES)]
        if j == T_AQ:
            parts = [rope(t) * q_scale for t in parts]
        elif j == T_BQ:
            parts = [rope(t) * (q_scale * math.log2(math.e)) for t in parts]
        elif j == T_BK:
            parts = [rope(t) for t in parts]
        elif j == T_KVA:
            gate_ref[...] = parts[2]
            parts[0] = rope(parts[0])
        for s, t in enumerate(parts):
            p_ref[:, j * TILE_N + s * LANES:j * TILE_N + (s + 1) * LANES] = t.astype(BF16)
    kt = lax.dot_general(wkt_ref[...], hb, NT_DIMS, preferred_element_type=F32)
    kt_ref[...] = (kt * (HEAD_DIM_C ** -0.5)).astype(BF16)


def _inproj(xs, mods, g_mix, w_big, w_kt, cos, sa, sb, n_ctx_tiles):
    b, l, d = xs.shape
    npad = w_big.shape[1]
    ck = w_kt.shape[0]
    grid = (b, l // TM)
    tok = lambda bi, ti: (bi, ti, 0)
    return pl.pallas_call(
        _inproj_kernel,
        grid=grid,
        in_specs=[pl.BlockSpec((None, TM, d), tok),
                  pl.BlockSpec((None, None, 8, d), lambda bi, ti: (bi, jnp.where(ti >= n_ctx_tiles, 1, 0), 0, 0)),
                  pl.BlockSpec((1, d), lambda bi, ti: (0, 0)),
                  pl.BlockSpec((d, npad), lambda bi, ti: (0, 0), pipeline_mode=pl.Buffered(1)),
                  pl.BlockSpec((ck, d), lambda bi, ti: (0, 0), pipeline_mode=pl.Buffered(1)),
                  pl.BlockSpec((TM, LANES), lambda bi, ti: (ti, 0)),
                  pl.BlockSpec((TM, LANES), lambda bi, ti: (ti, 0)),
                  pl.BlockSpec((TM, LANES), lambda bi, ti: (ti, 0))],
        out_specs=[pl.BlockSpec((None, TM, npad), tok),
                   pl.BlockSpec((None, TM, LANES), tok),
                   pl.BlockSpec((None, ck, TM), lambda bi, ti: (bi, 0, ti))],
        out_shape=[jax.ShapeDtypeStruct((b, l, npad), BF16),
                   jax.ShapeDtypeStruct((b, l, LANES), F32),
                   jax.ShapeDtypeStruct((b, ck, l), BF16)],
        compiler_params=_params(("parallel", "parallel")),
        name="inproj",
    )(xs, mods, g_mix, w_big, w_kt, cos, sa, sb)


def _mixa_kernel(sink_ref, q_ref, kp_ref, kc_ref, kn_ref, kx_ref, o_ref, *, n_ctx_blocks, n_blocks):
    i = pl.program_id(1)
    lat = i >= n_ctx_blocks
    has_prev = jnp.logical_and(lat, i > n_ctx_blocks)
    has_next = jnp.logical_and(lat, i < n_blocks - 1)
    r = lax.broadcasted_iota(jnp.int32, (CHUNK, CHUNK), 0)
    c = lax.broadcasted_iota(jnp.int32, (CHUNK, CHUNK), 1)
    n_ctx = kx_ref.shape[0]
    valid = jnp.concatenate([
        jnp.logical_and(c >= r, has_prev),
        jnp.broadcast_to(lat, (CHUNK, CHUNK)),
        jnp.logical_and(c <= r, has_next),
        jnp.ones((CHUNK, n_ctx), jnp.bool_)], axis=1)
    kcat = jnp.concatenate([kp_ref[:, :LANES], kc_ref[:, :LANES], kn_ref[:, :LANES], kx_ref[:, :LANES]], axis=0)
    vcat = jnp.concatenate([kp_ref[:, LANES:], kc_ref[:, LANES:], kn_ref[:, LANES:], kx_ref[:, LANES:]], axis=0)
    lane = lax.broadcasted_iota(jnp.int32, (CHUNK, LANES), 1)
    low = lane < HEAD_DIM
    n_pairs = N_HEADS_A // KV_HEADS_A
    outs = []
    for gk in range(KV_HEADS_A):
        keep = low if gk == 0 else jnp.logical_not(low)
        zero = jnp.zeros((CHUNK, LANES), BF16)
        lhs = jnp.concatenate([jnp.where(keep, q_ref[:, t * LANES:(t + 1) * LANES], zero) for t in range(n_pairs)],
                              axis=0)
        s = lax.dot_general(lhs, kcat, NT_DIMS, preferred_element_type=F32)
        o_g = []
        for t in range(n_pairs):
            st = jnp.where(valid, s[t * CHUNK:(t + 1) * CHUNK], NEG)
            sk = sink_ref[gk * n_pairs + t]
            m = jnp.maximum(jnp.max(st, axis=-1, keepdims=True), sk)
            p = jnp.exp(st - m)
            den = jnp.sum(p, axis=-1, keepdims=True) + jnp.exp(sk - m)
            o_g.append(jnp.dot(p.astype(BF16), vcat, preferred_element_type=F32) / den)
        outs.append(o_g)
    for t in range(n_pairs):
        o_ref[:, t * LANES:(t + 1) * LANES] = jnp.where(low, outs[0][t], outs[1][t]).astype(BF16)


def _mixer_a(p, sink, n_ctx):
    b, l, _ = p.shape
    nb = l // CHUNK
    ncb = n_ctx // CHUNK
    kvw = 2 * LANES
    kv_col = T_KVA * TILE_N // kvw
    aq_w = N_HEADS_A * HEAD_DIM
    kern = functools.partial(_mixa_kernel, n_ctx_blocks=ncb, n_blocks=nb)
    return pl.pallas_call(
        kern,
        grid=(b, nb),
        in_specs=[pl.BlockSpec(memory_space=pltpu.SMEM),
                  pl.BlockSpec((None, CHUNK, aq_w), lambda bi, i: (bi, i, T_AQ)),
                  pl.BlockSpec((None, CHUNK, kvw), lambda bi, i: (bi, jnp.maximum(i - 1, 0), kv_col)),
                  pl.BlockSpec((None, CHUNK, kvw), lambda bi, i: (bi, i, kv_col)),
                  pl.BlockSpec((None, CHUNK, kvw), lambda bi, i: (bi, jnp.minimum(i + 1, nb - 1), kv_col)),
                  pl.BlockSpec((None, n_ctx, kvw), lambda bi, i: (bi, 0, kv_col))],
        out_specs=pl.BlockSpec((None, CHUNK, aq_w), lambda bi, i: (bi, i, 0)),
        out_shape=jax.ShapeDtypeStruct((b, l, aq_w), BF16),
        compiler_params=_params(("parallel", "parallel")),
        name="mixer_a",
    )(sink, p, p, p, p, p)


def _fold_lanes(op, acc, s):
    for t in range(s.shape[1] // LANES):
        acc = op(acc, s[:, t * LANES:(t + 1) * LANES])
    return acc


def _mixb_kernel(lam_ref, gd_ref, q_ref, k_ref, v_ref, *rest, lam_init, chunks):
    o_ref, s_scr, va_scr = rest[-3], rest[-2], rest[-1]

    @pl.when(pl.program_id(2) == 0)
    def _():
        n_keys = v_ref.shape[0]
        va_scr[:, :LANES] = v_ref[...]
        va_scr[:, LANES:] = (lax.broadcasted_iota(jnp.int32, (n_keys, LANES), 1) == 0).astype(BF16)

    lp = lam_ref[...]
    lam = (jnp.exp(jnp.sum(lp[0:1] * lp[1:2], axis=-1, keepdims=True))
           - jnp.exp(jnp.sum(lp[2:3] * lp[3:4], axis=-1, keepdims=True)) + lam_init)
    q = q_ref[...]
    tq = q.shape[0]
    lane = lax.broadcasted_iota(jnp.int32, (tq, LANES), 1)
    zero = jnp.zeros_like(q)
    qs = (jnp.where(lane < HEAD_DIM, q, zero), jnp.where(lane >= HEAD_DIM, q, zero))
    rows = [slice(mi * tq, (mi + 1) * tq) for mi in range(2)]
    mrun = [jnp.full((tq, LANES), NEG, F32) for _ in range(2)]
    for off, sz in chunks:
        for mi in range(2):
            s_scr[rows[mi], off:off + sz] = lax.dot_general(qs[mi], k_ref[off:off + sz, :], NT_DIMS,
                                                            preferred_element_type=F32)
            mrun[mi] = _fold_lanes(jnp.maximum, mrun[mi], s_scr[rows[mi], off:off + sz])
    m = [jnp.max(mr, axis=-1, keepdims=True) for mr in mrun]
    acc = [jnp.zeros((tq, 2 * LANES), F32) for _ in range(2)]
    for off, sz in chunks:
        for mi in range(2):
            pr = jnp.exp2(s_scr[rows[mi], off:off + sz] - m[mi])
            acc[mi] = acc[mi] + jnp.dot(pr.astype(BF16), va_scr[off:off + sz, :], preferred_element_type=F32)
    outs = [a[:, :LANES] / a[:, LANES:LANES + 1] for a in acc]
    o = outs[0] - lam * outs[1]
    o_ref[...] = (_rms(o, gd_ref[...]) * (1.0 - lam_init)).astype(BF16)


def _mixer_b(p, lam_params, g_diff, lam_init, n_ctx):
    b, l, _ = p.shape
    tq = TM
    kl = min(MIXB_KEYS, l - n_ctx)
    assert (l - n_ctx) % kl == 0 and n_ctx % tq == 0
    q0 = T_BQ * TILE_N // LANES
    k0 = T_BK * TILE_N // LANES
    v0 = T_BV * TILE_N // LANES
    n_ctx_tiles = n_ctx // tq
    ctx_chunks = ((0, n_ctx),)
    all_chunks = ctx_chunks + tuple((n_ctx + c * kl, kl) for c in range((l - n_ctx) // kl))

    def call(chunks, n_keys, q_tiles, q_first):
        kern = functools.partial(_mixb_kernel, lam_init=lam_init, chunks=chunks)
        return pl.pallas_call(
            kern,
            grid=(b, N_HEADS_B, q_tiles),
            in_specs=[pl.BlockSpec((4, HEAD_DIM), lambda bi, h, qi: (0, 0)),
                      pl.BlockSpec((1, LANES), lambda bi, h, qi: (0, 0)),
                      pl.BlockSpec((None, tq, LANES), lambda bi, h, qi: (bi, qi + q_first, q0 + h)),
                      pl.BlockSpec((None, n_keys, LANES), lambda bi, h, qi: (bi, 0, k0 + h)),
                      pl.BlockSpec((None, n_keys, LANES), lambda bi, h, qi: (bi, 0, v0 + h))],
            out_specs=pl.BlockSpec((None, tq, LANES), lambda bi, h, qi: (bi, qi, h)),
            out_shape=jax.ShapeDtypeStruct((b, q_tiles * tq, N_HEADS_B * LANES), BF16),
            scratch_shapes=[pltpu.VMEM((2 * tq, n_keys), F32), pltpu.VMEM((n_keys, 2 * LANES), BF16)],
            compiler_params=_params(("parallel", "parallel", "arbitrary")),
            name="mixer_b",
        )(lam_params, g_diff, p, p, p)

    return call(ctx_chunks, n_ctx, n_ctx_tiles, 0), call(all_chunks, l, (l - n_ctx) // tq, n_ctx_tiles)


def _log_sigmoid(x):
    return jnp.minimum(x, 0.0) - jnp.log1p(jnp.exp(-jnp.abs(x)))


def _mlstm_kernel(q_ref, kt_ref, v_ref, gc_ref, gr_ref, bc_ref, br_ref, o_ref, s_scr, m_scr):
    d = pl.program_id(1)
    c = pl.program_id(2)

    @pl.when(c == 0)
    def _():
        s_scr[...] = jnp.zeros_like(s_scr)
        m_scr[...] = jnp.zeros_like(m_scr)

    fwd = d == 0
    r = lax.broadcasted_iota(jnp.int32, (CHUNK, CHUNK), 0)
    cc = lax.broadcasted_iota(jnp.int32, (CHUNK, CHUNK), 1)
    tri = jnp.where(fwd, r - cc, cc - r) >= 0
    trif = tri.astype(F32)
    gcol = gc_ref[...] + bc_ref[...]
    grow = gr_ref[...] + br_ref[...]
    lf_col = _log_sigmoid(gcol)
    lf_row = _log_sigmoid(grow)
    bcum_col = jnp.dot(trif, lf_col, precision=HIGHEST, preferred_element_type=F32)
    bcum_row = lax.dot_general(lf_row, trif, NT_DIMS, precision=HIGHEST, preferred_element_type=F32)
    tot_row = jnp.sum(lf_row, axis=-1, keepdims=True)
    lane = lax.broadcasted_iota(jnp.int32, (CHUNK, LANES), 1)
    ones_col = (lane == 0).astype(BF16)
    nh = N_HEADS_C

    for h in range(nh):
        def pick_col(a, kind):
            return jnp.where(fwd, a[:, kind * nh + h:kind * nh + h + 1],
                             a[:, (kind + 2) * nh + h:(kind + 2) * nh + h + 1])

        def pick_row(a, kind):
            return jnp.where(fwd, a[kind * nh + h:kind * nh + h + 1, :],
                             a[(kind + 2) * nh + h:(kind + 2) * nh + h + 1, :])

        ic_row = pick_row(grow, 0)
        b_col = pick_col(bcum_col, 1)
        b_row = pick_row(bcum_row, 1)
        total = pick_row(tot_row, 1)
        m_st = m_scr[h, 0:1, 0:1]
        dm = jnp.where(tri, b_col - b_row + ic_row, NEG)
        inter = b_col + m_st
        m_t = jnp.maximum(inter, jnp.max(dm, axis=-1, keepdims=True))
        e = jnp.exp(dm - m_t)
        qh = q_ref[:, h * LANES:(h + 1) * LANES]
        kth = kt_ref[h * LANES:(h + 1) * LANES, :]
        vaug = jnp.concatenate([v_ref[:, h * LANES:(h + 1) * LANES], ones_col], axis=1)
        s = jnp.dot(qh, kth, preferred_element_type=F32) * e
        st = s_scr[h]
        intra = jnp.dot(s.astype(BF16), vaug, preferred_element_type=F32)
        cross = jnp.dot(qh, st.astype(BF16), preferred_element_type=F32)
        nd = intra + jnp.exp(inter - m_t) * cross
        den = nd[:, LANES:LANES + 1]
        o_ref[:, h * LANES:(h + 1) * LANES] = nd[:, :LANES] / jnp.maximum(jnp.abs(den), jnp.exp(-m_t))
        gs_row = total - b_row + ic_row
        m_new = jnp.maximum(total + m_st, jnp.max(gs_row, axis=-1, keepdims=True))
        decay = jnp.exp(total + m_st - m_new)
        wkt = (kth.astype(F32) * jnp.exp(gs_row - m_new)).astype(BF16)
        s_scr[h] = decay * st + jnp.dot(wkt, vaug, preferred_element_type=F32)
        m_scr[h] = jnp.broadcast_to(m_new, m_scr.shape[1:])


def _mlstm(p, kt, gates, gates_t, bias_row, bias_col, n_ctx):
    b, l, _ = p.shape
    nc = l // CHUNK
    ncc = n_ctx // CHUNK
    cw = N_HEADS_C * HEAD_DIM_C

    def chunk(d, c):
        rev = jnp.where(c < ncc, ncc - 1 - c, nc + ncc - 1 - c)
        return jnp.where(d == 0, c, rev)

    return pl.pallas_call(
        _mlstm_kernel,
        grid=(b, 2, nc),
        in_specs=[pl.BlockSpec((None, CHUNK, cw), lambda bi, d, c: (bi, chunk(d, c), T_CQ)),
                  pl.BlockSpec((None, cw, CHUNK), lambda bi, d, c: (bi, 0, chunk(d, c))),
                  pl.BlockSpec((None, CHUNK, cw), lambda bi, d, c: (bi, chunk(d, c), T_CV)),
                  pl.BlockSpec((None, CHUNK, LANES), lambda bi, d, c: (bi, chunk(d, c), 0)),
                  pl.BlockSpec((None, 16, CHUNK), lambda bi, d, c: (bi, 0, chunk(d, c))),
                  pl.BlockSpec((1, LANES), lambda bi, d, c: (0, 0)),
                  pl.BlockSpec((16, LANES), lambda bi, d, c: (0, 0))],
        out_specs=pl.BlockSpec((None, None, CHUNK, cw), lambda bi, d, c: (d, bi, chunk(d, c), 0)),
        out_shape=jax.ShapeDtypeStruct((2, b, l, cw), F32),
        scratch_shapes=[pltpu.VMEM((N_HEADS_C, HEAD_DIM_C, 2 * LANES), F32),
                        pltpu.VMEM((N_HEADS_C, 8, LANES), F32)],
        compiler_params=_params(("parallel", "parallel", "arbitrary")),
        name="mlstm",
    )(p, kt, p, gates, gates_t, bias_row, bias_col)


def _merge_kernel(x_ref, mod_ref, oa_ref, obc_ref, obl_ref, hf_ref, hb_ref, co_ref, gt_ref, gm_ref,
                  wa_ref, wb_ref, wc_ref, wo_ref, xo_ref, *, n_ctx_tiles):
    d = x_ref.shape[-1]
    ob = jnp.where(pl.program_id(1) < n_ctx_tiles, obc_ref[...], obl_ref[...])
    hs = hf_ref[...] + hb_ref[...]
    co = co_ref[...].astype(F32)
    gm = gm_ref[...]
    oc = []
    for h in range(N_HEADS_C):
        sl = slice(h * LANES, (h + 1) * LANES)
        oc.append((_rms(hs[:, sl], gm[:, sl]) * _sigmoid(co[:, sl])).astype(BF16))
    oc = jnp.concatenate(oc, axis=1)
    y = (_sigmoid(gt_ref[:, 0:d].astype(F32)) * jnp.dot(oa_ref[...], wa_ref[...], preferred_element_type=F32)
         + _sigmoid(gt_ref[:, d:2 * d].astype(F32)) * jnp.dot(ob, wb_ref[...], preferred_element_type=F32)
         + _sigmoid(gt_ref[:, 2 * d:3 * d].astype(F32)) * jnp.dot(oc, wc_ref[...], preferred_element_type=F32))
    out = jnp.dot(y.astype(BF16), wo_ref[...], preferred_element_type=F32)
    xo_ref[...] = x_ref[...] + mod_ref[2:3, :] * out


def _merge(xs, mods, oa, ob_ctx, ob_lat, hm, p, g_mlstm, wa, wb, wc, wo, n_ctx_tiles):
    b, l, d = xs.shape
    tok = lambda bi, ti: (bi, ti, 0)
    cw = N_HEADS_C * HEAD_DIM_C
    const = lambda bi, ti: (0, 0)
    return pl.pallas_call(
        functools.partial(_merge_kernel, n_ctx_tiles=n_ctx_tiles),
        grid=(b, l // TM),
        in_specs=[pl.BlockSpec((None, TM, d), tok),
                  pl.BlockSpec((None, None, 8, d), lambda bi, ti: (bi, jnp.where(ti >= n_ctx_tiles, 1, 0), 0, 0)),
                  pl.BlockSpec((None, TM, oa.shape[-1]), tok),
                  pl.BlockSpec((None, TM, ob_ctx.shape[-1]), lambda bi, ti: (bi, jnp.minimum(ti, n_ctx_tiles - 1), 0)),
                  pl.BlockSpec((None, TM, ob_lat.shape[-1]), lambda bi, ti: (bi, jnp.maximum(ti - n_ctx_tiles, 0), 0)),
                  pl.BlockSpec((None, None, TM, cw), lambda bi, ti: (0, bi, ti, 0)),
                  pl.BlockSpec((None, None, TM, cw), lambda bi, ti: (1, bi, ti, 0)),
                  pl.BlockSpec((None, TM, cw), lambda bi, ti: (bi, ti, T_CO)),
                  pl.BlockSpec((None, TM, 3 * d), lambda bi, ti: (bi, ti, T_GT * TILE_N // (3 * d))),
                  pl.BlockSpec((1, cw), const),
                  pl.BlockSpec(wa.shape, const), pl.BlockSpec(wb.shape, const),
                  pl.BlockSpec(wc.shape, const), pl.BlockSpec(wo.shape, const)],
        out_specs=pl.BlockSpec((None, TM, d), tok),
        out_shape=jax.ShapeDtypeStruct((b, l, d), F32),
        compiler_params=_params(("parallel", "parallel")),
        name="merge",
    )(xs, mods, oa, ob_ctx, ob_lat, hm, hm, p, p, g_mlstm, wa, wb, wc, wo)


def _router_kernel(x_ref, mod_ref, g_ref, wrt_ref, br_ref, h_ref, idx_ref, wt_ref, cnt_ref):
    h = _rms(x_ref[...], g_ref[...]) * (1.0 + mod_ref[4:5, :]) + mod_ref[3:4, :]
    h_ref[...] = h.astype(BF16)
    tm = h.shape[0]
    per = N_EXPERTS // N_GROUPS
    lt = lax.dot_general(wrt_ref[...], h, NT_DIMS, precision=HIGHEST, preferred_element_type=F32)
    s = _sigmoid(lt)
    sel = s + br_ref[...]
    ninf = -jnp.inf
    sel3 = sel.reshape(N_GROUPS, per, tm)
    eidx = lax.broadcasted_iota(jnp.int32, (N_GROUPS, per, tm), 1)
    m1 = jnp.max(sel3, axis=1, keepdims=True)
    first = jnp.min(jnp.where(sel3 == m1, eidx, per), axis=1, keepdims=True)
    m2 = jnp.max(jnp.where(eidx == first, ninf, sel3), axis=1, keepdims=True)
    gscore = (m1 + m2).reshape(N_GROUPS, tm)
    gidx = lax.broadcasted_iota(jnp.int32, (N_GROUPS, tm), 0)
    gmask = jnp.zeros((N_GROUPS, tm), jnp.bool_)
    cur = gscore
    for _ in range(TOPK_GROUPS):
        mx = jnp.max(cur, axis=0, keepdims=True)
        hit = gidx == jnp.min(jnp.where(cur == mx, gidx, N_GROUPS), axis=0, keepdims=True)
        gmask = jnp.logical_or(gmask, hit)
        cur = jnp.where(hit, ninf, cur)
    cur = jnp.where(gmask.reshape(N_GROUPS, 1, tm), sel3, ninf).reshape(N_EXPERTS, tm)
    eid = lax.broadcasted_iota(jnp.int32, (N_EXPERTS, tm), 0)
    ids, ws = [], []
    chosen = jnp.zeros((N_EXPERTS, tm), F32)
    for _ in range(TOP_K):
        mx = jnp.max(cur, axis=0, keepdims=True)
        pick = jnp.min(jnp.where(cur == mx, eid, N_EXPERTS), axis=0, keepdims=True)
        hit = eid == pick
        ids.append(pick)
        ws.append(jnp.sum(jnp.where(hit, s, 0.0), axis=0, keepdims=True))
        cur = jnp.where(hit, ninf, cur)
        chosen = chosen + hit.astype(F32)
    wsum = ws[0]
    for w in ws[1:]:
        wsum = wsum + w
    idx_ref[...] = jnp.concatenate(ids, axis=0)
    wt_ref[...] = jnp.concatenate([w / wsum * ROUTED_SCALE for w in ws], axis=0)
    cnt_ref[...] = jnp.sum(chosen, axis=1, keepdims=True).astype(jnp.int32)


def _router(xs, mods, g_ffn, w_router_t, b_router, n_ctx_tiles):
    b, l, d = xs.shape
    tok = lambda bi, ti: (bi, ti, 0)
    const = lambda bi, ti: (0, 0)
    return pl.pallas_call(
        _router_kernel,
        grid=(b, l // TM),
        in_specs=[pl.BlockSpec((None, TM, d), tok),
                  pl.BlockSpec((None, None, 8, d), lambda bi, ti: (bi, jnp.where(ti >= n_ctx_tiles, 1, 0), 0, 0)),
                  pl.BlockSpec((1, d), const),
                  pl.BlockSpec((N_EXPERTS, d), const),
                  pl.BlockSpec((N_EXPERTS, 1), const)],
        out_specs=[pl.BlockSpec((None, TM, d), tok),
                   pl.BlockSpec((None, TOP_K, TM), lambda bi, ti: (bi, 0, ti)),
                   pl.BlockSpec((None, TOP_K, TM), lambda bi, ti: (bi, 0, ti)),
                   pl.BlockSpec((None, None, N_EXPERTS, 1), lambda bi, ti: (bi, ti, 0, 0))],
        out_shape=[jax.ShapeDtypeStruct((b, l, d), BF16),
                   jax.ShapeDtypeStruct((b, TOP_K, l), jnp.int32),
                   jax.ShapeDtypeStruct((b, TOP_K, l), F32),
                   jax.ShapeDtypeStruct((b, l // TM, N_EXPERTS, 1), jnp.int32)],
        compiler_params=_params(("parallel", "parallel")),
        name="router",
    )(xs, mods, g_ffn, w_router_t, b_router)


def _pack_bf16_pairs(x):
    half = x.shape[1] // 2
    lo = lax.bitcast_convert_type(x[:, :half], jnp.uint32) >> 16
    hi = lax.bitcast_convert_type(x[:, half:], jnp.uint32) & jnp.uint32(0xFFFF0000)
    return lo | hi


def _unpack_bf16_pairs(w):
    lo = lax.bitcast_convert_type(w << 16, F32)
    hi = lax.bitcast_convert_type(w & jnp.uint32(0xFFFF0000), F32)
    return jnp.concatenate([lo, hi], axis=1).astype(BF16)


def _sort_kernel(idx_ref, off_ref, h_ref, posl_ref, ts_ref):
    tm = h_ref.shape[0]
    idx = idx_ref[...]
    eid = lax.broadcasted_iota(jnp.int32, (N_EXPERTS, tm), 0)
    hits = [eid == idx[k:k + 1, :] for k in range(TOP_K)]
    chosen = hits[0].astype(BF16)
    for hk in hits[1:]:
        chosen = chosen + hk.astype(BF16)
    r = lax.broadcasted_iota(jnp.int32, (tm, tm), 0)
    c = lax.broadcasted_iota(jnp.int32, (tm, tm), 1)
    before = (r < c).astype(BF16)
    rank = jnp.dot(chosen, before, preferred_element_type=F32)
    slot = rank.astype(jnp.int32) + off_ref[...]
    posl = jnp.concatenate([jnp.sum(jnp.where(hk, slot, 0), axis=0, keepdims=True) for hk in hits], axis=0)
    posl_ref[...] = posl
    hb = h_ref[...]
    for rb in range(ts_ref.shape[0] // SORT_CHUNK):
        rows = lax.broadcasted_iota(jnp.int32, (SORT_CHUNK, tm), 0) + rb * SORT_CHUNK
        sel = rows == posl[0:1, :]
        for k in range(1, TOP_K):
            sel = jnp.logical_or(sel, rows == posl[k:k + 1, :])
        onehot = jnp.where(sel, 1.0, 0.0).astype(BF16)
        ts = jnp.dot(onehot, hb, preferred_element_type=F32)
        ts_ref[rb * SORT_CHUNK:(rb + 1) * SORT_CHUNK, :] = _pack_bf16_pairs(ts)


def _sort_rows(idx_t, off, h_flat):
    b, k, l = idx_t.shape
    n, d = h_flat.shape
    nt = l // TM
    return pl.pallas_call(
        _sort_kernel,
        grid=(b * nt,),
        in_specs=[pl.BlockSpec((None, k, TM), lambda i: (i // nt, 0, i % nt)),
                  pl.BlockSpec((None, N_EXPERTS, 1), lambda i: (i, 0, 0)),
                  pl.BlockSpec((TM, d), lambda i: (i, 0))],
        out_specs=[pl.BlockSpec((None, k, TM), lambda i: (i // nt, 0, i % nt)),
                   pl.BlockSpec((None, SORT_ROWS, d // 2), lambda i: (i, 0, 0))],
        out_shape=[jax.ShapeDtypeStruct((b, k, l), jnp.int32),
                   jax.ShapeDtypeStruct((b * nt, SORT_ROWS, d // 2), jnp.uint32)],
        compiler_params=_params(("parallel",)),
        name="sort_rows",
    )(idx_t, off, h_flat)


SLAB_SIZES = tuple(SUBLANES << s for s in range((EXPERT_ROWS // SUBLANES).bit_length()))


def _for_slabs(length, fn):
    off = 0
    for sz in SLAB_SIZES:
        take = (length & sz) != 0

        @pl.when(take)
        def _():
            fn(off, sz)

        off = off + jnp.where(take, sz, 0)


def _expert_kernel(be_ref, nu_ref, r0_ref, r1_ref, ro_ref, rr_ref, rl_ref, cov_ref,
                   ts_in, wg_ref, wu_ref, wd_ref, ts_out, xbuf, ybuf, sem_g, sem_s, *, n_tiles):
    j = pl.program_id(0)
    nu = nu_ref[0]
    blk = xbuf.shape[1]
    slot = j % 2

    def for_pieces(bj, fn):
        lo = bj * blk

        def body(r, carry):
            d, ln = rr_ref[r], rl_ref[r]
            ps, pe = jnp.maximum(d, lo), jnp.minimum(d + ln, lo + blk)
            fn(r % n_tiles, ro_ref[r] + ps - d, ps - lo, pe - ps)
            return carry

        lax.fori_loop(r0_ref[bj], r1_ref[bj], body, 0)

    def rows_of(ref, first, sz):
        return ref.at[pl.ds(pl.multiple_of(first, SUBLANES), sz)]

    def gather(bj, s):
        xbuf[s] = jnp.zeros(xbuf.shape[1:], xbuf.dtype)

        def piece(tile, trow, brow, ln):
            _for_slabs(ln, lambda off, sz: pltpu.make_async_copy(
                rows_of(ts_in.at[tile], trow + off, sz), rows_of(xbuf.at[s], brow + off, sz), sem_g.at[s]).start())

        for_pieces(bj, piece)

    def scatter(bj, s):
        def piece(tile, trow, brow, ln):
            _for_slabs(ln, lambda off, sz: pltpu.make_async_copy(
                rows_of(ybuf.at[s], brow + off, sz), rows_of(ts_out.at[tile], trow + off, sz), sem_s.at[s]).start())

        for_pieces(bj, piece)

    def wait_rows(bj, buf, sem, s):
        _for_slabs(cov_ref[bj], lambda off, sz: pltpu.make_async_copy(
            ts_in.at[0, pl.ds(0, sz)], buf.at[s, pl.ds(0, sz)], sem.at[s]).wait())

    @pl.when(j == 0)
    def _():
        gather(0, 0)

    @pl.when(j + 1 < nu)
    def _():
        gather(j + 1, 1 - slot)

    @pl.when(jnp.logical_and(j >= 2, j - 2 < nu))
    def _():
        wait_rows(j - 2, ybuf, sem_s, slot)

    @pl.when(j < nu)
    def _():
        wait_rows(j, xbuf, sem_g, slot)
        x = _unpack_bf16_pairs(xbuf[slot])
        g = jnp.dot(x, wg_ref[...], preferred_element_type=F32)
        u = jnp.dot(x, wu_ref[...], preferred_element_type=F32)
        a = (g * _sigmoid(g) * u).astype(BF16)
        y = jnp.dot(a, wd_ref[...], preferred_element_type=F32)
        ybuf[slot] = _pack_bf16_pairs(y.astype(BF16).astype(F32))
        scatter(j, slot)


def _experts(plan, tiles, wg, wu, wd):
    nt, _, w = tiles.shape
    blk = EXPERT_ROWS
    d, de = wg.shape[1:]
    n_blocks = plan["block_e"].shape[0]
    tables = (plan["block_e"], plan["n_used"], plan["blk_run0"], plan["blk_run1"],
              plan["run_tile_off"], plan["run_row"], plan["run_len"], plan["blk_rows"])
    wspec = lambda shape: pl.BlockSpec((None,) + shape, lambda i, be, *_: (be[i], 0, 0))
    grid_spec = pltpu.PrefetchScalarGridSpec(
        num_scalar_prefetch=len(tables),
        grid=(n_blocks,),
        in_specs=[pl.BlockSpec(memory_space=pl.ANY), wspec((d, de)), wspec((d, de)), wspec((de, d))],
        out_specs=pl.BlockSpec(memory_space=pl.ANY),
        scratch_shapes=[pltpu.VMEM((2, blk, w), tiles.dtype), pltpu.VMEM((2, blk, w), tiles.dtype),
                        pltpu.SemaphoreType.DMA((2,)), pltpu.SemaphoreType.DMA((2,))])
    return pl.pallas_call(
        functools.partial(_expert_kernel, n_tiles=nt),
        grid_spec=grid_spec,
        out_shape=jax.ShapeDtypeStruct(tiles.shape, tiles.dtype),
        input_output_aliases={len(tables): 0},
        compiler_params=_params(("arbitrary",)),
        name="experts",
    )(*tables, tiles, wg, wu, wd)


def _combine_kernel(ts_ref, posl_ref, w_ref, x_ref, h_ref, mod_ref, wsg_ref, wsu_ref, wsd_ref, o_ref):
    tm = x_ref.shape[0]
    hb = h_ref[...]
    g = jnp.dot(hb, wsg_ref[...], preferred_element_type=F32)
    u = jnp.dot(hb, wsu_ref[...], preferred_element_type=F32)
    acc = jnp.dot((g * _sigmoid(g) * u).astype(BF16), wsd_ref[...], preferred_element_type=F32)
    posl = posl_ref[...]
    w = w_ref[...]
    for rb in range(ts_ref.shape[0] // SORT_CHUNK):
        cols = lax.broadcasted_iota(jnp.int32, (tm, SORT_CHUNK), 1) + rb * SORT_CHUNK
        wm = jnp.zeros((tm, SORT_CHUNK), F32)
        for k in range(TOP_K):
            wm = jnp.where(cols == posl[:, k:k + 1], w[:, k:k + 1], wm)
        w_hi = wm.astype(BF16)
        w_lo = (wm - w_hi.astype(F32)).astype(BF16)
        ys = _unpack_bf16_pairs(ts_ref[rb * SORT_CHUNK:(rb + 1) * SORT_CHUNK, :])
        acc = acc + jnp.dot(w_hi, ys, preferred_element_type=F32) + jnp.dot(w_lo, ys, preferred_element_type=F32)
    o_ref[...] = x_ref[...] + mod_ref[5:6, :] * acc


def _combine(tiles, posl_tm, wts, x_flat, h_flat, mods, wsg, wsu, wsd, tiles_per_sample, n_ctx_tiles):
    n, d = x_flat.shape
    ds_ = wsg.shape[-1]
    tok = lambda i: (i, 0)
    const = lambda i: (0, 0)

    def mod_idx(i):
        return (i // tiles_per_sample, jnp.where(i % tiles_per_sample >= n_ctx_tiles, 1, 0), 0, 0)

    return pl.pallas_call(
        _combine_kernel,
        grid=(n // TM,),
        in_specs=[pl.BlockSpec((None,) + tiles.shape[1:], lambda i: (i, 0, 0)),
                  pl.BlockSpec((TM, TOP_K), tok),
                  pl.BlockSpec((TM, TOP_K), tok),
                  pl.BlockSpec((TM, d), tok),
                  pl.BlockSpec((TM, d), tok),
                  pl.BlockSpec((None, None, 8, d), mod_idx),
                  pl.BlockSpec((d, ds_), const), pl.BlockSpec((d, ds_), const), pl.BlockSpec((ds_, d), const)],
        out_specs=pl.BlockSpec((TM, d), tok),
        out_shape=jax.ShapeDtypeStruct((n, d), F32),
        compiler_params=_params(("parallel",)),
        name="combine",
    )(tiles, posl_tm, wts, x_flat, h_flat, mods, wsg, wsu, wsd)


def _final_kernel(x_ref, g_ref, o_ref):
    o_ref[...] = _rms(x_ref[...], g_ref[...])


def _final_norm(xs, g_final, n_ctx_tiles):
    b, l, d = xs.shape
    s_len = l - n_ctx_tiles * TM
    return pl.pallas_call(
        _final_kernel,
        grid=(b, s_len // TM),
        in_specs=[pl.BlockSpec((None, TM, d), lambda bi, ti: (bi, ti + n_ctx_tiles, 0)),
                  pl.BlockSpec((1, d), lambda bi, ti: (0, 0))],
        out_specs=pl.BlockSpec((None, TM, d), lambda bi, ti: (bi, ti, 0)),
        out_shape=jax.ShapeDtypeStruct((b, s_len, d), F32),
        compiler_params=_params(("parallel", "parallel")),
        name="final_norm",
    )(xs, g_final)


def _moe_plan(cnt, n_assign, blk):
    nt = cnt.shape[0]
    run = (cnt + SUBLANES - 1) // SUBLANES * SUBLANES
    tile_off = jnp.cumsum(run, axis=1) - run
    tot = jnp.sum(run, axis=0)
    padded = (tot + blk - 1) // blk * blk
    pad_end = jnp.cumsum(padded)
    pad_start = pad_end - padded
    row = (pad_start[None, :] + jnp.cumsum(run, axis=0) - run).T.reshape(-1)
    run_len = run.T.reshape(-1)
    n_blocks = -(-(n_assign + nt * N_EXPERTS * (SUBLANES - 1)) // blk) + N_EXPERTS + 2
    first_row = jnp.arange(n_blocks, dtype=jnp.int32) * blk
    count = lambda m: jnp.sum(m.astype(jnp.int32), axis=1)
    block_e = jnp.minimum(count(pad_end[None, :] <= first_row[:, None]), N_EXPERTS - 1)
    i32 = lambda a: a.astype(jnp.int32)
    return dict(
        tile_off=i32(tile_off)[:, :, None],
        run_tile_off=i32(tile_off.T.reshape(-1)), run_row=i32(row), run_len=i32(run_len),
        blk_run0=count((row + run_len)[None, :] <= first_row[:, None]),
        blk_run1=count(row[None, :] < first_row[:, None] + blk),
        blk_rows=i32(jnp.clip((pad_start + tot)[block_e] - first_row, 0, blk)),
        block_e=i32(block_e), n_used=i32(pad_end[-1] // blk).reshape(1))


def _rope_tables(s_len, n_ctx):
    rows = s_len // GRID_W
    row = jnp.repeat(jnp.arange(rows), GRID_W).astype(F32)
    col = jnp.tile(jnp.arange(GRID_W), rows).astype(F32)
    quarter = HEAD_DIM // 4
    inv = 1.0 / (ROPE_BASE ** (jnp.arange(quarter, dtype=F32) / quarter))
    ar, ac = row[:, None] * inv, col[:, None] * inv
    cr, sr, cc, sc = jnp.cos(ar), jnp.sin(ar), jnp.cos(ac), jnp.sin(ac)
    z = jnp.zeros_like(sr)
    cos = jnp.concatenate([cr, cr, cc, cc], axis=1)
    sa = jnp.concatenate([z, sr, z, sc], axis=1)
    sb = jnp.concatenate([-sr, z, -sc, z], axis=1)
    rep = LANES // HEAD_DIM

    def full(t, fill):
        t = jnp.tile(t, (1, rep))
        return jnp.concatenate([jnp.full((n_ctx, LANES), fill, F32), t], axis=0)

    return full(cos, 1.0), full(sa, 0.0), full(sb, 0.0)


def _pair_perm():
    g = N_HEADS_A // KV_HEADS_A
    heads = [h for t in range(g) for h in (t, t + g)]
    return jnp.concatenate([jnp.arange(h * HEAD_DIM, (h + 1) * HEAD_DIM) for h in heads])


def _split_w_in(w):
    a_q, a_kv = N_HEADS_A * HEAD_DIM, KV_HEADS_A * HEAD_DIM
    b_w = N_HEADS_B * 2 * HEAD_DIM
    c_w = N_HEADS_C * HEAD_DIM_C
    sizes = (a_q, a_kv, a_kv, b_w, b_w, b_w, c_w, c_w, c_w, c_w, 4 * N_HEADS_C, w.shape[1])
    parts, start = [], 0
    for sz in sizes[:-1]:
        parts.append(w[:, start:start + sz])
        start += sz
    parts.append(w[:, start:])
    return parts


def _pack_w_in(w):
    d = w.shape[0]
    aq, ak, av, bq, bk, bv, cq, ck, cv, co, cg, gt = _split_w_in(w)
    pad = lambda n: jnp.zeros((d, n), w.dtype)
    kva = jnp.concatenate([ak, av, cg, pad(TILE_N - ak.shape[1] - av.shape[1] - cg.shape[1])], axis=1)
    big = jnp.concatenate([aq[:, _pair_perm()], bq, bk, kva, bv, co, gt, cq, cv], axis=1)
    return big.astype(BF16), ck.T.astype(BF16)


def kernel(x, c, ctx, c_ctx, w_mod, b_mod, g_mix, g_ffn, w_in, b_gate, sink, lam_q1, lam_k1, lam_q2, lam_k2,
           g_diff, g_mlstm, w_a, w_b, w_c, w_out, w_router, b_router, w_exp_gate, w_exp_up, w_exp_down,
           w_sh_gate, w_sh_up, w_sh_down, g_final):
    b, s_len, d = x.shape
    n_ctx = ctx.shape[1]
    l = n_ctx + s_len
    depth = w_mod.shape[0]
    n_ctx_tiles = n_ctx // TM
    assert n_ctx % TM == 0 and s_len % TM == 0 and d % LANES == 0 and s_len % GRID_W == 0

    xs = jnp.concatenate([ctx, x], axis=1)
    cos, sa, sb = _rope_tables(s_len, n_ctx)

    rows_c = 16
    cs = jnp.concatenate([c, c_ctx[None], jnp.zeros((rows_c - b - 1, d), F32)], axis=0)
    mod_all = _mod_vectors(cs, w_mod, b_mod).reshape(depth, rows_c, N_MOD, d)
    mod_all = jnp.pad(mod_all, ((0, 0), (0, 0), (0, 8 - N_MOD), (0, 0)))

    perm = _pair_perm()
    for layer in range(depth):
        lam_init = 0.8 - 0.6 * math.exp(-0.3 * layer)
        mods = jnp.stack([jnp.broadcast_to(mod_all[layer, b], (b, 8, d)), mod_all[layer, :b]], axis=1)
        w_big, w_kt = _pack_w_in(w_in[layer])
        p, gates, kt = _inproj(xs, mods, g_mix[layer][None], w_big, w_kt, cos, sa, sb, n_ctx_tiles)

        oa = _mixer_a(p, sink[layer], n_ctx)
        lam_params = jnp.stack([lam_q1[layer], lam_k1[layer], lam_q2[layer], lam_k2[layer]])
        ob_ctx, ob_lat = _mixer_b(p, lam_params, g_diff[layer][None], lam_init, n_ctx)

        bias = b_gate[layer].reshape(-1)
        bias_row = jnp.pad(bias, (0, LANES - bias.shape[0]))[None]
        bias_col = jnp.broadcast_to(bias[:, None], (bias.shape[0], LANES))
        gates_t = jnp.transpose(gates[:, :, :bias.shape[0]], (0, 2, 1))
        hm = _mlstm(p, kt, gates, gates_t, bias_row, bias_col, n_ctx)

        xs = _merge(xs, mods, oa, ob_ctx, ob_lat, hm, p, g_mlstm[layer][None],
                    w_a[layer][perm].astype(BF16), w_b[layer].astype(BF16), w_c[layer].astype(BF16),
                    w_out[layer].astype(BF16), n_ctx_tiles)

        h, idx_t, wt_t, cnt = _router(xs, mods, g_ffn[layer][None], w_router[layer].T, b_router[layer][:, None],
                                      n_ctx_tiles)
        plan = _moe_plan(cnt.reshape(-1, N_EXPERTS), b * l * TOP_K, EXPERT_ROWS)
        h_flat = h.reshape(b * l, d)
        posl, tiles = _sort_rows(idx_t, plan["tile_off"], h_flat)
        tiles = _experts(plan, tiles, w_exp_gate[layer].astype(BF16), w_exp_up[layer].astype(BF16),
                         w_exp_down[layer].astype(BF16))
        to_rows = lambda a: jnp.transpose(a, (0, 2, 1)).reshape(b * l, TOP_K)
        xs = _combine(tiles, to_rows(posl), to_rows(wt_t), xs.reshape(b * l, d), h_flat, mods,
                      w_sh_gate[layer].astype(BF16), w_sh_up[layer].astype(BF16), w_sh_down[layer].astype(BF16),
                      l // TM, n_ctx_tiles).reshape(b, l, d)
    return _final_norm(xs, g_final[None], n_ctx_tiles)
```

```python
import functools
import math

import jax
import jax.numpy as jnp
from jax import lax
from jax.experimental import pallas as pl
from jax.experimental.pallas import tpu as pltpu

F32 = jnp.float32
BF16 = jnp.bfloat16
HIGHEST = lax.Precision.HIGHEST

GRID_W = 64
N_MOD = 6
HEAD_DIM = 64
N_HEADS_A = 8
KV_HEADS_A = 2
WINDOW = 128
N_HEADS_B = 4
N_HEADS_C = 4
HEAD_DIM_C = 128
N_EXPERTS = 64
N_GROUPS = 8
TOPK_GROUPS = 4
TOP_K = 8
ROUTED_SCALE = 2.5
ROPE_BASE = 10000.0
EPS = 1e-6

LANES = 128
CHUNK = 128
TILE_N = 512
TM = 256
MIXB_KEYS = 1024
MIXB_QUERIES = 512
EXPERT_ROWS = 512
SUBLANES = 8
SORT_ROWS = TOP_K * TM + N_EXPERTS * SUBLANES
SORT_CHUNK = 512
NEG = -1e30
VMEM_LIMIT = 56 * 1024 * 1024

T_AQ, T_BQ, T_BK, T_KVA, T_BV, T_CO, T_GT, T_CQ, T_CV, N_TILES = 0, 1, 2, 3, 4, 5, 6, 12, 13, 14

NT_DIMS = (((1,), (1,)), ((), ()))


def _params(sem):
    return pltpu.CompilerParams(dimension_semantics=sem, vmem_limit_bytes=VMEM_LIMIT)


def _rms(x, g):
    return x * lax.rsqrt(jnp.mean(x * x, axis=-1, keepdims=True) + EPS) * g


def _sigmoid(x):
    return jax.nn.sigmoid(x)


def _mod_kernel(c_ref, w_ref, b_ref, o_ref):
    c = c_ref[...]
    s = c * _sigmoid(c)
    o_ref[...] = jnp.dot(s, w_ref[...], precision=HIGHEST, preferred_element_type=F32) + b_ref[...]


def _mod_vectors(cs, w_mod, b_mod):
    depth, d, n = w_mod.shape
    r = cs.shape[0]
    tn = 3 * LANES
    return pl.pallas_call(
        _mod_kernel,
        grid=(depth, n // tn),
        in_specs=[pl.BlockSpec((r, d), lambda l, j: (0, 0)),
                  pl.BlockSpec((None, d, tn), lambda l, j: (l, 0, j)),
                  pl.BlockSpec((None, 1, tn), lambda l, j: (l, 0, j))],
        out_specs=pl.BlockSpec((None, r, tn), lambda l, j: (l, 0, j)),
        out_shape=jax.ShapeDtypeStruct((depth, r, n), F32),
        compiler_params=_params(("parallel", "parallel")),
        name="mod_vectors",
    )(cs, w_mod, b_mod.reshape(depth, 1, n))


def _inproj_kernel(x_ref, mod_ref, g_ref, w_ref, wkt_ref, cos_ref, sa_ref, sb_ref, p_ref, gate_ref, kt_ref):
    x = x_ref[...]
    h = _rms(x, g_ref[...]) * (1.0 + mod_ref[1:2, :]) + mod_ref[0:1, :]
    hb = h.astype(BF16)
    cos, sa, sb = cos_ref[...], sa_ref[...], sb_ref[...]

    def rope(t):
        return t * cos + pltpu.roll(t, 16, 1) * sa + pltpu.roll(t, LANES - 16, 1) * sb

    q_scale = HEAD_DIM ** -0.5
    for j in range(N_TILES):
        acc = jnp.dot(hb, w_ref[:, j * TILE_N:(j + 1) * TILE_N], preferred_element_type=F32)
        parts = [acc[:, s * LANES:(s + 1) * LANES] for s in range(TILE_N // LANES)]
        if j == T_AQ:
            parts = [rope(t) * q_scale for t in parts]
        elif j == T_BQ:
            parts = [rope(t) * (q_scale * math.log2(math.e)) for t in parts]
        elif j == T_BK:
            parts = [rope(t) for t in parts]
        elif j == T_KVA:
            gate_ref[...] = parts[2]
            parts[0] = rope(parts[0])
        for s, t in enumerate(parts):
            p_ref[:, j * TILE_N + s * LANES:j * TILE_N + (s + 1) * LANES] = t.astype(BF16)
    kt = lax.dot_general(wkt_ref[...], hb, NT_DIMS, preferred_element_type=F32)
    kt_ref[...] = (kt * (HEAD_DIM_C ** -0.5)).astype(BF16)


def _inproj(xs, mods, g_mix, w_big, w_kt, cos, sa, sb, n_ctx_tiles):
    b, l, d = xs.shape
    npad = w_big.shape[1]
    ck = w_kt.shape[0]
    grid = (b, l // TM)
    tok = lambda bi, ti: (bi, ti, 0)
    return pl.pallas_call(
        _inproj_kernel,
        grid=grid,
        in_specs=[pl.BlockSpec((None, TM, d), tok),
                  pl.BlockSpec((None, None, 8, d), lambda bi, ti: (bi, jnp.where(ti >= n_ctx_tiles, 1, 0), 0, 0)),
                  pl.BlockSpec((1, d), lambda bi, ti: (0, 0)),
                  pl.BlockSpec((d, npad), lambda bi, ti: (0, 0), pipeline_mode=pl.Buffered(1)),
                  pl.BlockSpec((ck, d), lambda bi, ti: (0, 0), pipeline_mode=pl.Buffered(1)),
                  pl.BlockSpec((TM, LANES), lambda bi, ti: (ti, 0)),
                  pl.BlockSpec((TM, LANES), lambda bi, ti: (ti, 0)),
                  pl.BlockSpec((TM, LANES), lambda bi, ti: (ti, 0))],
        out_specs=[pl.BlockSpec((None, TM, npad), tok),
                   pl.BlockSpec((None, TM, LANES), tok),
                   pl.BlockSpec((None, ck, TM), lambda bi, ti: (bi, 0, ti))],
        out_shape=[jax.ShapeDtypeStruct((b, l, npad), BF16),
                   jax.ShapeDtypeStruct((b, l, LANES), F32),
                   jax.ShapeDtypeStruct((b, ck, l), BF16)],
        compiler_params=_params(("parallel", "parallel")),
        name="inproj",
    )(xs, mods, g_mix, w_big, w_kt, cos, sa, sb)


def _mixa_kernel(sink_ref, q_ref, kp_ref, kc_ref, kn_ref, kx_ref, o_ref, *, n_ctx_blocks, n_blocks):
    i = pl.program_id(1)
    lat = i >= n_ctx_blocks
    has_prev = jnp.logical_and(lat, i > n_ctx_blocks)
    has_next = jnp.logical_and(lat, i < n_blocks - 1)
    r = lax.broadcasted_iota(jnp.int32, (CHUNK, CHUNK), 0)
    c = lax.broadcasted_iota(jnp.int32, (CHUNK, CHUNK), 1)
    n_ctx = kx_ref.shape[0]
    valid = jnp.concatenate([
        jnp.logical_and(c >= r, has_prev),
        jnp.broadcast_to(lat, (CHUNK, CHUNK)),
        jnp.logical_and(c <= r, has_next),
        jnp.ones((CHUNK, n_ctx), jnp.bool_)], axis=1)
    kcat = jnp.concatenate([kp_ref[:, :LANES], kc_ref[:, :LANES], kn_ref[:, :LANES], kx_ref[:, :LANES]], axis=0)
    vcat = jnp.concatenate([kp_ref[:, LANES:], kc_ref[:, LANES:], kn_ref[:, LANES:], kx_ref[:, LANES:]], axis=0)
    lane = lax.broadcasted_iota(jnp.int32, (CHUNK, LANES), 1)
    low = lane < HEAD_DIM
    n_pairs = N_HEADS_A // KV_HEADS_A
    outs = []
    for gk in range(KV_HEADS_A):
        keep = low if gk == 0 else jnp.logical_not(low)
        zero = jnp.zeros((CHUNK, LANES), BF16)
        lhs = jnp.concatenate([jnp.where(keep, q_ref[:, t * LANES:(t + 1) * LANES], zero) for t in range(n_pairs)],
                              axis=0)
        s = lax.dot_general(lhs, kcat, NT_DIMS, preferred_element_type=F32)
        o_g = []
        for t in range(n_pairs):
            st = jnp.where(valid, s[t * CHUNK:(t + 1) * CHUNK], NEG)
            sk = sink_ref[gk * n_pairs + t]
            m = jnp.maximum(jnp.max(st, axis=-1, keepdims=True), sk)
            p = jnp.exp(st - m)
            den = jnp.sum(p, axis=-1, keepdims=True) + jnp.exp(sk - m)
            o_g.append(jnp.dot(p.astype(BF16), vcat, preferred_element_type=F32) / den)
        outs.append(o_g)
    for t in range(n_pairs):
        o_ref[:, t * LANES:(t + 1) * LANES] = jnp.where(low, outs[0][t], outs[1][t]).astype(BF16)


def _mixer_a(p, sink, n_ctx):
    b, l, _ = p.shape
    nb = l // CHUNK
    ncb = n_ctx // CHUNK
    kvw = 2 * LANES
    kv_col = T_KVA * TILE_N // kvw
    aq_w = N_HEADS_A * HEAD_DIM
    kern = functools.partial(_mixa_kernel, n_ctx_blocks=ncb, n_blocks=nb)
    return pl.pallas_call(
        kern,
        grid=(b, nb),
        in_specs=[pl.BlockSpec(memory_space=pltpu.SMEM),
                  pl.BlockSpec((None, CHUNK, aq_w), lambda bi, i: (bi, i, T_AQ)),
                  pl.BlockSpec((None, CHUNK, kvw), lambda bi, i: (bi, jnp.maximum(i - 1, 0), kv_col)),
                  pl.BlockSpec((None, CHUNK, kvw), lambda bi, i: (bi, i, kv_col)),
                  pl.BlockSpec((None, CHUNK, kvw), lambda bi, i: (bi, jnp.minimum(i + 1, nb - 1), kv_col)),
                  pl.BlockSpec((None, n_ctx, kvw), lambda bi, i: (bi, 0, kv_col))],
        out_specs=pl.BlockSpec((None, CHUNK, aq_w), lambda bi, i: (bi, i, 0)),
        out_shape=jax.ShapeDtypeStruct((b, l, aq_w), BF16),
        compiler_params=_params(("parallel", "parallel")),
        name="mixer_a",
    )(sink, p, p, p, p, p)


def _fold_lanes(op, acc, s):
    for t in range(s.shape[1] // LANES):
        acc = op(acc, s[:, t * LANES:(t + 1) * LANES])
    return acc


def _mixb_kernel(lam_ref, gd_ref, k_ref, v_ref, *rest, lam_init, chunks):
    q_refs, (o_ref, s_scr, va_scr) = rest[:-3], rest[-3:]

    @pl.when(pl.program_id(2) == 0)
    def _():
        n_keys = v_ref.shape[0]
        va_scr[:, :LANES] = v_ref[...]
        va_scr[:, LANES:] = (lax.broadcasted_iota(jnp.int32, (n_keys, LANES), 1) == 0).astype(BF16)

    lp = lam_ref[...]
    lam = (jnp.exp(jnp.sum(lp[0:1] * lp[1:2], axis=-1, keepdims=True))
           - jnp.exp(jnp.sum(lp[2:3] * lp[3:4], axis=-1, keepdims=True)) + lam_init)
    q = jnp.concatenate([qr[...] for qr in q_refs], axis=0)
    tq = q.shape[0]
    lane = lax.broadcasted_iota(jnp.int32, (tq, LANES), 1)
    zero = jnp.zeros_like(q)
    qs = (jnp.where(lane < HEAD_DIM, q, zero), jnp.where(lane >= HEAD_DIM, q, zero))
    rows = [slice(mi * tq, (mi + 1) * tq) for mi in range(2)]
    mrun = [jnp.full((tq, LANES), NEG, F32) for _ in range(2)]
    for off, sz in chunks:
        for mi in range(2):
            s_scr[rows[mi], off:off + sz] = lax.dot_general(qs[mi], k_ref[off:off + sz, :], NT_DIMS,
                                                            preferred_element_type=F32)
            mrun[mi] = _fold_lanes(jnp.maximum, mrun[mi], s_scr[rows[mi], off:off + sz])
    m = [jnp.max(mr, axis=-1, keepdims=True) for mr in mrun]
    acc = [jnp.zeros((tq, 2 * LANES), F32) for _ in range(2)]
    for off, sz in chunks:
        for mi in range(2):
            pr = jnp.exp2(s_scr[rows[mi], off:off + sz] - m[mi])
            acc[mi] = acc[mi] + jnp.dot(pr.astype(BF16), va_scr[off:off + sz, :], preferred_element_type=F32)
    outs = [a[:, :LANES] / a[:, LANES:LANES + 1] for a in acc]
    o = outs[0] - lam * outs[1]
    o_ref[...] = (_rms(o, gd_ref[...]) * (1.0 - lam_init)).astype(BF16)


def _mixer_b(p, lam_params, g_diff, lam_init, n_ctx):
    b, l, _ = p.shape
    kl = min(MIXB_KEYS, l - n_ctx)
    tq_lat = min(MIXB_QUERIES, l - n_ctx)
    assert (l - n_ctx) % kl == 0 and (l - n_ctx) % tq_lat == 0 and tq_lat % TM == 0 and n_ctx % TM == 0
    q0 = T_BQ * TILE_N // LANES
    k0 = T_BK * TILE_N // LANES
    v0 = T_BV * TILE_N // LANES
    ctx_chunks = ((0, n_ctx),)
    all_chunks = ctx_chunks + tuple((n_ctx + c * kl, kl) for c in range((l - n_ctx) // kl))

    def call(chunks, n_keys, tq, q_tiles, first_row):
        nq = tq // TM
        q_spec = lambda part: pl.BlockSpec((None, TM, LANES),
                                           lambda bi, h, qi: (bi, first_row // TM + qi * nq + part, q0 + h))
        kern = functools.partial(_mixb_kernel, lam_init=lam_init, chunks=chunks)
        return pl.pallas_call(
            kern,
            grid=(b, N_HEADS_B, q_tiles),
            in_specs=[pl.BlockSpec((4, HEAD_DIM), lambda bi, h, qi: (0, 0)),
                      pl.BlockSpec((1, LANES), lambda bi, h, qi: (0, 0)),
                      pl.BlockSpec((None, n_keys, LANES), lambda bi, h, qi: (bi, 0, k0 + h)),
                      pl.BlockSpec((None, n_keys, LANES), lambda bi, h, qi: (bi, 0, v0 + h))]
                     + [q_spec(part) for part in range(nq)],
            out_specs=pl.BlockSpec((None, tq, LANES), lambda bi, h, qi: (bi, qi, h)),
            out_shape=jax.ShapeDtypeStruct((b, q_tiles * tq, N_HEADS_B * LANES), BF16),
            scratch_shapes=[pltpu.VMEM((2 * tq, n_keys), F32), pltpu.VMEM((n_keys, 2 * LANES), BF16)],
            compiler_params=_params(("parallel", "parallel", "arbitrary")),
            name="mixer_b",
        )(lam_params, g_diff, p, p, *([p] * nq))

    tq_ctx = min(tq_lat, n_ctx)
    return (call(ctx_chunks, n_ctx, tq_ctx, n_ctx // tq_ctx, 0),
            call(all_chunks, l, tq_lat, (l - n_ctx) // tq_lat, n_ctx))


def _log_sigmoid(x):
    return jnp.minimum(x, 0.0) - jnp.log1p(jnp.exp(-jnp.abs(x)))


def _mlstm_kernel(q_ref, kt_ref, v_ref, gc_ref, gr_ref, bc_ref, br_ref, o_ref, s_scr, m_scr):
    d = pl.program_id(1)
    c = pl.program_id(2)

    @pl.when(c == 0)
    def _():
        s_scr[...] = jnp.zeros_like(s_scr)
        m_scr[...] = jnp.zeros_like(m_scr)

    fwd = d == 0
    r = lax.broadcasted_iota(jnp.int32, (CHUNK, CHUNK), 0)
    cc = lax.broadcasted_iota(jnp.int32, (CHUNK, CHUNK), 1)
    tri = jnp.where(fwd, r - cc, cc - r) >= 0
    trif = tri.astype(F32)
    gcol = gc_ref[...] + bc_ref[...]
    grow = gr_ref[...] + br_ref[...]
    lf_col = _log_sigmoid(gcol)
    lf_row = _log_sigmoid(grow)
    bcum_col = jnp.dot(trif, lf_col, precision=HIGHEST, preferred_element_type=F32)
    bcum_row = lax.dot_general(lf_row, trif, NT_DIMS, precision=HIGHEST, preferred_element_type=F32)
    tot_row = jnp.sum(lf_row, axis=-1, keepdims=True)
    lane = lax.broadcasted_iota(jnp.int32, (CHUNK, LANES), 1)
    ones_col = (lane == 0).astype(BF16)
    nh = N_HEADS_C

    for h in range(nh):
        def pick_col(a, kind):
            return jnp.where(fwd, a[:, kind * nh + h:kind * nh + h + 1],
                             a[:, (kind + 2) * nh + h:(kind + 2) * nh + h + 1])

        def pick_row(a, kind):
            return jnp.where(fwd, a[kind * nh + h:kind * nh + h + 1, :],
                             a[(kind + 2) * nh + h:(kind + 2) * nh + h + 1, :])

        ic_row = pick_row(grow, 0)
        b_col = pick_col(bcum_col, 1)
        b_row = pick_row(bcum_row, 1)
        total = pick_row(tot_row, 1)
        m_st = m_scr[h, 0:1, 0:1]
        dm = jnp.where(tri, b_col - b_row + ic_row, NEG)
        inter = b_col + m_st
        m_t = jnp.maximum(inter, jnp.max(dm, axis=-1, keepdims=True))
        e = jnp.exp(dm - m_t)
        qh = q_ref[:, h * LANES:(h + 1) * LANES]
        kth = kt_ref[h * LANES:(h + 1) * LANES, :]
        vaug = jnp.concatenate([v_ref[:, h * LANES:(h + 1) * LANES], ones_col], axis=1)
        s = jnp.dot(qh, kth, preferred_element_type=F32) * e
        st = s_scr[h]
        intra = jnp.dot(s.astype(BF16), vaug, preferred_element_type=F32)
        cross = jnp.dot(qh, st.astype(BF16), preferred_element_type=F32)
        nd = intra + jnp.exp(inter - m_t) * cross
        den = nd[:, LANES:LANES + 1]
        o_ref[:, h * LANES:(h + 1) * LANES] = nd[:, :LANES] / jnp.maximum(jnp.abs(den), jnp.exp(-m_t))
        gs_row = total - b_row + ic_row
        m_new = jnp.maximum(total + m_st, jnp.max(gs_row, axis=-1, keepdims=True))
        decay = jnp.exp(total + m_st - m_new)
        wkt = (kth.astype(F32) * jnp.exp(gs_row - m_new)).astype(BF16)
        s_scr[h] = decay * st + jnp.dot(wkt, vaug, preferred_element_type=F32)
        m_scr[h] = jnp.broadcast_to(m_new, m_scr.shape[1:])


def _mlstm(p, kt, gates, gates_t, bias_row, bias_col, n_ctx):
    b, l, _ = p.shape
    nc = l // CHUNK
    ncc = n_ctx // CHUNK
    cw = N_HEADS_C * HEAD_DIM_C

    def chunk(d, c):
        rev = jnp.where(c < ncc, ncc - 1 - c, nc + ncc - 1 - c)
        return jnp.where(d == 0, c, rev)

    return pl.pallas_call(
        _mlstm_kernel,
        grid=(b, 2, nc),
        in_specs=[pl.BlockSpec((None, CHUNK, cw), lambda bi, d, c: (bi, chunk(d, c), T_CQ)),
                  pl.BlockSpec((None, cw, CHUNK), lambda bi, d, c: (bi, 0, chunk(d, c))),
                  pl.BlockSpec((None, CHUNK, cw), lambda bi, d, c: (bi, chunk(d, c), T_CV)),
                  pl.BlockSpec((None, CHUNK, LANES), lambda bi, d, c: (bi, chunk(d, c), 0)),
                  pl.BlockSpec((None, 16, CHUNK), lambda bi, d, c: (bi, 0, chunk(d, c))),
                  pl.BlockSpec((1, LANES), lambda bi, d, c: (0, 0)),
                  pl.BlockSpec((16, LANES), lambda bi, d, c: (0, 0))],
        out_specs=pl.BlockSpec((None, None, CHUNK, cw), lambda bi, d, c: (d, bi, chunk(d, c), 0)),
        out_shape=jax.ShapeDtypeStruct((2, b, l, cw), F32),
        scratch_shapes=[pltpu.VMEM((N_HEADS_C, HEAD_DIM_C, 2 * LANES), F32),
                        pltpu.VMEM((N_HEADS_C, 8, LANES), F32)],
        compiler_params=_params(("parallel", "parallel", "arbitrary")),
        name="mlstm",
    )(p, kt, p, gates, gates_t, bias_row, bias_col)


def _merge_kernel(x_ref, mod_ref, oa_ref, obc_ref, obl_ref, hf_ref, hb_ref, co_ref, gt_ref, gm_ref,
                  wa_ref, wb_ref, wc_ref, wo_ref, xo_ref, *, n_ctx_tiles):
    d = x_ref.shape[-1]
    ob = jnp.where(pl.program_id(1) < n_ctx_tiles, obc_ref[...], obl_ref[...])
    hs = hf_ref[...] + hb_ref[...]
    co = co_ref[...].astype(F32)
    gm = gm_ref[...]
    oc = []
    for h in range(N_HEADS_C):
        sl = slice(h * LANES, (h + 1) * LANES)
        oc.append((_rms(hs[:, sl], gm[:, sl]) * _sigmoid(co[:, sl])).astype(BF16))
    oc = jnp.concatenate(oc, axis=1)
    y = (_sigmoid(gt_ref[:, 0:d].astype(F32)) * jnp.dot(oa_ref[...], wa_ref[...], preferred_element_type=F32)
         + _sigmoid(gt_ref[:, d:2 * d].astype(F32)) * jnp.dot(ob, wb_ref[...], preferred_element_type=F32)
         + _sigmoid(gt_ref[:, 2 * d:3 * d].astype(F32)) * jnp.dot(oc, wc_ref[...], preferred_element_type=F32))
    out = jnp.dot(y.astype(BF16), wo_ref[...], preferred_element_type=F32)
    xo_ref[...] = x_ref[...] + mod_ref[2:3, :] * out


def _merge(xs, mods, oa, ob_ctx, ob_lat, hm, p, g_mlstm, wa, wb, wc, wo, n_ctx_tiles):
    b, l, d = xs.shape
    tok = lambda bi, ti: (bi, ti, 0)
    cw = N_HEADS_C * HEAD_DIM_C
    const = lambda bi, ti: (0, 0)
    return pl.pallas_call(
        functools.partial(_merge_kernel, n_ctx_tiles=n_ctx_tiles),
        grid=(b, l // TM),
        in_specs=[pl.BlockSpec((None, TM, d), tok),
                  pl.BlockSpec((None, None, 8, d), lambda bi, ti: (bi, jnp.where(ti >= n_ctx_tiles, 1, 0), 0, 0)),
                  pl.BlockSpec((None, TM, oa.shape[-1]), tok),
                  pl.BlockSpec((None, TM, ob_ctx.shape[-1]), lambda bi, ti: (bi, jnp.minimum(ti, n_ctx_tiles - 1), 0)),
                  pl.BlockSpec((None, TM, ob_lat.shape[-1]), lambda bi, ti: (bi, jnp.maximum(ti - n_ctx_tiles, 0), 0)),
                  pl.BlockSpec((None, None, TM, cw), lambda bi, ti: (0, bi, ti, 0)),
                  pl.BlockSpec((None, None, TM, cw), lambda bi, ti: (1, bi, ti, 0)),
                  pl.BlockSpec((None, TM, cw), lambda bi, ti: (bi, ti, T_CO)),
                  pl.BlockSpec((None, TM, 3 * d), lambda bi, ti: (bi, ti, T_GT * TILE_N // (3 * d))),
                  pl.BlockSpec((1, cw), const),
                  pl.BlockSpec(wa.shape, const), pl.BlockSpec(wb.shape, const),
                  pl.BlockSpec(wc.shape, const), pl.BlockSpec(wo.shape, const)],
        out_specs=pl.BlockSpec((None, TM, d), tok),
        out_shape=jax.ShapeDtypeStruct((b, l, d), F32),
        compiler_params=_params(("parallel", "parallel")),
        name="merge",
    )(xs, mods, oa, ob_ctx, ob_lat, hm, hm, p, p, g_mlstm, wa, wb, wc, wo)


def _router_kernel(x_ref, mod_ref, g_ref, wrt_ref, br_ref, h_ref, idx_ref, wt_ref, cnt_ref):
    h = _rms(x_ref[...], g_ref[...]) * (1.0 + mod_ref[4:5, :]) + mod_ref[3:4, :]
    h_ref[...] = h.astype(BF16)
    tm = h.shape[0]
    per = N_EXPERTS // N_GROUPS
    lt = lax.dot_general(wrt_ref[...], h, NT_DIMS, precision=HIGHEST, preferred_element_type=F32)
    s = _sigmoid(lt)
    sel = s + br_ref[...]
    ninf = -jnp.inf
    sel3 = sel.reshape(N_GROUPS, per, tm)
    eidx = lax.broadcasted_iota(jnp.int32, (N_GROUPS, per, tm), 1)
    m1 = jnp.max(sel3, axis=1, keepdims=True)
    first = jnp.min(jnp.where(sel3 == m1, eidx, per), axis=1, keepdims=True)
    m2 = jnp.max(jnp.where(eidx == first, ninf, sel3), axis=1, keepdims=True)
    gscore = (m1 + m2).reshape(N_GROUPS, tm)
    gidx = lax.broadcasted_iota(jnp.int32, (N_GROUPS, tm), 0)
    gmask = jnp.zeros((N_GROUPS, tm), jnp.bool_)
    cur = gscore
    for _ in range(TOPK_GROUPS):
        mx = jnp.max(cur, axis=0, keepdims=True)
        hit = gidx == jnp.min(jnp.where(cur == mx, gidx, N_GROUPS), axis=0, keepdims=True)
        gmask = jnp.logical_or(gmask, hit)
        cur = jnp.where(hit, ninf, cur)
    cur = jnp.where(gmask.reshape(N_GROUPS, 1, tm), sel3, ninf).reshape(N_EXPERTS, tm)
    eid = lax.broadcasted_iota(jnp.int32, (N_EXPERTS, tm), 0)
    ids, ws = [], []
    chosen = jnp.zeros((N_EXPERTS, tm), F32)
    for _ in range(TOP_K):
        mx = jnp.max(cur, axis=0, keepdims=True)
        pick = jnp.min(jnp.where(cur == mx, eid, N_EXPERTS), axis=0, keepdims=True)
        hit = eid == pick
        ids.append(pick)
        ws.append(jnp.sum(jnp.where(hit, s, 0.0), axis=0, keepdims=True))
        cur = jnp.where(hit, ninf, cur)
        chosen = chosen + hit.astype(F32)
    wsum = ws[0]
    for w in ws[1:]:
        wsum = wsum + w
    idx_ref[...] = jnp.concatenate(ids, axis=0)
    wt_ref[...] = jnp.concatenate([w / wsum * ROUTED_SCALE for w in ws], axis=0)
    cnt_ref[...] = jnp.sum(chosen, axis=1, keepdims=True).astype(jnp.int32)


def _router(xs, mods, g_ffn, w_router_t, b_router, n_ctx_tiles):
    b, l, d = xs.shape
    tok = lambda bi, ti: (bi, ti, 0)
    const = lambda bi, ti: (0, 0)
    return pl.pallas_call(
        _router_kernel,
        grid=(b, l // TM),
        in_specs=[pl.BlockSpec((None, TM, d), tok),
                  pl.BlockSpec((None, None, 8, d), lambda bi, ti: (bi, jnp.where(ti >= n_ctx_tiles, 1, 0), 0, 0)),
                  pl.BlockSpec((1, d), const),
                  pl.BlockSpec((N_EXPERTS, d), const),
                  pl.BlockSpec((N_EXPERTS, 1), const)],
        out_specs=[pl.BlockSpec((None, TM, d), tok),
                   pl.BlockSpec((None, TOP_K, TM), lambda bi, ti: (bi, 0, ti)),
                   pl.BlockSpec((None, TOP_K, TM), lambda bi, ti: (bi, 0, ti)),
                   pl.BlockSpec((None, None, N_EXPERTS, 1), lambda bi, ti: (bi, ti, 0, 0))],
        out_shape=[jax.ShapeDtypeStruct((b, l, d), BF16),
                   jax.ShapeDtypeStruct((b, TOP_K, l), jnp.int32),
                   jax.ShapeDtypeStruct((b, TOP_K, l), F32),
                   jax.ShapeDtypeStruct((b, l // TM, N_EXPERTS, 1), jnp.int32)],
        compiler_params=_params(("parallel", "parallel")),
        name="router",
    )(xs, mods, g_ffn, w_router_t, b_router)


def _pack_bf16_pairs(x):
    half = x.shape[1] // 2
    lo = lax.bitcast_convert_type(x[:, :half], jnp.uint32) >> 16
    hi = lax.bitcast_convert_type(x[:, half:], jnp.uint32) & jnp.uint32(0xFFFF0000)
    return lo | hi


def _unpack_bf16_pairs(w):
    lo = lax.bitcast_convert_type(w << 16, F32)
    hi = lax.bitcast_convert_type(w & jnp.uint32(0xFFFF0000), F32)
    return jnp.concatenate([lo, hi], axis=1).astype(BF16)


def _sort_kernel(idx_ref, off_ref, h_ref, posl_ref, ts_ref):
    tm = h_ref.shape[0]
    idx = idx_ref[...]
    eid = lax.broadcasted_iota(jnp.int32, (N_EXPERTS, tm), 0)
    hits = [eid == idx[k:k + 1, :] for k in range(TOP_K)]
    chosen = hits[0].astype(BF16)
    for hk in hits[1:]:
        chosen = chosen + hk.astype(BF16)
    r = lax.broadcasted_iota(jnp.int32, (tm, tm), 0)
    c = lax.broadcasted_iota(jnp.int32, (tm, tm), 1)
    before = (r < c).astype(BF16)
    rank = jnp.dot(chosen, before, preferred_element_type=F32)
    slot = rank.astype(jnp.int32) + off_ref[...]
    posl = jnp.concatenate([jnp.sum(jnp.where(hk, slot, 0), axis=0, keepdims=True) for hk in hits], axis=0)
    posl_ref[...] = posl
    hb = h_ref[...]
    for rb in range(ts_ref.shape[0] // SORT_CHUNK):
        rows = lax.broadcasted_iota(jnp.int32, (SORT_CHUNK, tm), 0) + rb * SORT_CHUNK
        sel = rows == posl[0:1, :]
        for k in range(1, TOP_K):
            sel = jnp.logical_or(sel, rows == posl[k:k + 1, :])
        onehot = jnp.where(sel, 1.0, 0.0).astype(BF16)
        ts = jnp.dot(onehot, hb, preferred_element_type=F32)
        ts_ref[rb * SORT_CHUNK:(rb + 1) * SORT_CHUNK, :] = _pack_bf16_pairs(ts)


def _sort_rows(idx_t, off, h_flat):
    b, k, l = idx_t.shape
    n, d = h_flat.shape
    nt = l // TM
    return pl.pallas_call(
        _sort_kernel,
        grid=(b * nt,),
        in_specs=[pl.BlockSpec((None, k, TM), lambda i: (i // nt, 0, i % nt)),
                  pl.BlockSpec((None, N_EXPERTS, 1), lambda i: (i, 0, 0)),
                  pl.BlockSpec((TM, d), lambda i: (i, 0))],
        out_specs=[pl.BlockSpec((None, k, TM), lambda i: (i // nt, 0, i % nt)),
                   pl.BlockSpec((None, SORT_ROWS, d // 2), lambda i: (i, 0, 0))],
        out_shape=[jax.ShapeDtypeStruct((b, k, l), jnp.int32),
                   jax.ShapeDtypeStruct((b * nt, SORT_ROWS, d // 2), jnp.uint32)],
        compiler_params=_params(("parallel",)),
        name="sort_rows",
    )(idx_t, off, h_flat)


SLAB_SIZES = tuple(SUBLANES << s for s in range((EXPERT_ROWS // SUBLANES).bit_length()))


def _for_slabs(length, fn):
    off = 0
    for sz in SLAB_SIZES:
        take = (length & sz) != 0

        @pl.when(take)
        def _():
            fn(off, sz)

        off = off + jnp.where(take, sz, 0)


def _expert_kernel(be_ref, nu_ref, r0_ref, r1_ref, ro_ref, rr_ref, rl_ref, cov_ref,
                   ts_in, wg_ref, wu_ref, wd_ref, ts_out, xbuf, ybuf, sem_g, sem_s, *, n_tiles):
    j = pl.program_id(0)
    nu = nu_ref[0]
    blk = xbuf.shape[1]
    slot = j % 2

    def for_pieces(bj, fn):
        lo = bj * blk

        def body(r, carry):
            d, ln = rr_ref[r], rl_ref[r]
            ps, pe = jnp.maximum(d, lo), jnp.minimum(d + ln, lo + blk)
            fn(r % n_tiles, ro_ref[r] + ps - d, ps - lo, pe - ps)
            return carry

        lax.fori_loop(r0_ref[bj], r1_ref[bj], body, 0)

    def rows_of(ref, first, sz):
        return ref.at[pl.ds(pl.multiple_of(first, SUBLANES), sz)]

    def gather(bj, s):
        xbuf[s] = jnp.zeros(xbuf.shape[1:], xbuf.dtype)

        def piece(tile, trow, brow, ln):
            _for_slabs(ln, lambda off, sz: pltpu.make_async_copy(
                rows_of(ts_in.at[tile], trow + off, sz), rows_of(xbuf.at[s], brow + off, sz), sem_g.at[s]).start())

        for_pieces(bj, piece)

    def scatter(bj, s):
        def piece(tile, trow, brow, ln):
            _for_slabs(ln, lambda off, sz: pltpu.make_async_copy(
                rows_of(ybuf.at[s], brow + off, sz), rows_of(ts_out.at[tile], trow + off, sz), sem_s.at[s]).start())

        for_pieces(bj, piece)

    def wait_rows(bj, buf, sem, s):
        _for_slabs(cov_ref[bj], lambda off, sz: pltpu.make_async_copy(
            ts_in.at[0, pl.ds(0, sz)], buf.at[s, pl.ds(0, sz)], sem.at[s]).wait())

    @pl.when(j == 0)
    def _():
        gather(0, 0)

    @pl.when(j + 1 < nu)
    def _():
        gather(j + 1, 1 - slot)

    @pl.when(jnp.logical_and(j >= 2, j - 2 < nu))
    def _():
        wait_rows(j - 2, ybuf, sem_s, slot)

    @pl.when(j < nu)
    def _():
        wait_rows(j, xbuf, sem_g, slot)
        x = _unpack_bf16_pairs(xbuf[slot])
        g = jnp.dot(x, wg_ref[...], preferred_element_type=F32)
        u = jnp.dot(x, wu_ref[...], preferred_element_type=F32)
        a = (g * _sigmoid(g) * u).astype(BF16)
        y = jnp.dot(a, wd_ref[...], preferred_element_type=F32)
        ybuf[slot] = _pack_bf16_pairs(y.astype(BF16).astype(F32))
        scatter(j, slot)


def _experts(plan, tiles, wg, wu, wd):
    nt, _, w = tiles.shape
    blk = EXPERT_ROWS
    d, de = wg.shape[1:]
    n_blocks = plan["block_e"].shape[0]
    tables = (plan["block_e"], plan["n_used"], plan["blk_run0"], plan["blk_run1"],
              plan["run_tile_off"], plan["run_row"], plan["run_len"], plan["blk_rows"])
    wspec = lambda shape: pl.BlockSpec((None,) + shape, lambda i, be, *_: (be[i], 0, 0))
    grid_spec = pltpu.PrefetchScalarGridSpec(
        num_scalar_prefetch=len(tables),
        grid=(n_blocks,),
        in_specs=[pl.BlockSpec(memory_space=pl.ANY), wspec((d, de)), wspec((d, de)), wspec((de, d))],
        out_specs=pl.BlockSpec(memory_space=pl.ANY),
        scratch_shapes=[pltpu.VMEM((2, blk, w), tiles.dtype), pltpu.VMEM((2, blk, w), tiles.dtype),
                        pltpu.SemaphoreType.DMA((2,)), pltpu.SemaphoreType.DMA((2,))])
    return pl.pallas_call(
        functools.partial(_expert_kernel, n_tiles=nt),
        grid_spec=grid_spec,
        out_shape=jax.ShapeDtypeStruct(tiles.shape, tiles.dtype),
        input_output_aliases={len(tables): 0},
        compiler_params=_params(("arbitrary",)),
        name="experts",
    )(*tables, tiles, wg, wu, wd)


def _combine_kernel(ts_ref, posl_ref, w_ref, x_ref, h_ref, mod_ref, wsg_ref, wsu_ref, wsd_ref, o_ref):
    tm = x_ref.shape[0]
    hb = h_ref[...]
    g = jnp.dot(hb, wsg_ref[...], preferred_element_type=F32)
    u = jnp.dot(hb, wsu_ref[...], preferred_element_type=F32)
    acc = jnp.dot((g * _sigmoid(g) * u).astype(BF16), wsd_ref[...], preferred_element_type=F32)
    posl = posl_ref[...]
    w = w_ref[...]
    for rb in range(ts_ref.shape[0] // SORT_CHUNK):
        cols = lax.broadcasted_iota(jnp.int32, (tm, SORT_CHUNK), 1) + rb * SORT_CHUNK
        wm = jnp.zeros((tm, SORT_CHUNK), F32)
        for k in range(TOP_K):
            wm = jnp.where(cols == posl[:, k:k + 1], w[:, k:k + 1], wm)
        ys = _unpack_bf16_pairs(ts_ref[rb * SORT_CHUNK:(rb + 1) * SORT_CHUNK, :])
        acc = acc + jnp.dot(wm.astype(BF16), ys, preferred_element_type=F32)
    o_ref[...] = x_ref[...] + mod_ref[5:6, :] * acc


def _combine(tiles, posl_tm, wts, x_flat, h_flat, mods, wsg, wsu, wsd, tiles_per_sample, n_ctx_tiles):
    n, d = x_flat.shape
    ds_ = wsg.shape[-1]
    tok = lambda i: (i, 0)
    const = lambda i: (0, 0)

    def mod_idx(i):
        return (i // tiles_per_sample, jnp.where(i % tiles_per_sample >= n_ctx_tiles, 1, 0), 0, 0)

    return pl.pallas_call(
        _combine_kernel,
        grid=(n // TM,),
        in_specs=[pl.BlockSpec((None,) + tiles.shape[1:], lambda i: (i, 0, 0)),
                  pl.BlockSpec((TM, TOP_K), tok),
                  pl.BlockSpec((TM, TOP_K), tok),
                  pl.BlockSpec((TM, d), tok),
                  pl.BlockSpec((TM, d), tok),
                  pl.BlockSpec((None, None, 8, d), mod_idx),
                  pl.BlockSpec((d, ds_), const), pl.BlockSpec((d, ds_), const), pl.BlockSpec((ds_, d), const)],
        out_specs=pl.BlockSpec((TM, d), tok),
        out_shape=jax.ShapeDtypeStruct((n, d), F32),
        compiler_params=_params(("parallel",)),
        name="combine",
    )(tiles, posl_tm, wts, x_flat, h_flat, mods, wsg, wsu, wsd)


def _final_kernel(x_ref, g_ref, o_ref):
    o_ref[...] = _rms(x_ref[...], g_ref[...])


def _final_norm(xs, g_final, n_ctx_tiles):
    b, l, d = xs.shape
    s_len = l - n_ctx_tiles * TM
    return pl.pallas_call(
        _final_kernel,
        grid=(b, s_len // TM),
        in_specs=[pl.BlockSpec((None, TM, d), lambda bi, ti: (bi, ti + n_ctx_tiles, 0)),
                  pl.BlockSpec((1, d), lambda bi, ti: (0, 0))],
        out_specs=pl.BlockSpec((None, TM, d), lambda bi, ti: (bi, ti, 0)),
        out_shape=jax.ShapeDtypeStruct((b, s_len, d), F32),
        compiler_params=_params(("parallel", "parallel")),
        name="final_norm",
    )(xs, g_final)


def _moe_plan(cnt, n_assign, blk):
    nt = cnt.shape[0]
    run = (cnt + SUBLANES - 1) // SUBLANES * SUBLANES
    tile_off = jnp.cumsum(run, axis=1) - run
    tot = jnp.sum(run, axis=0)
    padded = (tot + blk - 1) // blk * blk
    pad_end = jnp.cumsum(padded)
    pad_start = pad_end - padded
    row = (pad_start[None, :] + jnp.cumsum(run, axis=0) - run).T.reshape(-1)
    run_len = run.T.reshape(-1)
    n_blocks = -(-(n_assign + nt * N_EXPERTS * (SUBLANES - 1)) // blk) + N_EXPERTS + 2
    first_row = jnp.arange(n_blocks, dtype=jnp.int32) * blk
    count = lambda m: jnp.sum(m.astype(jnp.int32), axis=1)
    block_e = jnp.minimum(count(pad_end[None, :] <= first_row[:, None]), N_EXPERTS - 1)
    i32 = lambda a: a.astype(jnp.int32)
    return dict(
        tile_off=i32(tile_off)[:, :, None],
        run_tile_off=i32(tile_off.T.reshape(-1)), run_row=i32(row), run_len=i32(run_len),
        blk_run0=count((row + run_len)[None, :] <= first_row[:, None]),
        blk_run1=count(row[None, :] < first_row[:, None] + blk),
        blk_rows=i32(jnp.clip((pad_start + tot)[block_e] - first_row, 0, blk)),
        block_e=i32(block_e), n_used=i32(pad_end[-1] // blk).reshape(1))


def _rope_tables(s_len, n_ctx):
    rows = s_len // GRID_W
    row = jnp.repeat(jnp.arange(rows), GRID_W).astype(F32)
    col = jnp.tile(jnp.arange(GRID_W), rows).astype(F32)
    quarter = HEAD_DIM // 4
    inv = 1.0 / (ROPE_BASE ** (jnp.arange(quarter, dtype=F32) / quarter))
    ar, ac = row[:, None] * inv, col[:, None] * inv
    cr, sr, cc, sc = jnp.cos(ar), jnp.sin(ar), jnp.cos(ac), jnp.sin(ac)
    z = jnp.zeros_like(sr)
    cos = jnp.concatenate([cr, cr, cc, cc], axis=1)
    sa = jnp.concatenate([z, sr, z, sc], axis=1)
    sb = jnp.concatenate([-sr, z, -sc, z], axis=1)
    rep = LANES // HEAD_DIM

    def full(t, fill):
        t = jnp.tile(t, (1, rep))
        return jnp.concatenate([jnp.full((n_ctx, LANES), fill, F32), t], axis=0)

    return full(cos, 1.0), full(sa, 0.0), full(sb, 0.0)


def _pair_perm():
    g = N_HEADS_A // KV_HEADS_A
    heads = [h for t in range(g) for h in (t, t + g)]
    return jnp.concatenate([jnp.arange(h * HEAD_DIM, (h + 1) * HEAD_DIM) for h in heads])


def _split_w_in(w):
    a_q, a_kv = N_HEADS_A * HEAD_DIM, KV_HEADS_A * HEAD_DIM
    b_w = N_HEADS_B * 2 * HEAD_DIM
    c_w = N_HEADS_C * HEAD_DIM_C
    sizes = (a_q, a_kv, a_kv, b_w, b_w, b_w, c_w, c_w, c_w, c_w, 4 * N_HEADS_C, w.shape[1])
    parts, start = [], 0
    for sz in sizes[:-1]:
        parts.append(w[:, start:start + sz])
        start += sz
    parts.append(w[:, start:])
    return parts


def _pack_w_in(w):
    d = w.shape[0]
    aq, ak, av, bq, bk, bv, cq, ck, cv, co, cg, gt = _split_w_in(w)
    pad = lambda n: jnp.zeros((d, n), w.dtype)
    kva = jnp.concatenate([ak, av, cg, pad(TILE_N - ak.shape[1] - av.shape[1] - cg.shape[1])], axis=1)
    big = jnp.concatenate([aq[:, _pair_perm()], bq, bk, kva, bv, co, gt, cq, cv], axis=1)
    return big.astype(BF16), ck.T.astype(BF16)


def kernel(x, c, ctx, c_ctx, w_mod, b_mod, g_mix, g_ffn, w_in, b_gate, sink, lam_q1, lam_k1, lam_q2, lam_k2,
           g_diff, g_mlstm, w_a, w_b, w_c, w_out, w_router, b_router, w_exp_gate, w_exp_up, w_exp_down,
           w_sh_gate, w_sh_up, w_sh_down, g_final):
    b, s_len, d = x.shape
    n_ctx = ctx.shape[1]
    l = n_ctx + s_len
    depth = w_mod.shape[0]
    n_ctx_tiles = n_ctx // TM
    assert n_ctx % TM == 0 and s_len % TM == 0 and d % LANES == 0 and s_len % GRID_W == 0

    xs = jnp.concatenate([ctx, x], axis=1)
    cos, sa, sb = _rope_tables(s_len, n_ctx)

    rows_c = 16
    cs = jnp.concatenate([c, c_ctx[None], jnp.zeros((rows_c - b - 1, d), F32)], axis=0)
    mod_all = _mod_vectors(cs, w_mod, b_mod).reshape(depth, rows_c, N_MOD, d)
    mod_all = jnp.pad(mod_all, ((0, 0), (0, 0), (0, 8 - N_MOD), (0, 0)))

    perm = _pair_perm()
    for layer in range(depth):
        lam_init = 0.8 - 0.6 * math.exp(-0.3 * layer)
        mods = jnp.stack([jnp.broadcast_to(mod_all[layer, b], (b, 8, d)), mod_all[layer, :b]], axis=1)
        w_big, w_kt = _pack_w_in(w_in[layer])
        p, gates, kt = _inproj(xs, mods, g_mix[layer][None], w_big, w_kt, cos, sa, sb, n_ctx_tiles)

        oa = _mixer_a(p, sink[layer], n_ctx)
        lam_params = jnp.stack([lam_q1[layer], lam_k1[layer], lam_q2[layer], lam_k2[layer]])
        ob_ctx, ob_lat = _mixer_b(p, lam_params, g_diff[layer][None], lam_init, n_ctx)

        bias = b_gate[layer].reshape(-1)
        bias_row = jnp.pad(bias, (0, LANES - bias.shape[0]))[None]
        bias_col = jnp.broadcast_to(bias[:, None], (bias.shape[0], LANES))
        gates_t = jnp.transpose(gates[:, :, :bias.shape[0]], (0, 2, 1))
        hm = _mlstm(p, kt, gates, gates_t, bias_row, bias_col, n_ctx)

        xs = _merge(xs, mods, oa, ob_ctx, ob_lat, hm, p, g_mlstm[layer][None],
                    w_a[layer][perm].astype(BF16), w_b[layer].astype(BF16), w_c[layer].astype(BF16),
                    w_out[layer].astype(BF16), n_ctx_tiles)

        h, idx_t, wt_t, cnt = _router(xs, mods, g_ffn[layer][None], w_router[layer].T, b_router[layer][:, None],
                                      n_ctx_tiles)
        plan = _moe_plan(cnt.reshape(-1, N_EXPERTS), b * l * TOP_K, EXPERT_ROWS)
        h_flat = h.reshape(b * l, d)
        posl, tiles = _sort_rows(idx_t, plan["tile_off"], h_flat)
        tiles = _experts(plan, tiles, w_exp_gate[layer].astype(BF16), w_exp_up[layer].astype(BF16),
                         w_exp_down[layer].astype(BF16))
        to_rows = lambda a: jnp.transpose(a, (0, 2, 1)).reshape(b * l, TOP_K)
        xs = _combine(tiles, to_rows(posl), to_rows(wt_t), xs.reshape(b * l, d), h_flat, mods,
                      w_sh_gate[layer].astype(BF16), w_sh_up[layer].astype(BF16), w_sh_down[layer].astype(BF16),
                      l // TM, n_ctx_tiles).reshape(b, l, d)
    return _final_norm(xs, g_final[None], n_ctx_tiles)
```

```python
import functools
import math

import jax
import jax.numpy as jnp
from jax import lax
from jax.experimental import pallas as pl
from jax.experimental.pallas import tpu as pltpu

F32 = jnp.float32
BF16 = jnp.bfloat16
HIGHEST = lax.Precision.HIGHEST

GRID_W = 64
N_MOD = 6
HEAD_DIM = 64
N_HEADS_A = 8
KV_HEADS_A = 2
WINDOW = 128
N_HEADS_B = 4
N_HEADS_C = 4
HEAD_DIM_C = 128
N_EXPERTS = 64
N_GROUPS = 8
TOPK_GROUPS = 4
TOP_K = 8
ROUTED_SCALE = 2.5
ROPE_BASE = 10000.0
EPS = 1e-6

LANES = 128
CHUNK = 128
TILE_N = 512
TM = 256
MIXB_KEYS = 1024
MIXB_QUERIES = 512
EXPERT_ROWS = 512
SUBLANES = 8
SORT_ROWS = TOP_K * TM + N_EXPERTS * SUBLANES
SORT_CHUNK = 512
NEG = -1e30
VMEM_LIMIT = 56 * 1024 * 1024

T_AQ, T_BQ, T_BK, T_KVA, T_BV, T_CO, T_GT, T_CQ, T_CV, N_TILES = 0, 1, 2, 3, 4, 5, 6, 12, 13, 14

NT_DIMS = (((1,), (1,)), ((), ()))


def _params(sem):
    return pltpu.CompilerParams(dimension_semantics=sem, vmem_limit_bytes=VMEM_LIMIT)


def _rms(x, g):
    return x * lax.rsqrt(jnp.mean(x * x, axis=-1, keepdims=True) + EPS) * g


def _sigmoid(x):
    return jax.nn.sigmoid(x)


def _mod_kernel(c_ref, w_ref, b_ref, o_ref):
    c = c_ref[...]
    s = c * _sigmoid(c)
    o_ref[...] = jnp.dot(s, w_ref[...], precision=HIGHEST, preferred_element_type=F32) + b_ref[...]


def _mod_vectors(cs, w_mod, b_mod):
    depth, d, n = w_mod.shape
    r = cs.shape[0]
    tn = 3 * LANES
    return pl.pallas_call(
        _mod_kernel,
        grid=(depth, n // tn),
        in_specs=[pl.BlockSpec((r, d), lambda l, j: (0, 0)),
                  pl.BlockSpec((None, d, tn), lambda l, j: (l, 0, j)),
                  pl.BlockSpec((None, 1, tn), lambda l, j: (l, 0, j))],
        out_specs=pl.BlockSpec((None, r, tn), lambda l, j: (l, 0, j)),
        out_shape=jax.ShapeDtypeStruct((depth, r, n), F32),
        compiler_params=_params(("parallel", "parallel")),
        name="mod_vectors",
    )(cs, w_mod, b_mod.reshape(depth, 1, n))


def _inproj_kernel(x_ref, mod_ref, g_ref, w_ref, wkt_ref, cos_ref, sa_ref, sb_ref, p_ref, gate_ref, kt_ref):
    x = x_ref[...]
    h = _rms(x, g_ref[...]) * (1.0 + mod_ref[1:2, :]) + mod_ref[0:1, :]
    hb = h.astype(BF16)
    cos, sa, sb = cos_ref[...], sa_ref[...], sb_ref[...]

    def rope(t):
        return t * cos + pltpu.roll(t, 16, 1) * sa + pltpu.roll(t, LANES - 16, 1) * sb

    q_scale = HEAD_DIM ** -0.5
    for j in range(N_TILES):
        acc = jnp.dot(hb, w_ref[:, j * TILE_N:(j + 1) * TILE_N], preferred_element_type=F32)
        parts = [acc[:, s * LANES:(s + 1) * LANES] for s in range(TILE_N // LANES)]
        if j == T_AQ:
            parts = [rope(t) * q_scale for t in parts]
        elif j == T_BQ:
            parts = [rope(t) * (q_scale * math.log2(math.e)) for t in parts]
        elif j == T_BK:
            parts = [rope(t) for t in parts]
        elif j == T_KVA:
            gate_ref[...] = parts[2]
            parts[0] = rope(parts[0])
        for s, t in enumerate(parts):
            p_ref[:, j * TILE_N + s * LANES:j * TILE_N + (s + 1) * LANES] = t.astype(BF16)
    kt = lax.dot_general(wkt_ref[...], hb, NT_DIMS, preferred_element_type=F32)
    kt_ref[...] = (kt * (HEAD_DIM_C ** -0.5)).astype(BF16)


def _inproj(xs, mods, g_mix, w_big, w_kt, cos, sa, sb, n_ctx_tiles):
    b, l, d = xs.shape
    npad = w_big.shape[1]
    ck = w_kt.shape[0]
    grid = (b, l // TM)
    tok = lambda bi, ti: (bi, ti, 0)
    return pl.pallas_call(
        _inproj_kernel,
        grid=grid,
        in_specs=[pl.BlockSpec((None, TM, d), tok),
                  pl.BlockSpec((None, None, 8, d), lambda bi, ti: (bi, jnp.where(ti >= n_ctx_tiles, 1, 0), 0, 0)),
                  pl.BlockSpec((1, d), lambda bi, ti: (0, 0)),
                  pl.BlockSpec((d, npad), lambda bi, ti: (0, 0), pipeline_mode=pl.Buffered(1)),
                  pl.BlockSpec((ck, d), lambda bi, ti: (0, 0), pipeline_mode=pl.Buffered(1)),
                  pl.BlockSpec((TM, LANES), lambda bi, ti: (ti, 0)),
                  pl.BlockSpec((TM, LANES), lambda bi, ti: (ti, 0)),
                  pl.BlockSpec((TM, LANES), lambda bi, ti: (ti, 0))],
        out_specs=[pl.BlockSpec((None, TM, npad), tok),
                   pl.BlockSpec((None, TM, LANES), tok),
                   pl.BlockSpec((None, ck, TM), lambda bi, ti: (bi, 0, ti))],
        out_shape=[jax.ShapeDtypeStruct((b, l, npad), BF16),
                   jax.ShapeDtypeStruct((b, l, LANES), F32),
                   jax.ShapeDtypeStruct((b, ck, l), BF16)],
        compiler_params=_params(("parallel", "parallel")),
        name="inproj",
    )(xs, mods, g_mix, w_big, w_kt, cos, sa, sb)


def _mixa_kernel(sink_ref, q_ref, kp_ref, kc_ref, kn_ref, kx_ref, o_ref, *, n_ctx_blocks, n_blocks):
    i = pl.program_id(1)
    lat = i >= n_ctx_blocks
    has_prev = jnp.logical_and(lat, i > n_ctx_blocks)
    has_next = jnp.logical_and(lat, i < n_blocks - 1)
    r = lax.broadcasted_iota(jnp.int32, (CHUNK, CHUNK), 0)
    c = lax.broadcasted_iota(jnp.int32, (CHUNK, CHUNK), 1)
    n_ctx = kx_ref.shape[0]
    valid = jnp.concatenate([
        jnp.logical_and(c >= r, has_prev),
        jnp.broadcast_to(lat, (CHUNK, CHUNK)),
        jnp.logical_and(c <= r, has_next),
        jnp.ones((CHUNK, n_ctx), jnp.bool_)], axis=1)
    kcat = jnp.concatenate([kp_ref[:, :LANES], kc_ref[:, :LANES], kn_ref[:, :LANES], kx_ref[:, :LANES]], axis=0)
    vcat = jnp.concatenate([kp_ref[:, LANES:], kc_ref[:, LANES:], kn_ref[:, LANES:], kx_ref[:, LANES:]], axis=0)
    lane = lax.broadcasted_iota(jnp.int32, (CHUNK, LANES), 1)
    low = lane < HEAD_DIM
    n_pairs = N_HEADS_A // KV_HEADS_A
    outs = []
    for gk in range(KV_HEADS_A):
        keep = low if gk == 0 else jnp.logical_not(low)
        zero = jnp.zeros((CHUNK, LANES), BF16)
        lhs = jnp.concatenate([jnp.where(keep, q_ref[:, t * LANES:(t + 1) * LANES], zero) for t in range(n_pairs)],
                              axis=0)
        s = lax.dot_general(lhs, kcat, NT_DIMS, preferred_element_type=F32)
        o_g = []
        for t in range(n_pairs):
            st = jnp.where(valid, s[t * CHUNK:(t + 1) * CHUNK], NEG)
            sk = sink_ref[gk * n_pairs + t]
            m = jnp.maximum(jnp.max(st, axis=-1, keepdims=True), sk)
            p = jnp.exp(st - m)
            den = jnp.sum(p, axis=-1, keepdims=True) + jnp.exp(sk - m)
            o_g.append(jnp.dot(p.astype(BF16), vcat, preferred_element_type=F32) / den)
        outs.append(o_g)
    for t in range(n_pairs):
        o_ref[:, t * LANES:(t + 1) * LANES] = jnp.where(low, outs[0][t], outs[1][t]).astype(BF16)


def _mixer_a(p, sink, n_ctx):
    b, l, _ = p.shape
    nb = l // CHUNK
    ncb = n_ctx // CHUNK
    kvw = 2 * LANES
    kv_col = T_KVA * TILE_N // kvw
    aq_w = N_HEADS_A * HEAD_DIM
    kern = functools.partial(_mixa_kernel, n_ctx_blocks=ncb, n_blocks=nb)
    return pl.pallas_call(
        kern,
        grid=(b, nb),
        in_specs=[pl.BlockSpec(memory_space=pltpu.SMEM),
                  pl.BlockSpec((None, CHUNK, aq_w), lambda bi, i: (bi, i, T_AQ)),
                  pl.BlockSpec((None, CHUNK, kvw), lambda bi, i: (bi, jnp.maximum(i - 1, 0), kv_col)),
                  pl.BlockSpec((None, CHUNK, kvw), lambda bi, i: (bi, i, kv_col)),
                  pl.BlockSpec((None, CHUNK, kvw), lambda bi, i: (bi, jnp.minimum(i + 1, nb - 1), kv_col)),
                  pl.BlockSpec((None, n_ctx, kvw), lambda bi, i: (bi, 0, kv_col))],
        out_specs=pl.BlockSpec((None, CHUNK, aq_w), lambda bi, i: (bi, i, 0)),
        out_shape=jax.ShapeDtypeStruct((b, l, aq_w), BF16),
        compiler_params=_params(("parallel", "parallel")),
        name="mixer_a",
    )(sink, p, p, p, p, p)


def _fold_lanes(op, acc, s):
    for t in range(s.shape[1] // LANES):
        acc = op(acc, s[:, t * LANES:(t + 1) * LANES])
    return acc


def _mixb_kernel(lam_ref, gd_ref, k_ref, v_ref, *rest, lam_init, chunks):
    q_refs, (o_ref, s_scr, va_scr) = rest[:-3], rest[-3:]

    @pl.when(pl.program_id(2) == 0)
    def _():
        n_keys = v_ref.shape[0]
        va_scr[:, :LANES] = v_ref[...]
        va_scr[:, LANES:] = (lax.broadcasted_iota(jnp.int32, (n_keys, LANES), 1) == 0).astype(BF16)

    lp = lam_ref[...]
    lam = (jnp.exp(jnp.sum(lp[0:1] * lp[1:2], axis=-1, keepdims=True))
           - jnp.exp(jnp.sum(lp[2:3] * lp[3:4], axis=-1, keepdims=True)) + lam_init)
    q = jnp.concatenate([qr[...] for qr in q_refs], axis=0)
    tq = q.shape[0]
    lane = lax.broadcasted_iota(jnp.int32, (tq, LANES), 1)
    zero = jnp.zeros_like(q)
    qs = (jnp.where(lane < HEAD_DIM, q, zero), jnp.where(lane >= HEAD_DIM, q, zero))
    rows = [slice(mi * tq, (mi + 1) * tq) for mi in range(2)]
    mrun = [jnp.full((tq, LANES), NEG, F32) for _ in range(2)]
    for off, sz in chunks:
        for mi in range(2):
            s_scr[rows[mi], off:off + sz] = lax.dot_general(qs[mi], k_ref[off:off + sz, :], NT_DIMS,
                                                            preferred_element_type=F32)
            mrun[mi] = _fold_lanes(jnp.maximum, mrun[mi], s_scr[rows[mi], off:off + sz])
    m = [jnp.max(mr, axis=-1, keepdims=True) for mr in mrun]
    acc = [jnp.zeros((tq, 2 * LANES), F32) for _ in range(2)]
    for off, sz in chunks:
        for mi in range(2):
            pr = jnp.exp2(s_scr[rows[mi], off:off + sz] - m[mi])
            acc[mi] = acc[mi] + jnp.dot(pr.astype(BF16), va_scr[off:off + sz, :], preferred_element_type=F32)
    outs = [a[:, :LANES] / a[:, LANES:LANES + 1] for a in acc]
    o = outs[0] - lam * outs[1]
    o_ref[...] = (_rms(o, gd_ref[...]) * (1.0 - lam_init)).astype(BF16)


def _mixer_b(p, lam_params, g_diff, lam_init, n_ctx):
    b, l, _ = p.shape
    kl = min(MIXB_KEYS, l - n_ctx)
    tq_lat = min(MIXB_QUERIES, l - n_ctx)
    assert (l - n_ctx) % kl == 0 and (l - n_ctx) % tq_lat == 0 and tq_lat % TM == 0 and n_ctx % TM == 0
    q0 = T_BQ * TILE_N // LANES
    k0 = T_BK * TILE_N // LANES
    v0 = T_BV * TILE_N // LANES
    ctx_chunks = ((0, n_ctx),)
    all_chunks = ctx_chunks + tuple((n_ctx + c * kl, kl) for c in range((l - n_ctx) // kl))

    def call(chunks, n_keys, tq, q_tiles, first_row):
        nq = tq // TM
        q_spec = lambda part: pl.BlockSpec((None, TM, LANES),
                                           lambda bi, h, qi: (bi, first_row // TM + qi * nq + part, q0 + h))
        kern = functools.partial(_mixb_kernel, lam_init=lam_init, chunks=chunks)
        return pl.pallas_call(
            kern,
            grid=(b, N_HEADS_B, q_tiles),
            in_specs=[pl.BlockSpec((4, HEAD_DIM), lambda bi, h, qi: (0, 0)),
                      pl.BlockSpec((1, LANES), lambda bi, h, qi: (0, 0)),
                      pl.BlockSpec((None, n_keys, LANES), lambda bi, h, qi: (bi, 0, k0 + h)),
                      pl.BlockSpec((None, n_keys, LANES), lambda bi, h, qi: (bi, 0, v0 + h))]
                     + [q_spec(part) for part in range(nq)],
            out_specs=pl.BlockSpec((None, tq, LANES), lambda bi, h, qi: (bi, qi, h)),
            out_shape=jax.ShapeDtypeStruct((b, q_tiles * tq, N_HEADS_B * LANES), BF16),
            scratch_shapes=[pltpu.VMEM((2 * tq, n_keys), F32), pltpu.VMEM((n_keys, 2 * LANES), BF16)],
            compiler_params=_params(("parallel", "parallel", "arbitrary")),
            name="mixer_b",
        )(lam_params, g_diff, p, p, *([p] * nq))

    tq_ctx = min(tq_lat, n_ctx)
    return (call(ctx_chunks, n_ctx, tq_ctx, n_ctx // tq_ctx, 0),
            call(all_chunks, l, tq_lat, (l - n_ctx) // tq_lat, n_ctx))


def _log_sigmoid(x):
    return jnp.minimum(x, 0.0) - jnp.log1p(jnp.exp(-jnp.abs(x)))


def _mlstm_kernel(q_ref, kt_ref, v_ref, gc_ref, gr_ref, bc_ref, br_ref, o_ref, s_scr, m_scr):
    d = pl.program_id(1)
    c = pl.program_id(2)

    @pl.when(c == 0)
    def _():
        s_scr[...] = jnp.zeros_like(s_scr)
        m_scr[...] = jnp.zeros_like(m_scr)

    fwd = d == 0
    r = lax.broadcasted_iota(jnp.int32, (CHUNK, CHUNK), 0)
    cc = lax.broadcasted_iota(jnp.int32, (CHUNK, CHUNK), 1)
    tri = jnp.where(fwd, r - cc, cc - r) >= 0
    trif = tri.astype(F32)
    gcol = gc_ref[...] + bc_ref[...]
    grow = gr_ref[...] + br_ref[...]
    lf_col = _log_sigmoid(gcol)
    lf_row = _log_sigmoid(grow)
    bcum_col = jnp.dot(trif, lf_col, precision=HIGHEST, preferred_element_type=F32)
    bcum_row = lax.dot_general(lf_row, trif, NT_DIMS, precision=HIGHEST, preferred_element_type=F32)
    tot_row = jnp.sum(lf_row, axis=-1, keepdims=True)
    lane = lax.broadcasted_iota(jnp.int32, (CHUNK, LANES), 1)
    ones_col = (lane == 0).astype(BF16)
    nh = N_HEADS_C

    for h in range(nh):
        def pick_col(a, kind):
            return jnp.where(fwd, a[:, kind * nh + h:kind * nh + h + 1],
                             a[:, (kind + 2) * nh + h:(kind + 2) * nh + h + 1])

        def pick_row(a, kind):
            return jnp.where(fwd, a[kind * nh + h:kind * nh + h + 1, :],
                             a[(kind + 2) * nh + h:(kind + 2) * nh + h + 1, :])

        ic_row = pick_row(grow, 0)
        b_col = pick_col(bcum_col, 1)
        b_row = pick_row(bcum_row, 1)
        total = pick_row(tot_row, 1)
        m_st = m_scr[h, 0:1, 0:1]
        dm = jnp.where(tri, b_col - b_row + ic_row, NEG)
        inter = b_col + m_st
        m_t = jnp.maximum(inter, jnp.max(dm, axis=-1, keepdims=True))
        e = jnp.exp(dm - m_t)
        qh = q_ref[:, h * LANES:(h + 1) * LANES]
        kth = kt_ref[h * LANES:(h + 1) * LANES, :]
        vaug = jnp.concatenate([v_ref[:, h * LANES:(h + 1) * LANES], ones_col], axis=1)
        s = jnp.dot(qh, kth, preferred_element_type=F32) * e
        st = s_scr[h]
        intra = jnp.dot(s.astype(BF16), vaug, preferred_element_type=F32)
        cross = jnp.dot(qh, st.astype(BF16), preferred_element_type=F32)
        nd = intra + jnp.exp(inter - m_t) * cross
        den = nd[:, LANES:LANES + 1]
        o_ref[:, h * LANES:(h + 1) * LANES] = nd[:, :LANES] / jnp.maximum(jnp.abs(den), jnp.exp(-m_t))
        gs_row = total - b_row + ic_row
        m_new = jnp.maximum(total + m_st, jnp.max(gs_row, axis=-1, keepdims=True))
        decay = jnp.exp(total + m_st - m_new)
        wkt = (kth.astype(F32) * jnp.exp(gs_row - m_new)).astype(BF16)
        s_scr[h] = decay * st + jnp.dot(wkt, vaug, preferred_element_type=F32)
        m_scr[h] = jnp.broadcast_to(m_new, m_scr.shape[1:])


def _mlstm(p, kt, gates, gates_t, bias_row, bias_col, n_ctx):
    b, l, _ = p.shape
    nc = l // CHUNK
    ncc = n_ctx // CHUNK
    cw = N_HEADS_C * HEAD_DIM_C

    def chunk(d, c):
        rev = jnp.where(c < ncc, ncc - 1 - c, nc + ncc - 1 - c)
        return jnp.where(d == 0, c, rev)

    return pl.pallas_call(
        _mlstm_kernel,
        grid=(b, 2, nc),
        in_specs=[pl.BlockSpec((None, CHUNK, cw), lambda bi, d, c: (bi, chunk(d, c), T_CQ)),
                  pl.BlockSpec((None, cw, CHUNK), lambda bi, d, c: (bi, 0, chunk(d, c))),
                  pl.BlockSpec((None, CHUNK, cw), lambda bi, d, c: (bi, chunk(d, c), T_CV)),
                  pl.BlockSpec((None, CHUNK, LANES), lambda bi, d, c: (bi, chunk(d, c), 0)),
                  pl.BlockSpec((None, 16, CHUNK), lambda bi, d, c: (bi, 0, chunk(d, c))),
                  pl.BlockSpec((1, LANES), lambda bi, d, c: (0, 0)),
                  pl.BlockSpec((16, LANES), lambda bi, d, c: (0, 0))],
        out_specs=pl.BlockSpec((None, None, CHUNK, cw), lambda bi, d, c: (d, bi, chunk(d, c), 0)),
        out_shape=jax.ShapeDtypeStruct((2, b, l, cw), F32),
        scratch_shapes=[pltpu.VMEM((N_HEADS_C, HEAD_DIM_C, 2 * LANES), F32),
                        pltpu.VMEM((N_HEADS_C, 8, LANES), F32)],
        compiler_params=_params(("parallel", "parallel", "arbitrary")),
        name="mlstm",
    )(p, kt, p, gates, gates_t, bias_row, bias_col)


def _merge_kernel(x_ref, mod_ref, oa_ref, obc_ref, obl_ref, hf_ref, hb_ref, co_ref, gt_ref, gm_ref,
                  wa_ref, wb_ref, wc_ref, wo_ref, xo_ref, *, n_ctx_tiles):
    d = x_ref.shape[-1]
    ob = jnp.where(pl.program_id(1) < n_ctx_tiles, obc_ref[...], obl_ref[...])
    hs = hf_ref[...] + hb_ref[...]
    co = co_ref[...].astype(F32)
    gm = gm_ref[...]
    oc = []
    for h in range(N_HEADS_C):
        sl = slice(h * LANES, (h + 1) * LANES)
        oc.append((_rms(hs[:, sl], gm[:, sl]) * _sigmoid(co[:, sl])).astype(BF16))
    oc = jnp.concatenate(oc, axis=1)
    y = (_sigmoid(gt_ref[:, 0:d].astype(F32)) * jnp.dot(oa_ref[...], wa_ref[...], preferred_element_type=F32)
         + _sigmoid(gt_ref[:, d:2 * d].astype(F32)) * jnp.dot(ob, wb_ref[...], preferred_element_type=F32)
         + _sigmoid(gt_ref[:, 2 * d:3 * d].astype(F32)) * jnp.dot(oc, wc_ref[...], preferred_element_type=F32))
    out = jnp.dot(y.astype(BF16), wo_ref[...], preferred_element_type=F32)
    xo_ref[...] = x_ref[...] + mod_ref[2:3, :] * out


def _merge(xs, mods, oa, ob_ctx, ob_lat, hm, p, g_mlstm, wa, wb, wc, wo, n_ctx_tiles):
    b, l, d = xs.shape
    tok = lambda bi, ti: (bi, ti, 0)
    cw = N_HEADS_C * HEAD_DIM_C
    const = lambda bi, ti: (0, 0)
    return pl.pallas_call(
        functools.partial(_merge_kernel, n_ctx_tiles=n_ctx_tiles),
        grid=(b, l // TM),
        in_specs=[pl.BlockSpec((None, TM, d), tok),
                  pl.BlockSpec((None, None, 8, d), lambda bi, ti: (bi, jnp.where(ti >= n_ctx_tiles, 1, 0), 0, 0)),
                  pl.BlockSpec((None, TM, oa.shape[-1]), tok),
                  pl.BlockSpec((None, TM, ob_ctx.shape[-1]), lambda bi, ti: (bi, jnp.minimum(ti, n_ctx_tiles - 1), 0)),
                  pl.BlockSpec((None, TM, ob_lat.shape[-1]), lambda bi, ti: (bi, jnp.maximum(ti - n_ctx_tiles, 0), 0)),
                  pl.BlockSpec((None, None, TM, cw), lambda bi, ti: (0, bi, ti, 0)),
                  pl.BlockSpec((None, None, TM, cw), lambda bi, ti: (1, bi, ti, 0)),
                  pl.BlockSpec((None, TM, cw), lambda bi, ti: (bi, ti, T_CO)),
                  pl.BlockSpec((None, TM, 3 * d), lambda bi, ti: (bi, ti, T_GT * TILE_N // (3 * d))),
                  pl.BlockSpec((1, cw), const),
                  pl.BlockSpec(wa.shape, const), pl.BlockSpec(wb.shape, const),
                  pl.BlockSpec(wc.shape, const), pl.BlockSpec(wo.shape, const)],
        out_specs=pl.BlockSpec((None, TM, d), tok),
        out_shape=jax.ShapeDtypeStruct((b, l, d), F32),
        compiler_params=_params(("parallel", "parallel")),
        name="merge",
    )(xs, mods, oa, ob_ctx, ob_lat, hm, hm, p, p, g_mlstm, wa, wb, wc, wo)


def _router_kernel(x_ref, mod_ref, g_ref, wrt_ref, br_ref, h_ref, idx_ref, wt_ref, cnt_ref):
    h = _rms(x_ref[...], g_ref[...]) * (1.0 + mod_ref[4:5, :]) + mod_ref[3:4, :]
    h_ref[...] = h.astype(BF16)
    tm = h.shape[0]
    per = N_EXPERTS // N_GROUPS
    lt = lax.dot_general(wrt_ref[...], h, NT_DIMS, precision=HIGHEST, preferred_element_type=F32)
    s = _sigmoid(lt)
    sel = s + br_ref[...]
    ninf = -jnp.inf
    sel3 = sel.reshape(N_GROUPS, per, tm)
    eidx = lax.broadcasted_iota(jnp.int32, (N_GROUPS, per, tm), 1)
    m1 = jnp.max(sel3, axis=1, keepdims=True)
    first = jnp.min(jnp.where(sel3 == m1, eidx, per), axis=1, keepdims=True)
    m2 = jnp.max(jnp.where(eidx == first, ninf, sel3), axis=1, keepdims=True)
    gscore = (m1 + m2).reshape(N_GROUPS, tm)
    gidx = lax.broadcasted_iota(jnp.int32, (N_GROUPS, tm), 0)
    gmask = jnp.zeros((N_GROUPS, tm), jnp.bool_)
    cur = gscore
    for _ in range(TOPK_GROUPS):
        mx = jnp.max(cur, axis=0, keepdims=True)
        hit = gidx == jnp.min(jnp.where(cur == mx, gidx, N_GROUPS), axis=0, keepdims=True)
        gmask = jnp.logical_or(gmask, hit)
        cur = jnp.where(hit, ninf, cur)
    cur = jnp.where(gmask.reshape(N_GROUPS, 1, tm), sel3, ninf).reshape(N_EXPERTS, tm)
    eid = lax.broadcasted_iota(jnp.int32, (N_EXPERTS, tm), 0)
    ids, ws = [], []
    chosen = jnp.zeros((N_EXPERTS, tm), F32)
    for _ in range(TOP_K):
        mx = jnp.max(cur, axis=0, keepdims=True)
        pick = jnp.min(jnp.where(cur == mx, eid, N_EXPERTS), axis=0, keepdims=True)
        hit = eid == pick
        ids.append(pick)
        ws.append(jnp.sum(jnp.where(hit, s, 0.0), axis=0, keepdims=True))
        cur = jnp.where(hit, ninf, cur)
        chosen = chosen + hit.astype(F32)
    wsum = ws[0]
    for w in ws[1:]:
        wsum = wsum + w
    idx_ref[...] = jnp.concatenate(ids, axis=0)
    wt_ref[...] = jnp.concatenate([w / wsum * ROUTED_SCALE for w in ws], axis=0)
    cnt_ref[...] = jnp.sum(chosen, axis=1, keepdims=True).astype(jnp.int32)


def _router(xs, mods, g_ffn, w_router_t, b_router, n_ctx_tiles):
    b, l, d = xs.shape
    tok = lambda bi, ti: (bi, ti, 0)
    const = lambda bi, ti: (0, 0)
    return pl.pallas_call(
        _router_kernel,
        grid=(b, l // TM),
        in_specs=[pl.BlockSpec((None, TM, d), tok),
                  pl.BlockSpec((None, None, 8, d), lambda bi, ti: (bi, jnp.where(ti >= n_ctx_tiles, 1, 0), 0, 0)),
                  pl.BlockSpec((1, d), const),
                  pl.BlockSpec((N_EXPERTS, d), const),
                  pl.BlockSpec((N_EXPERTS, 1), const)],
        out_specs=[pl.BlockSpec((None, TM, d), tok),
                   pl.BlockSpec((None, TOP_K, TM), lambda bi, ti: (bi, 0, ti)),
                   pl.BlockSpec((None, TOP_K, TM), lambda bi, ti: (bi, 0, ti)),
                   pl.BlockSpec((None, None, N_EXPERTS, 1), lambda bi, ti: (bi, ti, 0, 0))],
        out_shape=[jax.ShapeDtypeStruct((b, l, d), BF16),
                   jax.ShapeDtypeStruct((b, TOP_K, l), jnp.int32),
                   jax.ShapeDtypeStruct((b, TOP_K, l), F32),
                   jax.ShapeDtypeStruct((b, l // TM, N_EXPERTS, 1), jnp.int32)],
        compiler_params=_params(("parallel", "parallel")),
        name="router",
    )(xs, mods, g_ffn, w_router_t, b_router)


def _pack_bf16_pairs(x):
    half = x.shape[1] // 2
    lo = lax.bitcast_convert_type(x[:, :half], jnp.uint32) >> 16
    hi = lax.bitcast_convert_type(x[:, half:], jnp.uint32) & jnp.uint32(0xFFFF0000)
    return lo | hi


def _unpack_bf16_pairs(w):
    lo = lax.bitcast_convert_type(w << 16, F32)
    hi = lax.bitcast_convert_type(w & jnp.uint32(0xFFFF0000), F32)
    return jnp.concatenate([lo, hi], axis=1).astype(BF16)


def _sort_kernel(idx_ref, off_ref, h_ref, posl_ref, ts_ref):
    tm = h_ref.shape[0]
    idx = idx_ref[...]
    eid = lax.broadcasted_iota(jnp.int32, (N_EXPERTS, tm), 0)
    hits = [eid == idx[k:k + 1, :] for k in range(TOP_K)]
    chosen = hits[0].astype(BF16)
    for hk in hits[1:]:
        chosen = chosen + hk.astype(BF16)
    r = lax.broadcasted_iota(jnp.int32, (tm, tm), 0)
    c = lax.broadcasted_iota(jnp.int32, (tm, tm), 1)
    before = (r < c).astype(BF16)
    rank = jnp.dot(chosen, before, preferred_element_type=F32)
    slot = rank.astype(jnp.int32) + off_ref[...]
    posl = jnp.concatenate([jnp.sum(jnp.where(hk, slot, 0), axis=0, keepdims=True) for hk in hits], axis=0)
    posl_ref[...] = posl
    hb = h_ref[...]
    for rb in range(ts_ref.shape[0] // SORT_CHUNK):
        rows = lax.broadcasted_iota(jnp.int32, (SORT_CHUNK, tm), 0) + rb * SORT_CHUNK
        sel = rows == posl[0:1, :]
        for k in range(1, TOP_K):
            sel = jnp.logical_or(sel, rows == posl[k:k + 1, :])
        onehot = jnp.where(sel, 1.0, 0.0).astype(BF16)
        ts = jnp.dot(onehot, hb, preferred_element_type=F32)
        ts_ref[rb * SORT_CHUNK:(rb + 1) * SORT_CHUNK, :] = _pack_bf16_pairs(ts)


def _sort_rows(idx_t, off, h_flat):
    b, k, l = idx_t.shape
    n, d = h_flat.shape
    nt = l // TM
    return pl.pallas_call(
        _sort_kernel,
        grid=(b * nt,),
        in_specs=[pl.BlockSpec((None, k, TM), lambda i: (i // nt, 0, i % nt)),
                  pl.BlockSpec((None, N_EXPERTS, 1), lambda i: (i, 0, 0)),
                  pl.BlockSpec((TM, d), lambda i: (i, 0))],
        out_specs=[pl.BlockSpec((None, k, TM), lambda i: (i // nt, 0, i % nt)),
                   pl.BlockSpec((None, SORT_ROWS, d // 2), lambda i: (i, 0, 0))],
        out_shape=[jax.ShapeDtypeStruct((b, k, l), jnp.int32),
                   jax.ShapeDtypeStruct((b * nt, SORT_ROWS, d // 2), jnp.uint32)],
        compiler_params=_params(("parallel",)),
        name="sort_rows",
    )(idx_t, off, h_flat)


SLAB_SIZES = tuple(SUBLANES << s for s in range((EXPERT_ROWS // SUBLANES).bit_length()))


def _expert_kernel(be_ref, nu_ref, cov_ref, grp_ref, ts_in, wg_ref, wu_ref, wd_ref, ts_out, xbuf, ybuf, sem_g, sem_s):
    j = pl.program_id(0)
    nu = nu_ref[0]
    blk = xbuf.shape[1]
    n_grp = blk // SUBLANES
    slot = j % 2

    def for_groups(bj, fn):
        for i in range(n_grp):
            sg = grp_ref[bj * n_grp + i]

            @pl.when(sg >= 0)
            def _():
                fn(pl.ds(pl.multiple_of(sg * SUBLANES, SUBLANES), SUBLANES), i * SUBLANES)

    def gather(bj, s):
        xbuf[s] = jnp.zeros(xbuf.shape[1:], xbuf.dtype)
        for_groups(bj, lambda src, row: pltpu.make_async_copy(
            ts_in.at[src], xbuf.at[s, pl.ds(row, SUBLANES)], sem_g.at[s]).start())

    def scatter(bj, s):
        for_groups(bj, lambda dst, row: pltpu.make_async_copy(
            ybuf.at[s, pl.ds(row, SUBLANES)], ts_out.at[dst], sem_s.at[s]).start())

    def wait_rows(bj, buf, sem, s):
        rows, off = cov_ref[bj], 0
        for sz in SLAB_SIZES:
            @pl.when((rows & sz) != 0)
            def _():
                pltpu.make_async_copy(ts_in.at[pl.ds(0, sz)], buf.at[s, pl.ds(0, sz)], sem.at[s]).wait()

    @pl.when(j == 0)
    def _():
        gather(0, 0)

    @pl.when(j + 1 < nu)
    def _():
        gather(j + 1, 1 - slot)

    @pl.when(jnp.logical_and(j >= 2, j - 2 < nu))
    def _():
        wait_rows(j - 2, ybuf, sem_s, slot)

    @pl.when(j < nu)
    def _():
        wait_rows(j, xbuf, sem_g, slot)
        x = _unpack_bf16_pairs(xbuf[slot])
        g = jnp.dot(x, wg_ref[...], preferred_element_type=F32)
        u = jnp.dot(x, wu_ref[...], preferred_element_type=F32)
        a = (g * _sigmoid(g) * u).astype(BF16)
        y = jnp.dot(a, wd_ref[...], preferred_element_type=F32)
        ybuf[slot] = _pack_bf16_pairs(y.astype(BF16).astype(F32))
        scatter(j, slot)


def _experts(plan, tiles, wg, wu, wd):
    nt, rows, w = tiles.shape
    blk = EXPERT_ROWS
    d, de = wg.shape[1:]
    n_blocks = plan["block_e"].shape[0]
    tables = (plan["block_e"], plan["n_used"], plan["blk_rows"], plan["blk_groups"])
    wspec = lambda shape: pl.BlockSpec((None,) + shape, lambda i, be, *_: (be[i], 0, 0))
    grid_spec = pltpu.PrefetchScalarGridSpec(
        num_scalar_prefetch=len(tables),
        grid=(n_blocks,),
        in_specs=[pl.BlockSpec(memory_space=pl.ANY), wspec((d, de)), wspec((d, de)), wspec((de, d))],
        out_specs=pl.BlockSpec(memory_space=pl.ANY),
        scratch_shapes=[pltpu.VMEM((2, blk, w), tiles.dtype), pltpu.VMEM((2, blk, w), tiles.dtype),
                        pltpu.SemaphoreType.DMA((2,)), pltpu.SemaphoreType.DMA((2,))])
    return pl.pallas_call(
        _expert_kernel,
        grid_spec=grid_spec,
        out_shape=jax.ShapeDtypeStruct((nt * rows, w), tiles.dtype),
        input_output_aliases={len(tables): 0},
        compiler_params=_params(("arbitrary",)),
        name="experts",
    )(*tables, tiles.reshape(nt * rows, w), wg, wu, wd).reshape(tiles.shape)


def _combine_kernel(ts_ref, posl_ref, w_ref, x_ref, h_ref, mod_ref, wsg_ref, wsu_ref, wsd_ref, *rest):
    tm = x_ref.shape[0]
    hb = h_ref[...]
    g = jnp.dot(hb, wsg_ref[...], preferred_element_type=F32)
    u = jnp.dot(hb, wsu_ref[...], preferred_element_type=F32)
    acc = jnp.dot((g * _sigmoid(g) * u).astype(BF16), wsd_ref[...], preferred_element_type=F32)
    posl = posl_ref[...]
    w = w_ref[...]
    for rb in range(ts_ref.shape[0] // SORT_CHUNK):
        cols = lax.broadcasted_iota(jnp.int32, (tm, SORT_CHUNK), 1) + rb * SORT_CHUNK
        wm = jnp.zeros((tm, SORT_CHUNK), F32)
        for k in range(TOP_K):
            wm = jnp.where(cols == posl[:, k:k + 1], w[:, k:k + 1], wm)
        ys = _unpack_bf16_pairs(ts_ref[rb * SORT_CHUNK:(rb + 1) * SORT_CHUNK, :])
        acc = acc + jnp.dot(wm.astype(BF16), ys, preferred_element_type=F32)
    out = x_ref[...] + mod_ref[5:6, :] * acc
    rest[-1][...] = _rms(out, rest[0][...]) if len(rest) == 2 else out


def _combine(tiles, posl_tm, wts, x_flat, h_flat, mods, wsg, wsu, wsd, tiles_per_sample, n_ctx_tiles, g_final=None):
    n, d = x_flat.shape
    ds_ = wsg.shape[-1]
    tok = lambda i: (i, 0)
    const = lambda i: (0, 0)

    def mod_idx(i):
        return (i // tiles_per_sample, jnp.where(i % tiles_per_sample >= n_ctx_tiles, 1, 0), 0, 0)

    in_extra, args_extra, out_rows, out_idx = [], [], n, tok
    if g_final is not None:
        lat_tiles = tiles_per_sample - n_ctx_tiles
        in_extra, args_extra = [pl.BlockSpec((1, d), const)], [g_final]
        out_rows = n // TM // tiles_per_sample * lat_tiles * TM
        out_idx = lambda i: (i // tiles_per_sample * lat_tiles + jnp.maximum(i % tiles_per_sample - n_ctx_tiles, 0), 0)
    return pl.pallas_call(
        _combine_kernel,
        grid=(n // TM,),
        in_specs=[pl.BlockSpec((None,) + tiles.shape[1:], lambda i: (i, 0, 0)),
                  pl.BlockSpec((TM, TOP_K), tok),
                  pl.BlockSpec((TM, TOP_K), tok),
                  pl.BlockSpec((TM, d), tok),
                  pl.BlockSpec((TM, d), tok),
                  pl.BlockSpec((None, None, 8, d), mod_idx),
                  pl.BlockSpec((d, ds_), const), pl.BlockSpec((d, ds_), const), pl.BlockSpec((ds_, d), const)]
                 + in_extra,
        out_specs=pl.BlockSpec((TM, d), out_idx),
        out_shape=jax.ShapeDtypeStruct((out_rows, d), F32),
        compiler_params=_params(("arbitrary",)),
        name="combine",
    )(tiles, posl_tm, wts, x_flat, h_flat, mods, wsg, wsu, wsd, *args_extra)


def _moe_plan(cnt, n_assign, blk):
    nt = cnt.shape[0]
    run = (cnt + SUBLANES - 1) // SUBLANES * SUBLANES
    tile_off = jnp.cumsum(run, axis=1) - run
    tot = jnp.sum(run, axis=0)
    padded = (tot + blk - 1) // blk * blk
    pad_end = jnp.cumsum(padded)
    pad_start = pad_end - padded
    row = (pad_start[None, :] + jnp.cumsum(run, axis=0) - run).T.reshape(-1)
    run_len = run.T.reshape(-1)
    n_blocks = -(-(n_assign + nt * N_EXPERTS * (SUBLANES - 1)) // blk) + N_EXPERTS + 2
    first_row = jnp.arange(n_blocks, dtype=jnp.int32) * blk
    count = lambda m: jnp.sum(m.astype(jnp.int32), axis=1)
    block_e = jnp.minimum(count(pad_end[None, :] <= first_row[:, None]), N_EXPERTS - 1)
    i32 = lambda a: a.astype(jnp.int32)
    grp_row = jnp.arange(n_blocks * (blk // SUBLANES), dtype=row.dtype) * SUBLANES
    grp_run = jnp.searchsorted(row, grp_row, side='right') - 1
    in_run = grp_row - row[grp_run] < run_len[grp_run]
    src_row = (grp_run % nt) * SORT_ROWS + tile_off.T.reshape(-1)[grp_run] + grp_row - row[grp_run]
    return dict(
        tile_off=i32(tile_off)[:, :, None],
        blk_groups=i32(jnp.where(in_run, src_row // SUBLANES, -1)),
        blk_rows=i32(jnp.clip((pad_start + tot)[block_e] - first_row, 0, blk)),
        block_e=i32(block_e), n_used=i32(pad_end[-1] // blk).reshape(1))


def _rope_tables(s_len, n_ctx):
    rows = s_len // GRID_W
    row = jnp.repeat(jnp.arange(rows), GRID_W).astype(F32)
    col = jnp.tile(jnp.arange(GRID_W), rows).astype(F32)
    quarter = HEAD_DIM // 4
    inv = 1.0 / (ROPE_BASE ** (jnp.arange(quarter, dtype=F32) / quarter))
    ar, ac = row[:, None] * inv, col[:, None] * inv
    cr, sr, cc, sc = jnp.cos(ar), jnp.sin(ar), jnp.cos(ac), jnp.sin(ac)
    z = jnp.zeros_like(sr)
    cos = jnp.concatenate([cr, cr, cc, cc], axis=1)
    sa = jnp.concatenate([z, sr, z, sc], axis=1)
    sb = jnp.concatenate([-sr, z, -sc, z], axis=1)
    rep = LANES // HEAD_DIM

    def full(t, fill):
        t = jnp.tile(t, (1, rep))
        return jnp.concatenate([jnp.full((n_ctx, LANES), fill, F32), t], axis=0)

    return full(cos, 1.0), full(sa, 0.0), full(sb, 0.0)


def _pair_perm():
    g = N_HEADS_A // KV_HEADS_A
    heads = [h for t in range(g) for h in (t, t + g)]
    return jnp.concatenate([jnp.arange(h * HEAD_DIM, (h + 1) * HEAD_DIM) for h in heads])


def _split_w_in(w):
    a_q, a_kv = N_HEADS_A * HEAD_DIM, KV_HEADS_A * HEAD_DIM
    b_w = N_HEADS_B * 2 * HEAD_DIM
    c_w = N_HEADS_C * HEAD_DIM_C
    sizes = (a_q, a_kv, a_kv, b_w, b_w, b_w, c_w, c_w, c_w, c_w, 4 * N_HEADS_C, w.shape[1])
    parts, start = [], 0
    for sz in sizes[:-1]:
        parts.append(w[:, start:start + sz])
        start += sz
    parts.append(w[:, start:])
    return parts


def _pack_w_in(w):
    d = w.shape[0]
    aq, ak, av, bq, bk, bv, cq, ck, cv, co, cg, gt = _split_w_in(w)
    pad = lambda n: jnp.zeros((d, n), w.dtype)
    kva = jnp.concatenate([ak, av, cg, pad(TILE_N - ak.shape[1] - av.shape[1] - cg.shape[1])], axis=1)
    big = jnp.concatenate([aq[:, _pair_perm()], bq, bk, kva, bv, co, gt, cq, cv], axis=1)
    return big.astype(BF16), ck.T.astype(BF16)


def kernel(x, c, ctx, c_ctx, w_mod, b_mod, g_mix, g_ffn, w_in, b_gate, sink, lam_q1, lam_k1, lam_q2, lam_k2,
           g_diff, g_mlstm, w_a, w_b, w_c, w_out, w_router, b_router, w_exp_gate, w_exp_up, w_exp_down,
           w_sh_gate, w_sh_up, w_sh_down, g_final):
    b, s_len, d = x.shape
    n_ctx = ctx.shape[1]
    l = n_ctx + s_len
    depth = w_mod.shape[0]
    n_ctx_tiles = n_ctx // TM
    assert n_ctx % TM == 0 and s_len % TM == 0 and d % LANES == 0 and s_len % GRID_W == 0

    xs = jnp.concatenate([ctx, x], axis=1)
    cos, sa, sb = _rope_tables(s_len, n_ctx)

    rows_c = 16
    cs = jnp.concatenate([c, c_ctx[None], jnp.zeros((rows_c - b - 1, d), F32)], axis=0)
    mod_all = _mod_vectors(cs, w_mod, b_mod).reshape(depth, rows_c, N_MOD, d)
    mod_all = jnp.pad(mod_all, ((0, 0), (0, 0), (0, 8 - N_MOD), (0, 0)))

    perm = _pair_perm()
    for layer in range(depth):
        lam_init = 0.8 - 0.6 * math.exp(-0.3 * layer)
        mods = jnp.stack([jnp.broadcast_to(mod_all[layer, b], (b, 8, d)), mod_all[layer, :b]], axis=1)
        w_big, w_kt = _pack_w_in(w_in[layer])
        p, gates, kt = _inproj(xs, mods, g_mix[layer][None], w_big, w_kt, cos, sa, sb, n_ctx_tiles)

        oa = _mixer_a(p, sink[layer], n_ctx)
        lam_params = jnp.stack([lam_q1[layer], lam_k1[layer], lam_q2[layer], lam_k2[layer]])
        ob_ctx, ob_lat = _mixer_b(p, lam_params, g_diff[layer][None], lam_init, n_ctx)

        bias = b_gate[layer].reshape(-1)
        bias_row = jnp.pad(bias, (0, LANES - bias.shape[0]))[None]
        bias_col = jnp.broadcast_to(bias[:, None], (bias.shape[0], LANES))
        gates_t = jnp.transpose(gates[:, :, :bias.shape[0]], (0, 2, 1))
        hm = _mlstm(p, kt, gates, gates_t, bias_row, bias_col, n_ctx)

        xs = _merge(xs, mods, oa, ob_ctx, ob_lat, hm, p, g_mlstm[layer][None],
                    w_a[layer][perm].astype(BF16), w_b[layer].astype(BF16), w_c[layer].astype(BF16),
                    w_out[layer].astype(BF16), n_ctx_tiles)

        h, idx_t, wt_t, cnt = _router(xs, mods, g_ffn[layer][None], w_router[layer].T, b_router[layer][:, None],
                                      n_ctx_tiles)
        plan = _moe_plan(cnt.reshape(-1, N_EXPERTS), b * l * TOP_K, EXPERT_ROWS)
        h_flat = h.reshape(b * l, d)
        posl, tiles = _sort_rows(idx_t, plan["tile_off"], h_flat)
        tiles = _experts(plan, tiles, w_exp_gate[layer].astype(BF16), w_exp_up[layer].astype(BF16),
                         w_exp_down[layer].astype(BF16))
        to_rows = lambda a: jnp.transpose(a, (0, 2, 1)).reshape(b * l, TOP_K)
        last = layer == depth - 1
        xs = _combine(tiles, to_rows(posl), to_rows(wt_t), xs.reshape(b * l, d), h_flat, mods,
                      w_sh_gate[layer].astype(BF16), w_sh_up[layer].astype(BF16), w_sh_down[layer].astype(BF16),
                      l // TM, n_ctx_tiles, g_final[None] if last else None)
        xs = xs.reshape(b, s_len if last else l, d)
    return xs
```

```python
import functools
import math

import jax
import jax.numpy as jnp
from jax import lax
from jax.experimental import pallas as pl
from jax.experimental.pallas import tpu as pltpu

F32 = jnp.float32
BF16 = jnp.bfloat16
HIGHEST = lax.Precision.HIGHEST

GRID_W = 64
N_MOD = 6
HEAD_DIM = 64
N_HEADS_A = 8
KV_HEADS_A = 2
WINDOW = 128
N_HEADS_B = 4
N_HEADS_C = 4
HEAD_DIM_C = 128
N_EXPERTS = 64
N_GROUPS = 8
TOPK_GROUPS = 4
TOP_K = 8
ROUTED_SCALE = 2.5
ROPE_BASE = 10000.0
EPS = 1e-6

LANES = 128
CHUNK = 128
TILE_N = 512
TM = 256
MIXB_KEYS = 1024
MIXB_QUERIES = 512
EXPERT_ROWS = 512
SUBLANES = 8
SORT_ROWS = TOP_K * TM + N_EXPERTS * SUBLANES
SORT_CHUNK = 512
NEG = -1e30
VMEM_LIMIT = 56 * 1024 * 1024

T_AQ, T_BQ, T_BK, T_KVA, T_BV, T_CO, T_GT, T_CQ, T_CV, N_TILES = 0, 1, 2, 3, 4, 5, 6, 12, 13, 14

NT_DIMS = (((1,), (1,)), ((), ()))


def _params(sem):
    return pltpu.CompilerParams(dimension_semantics=sem, vmem_limit_bytes=VMEM_LIMIT)


def _rms(x, g):
    return x * lax.rsqrt(jnp.mean(x * x, axis=-1, keepdims=True) + EPS) * g


def _sigmoid(x):
    return jax.nn.sigmoid(x)


def _mod_kernel(c_ref, w_ref, b_ref, o_ref):
    c = c_ref[...]
    s = c * _sigmoid(c)
    o_ref[...] = jnp.dot(s, w_ref[...], precision=HIGHEST, preferred_element_type=F32) + b_ref[...]


def _mod_vectors(cs, w_mod, b_mod):
    depth, d, n = w_mod.shape
    r = cs.shape[0]
    tn = 3 * LANES
    return pl.pallas_call(
        _mod_kernel,
        grid=(depth, n // tn),
        in_specs=[pl.BlockSpec((r, d), lambda l, j: (0, 0)),
                  pl.BlockSpec((None, d, tn), lambda l, j: (l, 0, j)),
                  pl.BlockSpec((None, 1, tn), lambda l, j: (l, 0, j))],
        out_specs=pl.BlockSpec((None, r, tn), lambda l, j: (l, 0, j)),
        out_shape=jax.ShapeDtypeStruct((depth, r, n), F32),
        compiler_params=_params(("parallel", "parallel")),
        name="mod_vectors",
    )(cs, w_mod, b_mod.reshape(depth, 1, n))


def _inproj_kernel(x_ref, mod_ref, g_ref, w_ref, wkt_ref, cos_ref, sa_ref, sb_ref, p_ref, gate_ref, kt_ref):
    x = x_ref[...]
    h = _rms(x, g_ref[...]) * (1.0 + mod_ref[1:2, :]) + mod_ref[0:1, :]
    hb = h.astype(BF16)
    cos, sa, sb = cos_ref[...], sa_ref[...], sb_ref[...]

    def rope(t):
        return t * cos + pltpu.roll(t, 16, 1) * sa + pltpu.roll(t, LANES - 16, 1) * sb

    q_scale = HEAD_DIM ** -0.5
    for j in range(N_TILES):
        acc = jnp.dot(hb, w_ref[:, j * TILE_N:(j + 1) * TILE_N], preferred_element_type=F32)
        parts = [acc[:, s * LANES:(s + 1) * LANES] for s in range(TILE_N // LANES)]
        if j == T_AQ:
            parts = [rope(t) * q_scale for t in parts]
        elif j == T_BQ:
            parts = [rope(t) * (q_scale * math.log2(math.e)) for t in parts]
        elif j == T_BK:
            parts = [rope(t) for t in parts]
        elif j == T_KVA:
            gate_ref[...] = parts[2]
            parts[0] = rope(parts[0])
        for s, t in enumerate(parts):
            p_ref[:, j * TILE_N + s * LANES:j * TILE_N + (s + 1) * LANES] = t.astype(BF16)
    kt = lax.dot_general(wkt_ref[...], hb, NT_DIMS, preferred_element_type=F32)
    kt_ref[...] = (kt * (HEAD_DIM_C ** -0.5)).astype(BF16)


def _inproj(xs, mods, g_mix, w_big, w_kt, cos, sa, sb, n_ctx_tiles):
    b, l, d = xs.shape
    npad = w_big.shape[1]
    ck = w_kt.shape[0]
    grid = (b, l // TM)
    tok = lambda bi, ti: (bi, ti, 0)
    return pl.pallas_call(
        _inproj_kernel,
        grid=grid,
        in_specs=[pl.BlockSpec((None, TM, d), tok),
                  pl.BlockSpec((None, None, 8, d), lambda bi, ti: (bi, jnp.where(ti >= n_ctx_tiles, 1, 0), 0, 0)),
                  pl.BlockSpec((1, d), lambda bi, ti: (0, 0)),
                  pl.BlockSpec((d, npad), lambda bi, ti: (0, 0), pipeline_mode=pl.Buffered(1)),
                  pl.BlockSpec((ck, d), lambda bi, ti: (0, 0), pipeline_mode=pl.Buffered(1)),
                  pl.BlockSpec((TM, LANES), lambda bi, ti: (ti, 0)),
                  pl.BlockSpec((TM, LANES), lambda bi, ti: (ti, 0)),
                  pl.BlockSpec((TM, LANES), lambda bi, ti: (ti, 0))],
        out_specs=[pl.BlockSpec((None, TM, npad), tok),
                   pl.BlockSpec((None, TM, LANES), tok),
                   pl.BlockSpec((None, ck, TM), lambda bi, ti: (bi, 0, ti))],
        out_shape=[jax.ShapeDtypeStruct((b, l, npad), BF16),
                   jax.ShapeDtypeStruct((b, l, LANES), F32),
                   jax.ShapeDtypeStruct((b, ck, l), BF16)],
        compiler_params=_params(("parallel", "parallel")),
        name="inproj",
    )(xs, mods, g_mix, w_big, w_kt, cos, sa, sb)


def _mixa_kernel(sink_ref, q_ref, kp_ref, kc_ref, kn_ref, kx_ref, o_ref, *, n_ctx_blocks, n_blocks):
    i = pl.program_id(1)
    lat = i >= n_ctx_blocks
    has_prev = jnp.logical_and(lat, i > n_ctx_blocks)
    has_next = jnp.logical_and(lat, i < n_blocks - 1)
    r = lax.broadcasted_iota(jnp.int32, (CHUNK, CHUNK), 0)
    c = lax.broadcasted_iota(jnp.int32, (CHUNK, CHUNK), 1)
    n_ctx = kx_ref.shape[0]
    valid = jnp.concatenate([
        jnp.logical_and(c >= r, has_prev),
        jnp.broadcast_to(lat, (CHUNK, CHUNK)),
        jnp.logical_and(c <= r, has_next),
        jnp.ones((CHUNK, n_ctx), jnp.bool_)], axis=1)
    kcat = jnp.concatenate([kp_ref[:, :LANES], kc_ref[:, :LANES], kn_ref[:, :LANES], kx_ref[:, :LANES]], axis=0)
    vcat = jnp.concatenate([kp_ref[:, LANES:], kc_ref[:, LANES:], kn_ref[:, LANES:], kx_ref[:, LANES:]], axis=0)
    lane = lax.broadcasted_iota(jnp.int32, (CHUNK, LANES), 1)
    low = lane < HEAD_DIM
    n_pairs = N_HEADS_A // KV_HEADS_A
    outs = []
    for gk in range(KV_HEADS_A):
        keep = low if gk == 0 else jnp.logical_not(low)
        zero = jnp.zeros((CHUNK, LANES), BF16)
        lhs = jnp.concatenate([jnp.where(keep, q_ref[:, t * LANES:(t + 1) * LANES], zero) for t in range(n_pairs)],
                              axis=0)
        s = lax.dot_general(lhs, kcat, NT_DIMS, preferred_element_type=F32)
        o_g = []
        for t in range(n_pairs):
            st = jnp.where(valid, s[t * CHUNK:(t + 1) * CHUNK], NEG)
            sk = sink_ref[gk * n_pairs + t]
            m = jnp.maximum(jnp.max(st, axis=-1, keepdims=True), sk)
            p = jnp.exp(st - m)
            den = jnp.sum(p, axis=-1, keepdims=True) + jnp.exp(sk - m)
            o_g.append(jnp.dot(p.astype(BF16), vcat, preferred_element_type=F32) / den)
        outs.append(o_g)
    for t in range(n_pairs):
        o_ref[:, t * LANES:(t + 1) * LANES] = jnp.where(low, outs[0][t], outs[1][t]).astype(BF16)


def _mixer_a(p, sink, n_ctx):
    b, l, _ = p.shape
    nb = l // CHUNK
    ncb = n_ctx // CHUNK
    kvw = 2 * LANES
    kv_col = T_KVA * TILE_N // kvw
    aq_w = N_HEADS_A * HEAD_DIM
    kern = functools.partial(_mixa_kernel, n_ctx_blocks=ncb, n_blocks=nb)
    return pl.pallas_call(
        kern,
        grid=(b, nb),
        in_specs=[pl.BlockSpec(memory_space=pltpu.SMEM),
                  pl.BlockSpec((None, CHUNK, aq_w), lambda bi, i: (bi, i, T_AQ)),
                  pl.BlockSpec((None, CHUNK, kvw), lambda bi, i: (bi, jnp.maximum(i - 1, 0), kv_col)),
                  pl.BlockSpec((None, CHUNK, kvw), lambda bi, i: (bi, i, kv_col)),
                  pl.BlockSpec((None, CHUNK, kvw), lambda bi, i: (bi, jnp.minimum(i + 1, nb - 1), kv_col)),
                  pl.BlockSpec((None, n_ctx, kvw), lambda bi, i: (bi, 0, kv_col))],
        out_specs=pl.BlockSpec((None, CHUNK, aq_w), lambda bi, i: (bi, i, 0)),
        out_shape=jax.ShapeDtypeStruct((b, l, aq_w), BF16),
        compiler_params=_params(("parallel", "parallel")),
        name="mixer_a",
    )(sink, p, p, p, p, p)


def _fold_lanes(op, acc, s):
    for t in range(s.shape[1] // LANES):
        acc = op(acc, s[:, t * LANES:(t + 1) * LANES])
    return acc


def _mixb_kernel(lam_ref, gd_ref, k_ref, v_ref, *rest, lam_init, chunks):
    q_refs, (o_ref, s_scr, va_scr) = rest[:-3], rest[-3:]

    @pl.when(pl.program_id(2) == 0)
    def _():
        n_keys = v_ref.shape[0]
        va_scr[:, :LANES] = v_ref[...]
        va_scr[:, LANES:] = (lax.broadcasted_iota(jnp.int32, (n_keys, LANES), 1) == 0).astype(BF16)

    lp = lam_ref[...]
    lam = (jnp.exp(jnp.sum(lp[0:1] * lp[1:2], axis=-1, keepdims=True))
           - jnp.exp(jnp.sum(lp[2:3] * lp[3:4], axis=-1, keepdims=True)) + lam_init)
    q = jnp.concatenate([qr[...] for qr in q_refs], axis=0)
    tq = q.shape[0]
    lane = lax.broadcasted_iota(jnp.int32, (tq, LANES), 1)
    zero = jnp.zeros_like(q)
    qs = (jnp.where(lane < HEAD_DIM, q, zero), jnp.where(lane >= HEAD_DIM, q, zero))
    rows = [slice(mi * tq, (mi + 1) * tq) for mi in range(2)]
    mrun = [jnp.full((tq, LANES), NEG, F32) for _ in range(2)]
    for off, sz in chunks:
        for mi in range(2):
            s_scr[rows[mi], off:off + sz] = lax.dot_general(qs[mi], k_ref[off:off + sz, :], NT_DIMS,
                                                            preferred_element_type=F32)
            mrun[mi] = _fold_lanes(jnp.maximum, mrun[mi], s_scr[rows[mi], off:off + sz])
    m = [jnp.max(mr, axis=-1, keepdims=True) for mr in mrun]
    acc = [jnp.zeros((tq, 2 * LANES), F32) for _ in range(2)]
    for off, sz in chunks:
        for mi in range(2):
            pr = jnp.exp2(s_scr[rows[mi], off:off + sz] - m[mi])
            acc[mi] = acc[mi] + jnp.dot(pr.astype(BF16), va_scr[off:off + sz, :], preferred_element_type=F32)
    outs = [a[:, :LANES] / a[:, LANES:LANES + 1] for a in acc]
    o = outs[0] - lam * outs[1]
    o_ref[...] = (_rms(o, gd_ref[...]) * (1.0 - lam_init)).astype(BF16)


def _mixer_b(p, lam_params, g_diff, lam_init, n_ctx):
    b, l, _ = p.shape
    kl = min(MIXB_KEYS, l - n_ctx)
    tq_lat = min(MIXB_QUERIES, l - n_ctx)
    assert (l - n_ctx) % kl == 0 and (l - n_ctx) % tq_lat == 0 and tq_lat % TM == 0 and n_ctx % TM == 0
    q0 = T_BQ * TILE_N // LANES
    k0 = T_BK * TILE_N // LANES
    v0 = T_BV * TILE_N // LANES
    ctx_chunks = ((0, n_ctx),)
    all_chunks = ctx_chunks + tuple((n_ctx + c * kl, kl) for c in range((l - n_ctx) // kl))

    def call(chunks, n_keys, tq, q_tiles, first_row):
        nq = tq // TM
        q_spec = lambda part: pl.BlockSpec((None, TM, LANES),
                                           lambda bi, h, qi: (bi, first_row // TM + qi * nq + part, q0 + h))
        kern = functools.partial(_mixb_kernel, lam_init=lam_init, chunks=chunks)
        return pl.pallas_call(
            kern,
            grid=(b, N_HEADS_B, q_tiles),
            in_specs=[pl.BlockSpec((4, HEAD_DIM), lambda bi, h, qi: (0, 0)),
                      pl.BlockSpec((1, LANES), lambda bi, h, qi: (0, 0)),
                      pl.BlockSpec((None, n_keys, LANES), lambda bi, h, qi: (bi, 0, k0 + h)),
                      pl.BlockSpec((None, n_keys, LANES), lambda bi, h, qi: (bi, 0, v0 + h))]
                     + [q_spec(part) for part in range(nq)],
            out_specs=pl.BlockSpec((None, tq, LANES), lambda bi, h, qi: (bi, qi, h)),
            out_shape=jax.ShapeDtypeStruct((b, q_tiles * tq, N_HEADS_B * LANES), BF16),
            scratch_shapes=[pltpu.VMEM((2 * tq, n_keys), F32), pltpu.VMEM((n_keys, 2 * LANES), BF16)],
            compiler_params=_params(("parallel", "parallel", "arbitrary")),
            name="mixer_b",
        )(lam_params, g_diff, p, p, *([p] * nq))

    tq_ctx = min(tq_lat, n_ctx)
    return (call(ctx_chunks, n_ctx, tq_ctx, n_ctx // tq_ctx, 0),
            call(all_chunks, l, tq_lat, (l - n_ctx) // tq_lat, n_ctx))


def _log_sigmoid(x):
    return jnp.minimum(x, 0.0) - jnp.log1p(jnp.exp(-jnp.abs(x)))


def _mlstm_kernel(q_ref, kt_ref, v_ref, gc_ref, gr_ref, bc_ref, br_ref, o_ref, s_scr, m_scr):
    d = pl.program_id(1)
    c = pl.program_id(2)

    @pl.when(c == 0)
    def _():
        s_scr[...] = jnp.zeros_like(s_scr)
        m_scr[...] = jnp.zeros_like(m_scr)

    fwd = d == 0
    r = lax.broadcasted_iota(jnp.int32, (CHUNK, CHUNK), 0)
    cc = lax.broadcasted_iota(jnp.int32, (CHUNK, CHUNK), 1)
    tri = jnp.where(fwd, r - cc, cc - r) >= 0
    trif = tri.astype(F32)
    gcol = gc_ref[...] + bc_ref[...]
    grow = gr_ref[...] + br_ref[...]
    lf_col = _log_sigmoid(gcol)
    lf_row = _log_sigmoid(grow)
    bcum_col = jnp.dot(trif, lf_col, precision=HIGHEST, preferred_element_type=F32)
    bcum_row = lax.dot_general(lf_row, trif, NT_DIMS, precision=HIGHEST, preferred_element_type=F32)
    tot_row = jnp.sum(lf_row, axis=-1, keepdims=True)
    lane = lax.broadcasted_iota(jnp.int32, (CHUNK, LANES), 1)
    ones_col = (lane == 0).astype(BF16)
    nh = N_HEADS_C

    for h in range(nh):
        def pick_col(a, kind):
            return jnp.where(fwd, a[:, kind * nh + h:kind * nh + h + 1],
                             a[:, (kind + 2) * nh + h:(kind + 2) * nh + h + 1])

        def pick_row(a, kind):
            return jnp.where(fwd, a[kind * nh + h:kind * nh + h + 1, :],
                             a[(kind + 2) * nh + h:(kind + 2) * nh + h + 1, :])

        ic_row = pick_row(grow, 0)
        b_col = pick_col(bcum_col, 1)
        b_row = pick_row(bcum_row, 1)
        total = pick_row(tot_row, 1)
        m_st = m_scr[h, 0:1, 0:1]
        dm = jnp.where(tri, b_col - b_row + ic_row, NEG)
        inter = b_col + m_st
        m_t = jnp.maximum(inter, jnp.max(dm, axis=-1, keepdims=True))
        e = jnp.exp(dm - m_t)
        qh = q_ref[:, h * LANES:(h + 1) * LANES]
        kth = kt_ref[h * LANES:(h + 1) * LANES, :]
        vaug = jnp.concatenate([v_ref[:, h * LANES:(h + 1) * LANES], ones_col], axis=1)
        s = jnp.dot(qh, kth, preferred_element_type=F32) * e
        st = s_scr[h]
        intra = jnp.dot(s.astype(BF16), vaug, preferred_element_type=F32)
        cross = jnp.dot(qh, st.astype(BF16), preferred_element_type=F32)
        nd = intra + jnp.exp(inter - m_t) * cross
        den = nd[:, LANES:LANES + 1]
        o_ref[:, h * LANES:(h + 1) * LANES] = nd[:, :LANES] / jnp.maximum(jnp.abs(den), jnp.exp(-m_t))
        gs_row = total - b_row + ic_row
        m_new = jnp.maximum(total + m_st, jnp.max(gs_row, axis=-1, keepdims=True))
        decay = jnp.exp(total + m_st - m_new)
        wkt = (kth.astype(F32) * jnp.exp(gs_row - m_new)).astype(BF16)
        s_scr[h] = decay * st + jnp.dot(wkt, vaug, preferred_element_type=F32)
        m_scr[h] = jnp.broadcast_to(m_new, m_scr.shape[1:])


def _mlstm(p, kt, gates, gates_t, bias_row, bias_col, n_ctx):
    b, l, _ = p.shape
    nc = l // CHUNK
    ncc = n_ctx // CHUNK
    cw = N_HEADS_C * HEAD_DIM_C

    def chunk(d, c):
        rev = jnp.where(c < ncc, ncc - 1 - c, nc + ncc - 1 - c)
        return jnp.where(d == 0, c, rev)

    return pl.pallas_call(
        _mlstm_kernel,
        grid=(b, 2, nc),
        in_specs=[pl.BlockSpec((None, CHUNK, cw), lambda bi, d, c: (bi, chunk(d, c), T_CQ)),
                  pl.BlockSpec((None, cw, CHUNK), lambda bi, d, c: (bi, 0, chunk(d, c))),
                  pl.BlockSpec((None, CHUNK, cw), lambda bi, d, c: (bi, chunk(d, c), T_CV)),
                  pl.BlockSpec((None, CHUNK, LANES), lambda bi, d, c: (bi, chunk(d, c), 0)),
                  pl.BlockSpec((None, 16, CHUNK), lambda bi, d, c: (bi, 0, chunk(d, c))),
                  pl.BlockSpec((1, LANES), lambda bi, d, c: (0, 0)),
                  pl.BlockSpec((16, LANES), lambda bi, d, c: (0, 0))],
        out_specs=pl.BlockSpec((None, None, CHUNK, cw), lambda bi, d, c: (d, bi, chunk(d, c), 0)),
        out_shape=jax.ShapeDtypeStruct((2, b, l, cw), F32),
        scratch_shapes=[pltpu.VMEM((N_HEADS_C, HEAD_DIM_C, 2 * LANES), F32),
                        pltpu.VMEM((N_HEADS_C, 8, LANES), F32)],
        compiler_params=_params(("parallel", "parallel", "arbitrary")),
        name="mlstm",
    )(p, kt, p, gates, gates_t, bias_row, bias_col)


def _merge_kernel(x_ref, mod_ref, oa_ref, obc_ref, obl_ref, hf_ref, hb_ref, co_ref, gt_ref, gm_ref,
                  wa_ref, wb_ref, wc_ref, wo_ref, xo_ref, *, n_ctx_tiles):
    d = x_ref.shape[-1]
    ob = jnp.where(pl.program_id(1) < n_ctx_tiles, obc_ref[...], obl_ref[...])
    hs = hf_ref[...] + hb_ref[...]
    co = co_ref[...].astype(F32)
    gm = gm_ref[...]
    oc = []
    for h in range(N_HEADS_C):
        sl = slice(h * LANES, (h + 1) * LANES)
        oc.append((_rms(hs[:, sl], gm[:, sl]) * _sigmoid(co[:, sl])).astype(BF16))
    oc = jnp.concatenate(oc, axis=1)
    y = (_sigmoid(gt_ref[:, 0:d].astype(F32)) * jnp.dot(oa_ref[...], wa_ref[...], preferred_element_type=F32)
         + _sigmoid(gt_ref[:, d:2 * d].astype(F32)) * jnp.dot(ob, wb_ref[...], preferred_element_type=F32)
         + _sigmoid(gt_ref[:, 2 * d:3 * d].astype(F32)) * jnp.dot(oc, wc_ref[...], preferred_element_type=F32))
    out = jnp.dot(y.astype(BF16), wo_ref[...], preferred_element_type=F32)
    xo_ref[...] = x_ref[...] + mod_ref[2:3, :] * out


def _merge(xs, mods, oa, ob_ctx, ob_lat, hm, p, g_mlstm, wa, wb, wc, wo, n_ctx_tiles):
    b, l, d = xs.shape
    tok = lambda bi, ti: (bi, ti, 0)
    cw = N_HEADS_C * HEAD_DIM_C
    const = lambda bi, ti: (0, 0)
    return pl.pallas_call(
        functools.partial(_merge_kernel, n_ctx_tiles=n_ctx_tiles),
        grid=(b, l // TM),
        in_specs=[pl.BlockSpec((None, TM, d), tok),
                  pl.BlockSpec((None, None, 8, d), lambda bi, ti: (bi, jnp.where(ti >= n_ctx_tiles, 1, 0), 0, 0)),
                  pl.BlockSpec((None, TM, oa.shape[-1]), tok),
                  pl.BlockSpec((None, TM, ob_ctx.shape[-1]), lambda bi, ti: (bi, jnp.minimum(ti, n_ctx_tiles - 1), 0)),
                  pl.BlockSpec((None, TM, ob_lat.shape[-1]), lambda bi, ti: (bi, jnp.maximum(ti - n_ctx_tiles, 0), 0)),
                  pl.BlockSpec((None, None, TM, cw), lambda bi, ti: (0, bi, ti, 0)),
                  pl.BlockSpec((None, None, TM, cw), lambda bi, ti: (1, bi, ti, 0)),
                  pl.BlockSpec((None, TM, cw), lambda bi, ti: (bi, ti, T_CO)),
                  pl.BlockSpec((None, TM, 3 * d), lambda bi, ti: (bi, ti, T_GT * TILE_N // (3 * d))),
                  pl.BlockSpec((1, cw), const),
                  pl.BlockSpec(wa.shape, const), pl.BlockSpec(wb.shape, const),
                  pl.BlockSpec(wc.shape, const), pl.BlockSpec(wo.shape, const)],
        out_specs=pl.BlockSpec((None, TM, d), tok),
        out_shape=jax.ShapeDtypeStruct((b, l, d), F32),
        compiler_params=_params(("parallel", "parallel")),
        name="merge",
    )(xs, mods, oa, ob_ctx, ob_lat, hm, hm, p, p, g_mlstm, wa, wb, wc, wo)


def _router_kernel(x_ref, mod_ref, g_ref, wrt_ref, br_ref, h_ref, idx_ref, wt_ref, cnt_ref):
    h = _rms(x_ref[...], g_ref[...]) * (1.0 + mod_ref[4:5, :]) + mod_ref[3:4, :]
    h_ref[...] = h.astype(BF16)
    tm = h.shape[0]
    per = N_EXPERTS // N_GROUPS
    lt = lax.dot_general(wrt_ref[...], h, NT_DIMS, precision=HIGHEST, preferred_element_type=F32)
    s = _sigmoid(lt)
    sel = s + br_ref[...]
    ninf = -jnp.inf
    sel3 = sel.reshape(N_GROUPS, per, tm)
    eidx = lax.broadcasted_iota(jnp.int32, (N_GROUPS, per, tm), 1)
    m1 = jnp.max(sel3, axis=1, keepdims=True)
    first = jnp.min(jnp.where(sel3 == m1, eidx, per), axis=1, keepdims=True)
    m2 = jnp.max(jnp.where(eidx == first, ninf, sel3), axis=1, keepdims=True)
    gscore = (m1 + m2).reshape(N_GROUPS, tm)
    gidx = lax.broadcasted_iota(jnp.int32, (N_GROUPS, tm), 0)
    gmask = jnp.zeros((N_GROUPS, tm), jnp.bool_)
    cur = gscore
    for _ in range(TOPK_GROUPS):
        mx = jnp.max(cur, axis=0, keepdims=True)
        hit = gidx == jnp.min(jnp.where(cur == mx, gidx, N_GROUPS), axis=0, keepdims=True)
        gmask = jnp.logical_or(gmask, hit)
        cur = jnp.where(hit, ninf, cur)
    cur = jnp.where(gmask.reshape(N_GROUPS, 1, tm), sel3, ninf).reshape(N_EXPERTS, tm)
    eid = lax.broadcasted_iota(jnp.int32, (N_EXPERTS, tm), 0)
    ids, ws = [], []
    chosen = jnp.zeros((N_EXPERTS, tm), F32)
    for _ in range(TOP_K):
        mx = jnp.max(cur, axis=0, keepdims=True)
        pick = jnp.min(jnp.where(cur == mx, eid, N_EXPERTS), axis=0, keepdims=True)
        hit = eid == pick
        ids.append(pick)
        ws.append(jnp.sum(jnp.where(hit, s, 0.0), axis=0, keepdims=True))
        cur = jnp.where(hit, ninf, cur)
        chosen = chosen + hit.astype(F32)
    wsum = ws[0]
    for w in ws[1:]:
        wsum = wsum + w
    idx_ref[...] = jnp.concatenate(ids, axis=0)
    wt_ref[...] = jnp.concatenate([w / wsum * ROUTED_SCALE for w in ws], axis=0)
    cnt_ref[...] = jnp.sum(chosen, axis=1, keepdims=True).astype(jnp.int32)


def _router(xs, mods, g_ffn, w_router_t, b_router, n_ctx_tiles):
    b, l, d = xs.shape
    tok = lambda bi, ti: (bi, ti, 0)
    const = lambda bi, ti: (0, 0)
    return pl.pallas_call(
        _router_kernel,
        grid=(b, l // TM),
        in_specs=[pl.BlockSpec((None, TM, d), tok),
                  pl.BlockSpec((None, None, 8, d), lambda bi, ti: (bi, jnp.where(ti >= n_ctx_tiles, 1, 0), 0, 0)),
                  pl.BlockSpec((1, d), const),
                  pl.BlockSpec((N_EXPERTS, d), const),
                  pl.BlockSpec((N_EXPERTS, 1), const)],
        out_specs=[pl.BlockSpec((None, TM, d), tok),
                   pl.BlockSpec((None, TOP_K, TM), lambda bi, ti: (bi, 0, ti)),
                   pl.BlockSpec((None, TOP_K, TM), lambda bi, ti: (bi, 0, ti)),
                   pl.BlockSpec((None, None, N_EXPERTS, 1), lambda bi, ti: (bi, ti, 0, 0))],
        out_shape=[jax.ShapeDtypeStruct((b, l, d), BF16),
                   jax.ShapeDtypeStruct((b, TOP_K, l), jnp.int32),
                   jax.ShapeDtypeStruct((b, TOP_K, l), F32),
                   jax.ShapeDtypeStruct((b, l // TM, N_EXPERTS, 1), jnp.int32)],
        compiler_params=_params(("parallel", "parallel")),
        name="router",
    )(xs, mods, g_ffn, w_router_t, b_router)


def _pack_bf16_pairs(x):
    half = x.shape[1] // 2
    lo = lax.bitcast_convert_type(x[:, :half], jnp.uint32) >> 16
    hi = lax.bitcast_convert_type(x[:, half:], jnp.uint32) & jnp.uint32(0xFFFF0000)
    return lo | hi


def _unpack_bf16_pairs(w):
    lo = lax.bitcast_convert_type(w << 16, F32)
    hi = lax.bitcast_convert_type(w & jnp.uint32(0xFFFF0000), F32)
    return jnp.concatenate([lo, hi], axis=1).astype(BF16)


def _sort_kernel(idx_ref, off_ref, h_ref, posl_ref, ts_ref):
    tm = h_ref.shape[0]
    idx = idx_ref[...]
    eid = lax.broadcasted_iota(jnp.int32, (N_EXPERTS, tm), 0)
    hits = [eid == idx[k:k + 1, :] for k in range(TOP_K)]
    chosen = hits[0].astype(BF16)
    for hk in hits[1:]:
        chosen = chosen + hk.astype(BF16)
    r = lax.broadcasted_iota(jnp.int32, (tm, tm), 0)
    c = lax.broadcasted_iota(jnp.int32, (tm, tm), 1)
    before = (r < c).astype(BF16)
    rank = jnp.dot(chosen, before, preferred_element_type=F32)
    slot = rank.astype(jnp.int32) + off_ref[...]
    posl = jnp.concatenate([jnp.sum(jnp.where(hk, slot, 0), axis=0, keepdims=True) for hk in hits], axis=0)
    posl_ref[...] = posl
    hb = h_ref[...]
    for rb in range(ts_ref.shape[0] // SORT_CHUNK):
        rows = lax.broadcasted_iota(jnp.int32, (SORT_CHUNK, tm), 0) + rb * SORT_CHUNK
        sel = rows == posl[0:1, :]
        for k in range(1, TOP_K):
            sel = jnp.logical_or(sel, rows == posl[k:k + 1, :])
        onehot = jnp.where(sel, 1.0, 0.0).astype(BF16)
        ts = jnp.dot(onehot, hb, preferred_element_type=F32)
        ts_ref[rb * SORT_CHUNK:(rb + 1) * SORT_CHUNK, :] = _pack_bf16_pairs(ts)


def _sort_rows(idx_t, off, h_flat):
    b, k, l = idx_t.shape
    n, d = h_flat.shape
    nt = l // TM
    return pl.pallas_call(
        _sort_kernel,
        grid=(b * nt,),
        in_specs=[pl.BlockSpec((None, k, TM), lambda i: (i // nt, 0, i % nt)),
                  pl.BlockSpec((None, N_EXPERTS, 1), lambda i: (i, 0, 0)),
                  pl.BlockSpec((TM, d), lambda i: (i, 0))],
        out_specs=[pl.BlockSpec((None, k, TM), lambda i: (i // nt, 0, i % nt)),
                   pl.BlockSpec((None, SORT_ROWS, d // 2), lambda i: (i, 0, 0))],
        out_shape=[jax.ShapeDtypeStruct((b, k, l), jnp.int32),
                   jax.ShapeDtypeStruct((b * nt, SORT_ROWS, d // 2), jnp.uint32)],
        compiler_params=_params(("parallel",)),
        name="sort_rows",
    )(idx_t, off, h_flat)


SLAB_SIZES = tuple(SUBLANES << s for s in range((EXPERT_ROWS // SUBLANES).bit_length()))


def _expert_kernel(be_ref, nu_ref, cov_ref, grp_ref, ts_in, wg_ref, wu_ref, wd_ref, ts_out, xbuf, ybuf, sem_g, sem_s):
    j = pl.program_id(0)
    nu = nu_ref[0]
    blk = xbuf.shape[1]
    n_grp = blk // SUBLANES
    slot = j % 2

    def for_groups(bj, fn):
        for i in range(n_grp):
            sg = grp_ref[bj * n_grp + i]

            @pl.when(sg >= 0)
            def _():
                fn(pl.ds(pl.multiple_of(sg * SUBLANES, SUBLANES), SUBLANES), i * SUBLANES)

    def gather(bj, s):
        xbuf[s] = jnp.zeros(xbuf.shape[1:], xbuf.dtype)
        for_groups(bj, lambda src, row: pltpu.make_async_copy(
            ts_in.at[src], xbuf.at[s, pl.ds(row, SUBLANES)], sem_g.at[s]).start())

    def scatter(bj, s):
        for_groups(bj, lambda dst, row: pltpu.make_async_copy(
            ybuf.at[s, pl.ds(row, SUBLANES)], ts_out.at[dst], sem_s.at[s]).start())

    def wait_rows(bj, buf, sem, s):
        rows, off = cov_ref[bj], 0
        for sz in SLAB_SIZES:
            @pl.when((rows & sz) != 0)
            def _():
                pltpu.make_async_copy(ts_in.at[pl.ds(0, sz)], buf.at[s, pl.ds(0, sz)], sem.at[s]).wait()

    @pl.when(j == 0)
    def _():
        gather(0, 0)

    @pl.when(j + 1 < nu)
    def _():
        gather(j + 1, 1 - slot)

    @pl.when(jnp.logical_and(j >= 2, j - 2 < nu))
    def _():
        wait_rows(j - 2, ybuf, sem_s, slot)

    @pl.when(j < nu)
    def _():
        wait_rows(j, xbuf, sem_g, slot)
        x = _unpack_bf16_pairs(xbuf[slot])
        g = jnp.dot(x, wg_ref[...], preferred_element_type=F32)
        u = jnp.dot(x, wu_ref[...], preferred_element_type=F32)
        a = (g * _sigmoid(g) * u).astype(BF16)
        y = jnp.dot(a, wd_ref[...], preferred_element_type=F32)
        ybuf[slot] = _pack_bf16_pairs(y.astype(BF16).astype(F32))
        scatter(j, slot)


def _experts(plan, tiles, wg, wu, wd):
    nt, rows, w = tiles.shape
    blk = EXPERT_ROWS
    d, de = wg.shape[1:]
    n_blocks = plan["block_e"].shape[0]
    tables = (plan["block_e"], plan["n_used"], plan["blk_rows"], plan["blk_groups"])
    wspec = lambda shape: pl.BlockSpec((None,) + shape, lambda i, be, *_: (be[i], 0, 0))
    grid_spec = pltpu.PrefetchScalarGridSpec(
        num_scalar_prefetch=len(tables),
        grid=(n_blocks,),
        in_specs=[pl.BlockSpec(memory_space=pl.ANY), wspec((d, de)), wspec((d, de)), wspec((de, d))],
        out_specs=pl.BlockSpec(memory_space=pl.ANY),
        scratch_shapes=[pltpu.VMEM((2, blk, w), tiles.dtype), pltpu.VMEM((2, blk, w), tiles.dtype),
                        pltpu.SemaphoreType.DMA((2,)), pltpu.SemaphoreType.DMA((2,))])
    return pl.pallas_call(
        _expert_kernel,
        grid_spec=grid_spec,
        out_shape=jax.ShapeDtypeStruct((nt * rows, w), tiles.dtype),
        input_output_aliases={len(tables): 0},
        compiler_params=_params(("arbitrary",)),
        name="experts",
    )(*tables, tiles.reshape(nt * rows, w), wg, wu, wd).reshape(tiles.shape)


def _combine_kernel(ts_ref, posl_ref, w_ref, x_ref, h_ref, mod_ref, wsg_ref, wsu_ref, wsd_ref, *rest):
    tm = x_ref.shape[0]
    hb = h_ref[...]
    g = jnp.dot(hb, wsg_ref[...], preferred_element_type=F32)
    u = jnp.dot(hb, wsu_ref[...], preferred_element_type=F32)
    acc = jnp.dot((g * _sigmoid(g) * u).astype(BF16), wsd_ref[...], preferred_element_type=F32)
    posl = posl_ref[...]
    w = w_ref[...]
    for rb in range(ts_ref.shape[0] // SORT_CHUNK):
        cols = lax.broadcasted_iota(jnp.int32, (tm, SORT_CHUNK), 1) + rb * SORT_CHUNK
        wm = jnp.zeros((tm, SORT_CHUNK), F32)
        for k in range(TOP_K):
            wm = jnp.where(cols == posl[:, k:k + 1], w[:, k:k + 1], wm)
        ys = _unpack_bf16_pairs(ts_ref[rb * SORT_CHUNK:(rb + 1) * SORT_CHUNK, :])
        acc = acc + jnp.dot(wm.astype(BF16), ys, preferred_element_type=F32)
    out = x_ref[...] + mod_ref[5:6, :] * acc
    rest[-1][...] = _rms(out, rest[0][...]) if len(rest) == 2 else out


def _combine(tiles, posl_tm, wts, x_flat, h_flat, mods, wsg, wsu, wsd, tiles_per_sample, n_ctx_tiles, g_final=None):
    n, d = x_flat.shape
    ds_ = wsg.shape[-1]
    tok = lambda i: (i, 0)
    const = lambda i: (0, 0)

    def mod_idx(i):
        return (i // tiles_per_sample, jnp.where(i % tiles_per_sample >= n_ctx_tiles, 1, 0), 0, 0)

    in_extra, args_extra, out_rows, out_idx = [], [], n, tok
    if g_final is not None:
        lat_tiles = tiles_per_sample - n_ctx_tiles
        in_extra, args_extra = [pl.BlockSpec((1, d), const)], [g_final]
        out_rows = n // TM // tiles_per_sample * lat_tiles * TM
        out_idx = lambda i: (i // tiles_per_sample * lat_tiles + jnp.maximum(i % tiles_per_sample - n_ctx_tiles, 0), 0)
    return pl.pallas_call(
        _combine_kernel,
        grid=(n // TM,),
        in_specs=[pl.BlockSpec((None,) + tiles.shape[1:], lambda i: (i, 0, 0)),
                  pl.BlockSpec((TM, TOP_K), tok),
                  pl.BlockSpec((TM, TOP_K), tok),
                  pl.BlockSpec((TM, d), tok),
                  pl.BlockSpec((TM, d), tok),
                  pl.BlockSpec((None, None, 8, d), mod_idx),
                  pl.BlockSpec((d, ds_), const), pl.BlockSpec((d, ds_), const), pl.BlockSpec((ds_, d), const)]
                 + in_extra,
        out_specs=pl.BlockSpec((TM, d), out_idx),
        out_shape=jax.ShapeDtypeStruct((out_rows, d), F32),
        compiler_params=_params(("arbitrary",)),
        name="combine",
    )(tiles, posl_tm, wts, x_flat, h_flat, mods, wsg, wsu, wsd, *args_extra)


def _moe_plan(cnt, n_assign, blk):
    nt = cnt.shape[0]
    run = (cnt + SUBLANES - 1) // SUBLANES * SUBLANES
    tile_off = jnp.cumsum(run, axis=1) - run
    tot = jnp.sum(run, axis=0)
    padded = (tot + blk - 1) // blk * blk
    pad_end = jnp.cumsum(padded)
    pad_start = pad_end - padded
    n_blocks = -(-(n_assign + nt * N_EXPERTS * (SUBLANES - 1)) // blk) + N_EXPERTS + 2
    first_row = jnp.arange(n_blocks, dtype=jnp.int32) * blk
    count = lambda m: jnp.sum(m.astype(jnp.int32), axis=1)
    block_e = jnp.minimum(count(pad_end[None, :] <= first_row[:, None]), N_EXPERTS - 1)
    i32 = lambda a: a.astype(jnp.int32)
    run_end = jnp.cumsum(run, axis=0).T[block_e][:, None, :]
    run_start = run_end - run.T[block_e][:, None, :]
    run_src = (jnp.arange(nt) * SORT_ROWS)[None, :] + tile_off.T[block_e]
    local = (first_row - pad_start[block_e])[:, None] + jnp.arange(0, blk, SUBLANES)[None, :]
    hit = jnp.logical_and(run_start <= local[:, :, None], local[:, :, None] < run_end)
    src_row = jnp.sum(jnp.where(hit, run_src[:, None, :] + local[:, :, None] - run_start, 0), axis=2)
    blk_groups = jnp.where(jnp.any(hit, axis=2), src_row // SUBLANES, -1).reshape(-1)
    return dict(
        tile_off=i32(tile_off)[:, :, None],
        blk_groups=i32(blk_groups),
        blk_rows=i32(jnp.clip((pad_start + tot)[block_e] - first_row, 0, blk)),
        block_e=i32(block_e), n_used=i32(pad_end[-1] // blk).reshape(1))


def _rope_tables(s_len, n_ctx):
    rows = s_len // GRID_W
    row = jnp.repeat(jnp.arange(rows), GRID_W).astype(F32)
    col = jnp.tile(jnp.arange(GRID_W), rows).astype(F32)
    quarter = HEAD_DIM // 4
    inv = 1.0 / (ROPE_BASE ** (jnp.arange(quarter, dtype=F32) / quarter))
    ar, ac = row[:, None] * inv, col[:, None] * inv
    cr, sr, cc, sc = jnp.cos(ar), jnp.sin(ar), jnp.cos(ac), jnp.sin(ac)
    z = jnp.zeros_like(sr)
    cos = jnp.concatenate([cr, cr, cc, cc], axis=1)
    sa = jnp.concatenate([z, sr, z, sc], axis=1)
    sb = jnp.concatenate([-sr, z, -sc, z], axis=1)
    rep = LANES // HEAD_DIM

    def full(t, fill):
        t = jnp.tile(t, (1, rep))
        return jnp.concatenate([jnp.full((n_ctx, LANES), fill, F32), t], axis=0)

    return full(cos, 1.0), full(sa, 0.0), full(sb, 0.0)


def _pair_perm():
    g = N_HEADS_A // KV_HEADS_A
    heads = [h for t in range(g) for h in (t, t + g)]
    return jnp.concatenate([jnp.arange(h * HEAD_DIM, (h + 1) * HEAD_DIM) for h in heads])


def _split_w_in(w):
    a_q, a_kv = N_HEADS_A * HEAD_DIM, KV_HEADS_A * HEAD_DIM
    b_w = N_HEADS_B * 2 * HEAD_DIM
    c_w = N_HEADS_C * HEAD_DIM_C
    sizes = (a_q, a_kv, a_kv, b_w, b_w, b_w, c_w, c_w, c_w, c_w, 4 * N_HEADS_C, w.shape[1])
    parts, start = [], 0
    for sz in sizes[:-1]:
        parts.append(w[:, start:start + sz])
        start += sz
    parts.append(w[:, start:])
    return parts


def _pack_w_in(w):
    d = w.shape[0]
    aq, ak, av, bq, bk, bv, cq, ck, cv, co, cg, gt = _split_w_in(w)
    pad = lambda n: jnp.zeros((d, n), w.dtype)
    kva = jnp.concatenate([ak, av, cg, pad(TILE_N - ak.shape[1] - av.shape[1] - cg.shape[1])], axis=1)
    big = jnp.concatenate([aq[:, _pair_perm()], bq, bk, kva, bv, co, gt, cq, cv], axis=1)
    return big.astype(BF16), ck.T.astype(BF16)


def kernel(x, c, ctx, c_ctx, w_mod, b_mod, g_mix, g_ffn, w_in, b_gate, sink, lam_q1, lam_k1, lam_q2, lam_k2,
           g_diff, g_mlstm, w_a, w_b, w_c, w_out, w_router, b_router, w_exp_gate, w_exp_up, w_exp_down,
           w_sh_gate, w_sh_up, w_sh_down, g_final):
    b, s_len, d = x.shape
    n_ctx = ctx.shape[1]
    l = n_ctx + s_len
    depth = w_mod.shape[0]
    n_ctx_tiles = n_ctx // TM
    assert n_ctx % TM == 0 and s_len % TM == 0 and d % LANES == 0 and s_len % GRID_W == 0

    xs = jnp.concatenate([ctx, x], axis=1)
    cos, sa, sb = _rope_tables(s_len, n_ctx)

    rows_c = 16
    cs = jnp.concatenate([c, c_ctx[None], jnp.zeros((rows_c - b - 1, d), F32)], axis=0)
    mod_all = _mod_vectors(cs, w_mod, b_mod).reshape(depth, rows_c, N_MOD, d)
    mod_all = jnp.pad(mod_all, ((0, 0), (0, 0), (0, 8 - N_MOD), (0, 0)))

    perm = _pair_perm()
    for layer in range(depth):
        lam_init = 0.8 - 0.6 * math.exp(-0.3 * layer)
        mods = jnp.stack([jnp.broadcast_to(mod_all[layer, b], (b, 8, d)), mod_all[layer, :b]], axis=1)
        w_big, w_kt = _pack_w_in(w_in[layer])
        p, gates, kt = _inproj(xs, mods, g_mix[layer][None], w_big, w_kt, cos, sa, sb, n_ctx_tiles)

        oa = _mixer_a(p, sink[layer], n_ctx)
        lam_params = jnp.stack([lam_q1[layer], lam_k1[layer], lam_q2[layer], lam_k2[layer]])
        ob_ctx, ob_lat = _mixer_b(p, lam_params, g_diff[layer][None], lam_init, n_ctx)

        bias = b_gate[layer].reshape(-1)
        bias_row = jnp.pad(bias, (0, LANES - bias.shape[0]))[None]
        bias_col = jnp.broadcast_to(bias[:, None], (bias.shape[0], LANES))
        gates_t = jnp.transpose(gates[:, :, :bias.shape[0]], (0, 2, 1))
        hm = _mlstm(p, kt, gates, gates_t, bias_row, bias_col, n_ctx)

        xs = _merge(xs, mods, oa, ob_ctx, ob_lat, hm, p, g_mlstm[layer][None],
                    w_a[layer][perm].astype(BF16), w_b[layer].astype(BF16), w_c[layer].astype(BF16),
                    w_out[layer].astype(BF16), n_ctx_tiles)

        h, idx_t, wt_t, cnt = _router(xs, mods, g_ffn[layer][None], w_router[layer].T, b_router[layer][:, None],
                                      n_ctx_tiles)
        plan = _moe_plan(cnt.reshape(-1, N_EXPERTS), b * l * TOP_K, EXPERT_ROWS)
        h_flat = h.reshape(b * l, d)
        posl, tiles = _sort_rows(idx_t, plan["tile_off"], h_flat)
        tiles = _experts(plan, tiles, w_exp_gate[layer].astype(BF16), w_exp_up[layer].astype(BF16),
                         w_exp_down[layer].astype(BF16))
        to_rows = lambda a: jnp.transpose(a, (0, 2, 1)).reshape(b * l, TOP_K)
        last = layer == depth - 1
        xs = _combine(tiles, to_rows(posl), to_rows(wt_t), xs.reshape(b * l, d), h_flat, mods,
                      w_sh_gate[layer].astype(BF16), w_sh_up[layer].astype(BF16), w_sh_down[layer].astype(BF16),
                      l // TM, n_ctx_tiles, g_final[None] if last else None)
        xs = xs.reshape(b, s_len if last else l, d)
    return xs
```

```python
import functools
import math

import jax
import jax.numpy as jnp
from jax import lax
from jax.experimental import pallas as pl
from jax.experimental.pallas import tpu as pltpu

F32 = jnp.float32
BF16 = jnp.bfloat16
HIGHEST = lax.Precision.HIGHEST

GRID_W = 64
N_MOD = 6
HEAD_DIM = 64
N_HEADS_A = 8
KV_HEADS_A = 2
WINDOW = 128
N_HEADS_B = 4
N_HEADS_C = 4
HEAD_DIM_C = 128
N_EXPERTS = 64
N_GROUPS = 8
TOPK_GROUPS = 4
TOP_K = 8
ROUTED_SCALE = 2.5
ROPE_BASE = 10000.0
EPS = 1e-6

LANES = 128
CHUNK = 128
TILE_N = 512
TM = 256
MIXB_KEYS = 1024
MIXB_QUERIES = 512
EXPERT_ROWS = 512
SUBLANES = 8
SORT_ROWS = TOP_K * TM + N_EXPERTS * SUBLANES
SORT_CHUNK = 512
NEG = -1e30
VMEM_LIMIT = 56 * 1024 * 1024

T_AQ, T_BQ, T_BK, T_KVA, T_BV, T_CO, T_GT, T_CQ, T_CV, N_TILES = 0, 1, 2, 3, 4, 5, 6, 12, 13, 14

NT_DIMS = (((1,), (1,)), ((), ()))


def _params(sem):
    return pltpu.CompilerParams(dimension_semantics=sem, vmem_limit_bytes=VMEM_LIMIT)


def _rms(x, g):
    return x * lax.rsqrt(jnp.mean(x * x, axis=-1, keepdims=True) + EPS) * g


def _sigmoid(x):
    return jax.nn.sigmoid(x)


def _mod_kernel(c_ref, w_ref, b_ref, o_ref):
    c = c_ref[...]
    s = c * _sigmoid(c)
    o_ref[...] = jnp.dot(s, w_ref[...], precision=HIGHEST, preferred_element_type=F32) + b_ref[...]


def _mod_vectors(cs, w_mod, b_mod):
    depth, d, n = w_mod.shape
    r = cs.shape[0]
    tn = 3 * LANES
    return pl.pallas_call(
        _mod_kernel,
        grid=(depth, n // tn),
        in_specs=[pl.BlockSpec((r, d), lambda l, j: (0, 0)),
                  pl.BlockSpec((None, d, tn), lambda l, j: (l, 0, j)),
                  pl.BlockSpec((None, 1, tn), lambda l, j: (l, 0, j))],
        out_specs=pl.BlockSpec((None, r, tn), lambda l, j: (l, 0, j)),
        out_shape=jax.ShapeDtypeStruct((depth, r, n), F32),
        compiler_params=_params(("parallel", "parallel")),
        name="mod_vectors",
    )(cs, w_mod, b_mod.reshape(depth, 1, n))


def _stream_specs(xs, n_ctx_tiles, t0=0):
    if not isinstance(xs, tuple):
        return [pl.BlockSpec((None, TM, xs.shape[-1]), lambda bi, ti: (bi, ti + t0, 0))], [xs]
    d = xs[0].shape[-1]
    return ([pl.BlockSpec((None, TM, d), lambda bi, ti: (bi, jnp.minimum(ti + t0, n_ctx_tiles - 1), 0)),
             pl.BlockSpec((None, TM, d), lambda bi, ti: (bi, jnp.maximum(ti + t0 - n_ctx_tiles, 0), 0))], list(xs))


def _read_stream(x_refs, is_ctx_tile):
    if len(x_refs) == 1:
        return x_refs[0][...]
    return jnp.where(is_ctx_tile, x_refs[0][...], x_refs[1][...])


def _inproj_kernel(*refs, n_ctx_tiles):
    x_refs, (mod_ref, g_ref, w_ref, wkt_ref, cos_ref, sa_ref, sb_ref, p_ref, gate_ref, kt_ref) = refs[:-10], refs[-10:]
    x = _read_stream(x_refs, pl.program_id(1) < n_ctx_tiles)
    h = _rms(x, g_ref[...]) * (1.0 + mod_ref[1:2, :]) + mod_ref[0:1, :]
    hb = h.astype(BF16)
    cos, sa, sb = cos_ref[...], sa_ref[...], sb_ref[...]

    def rope(t):
        return t * cos + pltpu.roll(t, 16, 1) * sa + pltpu.roll(t, LANES - 16, 1) * sb

    q_scale = HEAD_DIM ** -0.5
    for j in range(N_TILES):
        acc = jnp.dot(hb, w_ref[:, j * TILE_N:(j + 1) * TILE_N], preferred_element_type=F32)
        parts = [acc[:, s * LANES:(s + 1) * LANES] for s in range(TILE_N // LANES)]
        if j == T_AQ:
            parts = [rope(t) * q_scale for t in parts]
        elif j == T_BQ:
            parts = [rope(t) * (q_scale * math.log2(math.e)) for t in parts]
        elif j == T_BK:
            parts = [rope(t) for t in parts]
        elif j == T_KVA:
            gate_ref[...] = parts[2]
            parts[0] = rope(parts[0])
        for s, t in enumerate(parts):
            p_ref[:, j * TILE_N + s * LANES:j * TILE_N + (s + 1) * LANES] = t.astype(BF16)
    kt = lax.dot_general(wkt_ref[...], hb, NT_DIMS, preferred_element_type=F32)
    kt_ref[...] = (kt * (HEAD_DIM_C ** -0.5)).astype(BF16)


def _inproj(xs, mods, g_mix, w_big, w_kt, cos, sa, sb, n_ctx_tiles):
    x_specs, x_args = _stream_specs(xs, n_ctx_tiles)
    b, d = x_args[0].shape[0], x_args[0].shape[-1]
    l = sum(a.shape[1] for a in x_args)
    npad = w_big.shape[1]
    ck = w_kt.shape[0]
    grid = (b, l // TM)
    tok = lambda bi, ti: (bi, ti, 0)
    return pl.pallas_call(
        functools.partial(_inproj_kernel, n_ctx_tiles=n_ctx_tiles),
        grid=grid,
        in_specs=x_specs + [
                  pl.BlockSpec((None, None, 8, d), lambda bi, ti: (bi, jnp.where(ti >= n_ctx_tiles, 1, 0), 0, 0)),
                  pl.BlockSpec((1, d), lambda bi, ti: (0, 0)),
                  pl.BlockSpec((d, npad), lambda bi, ti: (0, 0), pipeline_mode=pl.Buffered(1)),
                  pl.BlockSpec((ck, d), lambda bi, ti: (0, 0), pipeline_mode=pl.Buffered(1)),
                  pl.BlockSpec((TM, LANES), lambda bi, ti: (ti, 0)),
                  pl.BlockSpec((TM, LANES), lambda bi, ti: (ti, 0)),
                  pl.BlockSpec((TM, LANES), lambda bi, ti: (ti, 0))],
        out_specs=[pl.BlockSpec((None, TM, npad), tok),
                   pl.BlockSpec((None, TM, LANES), tok),
                   pl.BlockSpec((None, ck, TM), lambda bi, ti: (bi, 0, ti))],
        out_shape=[jax.ShapeDtypeStruct((b, l, npad), BF16),
                   jax.ShapeDtypeStruct((b, l, LANES), F32),
                   jax.ShapeDtypeStruct((b, ck, l), BF16)],
        compiler_params=_params(("parallel", "parallel")),
        name="inproj",
    )(*x_args, mods, g_mix, w_big, w_kt, cos, sa, sb)


def _mixa_kernel(sink_ref, q_ref, kp_ref, kc_ref, kn_ref, kx_ref, o_ref, *, n_ctx_blocks, n_blocks):
    i = pl.program_id(1)
    lat = i >= n_ctx_blocks
    has_prev = jnp.logical_and(lat, i > n_ctx_blocks)
    has_next = jnp.logical_and(lat, i < n_blocks - 1)
    r = lax.broadcasted_iota(jnp.int32, (CHUNK, CHUNK), 0)
    c = lax.broadcasted_iota(jnp.int32, (CHUNK, CHUNK), 1)
    n_ctx = kx_ref.shape[0]
    valid = jnp.concatenate([
        jnp.logical_and(c >= r, has_prev),
        jnp.broadcast_to(lat, (CHUNK, CHUNK)),
        jnp.logical_and(c <= r, has_next),
        jnp.ones((CHUNK, n_ctx), jnp.bool_)], axis=1)
    kcat = jnp.concatenate([kp_ref[:, :LANES], kc_ref[:, :LANES], kn_ref[:, :LANES], kx_ref[:, :LANES]], axis=0)
    vcat = jnp.concatenate([kp_ref[:, LANES:], kc_ref[:, LANES:], kn_ref[:, LANES:], kx_ref[:, LANES:]], axis=0)
    lane = lax.broadcasted_iota(jnp.int32, (CHUNK, LANES), 1)
    low = lane < HEAD_DIM
    n_pairs = N_HEADS_A // KV_HEADS_A
    outs = []
    for gk in range(KV_HEADS_A):
        keep = low if gk == 0 else jnp.logical_not(low)
        zero = jnp.zeros((CHUNK, LANES), BF16)
        lhs = jnp.concatenate([jnp.where(keep, q_ref[:, t * LANES:(t + 1) * LANES], zero) for t in range(n_pairs)],
                              axis=0)
        s = lax.dot_general(lhs, kcat, NT_DIMS, preferred_element_type=F32)
        o_g = []
        for t in range(n_pairs):
            st = jnp.where(valid, s[t * CHUNK:(t + 1) * CHUNK], NEG)
            sk = sink_ref[gk * n_pairs + t]
            m = jnp.maximum(jnp.max(st, axis=-1, keepdims=True), sk)
            p = jnp.exp(st - m)
            den = jnp.sum(p, axis=-1, keepdims=True) + jnp.exp(sk - m)
            o_g.append(jnp.dot(p.astype(BF16), vcat, preferred_element_type=F32) / den)
        outs.append(o_g)
    for t in range(n_pairs):
        o_ref[:, t * LANES:(t + 1) * LANES] = jnp.where(low, outs[0][t], outs[1][t]).astype(BF16)


def _mixer_a(p, sink, n_ctx):
    b, l, _ = p.shape
    nb = l // CHUNK
    ncb = n_ctx // CHUNK
    kvw = 2 * LANES
    kv_col = T_KVA * TILE_N // kvw
    aq_w = N_HEADS_A * HEAD_DIM
    kern = functools.partial(_mixa_kernel, n_ctx_blocks=ncb, n_blocks=nb)
    return pl.pallas_call(
        kern,
        grid=(b, nb),
        in_specs=[pl.BlockSpec(memory_space=pltpu.SMEM),
                  pl.BlockSpec((None, CHUNK, aq_w), lambda bi, i: (bi, i, T_AQ)),
                  pl.BlockSpec((None, CHUNK, kvw), lambda bi, i: (bi, jnp.maximum(i - 1, 0), kv_col)),
                  pl.BlockSpec((None, CHUNK, kvw), lambda bi, i: (bi, i, kv_col)),
                  pl.BlockSpec((None, CHUNK, kvw), lambda bi, i: (bi, jnp.minimum(i + 1, nb - 1), kv_col)),
                  pl.BlockSpec((None, n_ctx, kvw), lambda bi, i: (bi, 0, kv_col))],
        out_specs=pl.BlockSpec((None, CHUNK, aq_w), lambda bi, i: (bi, i, 0)),
        out_shape=jax.ShapeDtypeStruct((b, l, aq_w), BF16),
        compiler_params=_params(("parallel", "parallel")),
        name="mixer_a",
    )(sink, p, p, p, p, p)


def _fold_lanes(op, acc, s):
    for t in range(s.shape[1] // LANES):
        acc = op(acc, s[:, t * LANES:(t + 1) * LANES])
    return acc


def _mixb_kernel(lam_ref, gd_ref, k_ref, v_ref, *rest, lam_init, chunks):
    q_refs, (o_ref, s_scr, va_scr) = rest[:-3], rest[-3:]

    @pl.when(pl.program_id(2) == 0)
    def _():
        n_keys = v_ref.shape[0]
        va_scr[:, :LANES] = v_ref[...]
        va_scr[:, LANES:] = (lax.broadcasted_iota(jnp.int32, (n_keys, LANES), 1) == 0).astype(BF16)

    lp = lam_ref[...]
    lam = (jnp.exp(jnp.sum(lp[0:1] * lp[1:2], axis=-1, keepdims=True))
           - jnp.exp(jnp.sum(lp[2:3] * lp[3:4], axis=-1, keepdims=True)) + lam_init)
    q = jnp.concatenate([qr[...] for qr in q_refs], axis=0)
    tq = q.shape[0]
    lane = lax.broadcasted_iota(jnp.int32, (tq, LANES), 1)
    zero = jnp.zeros_like(q)
    qs = (jnp.where(lane < HEAD_DIM, q, zero), jnp.where(lane >= HEAD_DIM, q, zero))
    rows = [slice(mi * tq, (mi + 1) * tq) for mi in range(2)]
    mrun = [jnp.full((tq, LANES), NEG, F32) for _ in range(2)]
    for off, sz in chunks:
        for mi in range(2):
            s_scr[rows[mi], off:off + sz] = lax.dot_general(qs[mi], k_ref[off:off + sz, :], NT_DIMS,
                                                            preferred_element_type=F32)
            mrun[mi] = _fold_lanes(jnp.maximum, mrun[mi], s_scr[rows[mi], off:off + sz])
    m = [jnp.max(mr, axis=-1, keepdims=True) for mr in mrun]
    acc = [jnp.zeros((tq, 2 * LANES), F32) for _ in range(2)]
    for off, sz in chunks:
        for mi in range(2):
            pr = jnp.exp2(s_scr[rows[mi], off:off + sz] - m[mi])
            acc[mi] = acc[mi] + jnp.dot(pr.astype(BF16), va_scr[off:off + sz, :], preferred_element_type=F32)
    outs = [a[:, :LANES] / a[:, LANES:LANES + 1] for a in acc]
    o = outs[0] - lam * outs[1]
    o_ref[...] = (_rms(o, gd_ref[...]) * (1.0 - lam_init)).astype(BF16)


def _mixer_b(p, lam_params, g_diff, lam_init, n_ctx):
    b, l, _ = p.shape
    kl = min(MIXB_KEYS, l - n_ctx)
    tq_lat = min(MIXB_QUERIES, l - n_ctx)
    assert (l - n_ctx) % kl == 0 and (l - n_ctx) % tq_lat == 0 and tq_lat % TM == 0 and n_ctx % TM == 0
    q0 = T_BQ * TILE_N // LANES
    k0 = T_BK * TILE_N // LANES
    v0 = T_BV * TILE_N // LANES
    ctx_chunks = ((0, n_ctx),)
    all_chunks = ctx_chunks + tuple((n_ctx + c * kl, kl) for c in range((l - n_ctx) // kl))

    def call(chunks, n_keys, tq, q_tiles, first_row):
        nq = tq // TM
        q_spec = lambda part: pl.BlockSpec((None, TM, LANES),
                                           lambda bi, h, qi: (bi, first_row // TM + qi * nq + part, q0 + h))
        kern = functools.partial(_mixb_kernel, lam_init=lam_init, chunks=chunks)
        return pl.pallas_call(
            kern,
            grid=(b, N_HEADS_B, q_tiles),
            in_specs=[pl.BlockSpec((4, HEAD_DIM), lambda bi, h, qi: (0, 0)),
                      pl.BlockSpec((1, LANES), lambda bi, h, qi: (0, 0)),
                      pl.BlockSpec((None, n_keys, LANES), lambda bi, h, qi: (bi, 0, k0 + h)),
                      pl.BlockSpec((None, n_keys, LANES), lambda bi, h, qi: (bi, 0, v0 + h))]
                     + [q_spec(part) for part in range(nq)],
            out_specs=pl.BlockSpec((None, tq, LANES), lambda bi, h, qi: (bi, qi, h)),
            out_shape=jax.ShapeDtypeStruct((b, q_tiles * tq, N_HEADS_B * LANES), BF16),
            scratch_shapes=[pltpu.VMEM((2 * tq, n_keys), F32), pltpu.VMEM((n_keys, 2 * LANES), BF16)],
            compiler_params=_params(("parallel", "parallel", "arbitrary")),
            name="mixer_b",
        )(lam_params, g_diff, p, p, *([p] * nq))

    tq_ctx = min(tq_lat, n_ctx)
    return (call(ctx_chunks, n_ctx, tq_ctx, n_ctx // tq_ctx, 0),
            call(all_chunks, l, tq_lat, (l - n_ctx) // tq_lat, n_ctx))


def _log_sigmoid(x):
    return jnp.minimum(x, 0.0) - jnp.log1p(jnp.exp(-jnp.abs(x)))


def _mlstm_kernel(q_ref, kt_ref, v_ref, gc_ref, gr_ref, bc_ref, br_ref, o_ref, s_scr, m_scr):
    d = pl.program_id(1)
    c = pl.program_id(2)

    @pl.when(c == 0)
    def _():
        s_scr[...] = jnp.zeros_like(s_scr)
        m_scr[...] = jnp.zeros_like(m_scr)

    fwd = d == 0
    r = lax.broadcasted_iota(jnp.int32, (CHUNK, CHUNK), 0)
    cc = lax.broadcasted_iota(jnp.int32, (CHUNK, CHUNK), 1)
    tri = jnp.where(fwd, r - cc, cc - r) >= 0
    trif = tri.astype(F32)
    gcol = gc_ref[...] + bc_ref[...]
    grow = gr_ref[...] + br_ref[...]
    lf_col = _log_sigmoid(gcol)
    lf_row = _log_sigmoid(grow)
    bcum_col = jnp.dot(trif, lf_col, precision=HIGHEST, preferred_element_type=F32)
    bcum_row = lax.dot_general(lf_row, trif, NT_DIMS, precision=HIGHEST, preferred_element_type=F32)
    tot_row = jnp.sum(lf_row, axis=-1, keepdims=True)
    lane = lax.broadcasted_iota(jnp.int32, (CHUNK, LANES), 1)
    ones_col = (lane == 0).astype(BF16)
    nh = N_HEADS_C

    for h in range(nh):
        def pick_col(a, kind):
            return jnp.where(fwd, a[:, kind * nh + h:kind * nh + h + 1],
                             a[:, (kind + 2) * nh + h:(kind + 2) * nh + h + 1])

        def pick_row(a, kind):
            return jnp.where(fwd, a[kind * nh + h:kind * nh + h + 1, :],
                             a[(kind + 2) * nh + h:(kind + 2) * nh + h + 1, :])

        ic_row = pick_row(grow, 0)
        b_col = pick_col(bcum_col, 1)
        b_row = pick_row(bcum_row, 1)
        total = pick_row(tot_row, 1)
        m_st = m_scr[h, 0:1, 0:1]
        dm = jnp.where(tri, b_col - b_row + ic_row, NEG)
        inter = b_col + m_st
        m_t = jnp.maximum(inter, jnp.max(dm, axis=-1, keepdims=True))
        e = jnp.exp(dm - m_t)
        qh = q_ref[:, h * LANES:(h + 1) * LANES]
        kth = kt_ref[h * LANES:(h + 1) * LANES, :]
        vaug = jnp.concatenate([v_ref[:, h * LANES:(h + 1) * LANES], ones_col], axis=1)
        s = jnp.dot(qh, kth, preferred_element_type=F32) * e
        st = s_scr[h]
        intra = jnp.dot(s.astype(BF16), vaug, preferred_element_type=F32)
        cross = jnp.dot(qh, st.astype(BF16), preferred_element_type=F32)
        nd = intra + jnp.exp(inter - m_t) * cross
        den = nd[:, LANES:LANES + 1]
        o_ref[:, h * LANES:(h + 1) * LANES] = nd[:, :LANES] / jnp.maximum(jnp.abs(den), jnp.exp(-m_t))
        gs_row = total - b_row + ic_row
        m_new = jnp.maximum(total + m_st, jnp.max(gs_row, axis=-1, keepdims=True))
        decay = jnp.exp(total + m_st - m_new)
        wkt = (kth.astype(F32) * jnp.exp(gs_row - m_new)).astype(BF16)
        s_scr[h] = decay * st + jnp.dot(wkt, vaug, preferred_element_type=F32)
        m_scr[h] = jnp.broadcast_to(m_new, m_scr.shape[1:])


def _mlstm(p, kt, gates, gates_t, bias_row, bias_col, n_ctx):
    b, l, _ = p.shape
    nc = l // CHUNK
    ncc = n_ctx // CHUNK
    cw = N_HEADS_C * HEAD_DIM_C

    def chunk(d, c):
        rev = jnp.where(c < ncc, ncc - 1 - c, nc + ncc - 1 - c)
        return jnp.where(d == 0, c, rev)

    return pl.pallas_call(
        _mlstm_kernel,
        grid=(b, 2, nc),
        in_specs=[pl.BlockSpec((None, CHUNK, cw), lambda bi, d, c: (bi, chunk(d, c), T_CQ)),
                  pl.BlockSpec((None, cw, CHUNK), lambda bi, d, c: (bi, 0, chunk(d, c))),
                  pl.BlockSpec((None, CHUNK, cw), lambda bi, d, c: (bi, chunk(d, c), T_CV)),
                  pl.BlockSpec((None, CHUNK, LANES), lambda bi, d, c: (bi, chunk(d, c), 0)),
                  pl.BlockSpec((None, 16, CHUNK), lambda bi, d, c: (bi, 0, chunk(d, c))),
                  pl.BlockSpec((1, LANES), lambda bi, d, c: (0, 0)),
                  pl.BlockSpec((16, LANES), lambda bi, d, c: (0, 0))],
        out_specs=pl.BlockSpec((None, None, CHUNK, cw), lambda bi, d, c: (d, bi, chunk(d, c), 0)),
        out_shape=jax.ShapeDtypeStruct((2, b, l, cw), F32),
        scratch_shapes=[pltpu.VMEM((N_HEADS_C, HEAD_DIM_C, 2 * LANES), F32),
                        pltpu.VMEM((N_HEADS_C, 8, LANES), F32)],
        compiler_params=_params(("parallel", "parallel", "arbitrary")),
        name="mlstm",
    )(p, kt, p, gates, gates_t, bias_row, bias_col)


def _merge_kernel(*refs, n_ctx_tiles, t0):
    x_refs, (mod_ref, oa_ref, obc_ref, obl_ref, hf_ref, hb_ref, co_ref, gt_ref, gm_ref,
             wa_ref, wb_ref, wc_ref, wo_ref, xo_ref) = refs[:-14], refs[-14:]
    d = xo_ref.shape[-1]
    is_ctx = pl.program_id(1) + t0 < n_ctx_tiles
    ob = jnp.where(is_ctx, obc_ref[...], obl_ref[...])
    hs = hf_ref[...] + hb_ref[...]
    co = co_ref[...].astype(F32)
    gm = gm_ref[...]
    oc = []
    for h in range(N_HEADS_C):
        sl = slice(h * LANES, (h + 1) * LANES)
        oc.append((_rms(hs[:, sl], gm[:, sl]) * _sigmoid(co[:, sl])).astype(BF16))
    oc = jnp.concatenate(oc, axis=1)
    y = (_sigmoid(gt_ref[:, 0:d].astype(F32)) * jnp.dot(oa_ref[...], wa_ref[...], preferred_element_type=F32)
         + _sigmoid(gt_ref[:, d:2 * d].astype(F32)) * jnp.dot(ob, wb_ref[...], preferred_element_type=F32)
         + _sigmoid(gt_ref[:, 2 * d:3 * d].astype(F32)) * jnp.dot(oc, wc_ref[...], preferred_element_type=F32))
    out = jnp.dot(y.astype(BF16), wo_ref[...], preferred_element_type=F32)
    xo_ref[...] = _read_stream(x_refs, is_ctx) + mod_ref[2:3, :] * out


def _merge(xs, mods, oa, ob_ctx, ob_lat, hm, p, g_mlstm, wa, wb, wc, wo, n_ctx_tiles, latent_only):
    t0 = n_ctx_tiles if latent_only else 0
    x_specs, x_args = _stream_specs(xs, n_ctx_tiles, t0)
    b, d = x_args[0].shape[0], x_args[0].shape[-1]
    n_tiles = oa.shape[1] // TM - t0
    tok = lambda bi, ti: (bi, ti + t0, 0)
    cw = N_HEADS_C * HEAD_DIM_C
    const = lambda bi, ti: (0, 0)
    return pl.pallas_call(
        functools.partial(_merge_kernel, n_ctx_tiles=n_ctx_tiles, t0=t0),
        grid=(b, n_tiles),
        in_specs=x_specs + [
                  pl.BlockSpec((None, None, 8, d), lambda bi, ti: (bi, jnp.where(ti + t0 >= n_ctx_tiles, 1, 0), 0, 0)),
                  pl.BlockSpec((None, TM, oa.shape[-1]), tok),
                  pl.BlockSpec((None, TM, ob_ctx.shape[-1]),
                               lambda bi, ti: (bi, jnp.minimum(ti + t0, n_ctx_tiles - 1), 0)),
                  pl.BlockSpec((None, TM, ob_lat.shape[-1]),
                               lambda bi, ti: (bi, jnp.maximum(ti + t0 - n_ctx_tiles, 0), 0)),
                  pl.BlockSpec((None, None, TM, cw), lambda bi, ti: (0, bi, ti + t0, 0)),
                  pl.BlockSpec((None, None, TM, cw), lambda bi, ti: (1, bi, ti + t0, 0)),
                  pl.BlockSpec((None, TM, cw), lambda bi, ti: (bi, ti + t0, T_CO)),
                  pl.BlockSpec((None, TM, 3 * d), lambda bi, ti: (bi, ti + t0, T_GT * TILE_N // (3 * d))),
                  pl.BlockSpec((1, cw), const),
                  pl.BlockSpec(wa.shape, const), pl.BlockSpec(wb.shape, const),
                  pl.BlockSpec(wc.shape, const), pl.BlockSpec(wo.shape, const)],
        out_specs=pl.BlockSpec((None, TM, d), lambda bi, ti: (bi, ti, 0)),
        out_shape=jax.ShapeDtypeStruct((b, n_tiles * TM, d), F32),
        compiler_params=_params(("parallel", "parallel")),
        name="merge",
    )(*x_args, mods, oa, ob_ctx, ob_lat, hm, hm, p, p, g_mlstm, wa, wb, wc, wo)


def _router_kernel(x_ref, mod_ref, g_ref, wrt_ref, br_ref, h_ref, idx_ref, wt_ref, cnt_ref):
    h = _rms(x_ref[...], g_ref[...]) * (1.0 + mod_ref[4:5, :]) + mod_ref[3:4, :]
    h_ref[...] = h.astype(BF16)
    tm = h.shape[0]
    per = N_EXPERTS // N_GROUPS
    lt = lax.dot_general(wrt_ref[...], h, NT_DIMS, precision=HIGHEST, preferred_element_type=F32)
    s = _sigmoid(lt)
    sel = s + br_ref[...]
    ninf = -jnp.inf
    sel3 = sel.reshape(N_GROUPS, per, tm)
    eidx = lax.broadcasted_iota(jnp.int32, (N_GROUPS, per, tm), 1)
    m1 = jnp.max(sel3, axis=1, keepdims=True)
    first = jnp.min(jnp.where(sel3 == m1, eidx, per), axis=1, keepdims=True)
    m2 = jnp.max(jnp.where(eidx == first, ninf, sel3), axis=1, keepdims=True)
    gscore = (m1 + m2).reshape(N_GROUPS, tm)
    gidx = lax.broadcasted_iota(jnp.int32, (N_GROUPS, tm), 0)
    gmask = jnp.zeros((N_GROUPS, tm), jnp.bool_)
    cur = gscore
    for _ in range(TOPK_GROUPS):
        mx = jnp.max(cur, axis=0, keepdims=True)
        hit = gidx == jnp.min(jnp.where(cur == mx, gidx, N_GROUPS), axis=0, keepdims=True)
        gmask = jnp.logical_or(gmask, hit)
        cur = jnp.where(hit, ninf, cur)
    cur = jnp.where(gmask.reshape(N_GROUPS, 1, tm), sel3, ninf).reshape(N_EXPERTS, tm)
    eid = lax.broadcasted_iota(jnp.int32, (N_EXPERTS, tm), 0)
    ids, ws = [], []
    chosen = jnp.zeros((N_EXPERTS, tm), F32)
    for _ in range(TOP_K):
        mx = jnp.max(cur, axis=0, keepdims=True)
        pick = jnp.min(jnp.where(cur == mx, eid, N_EXPERTS), axis=0, keepdims=True)
        hit = eid == pick
        ids.append(pick)
        ws.append(jnp.sum(jnp.where(hit, s, 0.0), axis=0, keepdims=True))
        cur = jnp.where(hit, ninf, cur)
        chosen = chosen + hit.astype(F32)
    wsum = ws[0]
    for w in ws[1:]:
        wsum = wsum + w
    idx_ref[...] = jnp.concatenate(ids, axis=0)
    wt_ref[...] = jnp.concatenate([w / wsum * ROUTED_SCALE for w in ws], axis=0)
    cnt_ref[...] = jnp.sum(chosen, axis=1, keepdims=True).astype(jnp.int32)


def _router(xs, mods, g_ffn, w_router_t, b_router, n_ctx_tiles):
    b, l, d = xs.shape
    tok = lambda bi, ti: (bi, ti, 0)
    const = lambda bi, ti: (0, 0)
    return pl.pallas_call(
        _router_kernel,
        grid=(b, l // TM),
        in_specs=[pl.BlockSpec((None, TM, d), tok),
                  pl.BlockSpec((None, None, 8, d), lambda bi, ti: (bi, jnp.where(ti >= n_ctx_tiles, 1, 0), 0, 0)),
                  pl.BlockSpec((1, d), const),
                  pl.BlockSpec((N_EXPERTS, d), const),
                  pl.BlockSpec((N_EXPERTS, 1), const)],
        out_specs=[pl.BlockSpec((None, TM, d), tok),
                   pl.BlockSpec((None, TOP_K, TM), lambda bi, ti: (bi, 0, ti)),
                   pl.BlockSpec((None, TOP_K, TM), lambda bi, ti: (bi, 0, ti)),
                   pl.BlockSpec((None, None, N_EXPERTS, 1), lambda bi, ti: (bi, ti, 0, 0))],
        out_shape=[jax.ShapeDtypeStruct((b, l, d), BF16),
                   jax.ShapeDtypeStruct((b, TOP_K, l), jnp.int32),
                   jax.ShapeDtypeStruct((b, TOP_K, l), F32),
                   jax.ShapeDtypeStruct((b, l // TM, N_EXPERTS, 1), jnp.int32)],
        compiler_params=_params(("parallel", "parallel")),
        name="router",
    )(xs, mods, g_ffn, w_router_t, b_router)


def _pack_bf16_pairs(x):
    half = x.shape[1] // 2
    lo = lax.bitcast_convert_type(x[:, :half], jnp.uint32) >> 16
    hi = lax.bitcast_convert_type(x[:, half:], jnp.uint32) & jnp.uint32(0xFFFF0000)
    return lo | hi


def _unpack_bf16_pairs(w):
    lo = lax.bitcast_convert_type(w << 16, F32)
    hi = lax.bitcast_convert_type(w & jnp.uint32(0xFFFF0000), F32)
    return jnp.concatenate([lo, hi], axis=1).astype(BF16)


def _sort_kernel(idx_ref, off_ref, h_ref, posl_ref, ts_ref):
    tm = h_ref.shape[0]
    idx = idx_ref[...]
    eid = lax.broadcasted_iota(jnp.int32, (N_EXPERTS, tm), 0)
    hits = [eid == idx[k:k + 1, :] for k in range(TOP_K)]
    chosen = hits[0].astype(BF16)
    for hk in hits[1:]:
        chosen = chosen + hk.astype(BF16)
    r = lax.broadcasted_iota(jnp.int32, (tm, tm), 0)
    c = lax.broadcasted_iota(jnp.int32, (tm, tm), 1)
    before = (r < c).astype(BF16)
    rank = jnp.dot(chosen, before, preferred_element_type=F32)
    slot = rank.astype(jnp.int32) + off_ref[...]
    posl = jnp.concatenate([jnp.sum(jnp.where(hk, slot, 0), axis=0, keepdims=True) for hk in hits], axis=0)
    posl_ref[...] = posl
    hb = h_ref[...]
    for rb in range(ts_ref.shape[0] // SORT_CHUNK):
        rows = lax.broadcasted_iota(jnp.int32, (SORT_CHUNK, tm), 0) + rb * SORT_CHUNK
        sel = rows == posl[0:1, :]
        for k in range(1, TOP_K):
            sel = jnp.logical_or(sel, rows == posl[k:k + 1, :])
        onehot = jnp.where(sel, 1.0, 0.0).astype(BF16)
        ts = jnp.dot(onehot, hb, preferred_element_type=F32)
        ts_ref[rb * SORT_CHUNK:(rb + 1) * SORT_CHUNK, :] = _pack_bf16_pairs(ts)


def _sort_rows(idx_t, off, h_flat):
    b, k, l = idx_t.shape
    n, d = h_flat.shape
    nt = l // TM
    return pl.pallas_call(
        _sort_kernel,
        grid=(b * nt,),
        in_specs=[pl.BlockSpec((None, k, TM), lambda i: (i // nt, 0, i % nt)),
                  pl.BlockSpec((None, N_EXPERTS, 1), lambda i: (i, 0, 0)),
                  pl.BlockSpec((TM, d), lambda i: (i, 0))],
        out_specs=[pl.BlockSpec((None, k, TM), lambda i: (i // nt, 0, i % nt)),
                   pl.BlockSpec((None, SORT_ROWS, d // 2), lambda i: (i, 0, 0))],
        out_shape=[jax.ShapeDtypeStruct((b, k, l), jnp.int32),
                   jax.ShapeDtypeStruct((b * nt, SORT_ROWS, d // 2), jnp.uint32)],
        compiler_params=_params(("parallel",)),
        name="sort_rows",
    )(idx_t, off, h_flat)


SLAB_SIZES = tuple(SUBLANES << s for s in range((EXPERT_ROWS // SUBLANES).bit_length()))


def _expert_kernel(be_ref, nu_ref, cov_ref, grp_ref, ts_in, wg_ref, wu_ref, wd_ref, ts_out, xbuf, ybuf, sem_g, sem_s):
    j = pl.program_id(0)
    nu = nu_ref[0]
    blk = xbuf.shape[1]
    n_grp = blk // SUBLANES
    slot = j % 2

    def for_groups(bj, fn):
        for i in range(n_grp):
            sg = grp_ref[bj * n_grp + i]

            @pl.when(sg >= 0)
            def _():
                fn(pl.ds(pl.multiple_of(sg * SUBLANES, SUBLANES), SUBLANES), i * SUBLANES)

    def gather(bj, s):
        xbuf[s] = jnp.zeros(xbuf.shape[1:], xbuf.dtype)
        for_groups(bj, lambda src, row: pltpu.make_async_copy(
            ts_in.at[src], xbuf.at[s, pl.ds(row, SUBLANES)], sem_g.at[s]).start())

    def scatter(bj, s):
        for_groups(bj, lambda dst, row: pltpu.make_async_copy(
            ybuf.at[s, pl.ds(row, SUBLANES)], ts_out.at[dst], sem_s.at[s]).start())

    def wait_rows(bj, buf, sem, s):
        rows, off = cov_ref[bj], 0
        for sz in SLAB_SIZES:
            @pl.when((rows & sz) != 0)
            def _():
                pltpu.make_async_copy(ts_in.at[pl.ds(0, sz)], buf.at[s, pl.ds(0, sz)], sem.at[s]).wait()

    @pl.when(j == 0)
    def _():
        gather(0, 0)

    @pl.when(j + 1 < nu)
    def _():
        gather(j + 1, 1 - slot)

    @pl.when(jnp.logical_and(j >= 2, j - 2 < nu))
    def _():
        wait_rows(j - 2, ybuf, sem_s, slot)

    @pl.when(j < nu)
    def _():
        wait_rows(j, xbuf, sem_g, slot)
        x = _unpack_bf16_pairs(xbuf[slot])
        g = jnp.dot(x, wg_ref[...], preferred_element_type=F32)
        u = jnp.dot(x, wu_ref[...], preferred_element_type=F32)
        a = (g * _sigmoid(g) * u).astype(BF16)
        y = jnp.dot(a, wd_ref[...], preferred_element_type=F32)
        ybuf[slot] = _pack_bf16_pairs(y.astype(BF16).astype(F32))
        scatter(j, slot)


def _experts(plan, tiles, wg, wu, wd):
    nt, rows, w = tiles.shape
    blk = EXPERT_ROWS
    d, de = wg.shape[1:]
    n_blocks = plan["block_e"].shape[0]
    tables = (plan["block_e"], plan["n_used"], plan["blk_rows"], plan["blk_groups"])
    wspec = lambda shape: pl.BlockSpec((None,) + shape, lambda i, be, *_: (be[i], 0, 0))
    grid_spec = pltpu.PrefetchScalarGridSpec(
        num_scalar_prefetch=len(tables),
        grid=(n_blocks,),
        in_specs=[pl.BlockSpec(memory_space=pl.ANY), wspec((d, de)), wspec((d, de)), wspec((de, d))],
        out_specs=pl.BlockSpec(memory_space=pl.ANY),
        scratch_shapes=[pltpu.VMEM((2, blk, w), tiles.dtype), pltpu.VMEM((2, blk, w), tiles.dtype),
                        pltpu.SemaphoreType.DMA((2,)), pltpu.SemaphoreType.DMA((2,))])
    return pl.pallas_call(
        _expert_kernel,
        grid_spec=grid_spec,
        out_shape=jax.ShapeDtypeStruct((nt * rows, w), tiles.dtype),
        input_output_aliases={len(tables): 0},
        compiler_params=_params(("arbitrary",)),
        name="experts",
    )(*tables, tiles.reshape(nt * rows, w), wg, wu, wd).reshape(tiles.shape)


def _combine_kernel(ts_ref, posl_ref, w_ref, x_ref, h_ref, mod_ref, wsg_ref, wsu_ref, wsd_ref, *rest):
    tm = x_ref.shape[0]
    hb = h_ref[...]
    g = jnp.dot(hb, wsg_ref[...], preferred_element_type=F32)
    u = jnp.dot(hb, wsu_ref[...], preferred_element_type=F32)
    acc = jnp.dot((g * _sigmoid(g) * u).astype(BF16), wsd_ref[...], preferred_element_type=F32)
    posl = posl_ref[...]
    w = w_ref[...]
    for rb in range(ts_ref.shape[0] // SORT_CHUNK):
        cols = lax.broadcasted_iota(jnp.int32, (tm, SORT_CHUNK), 1) + rb * SORT_CHUNK
        wm = jnp.zeros((tm, SORT_CHUNK), F32)
        for k in range(TOP_K):
            wm = jnp.where(cols == posl[:, k:k + 1], w[:, k:k + 1], wm)
        ys = _unpack_bf16_pairs(ts_ref[rb * SORT_CHUNK:(rb + 1) * SORT_CHUNK, :])
        acc = acc + jnp.dot(wm.astype(BF16), ys, preferred_element_type=F32)
    out = x_ref[...] + mod_ref[5:6, :] * acc
    rest[-1][...] = _rms(out, rest[0][...]) if len(rest) == 2 else out


def _combine(tiles, posl_tm, wts, x_flat, h_flat, mods, wsg, wsu, wsd, tiles_per_sample, n_ctx_tiles, g_final=None):
    n, d = x_flat.shape
    ds_ = wsg.shape[-1]
    tok = lambda i: (i, 0)
    const = lambda i: (0, 0)

    def mod_idx(i):
        return (i // tiles_per_sample, jnp.where(i % tiles_per_sample >= n_ctx_tiles, 1, 0), 0, 0)

    in_extra, args_extra, out_rows, out_idx = [], [], n, tok
    if g_final is not None:
        lat_tiles = tiles_per_sample - n_ctx_tiles
        in_extra, args_extra = [pl.BlockSpec((1, d), const)], [g_final]
        out_rows = n // TM // tiles_per_sample * lat_tiles * TM
        out_idx = lambda i: (i // tiles_per_sample * lat_tiles + jnp.maximum(i % tiles_per_sample - n_ctx_tiles, 0), 0)
    return pl.pallas_call(
        _combine_kernel,
        grid=(n // TM,),
        in_specs=[pl.BlockSpec((None,) + tiles.shape[1:], lambda i: (i, 0, 0)),
                  pl.BlockSpec((TM, TOP_K), tok),
                  pl.BlockSpec((TM, TOP_K), tok),
                  pl.BlockSpec((TM, d), tok),
                  pl.BlockSpec((TM, d), tok),
                  pl.BlockSpec((None, None, 8, d), mod_idx),
                  pl.BlockSpec((d, ds_), const), pl.BlockSpec((d, ds_), const), pl.BlockSpec((ds_, d), const)]
                 + in_extra,
        out_specs=pl.BlockSpec((TM, d), out_idx),
        out_shape=jax.ShapeDtypeStruct((out_rows, d), F32),
        compiler_params=_params(("arbitrary",)),
        name="combine",
    )(tiles, posl_tm, wts, x_flat, h_flat, mods, wsg, wsu, wsd, *args_extra)


def _moe_plan(cnt, n_assign, blk):
    nt = cnt.shape[0]
    run = (cnt + SUBLANES - 1) // SUBLANES * SUBLANES
    tile_off = jnp.cumsum(run, axis=1) - run
    tot = jnp.sum(run, axis=0)
    padded = (tot + blk - 1) // blk * blk
    pad_end = jnp.cumsum(padded)
    pad_start = pad_end - padded
    n_blocks = -(-(n_assign + nt * N_EXPERTS * (SUBLANES - 1)) // blk) + N_EXPERTS + 2
    first_row = jnp.arange(n_blocks, dtype=jnp.int32) * blk
    count = lambda m: jnp.sum(m.astype(jnp.int32), axis=1)
    block_e = jnp.minimum(count(pad_end[None, :] <= first_row[:, None]), N_EXPERTS - 1)
    i32 = lambda a: a.astype(jnp.int32)
    run_end = jnp.cumsum(run, axis=0).T[block_e][:, None, :]
    run_start = run_end - run.T[block_e][:, None, :]
    run_src = (jnp.arange(nt) * SORT_ROWS)[None, :] + tile_off.T[block_e]
    local = (first_row - pad_start[block_e])[:, None] + jnp.arange(0, blk, SUBLANES)[None, :]
    hit = jnp.logical_and(run_start <= local[:, :, None], local[:, :, None] < run_end)
    src_row = jnp.sum(jnp.where(hit, run_src[:, None, :] + local[:, :, None] - run_start, 0), axis=2)
    blk_groups = jnp.where(jnp.any(hit, axis=2), src_row // SUBLANES, -1).reshape(-1)
    return dict(
        tile_off=i32(tile_off)[:, :, None],
        blk_groups=i32(blk_groups),
        blk_rows=i32(jnp.clip((pad_start + tot)[block_e] - first_row, 0, blk)),
        block_e=i32(block_e), n_used=i32(pad_end[-1] // blk).reshape(1))


def _rope_tables(s_len, n_ctx):
    rows = s_len // GRID_W
    row = jnp.repeat(jnp.arange(rows), GRID_W).astype(F32)
    col = jnp.tile(jnp.arange(GRID_W), rows).astype(F32)
    quarter = HEAD_DIM // 4
    inv = 1.0 / (ROPE_BASE ** (jnp.arange(quarter, dtype=F32) / quarter))
    ar, ac = row[:, None] * inv, col[:, None] * inv
    cr, sr, cc, sc = jnp.cos(ar), jnp.sin(ar), jnp.cos(ac), jnp.sin(ac)
    z = jnp.zeros_like(sr)
    cos = jnp.concatenate([cr, cr, cc, cc], axis=1)
    sa = jnp.concatenate([z, sr, z, sc], axis=1)
    sb = jnp.concatenate([-sr, z, -sc, z], axis=1)
    rep = LANES // HEAD_DIM

    def full(t, fill):
        t = jnp.tile(t, (1, rep))
        return jnp.concatenate([jnp.full((n_ctx, LANES), fill, F32), t], axis=0)

    return full(cos, 1.0), full(sa, 0.0), full(sb, 0.0)


def _pair_perm():
    g = N_HEADS_A // KV_HEADS_A
    heads = [h for t in range(g) for h in (t, t + g)]
    return jnp.concatenate([jnp.arange(h * HEAD_DIM, (h + 1) * HEAD_DIM) for h in heads])


def _split_w_in(w):
    a_q, a_kv = N_HEADS_A * HEAD_DIM, KV_HEADS_A * HEAD_DIM
    b_w = N_HEADS_B * 2 * HEAD_DIM
    c_w = N_HEADS_C * HEAD_DIM_C
    sizes = (a_q, a_kv, a_kv, b_w, b_w, b_w, c_w, c_w, c_w, c_w, 4 * N_HEADS_C, w.shape[1])
    parts, start = [], 0
    for sz in sizes[:-1]:
        parts.append(w[:, start:start + sz])
        start += sz
    parts.append(w[:, start:])
    return parts


def _pack_w_in(w):
    d = w.shape[0]
    aq, ak, av, bq, bk, bv, cq, ck, cv, co, cg, gt = _split_w_in(w)
    pad = lambda n: jnp.zeros((d, n), w.dtype)
    kva = jnp.concatenate([ak, av, cg, pad(TILE_N - ak.shape[1] - av.shape[1] - cg.shape[1])], axis=1)
    big = jnp.concatenate([aq[:, _pair_perm()], bq, bk, kva, bv, co, gt, cq, cv], axis=1)
    return big.astype(BF16), ck.T.astype(BF16)


def kernel(x, c, ctx, c_ctx, w_mod, b_mod, g_mix, g_ffn, w_in, b_gate, sink, lam_q1, lam_k1, lam_q2, lam_k2,
           g_diff, g_mlstm, w_a, w_b, w_c, w_out, w_router, b_router, w_exp_gate, w_exp_up, w_exp_down,
           w_sh_gate, w_sh_up, w_sh_down, g_final):
    b, s_len, d = x.shape
    n_ctx = ctx.shape[1]
    l = n_ctx + s_len
    depth = w_mod.shape[0]
    n_ctx_tiles = n_ctx // TM
    assert n_ctx % TM == 0 and s_len % TM == 0 and d % LANES == 0 and s_len % GRID_W == 0

    xs = (ctx, x)
    cos, sa, sb = _rope_tables(s_len, n_ctx)

    rows_c = 16
    cs = jnp.concatenate([c, c_ctx[None], jnp.zeros((rows_c - b - 1, d), F32)], axis=0)
    mod_all = _mod_vectors(cs, w_mod, b_mod).reshape(depth, rows_c, N_MOD, d)
    mod_all = jnp.pad(mod_all, ((0, 0), (0, 0), (0, 8 - N_MOD), (0, 0)))

    perm = _pair_perm()
    for layer in range(depth):
        lam_init = 0.8 - 0.6 * math.exp(-0.3 * layer)
        mods = jnp.stack([jnp.broadcast_to(mod_all[layer, b], (b, 8, d)), mod_all[layer, :b]], axis=1)
        w_big, w_kt = _pack_w_in(w_in[layer])
        p, gates, kt = _inproj(xs, mods, g_mix[layer][None], w_big, w_kt, cos, sa, sb, n_ctx_tiles)

        oa = _mixer_a(p, sink[layer], n_ctx)
        lam_params = jnp.stack([lam_q1[layer], lam_k1[layer], lam_q2[layer], lam_k2[layer]])
        ob_ctx, ob_lat = _mixer_b(p, lam_params, g_diff[layer][None], lam_init, n_ctx)

        bias = b_gate[layer].reshape(-1)
        bias_row = jnp.pad(bias, (0, LANES - bias.shape[0]))[None]
        bias_col = jnp.broadcast_to(bias[:, None], (bias.shape[0], LANES))
        gates_t = jnp.transpose(gates[:, :, :bias.shape[0]], (0, 2, 1))
        hm = _mlstm(p, kt, gates, gates_t, bias_row, bias_col, n_ctx)

        last = layer == depth - 1
        xs = _merge(xs, mods, oa, ob_ctx, ob_lat, hm, p, g_mlstm[layer][None],
                    w_a[layer][perm].astype(BF16), w_b[layer].astype(BF16), w_c[layer].astype(BF16),
                    w_out[layer].astype(BF16), n_ctx_tiles, last)

        lf, ctx_tiles = (s_len, 0) if last else (l, n_ctx_tiles)
        h, idx_t, wt_t, cnt = _router(xs, mods, g_ffn[layer][None], w_router[layer].T, b_router[layer][:, None],
                                      ctx_tiles)
        plan = _moe_plan(cnt.reshape(-1, N_EXPERTS), b * lf * TOP_K, EXPERT_ROWS)
        h_flat = h.reshape(b * lf, d)
        posl, tiles = _sort_rows(idx_t, plan["tile_off"], h_flat)
        tiles = _experts(plan, tiles, w_exp_gate[layer].astype(BF16), w_exp_up[layer].astype(BF16),
                         w_exp_down[layer].astype(BF16))
        to_rows = lambda a: jnp.transpose(a, (0, 2, 1)).reshape(b * lf, TOP_K)
        xs = _combine(tiles, to_rows(posl), to_rows(wt_t), xs.reshape(b * lf, d), h_flat, mods,
                      w_sh_gate[layer].astype(BF16), w_sh_up[layer].astype(BF16), w_sh_down[layer].astype(BF16),
                      lf // TM, ctx_tiles, g_final[None] if last else None).reshape(b, lf, d)
    return xs
```

```python
import functools
import math

import jax
import jax.numpy as jnp
from jax import lax
from jax.experimental import pallas as pl
from jax.experimental.pallas import tpu as pltpu

F32 = jnp.float32
BF16 = jnp.bfloat16
HIGHEST = lax.Precision.HIGHEST

GRID_W = 64
N_MOD = 6
HEAD_DIM = 64
N_HEADS_A = 8
KV_HEADS_A = 2
WINDOW = 128
N_HEADS_B = 4
N_HEADS_C = 4
HEAD_DIM_C = 128
N_EXPERTS = 64
N_GROUPS = 8
TOPK_GROUPS = 4
TOP_K = 8
ROUTED_SCALE = 2.5
ROPE_BASE = 10000.0
EPS = 1e-6

LANES = 128
CHUNK = 128
TILE_N = 512
TM = 256
MIXB_KEYS = 1024
MIXB_QUERIES = 512
EXPERT_ROWS = 512
SUBLANES = 8
SORT_ROWS = TOP_K * TM + N_EXPERTS * SUBLANES
SORT_CHUNK = 512
NEG = -1e30
VMEM_LIMIT = 56 * 1024 * 1024

T_AQ, T_BQ, T_BK, T_KVA, T_BV, T_CO, T_GT, T_CQ, T_CV, N_TILES = 0, 1, 2, 3, 4, 5, 6, 12, 13, 14

NT_DIMS = (((1,), (1,)), ((), ()))


def _params(sem):
    return pltpu.CompilerParams(dimension_semantics=sem, vmem_limit_bytes=VMEM_LIMIT)


def _rms(x, g):
    return x * lax.rsqrt(jnp.mean(x * x, axis=-1, keepdims=True) + EPS) * g


def _sigmoid(x):
    return jax.nn.sigmoid(x)


def _mod_kernel(c_ref, w_ref, b_ref, o_ref):
    c = c_ref[...]
    s = c * _sigmoid(c)
    o_ref[...] = jnp.dot(s, w_ref[...], precision=HIGHEST, preferred_element_type=F32) + b_ref[...]


def _mod_vectors(cs, w_mod, b_mod):
    depth, d, n = w_mod.shape
    r = cs.shape[0]
    tn = 3 * LANES
    return pl.pallas_call(
        _mod_kernel,
        grid=(depth, n // tn),
        in_specs=[pl.BlockSpec((r, d), lambda l, j: (0, 0)),
                  pl.BlockSpec((None, d, tn), lambda l, j: (l, 0, j)),
                  pl.BlockSpec((None, 1, tn), lambda l, j: (l, 0, j))],
        out_specs=pl.BlockSpec((None, r, tn), lambda l, j: (l, 0, j)),
        out_shape=jax.ShapeDtypeStruct((depth, r, n), F32),
        compiler_params=_params(("parallel", "parallel")),
        name="mod_vectors",
    )(cs, w_mod, b_mod.reshape(depth, 1, n))


def _stream_specs(xs, n_ctx_tiles, t0=0):
    if not isinstance(xs, tuple):
        return [pl.BlockSpec((None, TM, xs.shape[-1]), lambda bi, ti: (bi, ti + t0, 0))], [xs]
    d = xs[0].shape[-1]
    return ([pl.BlockSpec((None, TM, d), lambda bi, ti: (bi, jnp.minimum(ti + t0, n_ctx_tiles - 1), 0)),
             pl.BlockSpec((None, TM, d), lambda bi, ti: (bi, jnp.maximum(ti + t0 - n_ctx_tiles, 0), 0))], list(xs))


def _read_stream(x_refs, is_ctx_tile):
    if len(x_refs) == 1:
        return x_refs[0][...]
    return jnp.where(is_ctx_tile, x_refs[0][...], x_refs[1][...])


def _inproj_kernel(*refs, n_ctx_tiles):
    x_refs, (mod_ref, g_ref, w_ref, wkt_ref, cos_ref, sa_ref, sb_ref, p_ref, gate_ref, kt_ref) = refs[:-10], refs[-10:]
    x = _read_stream(x_refs, pl.program_id(1) < n_ctx_tiles)
    h = _rms(x, g_ref[...]) * (1.0 + mod_ref[1:2, :]) + mod_ref[0:1, :]
    hb = h.astype(BF16)
    cos, sa, sb = cos_ref[...], sa_ref[...], sb_ref[...]

    def rope(t):
        return t * cos + pltpu.roll(t, 16, 1) * sa + pltpu.roll(t, LANES - 16, 1) * sb

    q_scale = HEAD_DIM ** -0.5
    for j in range(N_TILES):
        acc = jnp.dot(hb, w_ref[:, j * TILE_N:(j + 1) * TILE_N], preferred_element_type=F32)
        parts = [acc[:, s * LANES:(s + 1) * LANES] for s in range(TILE_N // LANES)]
        if j == T_AQ:
            parts = [rope(t) * q_scale for t in parts]
        elif j == T_BQ:
            parts = [rope(t) * (q_scale * math.log2(math.e)) for t in parts]
        elif j == T_BK:
            parts = [rope(t) for t in parts]
        elif j == T_KVA:
            gate_ref[...] = parts[2]
            parts[0] = rope(parts[0])
        for s, t in enumerate(parts):
            p_ref[:, j * TILE_N + s * LANES:j * TILE_N + (s + 1) * LANES] = t.astype(BF16)
    kt = lax.dot_general(wkt_ref[...], hb, NT_DIMS, preferred_element_type=F32)
    kt_ref[...] = (kt * (HEAD_DIM_C ** -0.5)).astype(BF16)


def _inproj(xs, mods, g_mix, w_big, w_kt, cos, sa, sb, n_ctx_tiles):
    x_specs, x_args = _stream_specs(xs, n_ctx_tiles)
    b, d = x_args[0].shape[0], x_args[0].shape[-1]
    l = sum(a.shape[1] for a in x_args)
    npad = w_big.shape[1]
    ck = w_kt.shape[0]
    grid = (b, l // TM)
    tok = lambda bi, ti: (bi, ti, 0)
    return pl.pallas_call(
        functools.partial(_inproj_kernel, n_ctx_tiles=n_ctx_tiles),
        grid=grid,
        in_specs=x_specs + [
                  pl.BlockSpec((None, None, 8, d), lambda bi, ti: (bi, jnp.where(ti >= n_ctx_tiles, 1, 0), 0, 0)),
                  pl.BlockSpec((1, d), lambda bi, ti: (0, 0)),
                  pl.BlockSpec((d, npad), lambda bi, ti: (0, 0), pipeline_mode=pl.Buffered(1)),
                  pl.BlockSpec((ck, d), lambda bi, ti: (0, 0), pipeline_mode=pl.Buffered(1)),
                  pl.BlockSpec((TM, LANES), lambda bi, ti: (ti, 0)),
                  pl.BlockSpec((TM, LANES), lambda bi, ti: (ti, 0)),
                  pl.BlockSpec((TM, LANES), lambda bi, ti: (ti, 0))],
        out_specs=[pl.BlockSpec((None, TM, npad), tok),
                   pl.BlockSpec((None, TM, LANES), tok),
                   pl.BlockSpec((None, ck, TM), lambda bi, ti: (bi, 0, ti))],
        out_shape=[jax.ShapeDtypeStruct((b, l, npad), BF16),
                   jax.ShapeDtypeStruct((b, l, LANES), F32),
                   jax.ShapeDtypeStruct((b, ck, l), BF16)],
        compiler_params=_params(("parallel", "parallel")),
        name="inproj",
    )(*x_args, mods, g_mix, w_big, w_kt, cos, sa, sb)


def _mixa_kernel(sink_ref, q_ref, kp_ref, kc_ref, kn_ref, kx_ref, o_ref, *, n_ctx_blocks, n_blocks):
    i = pl.program_id(1)
    lat = i >= n_ctx_blocks
    has_prev = jnp.logical_and(lat, i > n_ctx_blocks)
    has_next = jnp.logical_and(lat, i < n_blocks - 1)
    r = lax.broadcasted_iota(jnp.int32, (CHUNK, CHUNK), 0)
    c = lax.broadcasted_iota(jnp.int32, (CHUNK, CHUNK), 1)
    n_ctx = kx_ref.shape[0]
    valid = jnp.concatenate([
        jnp.logical_and(c >= r, has_prev),
        jnp.broadcast_to(lat, (CHUNK, CHUNK)),
        jnp.logical_and(c <= r, has_next),
        jnp.ones((CHUNK, n_ctx), jnp.bool_)], axis=1)
    kcat = jnp.concatenate([kp_ref[:, :LANES], kc_ref[:, :LANES], kn_ref[:, :LANES], kx_ref[:, :LANES]], axis=0)
    vcat = jnp.concatenate([kp_ref[:, LANES:], kc_ref[:, LANES:], kn_ref[:, LANES:], kx_ref[:, LANES:]], axis=0)
    lane = lax.broadcasted_iota(jnp.int32, (CHUNK, LANES), 1)
    low = lane < HEAD_DIM
    n_pairs = N_HEADS_A // KV_HEADS_A
    outs = []
    for gk in range(KV_HEADS_A):
        keep = low if gk == 0 else jnp.logical_not(low)
        zero = jnp.zeros((CHUNK, LANES), BF16)
        lhs = jnp.concatenate([jnp.where(keep, q_ref[:, t * LANES:(t + 1) * LANES], zero) for t in range(n_pairs)],
                              axis=0)
        s = lax.dot_general(lhs, kcat, NT_DIMS, preferred_element_type=F32)
        o_g = []
        for t in range(n_pairs):
            st = jnp.where(valid, s[t * CHUNK:(t + 1) * CHUNK], NEG)
            sk = sink_ref[gk * n_pairs + t]
            m = jnp.maximum(jnp.max(st, axis=-1, keepdims=True), sk)
            p = jnp.exp(st - m)
            den = jnp.sum(p, axis=-1, keepdims=True) + jnp.exp(sk - m)
            o_g.append(jnp.dot(p.astype(BF16), vcat, preferred_element_type=F32) / den)
        outs.append(o_g)
    for t in range(n_pairs):
        o_ref[:, t * LANES:(t + 1) * LANES] = jnp.where(low, outs[0][t], outs[1][t]).astype(BF16)


def _mixer_a(p, sink, n_ctx):
    b, l, _ = p.shape
    nb = l // CHUNK
    ncb = n_ctx // CHUNK
    kvw = 2 * LANES
    kv_col = T_KVA * TILE_N // kvw
    aq_w = N_HEADS_A * HEAD_DIM
    kern = functools.partial(_mixa_kernel, n_ctx_blocks=ncb, n_blocks=nb)
    return pl.pallas_call(
        kern,
        grid=(b, nb),
        in_specs=[pl.BlockSpec(memory_space=pltpu.SMEM),
                  pl.BlockSpec((None, CHUNK, aq_w), lambda bi, i: (bi, i, T_AQ)),
                  pl.BlockSpec((None, CHUNK, kvw), lambda bi, i: (bi, jnp.maximum(i - 1, 0), kv_col)),
                  pl.BlockSpec((None, CHUNK, kvw), lambda bi, i: (bi, i, kv_col)),
                  pl.BlockSpec((None, CHUNK, kvw), lambda bi, i: (bi, jnp.minimum(i + 1, nb - 1), kv_col)),
                  pl.BlockSpec((None, n_ctx, kvw), lambda bi, i: (bi, 0, kv_col))],
        out_specs=pl.BlockSpec((None, CHUNK, aq_w), lambda bi, i: (bi, i, 0)),
        out_shape=jax.ShapeDtypeStruct((b, l, aq_w), BF16),
        compiler_params=_params(("parallel", "parallel")),
        name="mixer_a",
    )(sink, p, p, p, p, p)


def _fold_lanes(op, acc, s):
    for t in range(s.shape[1] // LANES):
        acc = op(acc, s[:, t * LANES:(t + 1) * LANES])
    return acc


def _mixb_kernel(lam_ref, gd_ref, k_ref, v_ref, *rest, lam_init, chunks):
    q_refs, (o_ref, s_scr, va_scr) = rest[:-3], rest[-3:]

    @pl.when(pl.program_id(2) == 0)
    def _():
        n_keys = v_ref.shape[0]
        va_scr[:, :LANES] = v_ref[...]
        va_scr[:, LANES:] = (lax.broadcasted_iota(jnp.int32, (n_keys, LANES), 1) == 0).astype(BF16)

    lp = lam_ref[...]
    lam = (jnp.exp(jnp.sum(lp[0:1] * lp[1:2], axis=-1, keepdims=True))
           - jnp.exp(jnp.sum(lp[2:3] * lp[3:4], axis=-1, keepdims=True)) + lam_init)
    q = jnp.concatenate([qr[...] for qr in q_refs], axis=0)
    tq = q.shape[0]
    lane = lax.broadcasted_iota(jnp.int32, (tq, LANES), 1)
    zero = jnp.zeros_like(q)
    qs = (jnp.where(lane < HEAD_DIM, q, zero), jnp.where(lane >= HEAD_DIM, q, zero))
    rows = [slice(mi * tq, (mi + 1) * tq) for mi in range(2)]
    mrun = [jnp.full((tq, LANES), NEG, F32) for _ in range(2)]
    for off, sz in chunks:
        for mi in range(2):
            s_scr[rows[mi], off:off + sz] = lax.dot_general(qs[mi], k_ref[off:off + sz, :], NT_DIMS,
                                                            preferred_element_type=F32)
            mrun[mi] = _fold_lanes(jnp.maximum, mrun[mi], s_scr[rows[mi], off:off + sz])
    m = [jnp.max(mr, axis=-1, keepdims=True) for mr in mrun]
    acc = [jnp.zeros((tq, 2 * LANES), F32) for _ in range(2)]
    for off, sz in chunks:
        for mi in range(2):
            pr = jnp.exp2(s_scr[rows[mi], off:off + sz] - m[mi])
            acc[mi] = acc[mi] + jnp.dot(pr.astype(BF16), va_scr[off:off + sz, :], preferred_element_type=F32)
    outs = [a[:, :LANES] / a[:, LANES:LANES + 1] for a in acc]
    o = outs[0] - lam * outs[1]
    o_ref[...] = (_rms(o, gd_ref[...]) * (1.0 - lam_init)).astype(BF16)


def _mixer_b(p, lam_params, g_diff, lam_init, n_ctx):
    b, l, _ = p.shape
    kl = min(MIXB_KEYS, l - n_ctx)
    tq_lat = min(MIXB_QUERIES, l - n_ctx)
    assert (l - n_ctx) % kl == 0 and (l - n_ctx) % tq_lat == 0 and tq_lat % TM == 0 and n_ctx % TM == 0
    q0 = T_BQ * TILE_N // LANES
    k0 = T_BK * TILE_N // LANES
    v0 = T_BV * TILE_N // LANES
    ctx_chunks = ((0, n_ctx),)
    all_chunks = ctx_chunks + tuple((n_ctx + c * kl, kl) for c in range((l - n_ctx) // kl))

    def call(chunks, n_keys, tq, q_tiles, first_row):
        nq = tq // TM
        q_spec = lambda part: pl.BlockSpec((None, TM, LANES),
                                           lambda bi, h, qi: (bi, first_row // TM + qi * nq + part, q0 + h))
        kern = functools.partial(_mixb_kernel, lam_init=lam_init, chunks=chunks)
        return pl.pallas_call(
            kern,
            grid=(b, N_HEADS_B, q_tiles),
            in_specs=[pl.BlockSpec((4, HEAD_DIM), lambda bi, h, qi: (0, 0)),
                      pl.BlockSpec((1, LANES), lambda bi, h, qi: (0, 0)),
                      pl.BlockSpec((None, n_keys, LANES), lambda bi, h, qi: (bi, 0, k0 + h)),
                      pl.BlockSpec((None, n_keys, LANES), lambda bi, h, qi: (bi, 0, v0 + h))]
                     + [q_spec(part) for part in range(nq)],
            out_specs=pl.BlockSpec((None, tq, LANES), lambda bi, h, qi: (bi, qi, h)),
            out_shape=jax.ShapeDtypeStruct((b, q_tiles * tq, N_HEADS_B * LANES), BF16),
            scratch_shapes=[pltpu.VMEM((2 * tq, n_keys), F32), pltpu.VMEM((n_keys, 2 * LANES), BF16)],
            compiler_params=_params(("parallel", "parallel", "arbitrary")),
            name="mixer_b",
        )(lam_params, g_diff, p, p, *([p] * nq))

    tq_ctx = min(tq_lat, n_ctx)
    return (call(ctx_chunks, n_ctx, tq_ctx, n_ctx // tq_ctx, 0),
            call(all_chunks, l, tq_lat, (l - n_ctx) // tq_lat, n_ctx))


def _log_sigmoid(x):
    return jnp.minimum(x, 0.0) - jnp.log1p(jnp.exp(-jnp.abs(x)))


def _mlstm_kernel(qf_ref, ktf_ref, vf_ref, gcf_ref, grf_ref, qb_ref, ktb_ref, vb_ref, gcb_ref, grb_ref,
                  bc_ref, br_ref, of_ref, ob_ref, s_scr, m_scr):
    @pl.when(pl.program_id(1) == 0)
    def _():
        s_scr[...] = jnp.zeros_like(s_scr)
        m_scr[...] = jnp.zeros_like(m_scr)

    r = lax.broadcasted_iota(jnp.int32, (CHUNK, CHUNK), 0)
    cc = lax.broadcasted_iota(jnp.int32, (CHUNK, CHUNK), 1)
    lane = lax.broadcasted_iota(jnp.int32, (CHUNK, LANES), 1)
    ones_col = (lane == 0).astype(BF16)
    nh = N_HEADS_C
    tris = (r >= cc, r <= cc)
    refs = ((qf_ref, ktf_ref, vf_ref, gcf_ref, grf_ref, of_ref), (qb_ref, ktb_ref, vb_ref, gcb_ref, grb_ref, ob_ref))
    grow, bcum_col, bcum_row, tot_row = [], [], [], []
    for di in range(2):
        trif = tris[di].astype(F32)
        gcol = refs[di][3][...] + bc_ref[...]
        grow.append(refs[di][4][...] + br_ref[...])
        lf_col = _log_sigmoid(gcol)
        lf_row = _log_sigmoid(grow[di])
        bcum_col.append(jnp.dot(trif, lf_col, precision=HIGHEST, preferred_element_type=F32))
        bcum_row.append(lax.dot_general(lf_row, trif, NT_DIMS, precision=HIGHEST, preferred_element_type=F32))
        tot_row.append(jnp.sum(lf_row, axis=-1, keepdims=True))
    ch = [(di, h) for di in range(2) for h in range(nh)]
    gi = {c: (2 * c[0]) * nh + c[1] for c in ch}
    gf = {c: (2 * c[0] + 1) * nh + c[1] for c in ch}
    ic_row = {c: grow[c[0]][gi[c]:gi[c] + 1, :] for c in ch}
    b_col = {c: bcum_col[c[0]][:, gf[c]:gf[c] + 1] for c in ch}
    b_row = {c: bcum_row[c[0]][gf[c]:gf[c] + 1, :] for c in ch}
    total = {c: tot_row[c[0]][gf[c]:gf[c] + 1, :] for c in ch}
    m_st = {c: m_scr[c[0], c[1], 0:1, 0:1] for c in ch}
    qh = {c: refs[c[0]][0][:, c[1] * LANES:(c[1] + 1) * LANES] for c in ch}
    kth = {c: refs[c[0]][1][c[1] * LANES:(c[1] + 1) * LANES, :] for c in ch}
    vaug = {c: jnp.concatenate([refs[c[0]][2][:, c[1] * LANES:(c[1] + 1) * LANES], ones_col], axis=1) for c in ch}
    st = {c: s_scr[c[0], c[1]] for c in ch}
    qk = {c: jnp.dot(qh[c], kth[c], preferred_element_type=F32) for c in ch}
    cross = {c: jnp.dot(qh[c], st[c].astype(BF16), preferred_element_type=F32) for c in ch}
    gs_row = {c: total[c] - b_row[c] + ic_row[c] for c in ch}
    m_new = {c: jnp.maximum(total[c] + m_st[c], jnp.max(gs_row[c], axis=-1, keepdims=True)) for c in ch}
    wkt = {c: (kth[c].astype(F32) * jnp.exp(gs_row[c] - m_new[c])).astype(BF16) for c in ch}
    upd = {c: jnp.dot(wkt[c], vaug[c], preferred_element_type=F32) for c in ch}
    dm = {c: jnp.where(tris[c[0]], b_col[c] - b_row[c] + ic_row[c], NEG) for c in ch}
    inter = {c: b_col[c] + m_st[c] for c in ch}
    m_t = {c: jnp.maximum(inter[c], jnp.max(dm[c], axis=-1, keepdims=True)) for c in ch}
    sc = {c: (qk[c] * jnp.exp(dm[c] - m_t[c])).astype(BF16) for c in ch}
    intra = {c: jnp.dot(sc[c], vaug[c], preferred_element_type=F32) for c in ch}
    for c in ch:
        di, h = c
        nd = intra[c] + jnp.exp(inter[c] - m_t[c]) * cross[c]
        den = nd[:, LANES:LANES + 1]
        refs[di][5][:, h * LANES:(h + 1) * LANES] = nd[:, :LANES] / jnp.maximum(jnp.abs(den), jnp.exp(-m_t[c]))
        s_scr[di, h] = jnp.exp(total[c] + m_st[c] - m_new[c]) * st[c] + upd[c]
        m_scr[di, h] = jnp.broadcast_to(m_new[c], m_scr.shape[2:])


def _mlstm(p, kt, gates, gates_t, bias_row, bias_col, n_ctx):
    b, l, _ = p.shape
    nc = l // CHUNK
    ncc = n_ctx // CHUNK
    cw = N_HEADS_C * HEAD_DIM_C

    def rev(c):
        return jnp.where(c < ncc, ncc - 1 - c, nc + ncc - 1 - c)

    def specs(ch):
        return [pl.BlockSpec((None, CHUNK, cw), lambda bi, c: (bi, ch(c), T_CQ)),
                pl.BlockSpec((None, cw, CHUNK), lambda bi, c: (bi, 0, ch(c))),
                pl.BlockSpec((None, CHUNK, cw), lambda bi, c: (bi, ch(c), T_CV)),
                pl.BlockSpec((None, CHUNK, LANES), lambda bi, c: (bi, ch(c), 0)),
                pl.BlockSpec((None, 16, CHUNK), lambda bi, c: (bi, 0, ch(c)))]

    fwd = lambda c: c
    out = jax.ShapeDtypeStruct((b, l, cw), F32)
    return pl.pallas_call(
        _mlstm_kernel,
        grid=(b, nc),
        in_specs=specs(fwd) + specs(rev) + [pl.BlockSpec((1, LANES), lambda bi, c: (0, 0)),
                                            pl.BlockSpec((16, LANES), lambda bi, c: (0, 0))],
        out_specs=[pl.BlockSpec((None, CHUNK, cw), lambda bi, c: (bi, c, 0)),
                   pl.BlockSpec((None, CHUNK, cw), lambda bi, c: (bi, rev(c), 0))],
        out_shape=[out, out],
        scratch_shapes=[pltpu.VMEM((2, N_HEADS_C, HEAD_DIM_C, 2 * LANES), F32),
                        pltpu.VMEM((2, N_HEADS_C, 8, LANES), F32)],
        compiler_params=_params(("parallel", "arbitrary")),
        name="mlstm",
    )(p, kt, p, gates, gates_t, p, kt, p, gates, gates_t, bias_row, bias_col)


def _merge_kernel(*refs, n_ctx_tiles, t0):
    x_refs, (mod_ref, oa_ref, obc_ref, obl_ref, hf_ref, hb_ref, co_ref, gt_ref, gm_ref,
             wa_ref, wb_ref, wc_ref, wo_ref, xo_ref) = refs[:-14], refs[-14:]
    d = xo_ref.shape[-1]
    is_ctx = pl.program_id(1) + t0 < n_ctx_tiles
    ob = jnp.where(is_ctx, obc_ref[...], obl_ref[...])
    hs = hf_ref[...] + hb_ref[...]
    co = co_ref[...].astype(F32)
    gm = gm_ref[...]
    oc = []
    for h in range(N_HEADS_C):
        sl = slice(h * LANES, (h + 1) * LANES)
        oc.append((_rms(hs[:, sl], gm[:, sl]) * _sigmoid(co[:, sl])).astype(BF16))
    oc = jnp.concatenate(oc, axis=1)
    y = (_sigmoid(gt_ref[:, 0:d].astype(F32)) * jnp.dot(oa_ref[...], wa_ref[...], preferred_element_type=F32)
         + _sigmoid(gt_ref[:, d:2 * d].astype(F32)) * jnp.dot(ob, wb_ref[...], preferred_element_type=F32)
         + _sigmoid(gt_ref[:, 2 * d:3 * d].astype(F32)) * jnp.dot(oc, wc_ref[...], preferred_element_type=F32))
    out = jnp.dot(y.astype(BF16), wo_ref[...], preferred_element_type=F32)
    xo_ref[...] = _read_stream(x_refs, is_ctx) + mod_ref[2:3, :] * out


def _merge(xs, mods, oa, ob_ctx, ob_lat, hf, hb, p, g_mlstm, wa, wb, wc, wo, n_ctx_tiles, latent_only):
    t0 = n_ctx_tiles if latent_only else 0
    x_specs, x_args = _stream_specs(xs, n_ctx_tiles, t0)
    b, d = x_args[0].shape[0], x_args[0].shape[-1]
    n_tiles = oa.shape[1] // TM - t0
    tok = lambda bi, ti: (bi, ti + t0, 0)
    cw = N_HEADS_C * HEAD_DIM_C
    const = lambda bi, ti: (0, 0)
    return pl.pallas_call(
        functools.partial(_merge_kernel, n_ctx_tiles=n_ctx_tiles, t0=t0),
        grid=(b, n_tiles),
        in_specs=x_specs + [
                  pl.BlockSpec((None, None, 8, d), lambda bi, ti: (bi, jnp.where(ti + t0 >= n_ctx_tiles, 1, 0), 0, 0)),
                  pl.BlockSpec((None, TM, oa.shape[-1]), tok),
                  pl.BlockSpec((None, TM, ob_ctx.shape[-1]),
                               lambda bi, ti: (bi, jnp.minimum(ti + t0, n_ctx_tiles - 1), 0)),
                  pl.BlockSpec((None, TM, ob_lat.shape[-1]),
                               lambda bi, ti: (bi, jnp.maximum(ti + t0 - n_ctx_tiles, 0), 0)),
                  pl.BlockSpec((None, TM, cw), tok),
                  pl.BlockSpec((None, TM, cw), tok),
                  pl.BlockSpec((None, TM, cw), lambda bi, ti: (bi, ti + t0, T_CO)),
                  pl.BlockSpec((None, TM, 3 * d), lambda bi, ti: (bi, ti + t0, T_GT * TILE_N // (3 * d))),
                  pl.BlockSpec((1, cw), const),
                  pl.BlockSpec(wa.shape, const), pl.BlockSpec(wb.shape, const),
                  pl.BlockSpec(wc.shape, const), pl.BlockSpec(wo.shape, const)],
        out_specs=pl.BlockSpec((None, TM, d), lambda bi, ti: (bi, ti, 0)),
        out_shape=jax.ShapeDtypeStruct((b, n_tiles * TM, d), F32),
        compiler_params=_params(("parallel", "parallel")),
        name="merge",
    )(*x_args, mods, oa, ob_ctx, ob_lat, hf, hb, p, p, g_mlstm, wa, wb, wc, wo)


def _router_kernel(x_ref, mod_ref, g_ref, wrt_ref, br_ref, h_ref, idx_ref, wt_ref, cnt_ref):
    h = _rms(x_ref[...], g_ref[...]) * (1.0 + mod_ref[4:5, :]) + mod_ref[3:4, :]
    h_ref[...] = h.astype(BF16)
    tm = h.shape[0]
    per = N_EXPERTS // N_GROUPS
    lt = lax.dot_general(wrt_ref[...], h, NT_DIMS, precision=HIGHEST, preferred_element_type=F32)
    s = _sigmoid(lt)
    sel = s + br_ref[...]
    ninf = -jnp.inf
    sel3 = sel.reshape(N_GROUPS, per, tm)
    eidx = lax.broadcasted_iota(jnp.int32, (N_GROUPS, per, tm), 1)
    m1 = jnp.max(sel3, axis=1, keepdims=True)
    first = jnp.min(jnp.where(sel3 == m1, eidx, per), axis=1, keepdims=True)
    m2 = jnp.max(jnp.where(eidx == first, ninf, sel3), axis=1, keepdims=True)
    gscore = (m1 + m2).reshape(N_GROUPS, tm)
    gidx = lax.broadcasted_iota(jnp.int32, (N_GROUPS, tm), 0)
    gmask = jnp.zeros((N_GROUPS, tm), jnp.bool_)
    cur = gscore
    for _ in range(TOPK_GROUPS):
        mx = jnp.max(cur, axis=0, keepdims=True)
        hit = gidx == jnp.min(jnp.where(cur == mx, gidx, N_GROUPS), axis=0, keepdims=True)
        gmask = jnp.logical_or(gmask, hit)
        cur = jnp.where(hit, ninf, cur)
    cur = jnp.where(gmask.reshape(N_GROUPS, 1, tm), sel3, ninf).reshape(N_EXPERTS, tm)
    eid = lax.broadcasted_iota(jnp.int32, (N_EXPERTS, tm), 0)
    ids, ws = [], []
    chosen = jnp.zeros((N_EXPERTS, tm), F32)
    for _ in range(TOP_K):
        mx = jnp.max(cur, axis=0, keepdims=True)
        pick = jnp.min(jnp.where(cur == mx, eid, N_EXPERTS), axis=0, keepdims=True)
        hit = eid == pick
        ids.append(pick)
        ws.append(jnp.sum(jnp.where(hit, s, 0.0), axis=0, keepdims=True))
        cur = jnp.where(hit, ninf, cur)
        chosen = chosen + hit.astype(F32)
    wsum = ws[0]
    for w in ws[1:]:
        wsum = wsum + w
    idx_ref[...] = jnp.concatenate(ids, axis=0)
    wt_ref[...] = jnp.concatenate([w / wsum * ROUTED_SCALE for w in ws], axis=0)
    cnt_ref[...] = jnp.sum(chosen, axis=1, keepdims=True).astype(jnp.int32)


def _router(xs, mods, g_ffn, w_router_t, b_router, n_ctx_tiles):
    b, l, d = xs.shape
    tok = lambda bi, ti: (bi, ti, 0)
    const = lambda bi, ti: (0, 0)
    return pl.pallas_call(
        _router_kernel,
        grid=(b, l // TM),
        in_specs=[pl.BlockSpec((None, TM, d), tok),
                  pl.BlockSpec((None, None, 8, d), lambda bi, ti: (bi, jnp.where(ti >= n_ctx_tiles, 1, 0), 0, 0)),
                  pl.BlockSpec((1, d), const),
                  pl.BlockSpec((N_EXPERTS, d), const),
                  pl.BlockSpec((N_EXPERTS, 1), const)],
        out_specs=[pl.BlockSpec((None, TM, d), tok),
                   pl.BlockSpec((None, TOP_K, TM), lambda bi, ti: (bi, 0, ti)),
                   pl.BlockSpec((None, TOP_K, TM), lambda bi, ti: (bi, 0, ti)),
                   pl.BlockSpec((None, None, N_EXPERTS, 1), lambda bi, ti: (bi, ti, 0, 0))],
        out_shape=[jax.ShapeDtypeStruct((b, l, d), BF16),
                   jax.ShapeDtypeStruct((b, TOP_K, l), jnp.int32),
                   jax.ShapeDtypeStruct((b, TOP_K, l), F32),
                   jax.ShapeDtypeStruct((b, l // TM, N_EXPERTS, 1), jnp.int32)],
        compiler_params=_params(("parallel", "parallel")),
        name="router",
    )(xs, mods, g_ffn, w_router_t, b_router)


def _pack_bf16_pairs(x):
    half = x.shape[1] // 2
    lo = lax.bitcast_convert_type(x[:, :half], jnp.uint32) >> 16
    hi = lax.bitcast_convert_type(x[:, half:], jnp.uint32) & jnp.uint32(0xFFFF0000)
    return lo | hi


def _unpack_bf16_pairs(w):
    lo = lax.bitcast_convert_type(w << 16, F32)
    hi = lax.bitcast_convert_type(w & jnp.uint32(0xFFFF0000), F32)
    return jnp.concatenate([lo, hi], axis=1).astype(BF16)


def _sort_kernel(idx_ref, off_ref, h_ref, posl_ref, ts_ref):
    tm = h_ref.shape[0]
    idx = idx_ref[...]
    eid = lax.broadcasted_iota(jnp.int32, (N_EXPERTS, tm), 0)
    hits = [eid == idx[k:k + 1, :] for k in range(TOP_K)]
    chosen = hits[0].astype(BF16)
    for hk in hits[1:]:
        chosen = chosen + hk.astype(BF16)
    r = lax.broadcasted_iota(jnp.int32, (tm, tm), 0)
    c = lax.broadcasted_iota(jnp.int32, (tm, tm), 1)
    before = (r < c).astype(BF16)
    rank = jnp.dot(chosen, before, preferred_element_type=F32)
    slot = rank.astype(jnp.int32) + off_ref[...]
    posl = jnp.concatenate([jnp.sum(jnp.where(hk, slot, 0), axis=0, keepdims=True) for hk in hits], axis=0)
    posl_ref[...] = posl
    hb = h_ref[...]
    for rb in range(ts_ref.shape[0] // SORT_CHUNK):
        rows = lax.broadcasted_iota(jnp.int32, (SORT_CHUNK, tm), 0) + rb * SORT_CHUNK
        sel = rows == posl[0:1, :]
        for k in range(1, TOP_K):
            sel = jnp.logical_or(sel, rows == posl[k:k + 1, :])
        onehot = jnp.where(sel, 1.0, 0.0).astype(BF16)
        ts = jnp.dot(onehot, hb, preferred_element_type=F32)
        ts_ref[rb * SORT_CHUNK:(rb + 1) * SORT_CHUNK, :] = _pack_bf16_pairs(ts)


def _sort_rows(idx_t, off, h_flat):
    b, k, l = idx_t.shape
    n, d = h_flat.shape
    nt = l // TM
    return pl.pallas_call(
        _sort_kernel,
        grid=(b * nt,),
        in_specs=[pl.BlockSpec((None, k, TM), lambda i: (i // nt, 0, i % nt)),
                  pl.BlockSpec((None, N_EXPERTS, 1), lambda i: (i, 0, 0)),
                  pl.BlockSpec((TM, d), lambda i: (i, 0))],
        out_specs=[pl.BlockSpec((None, k, TM), lambda i: (i // nt, 0, i % nt)),
                   pl.BlockSpec((None, SORT_ROWS, d // 2), lambda i: (i, 0, 0))],
        out_shape=[jax.ShapeDtypeStruct((b, k, l), jnp.int32),
                   jax.ShapeDtypeStruct((b * nt, SORT_ROWS, d // 2), jnp.uint32)],
        compiler_params=_params(("parallel",)),
        name="sort_rows",
    )(idx_t, off, h_flat)


SLAB_SIZES = tuple(SUBLANES << s for s in range((EXPERT_ROWS // SUBLANES).bit_length()))


def _expert_kernel(be_ref, nu_ref, cov_ref, grp_ref, ts_in, wg_ref, wu_ref, wd_ref, ts_out, xbuf, ybuf, sem_g, sem_s):
    j = pl.program_id(0)
    nu = nu_ref[0]
    blk = xbuf.shape[1]
    n_grp = blk // SUBLANES
    slot = j % 2

    def for_groups(bj, fn):
        for i in range(n_grp):
            sg = grp_ref[bj * n_grp + i]

            @pl.when(sg >= 0)
            def _():
                fn(pl.ds(pl.multiple_of(sg * SUBLANES, SUBLANES), SUBLANES), i * SUBLANES)

    def gather(bj, s):
        xbuf[s] = jnp.zeros(xbuf.shape[1:], xbuf.dtype)
        for_groups(bj, lambda src, row: pltpu.make_async_copy(
            ts_in.at[src], xbuf.at[s, pl.ds(row, SUBLANES)], sem_g.at[s]).start())

    def scatter(bj, s):
        for_groups(bj, lambda dst, row: pltpu.make_async_copy(
            ybuf.at[s, pl.ds(row, SUBLANES)], ts_out.at[dst], sem_s.at[s]).start())

    def wait_rows(bj, buf, sem, s):
        rows, off = cov_ref[bj], 0
        for sz in SLAB_SIZES:
            @pl.when((rows & sz) != 0)
            def _():
                pltpu.make_async_copy(ts_in.at[pl.ds(0, sz)], buf.at[s, pl.ds(0, sz)], sem.at[s]).wait()

    @pl.when(j == 0)
    def _():
        gather(0, 0)

    @pl.when(j + 1 < nu)
    def _():
        gather(j + 1, 1 - slot)

    @pl.when(jnp.logical_and(j >= 2, j - 2 < nu))
    def _():
        wait_rows(j - 2, ybuf, sem_s, slot)

    @pl.when(j < nu)
    def _():
        wait_rows(j, xbuf, sem_g, slot)
        x = _unpack_bf16_pairs(xbuf[slot])
        g = jnp.dot(x, wg_ref[...], preferred_element_type=F32)
        u = jnp.dot(x, wu_ref[...], preferred_element_type=F32)
        a = (g * _sigmoid(g) * u).astype(BF16)
        y = jnp.dot(a, wd_ref[...], preferred_element_type=F32)
        ybuf[slot] = _pack_bf16_pairs(y.astype(BF16).astype(F32))
        scatter(j, slot)


def _experts(plan, tiles, wg, wu, wd):
    nt, rows, w = tiles.shape
    blk = EXPERT_ROWS
    d, de = wg.shape[1:]
    n_blocks = plan["block_e"].shape[0]
    tables = (plan["block_e"], plan["n_used"], plan["blk_rows"], plan["blk_groups"])
    wspec = lambda shape: pl.BlockSpec((None,) + shape, lambda i, be, *_: (be[i], 0, 0))
    grid_spec = pltpu.PrefetchScalarGridSpec(
        num_scalar_prefetch=len(tables),
        grid=(n_blocks,),
        in_specs=[pl.BlockSpec(memory_space=pl.ANY), wspec((d, de)), wspec((d, de)), wspec((de, d))],
        out_specs=pl.BlockSpec(memory_space=pl.ANY),
        scratch_shapes=[pltpu.VMEM((2, blk, w), tiles.dtype), pltpu.VMEM((2, blk, w), tiles.dtype),
                        pltpu.SemaphoreType.DMA((2,)), pltpu.SemaphoreType.DMA((2,))])
    return pl.pallas_call(
        _expert_kernel,
        grid_spec=grid_spec,
        out_shape=jax.ShapeDtypeStruct((nt * rows, w), tiles.dtype),
        input_output_aliases={len(tables): 0},
        compiler_params=_params(("arbitrary",)),
        name="experts",
    )(*tables, tiles.reshape(nt * rows, w), wg, wu, wd).reshape(tiles.shape)


def _combine_kernel(ts_ref, posl_ref, w_ref, x_ref, h_ref, mod_ref, wsg_ref, wsu_ref, wsd_ref, *rest):
    tm = x_ref.shape[0]
    hb = h_ref[...]
    g = jnp.dot(hb, wsg_ref[...], preferred_element_type=F32)
    u = jnp.dot(hb, wsu_ref[...], preferred_element_type=F32)
    acc = jnp.dot((g * _sigmoid(g) * u).astype(BF16), wsd_ref[...], preferred_element_type=F32)
    posl = posl_ref[...]
    w = w_ref[...]
    for rb in range(ts_ref.shape[0] // SORT_CHUNK):
        cols = lax.broadcasted_iota(jnp.int32, (tm, SORT_CHUNK), 1) + rb * SORT_CHUNK
        wm = jnp.zeros((tm, SORT_CHUNK), F32)
        for k in range(TOP_K):
            wm = jnp.where(cols == posl[:, k:k + 1], w[:, k:k + 1], wm)
        ys = _unpack_bf16_pairs(ts_ref[rb * SORT_CHUNK:(rb + 1) * SORT_CHUNK, :])
        acc = acc + jnp.dot(wm.astype(BF16), ys, preferred_element_type=F32)
    out = x_ref[...] + mod_ref[5:6, :] * acc
    rest[-1][...] = _rms(out, rest[0][...]) if len(rest) == 2 else out


def _combine(tiles, posl_tm, wts, x_flat, h_flat, mods, wsg, wsu, wsd, tiles_per_sample, n_ctx_tiles, g_final=None):
    n, d = x_flat.shape
    ds_ = wsg.shape[-1]
    tok = lambda i: (i, 0)
    const = lambda i: (0, 0)

    def mod_idx(i):
        return (i // tiles_per_sample, jnp.where(i % tiles_per_sample >= n_ctx_tiles, 1, 0), 0, 0)

    in_extra, args_extra, out_rows, out_idx = [], [], n, tok
    if g_final is not None:
        lat_tiles = tiles_per_sample - n_ctx_tiles
        in_extra, args_extra = [pl.BlockSpec((1, d), const)], [g_final]
        out_rows = n // TM // tiles_per_sample * lat_tiles * TM
        out_idx = lambda i: (i // tiles_per_sample * lat_tiles + jnp.maximum(i % tiles_per_sample - n_ctx_tiles, 0), 0)
    return pl.pallas_call(
        _combine_kernel,
        grid=(n // TM,),
        in_specs=[pl.BlockSpec((None,) + tiles.shape[1:], lambda i: (i, 0, 0)),
                  pl.BlockSpec((TM, TOP_K), tok),
                  pl.BlockSpec((TM, TOP_K), tok),
                  pl.BlockSpec((TM, d), tok),
                  pl.BlockSpec((TM, d), tok),
                  pl.BlockSpec((None, None, 8, d), mod_idx),
                  pl.BlockSpec((d, ds_), const), pl.BlockSpec((d, ds_), const), pl.BlockSpec((ds_, d), const)]
                 + in_extra,
        out_specs=pl.BlockSpec((TM, d), out_idx),
        out_shape=jax.ShapeDtypeStruct((out_rows, d), F32),
        compiler_params=_params(("arbitrary",)),
        name="combine",
    )(tiles, posl_tm, wts, x_flat, h_flat, mods, wsg, wsu, wsd, *args_extra)


def _moe_plan(cnt, n_assign, blk):
    nt = cnt.shape[0]
    run = (cnt + SUBLANES - 1) // SUBLANES * SUBLANES
    tile_off = jnp.cumsum(run, axis=1) - run
    tot = jnp.sum(run, axis=0)
    padded = (tot + blk - 1) // blk * blk
    pad_end = jnp.cumsum(padded)
    pad_start = pad_end - padded
    n_blocks = -(-(n_assign + nt * N_EXPERTS * (SUBLANES - 1)) // blk) + N_EXPERTS + 2
    first_row = jnp.arange(n_blocks, dtype=jnp.int32) * blk
    count = lambda m: jnp.sum(m.astype(jnp.int32), axis=1)
    block_e = jnp.minimum(count(pad_end[None, :] <= first_row[:, None]), N_EXPERTS - 1)
    i32 = lambda a: a.astype(jnp.int32)
    run_end = jnp.cumsum(run, axis=0).T[block_e][:, None, :]
    run_start = run_end - run.T[block_e][:, None, :]
    run_src = (jnp.arange(nt) * SORT_ROWS)[None, :] + tile_off.T[block_e]
    local = (first_row - pad_start[block_e])[:, None] + jnp.arange(0, blk, SUBLANES)[None, :]
    hit = jnp.logical_and(run_start <= local[:, :, None], local[:, :, None] < run_end)
    src_row = jnp.sum(jnp.where(hit, run_src[:, None, :] + local[:, :, None] - run_start, 0), axis=2)
    blk_groups = jnp.where(jnp.any(hit, axis=2), src_row // SUBLANES, -1).reshape(-1)
    return dict(
        tile_off=i32(tile_off)[:, :, None],
        blk_groups=i32(blk_groups),
        blk_rows=i32(jnp.clip((pad_start + tot)[block_e] - first_row, 0, blk)),
        block_e=i32(block_e), n_used=i32(pad_end[-1] // blk).reshape(1))


def _rope_tables(s_len, n_ctx):
    rows = s_len // GRID_W
    row = jnp.repeat(jnp.arange(rows), GRID_W).astype(F32)
    col = jnp.tile(jnp.arange(GRID_W), rows).astype(F32)
    quarter = HEAD_DIM // 4
    inv = 1.0 / (ROPE_BASE ** (jnp.arange(quarter, dtype=F32) / quarter))
    ar, ac = row[:, None] * inv, col[:, None] * inv
    cr, sr, cc, sc = jnp.cos(ar), jnp.sin(ar), jnp.cos(ac), jnp.sin(ac)
    z = jnp.zeros_like(sr)
    cos = jnp.concatenate([cr, cr, cc, cc], axis=1)
    sa = jnp.concatenate([z, sr, z, sc], axis=1)
    sb = jnp.concatenate([-sr, z, -sc, z], axis=1)
    rep = LANES // HEAD_DIM

    def full(t, fill):
        t = jnp.tile(t, (1, rep))
        return jnp.concatenate([jnp.full((n_ctx, LANES), fill, F32), t], axis=0)

    return full(cos, 1.0), full(sa, 0.0), full(sb, 0.0)


def _pair_perm():
    g = N_HEADS_A // KV_HEADS_A
    heads = [h for t in range(g) for h in (t, t + g)]
    return jnp.concatenate([jnp.arange(h * HEAD_DIM, (h + 1) * HEAD_DIM) for h in heads])


def _split_w_in(w):
    a_q, a_kv = N_HEADS_A * HEAD_DIM, KV_HEADS_A * HEAD_DIM
    b_w = N_HEADS_B * 2 * HEAD_DIM
    c_w = N_HEADS_C * HEAD_DIM_C
    sizes = (a_q, a_kv, a_kv, b_w, b_w, b_w, c_w, c_w, c_w, c_w, 4 * N_HEADS_C, w.shape[1])
    parts, start = [], 0
    for sz in sizes[:-1]:
        parts.append(w[:, start:start + sz])
        start += sz
    parts.append(w[:, start:])
    return parts


def _pack_w_in(w):
    d = w.shape[0]
    aq, ak, av, bq, bk, bv, cq, ck, cv, co, cg, gt = _split_w_in(w)
    pad = lambda n: jnp.zeros((d, n), w.dtype)
    kva = jnp.concatenate([ak, av, cg, pad(TILE_N - ak.shape[1] - av.shape[1] - cg.shape[1])], axis=1)
    big = jnp.concatenate([aq[:, _pair_perm()], bq, bk, kva, bv, co, gt, cq, cv], axis=1)
    return big.astype(BF16), ck.T.astype(BF16)


def kernel(x, c, ctx, c_ctx, w_mod, b_mod, g_mix, g_ffn, w_in, b_gate, sink, lam_q1, lam_k1, lam_q2, lam_k2,
           g_diff, g_mlstm, w_a, w_b, w_c, w_out, w_router, b_router, w_exp_gate, w_exp_up, w_exp_down,
           w_sh_gate, w_sh_up, w_sh_down, g_final):
    b, s_len, d = x.shape
    n_ctx = ctx.shape[1]
    l = n_ctx + s_len
    depth = w_mod.shape[0]
    n_ctx_tiles = n_ctx // TM
    assert n_ctx % TM == 0 and s_len % TM == 0 and d % LANES == 0 and s_len % GRID_W == 0

    xs = (ctx, x)
    cos, sa, sb = _rope_tables(s_len, n_ctx)

    rows_c = 16
    cs = jnp.concatenate([c, c_ctx[None], jnp.zeros((rows_c - b - 1, d), F32)], axis=0)
    mod_all = _mod_vectors(cs, w_mod, b_mod).reshape(depth, rows_c, N_MOD, d)
    mod_all = jnp.pad(mod_all, ((0, 0), (0, 0), (0, 8 - N_MOD), (0, 0)))

    perm = _pair_perm()
    for layer in range(depth):
        lam_init = 0.8 - 0.6 * math.exp(-0.3 * layer)
        mods = jnp.stack([jnp.broadcast_to(mod_all[layer, b], (b, 8, d)), mod_all[layer, :b]], axis=1)
        w_big, w_kt = _pack_w_in(w_in[layer])
        p, gates, kt = _inproj(xs, mods, g_mix[layer][None], w_big, w_kt, cos, sa, sb, n_ctx_tiles)

        oa = _mixer_a(p, sink[layer], n_ctx)
        lam_params = jnp.stack([lam_q1[layer], lam_k1[layer], lam_q2[layer], lam_k2[layer]])
        ob_ctx, ob_lat = _mixer_b(p, lam_params, g_diff[layer][None], lam_init, n_ctx)

        bias = b_gate[layer].reshape(-1)
        bias_row = jnp.pad(bias, (0, LANES - bias.shape[0]))[None]
        bias_col = jnp.broadcast_to(bias[:, None], (bias.shape[0], LANES))
        gates_t = jnp.transpose(gates[:, :, :bias.shape[0]], (0, 2, 1))
        hf, hb = _mlstm(p, kt, gates, gates_t, bias_row, bias_col, n_ctx)

        last = layer == depth - 1
        xs = _merge(xs, mods, oa, ob_ctx, ob_lat, hf, hb, p, g_mlstm[layer][None],
                    w_a[layer][perm].astype(BF16), w_b[layer].astype(BF16), w_c[layer].astype(BF16),
                    w_out[layer].astype(BF16), n_ctx_tiles, last)

        lf, ctx_tiles = (s_len, 0) if last else (l, n_ctx_tiles)
        h, idx_t, wt_t, cnt = _router(xs, mods, g_ffn[layer][None], w_router[layer].T, b_router[layer][:, None],
                                      ctx_tiles)
        plan = _moe_plan(cnt.reshape(-1, N_EXPERTS), b * lf * TOP_K, EXPERT_ROWS)
        h_flat = h.reshape(b * lf, d)
        posl, tiles = _sort_rows(idx_t, plan["tile_off"], h_flat)
        tiles = _experts(plan, tiles, w_exp_gate[layer].astype(BF16), w_exp_up[layer].astype(BF16),
                         w_exp_down[layer].astype(BF16))
        to_rows = lambda a: jnp.transpose(a, (0, 2, 1)).reshape(b * lf, TOP_K)
        xs = _combine(tiles, to_rows(posl), to_rows(wt_t), xs.reshape(b * lf, d), h_flat, mods,
                      w_sh_gate[layer].astype(BF16), w_sh_up[layer].astype(BF16), w_sh_down[layer].astype(BF16),
                      lf // TM, ctx_tiles, g_final[None] if last else None).reshape(b, lf, d)
    return xs
```

```python
import functools
import math

import jax
import jax.numpy as jnp
from jax import lax
from jax.experimental import pallas as pl
from jax.experimental.pallas import tpu as pltpu

F32 = jnp.float32
BF16 = jnp.bfloat16
HIGHEST = lax.Precision.HIGHEST

GRID_W = 64
N_MOD = 6
HEAD_DIM = 64
N_HEADS_A = 8
KV_HEADS_A = 2
WINDOW = 128
N_HEADS_B = 4
N_HEADS_C = 4
HEAD_DIM_C = 128
N_EXPERTS = 64
N_GROUPS = 8
TOPK_GROUPS = 4
TOP_K = 8
ROUTED_SCALE = 2.5
ROPE_BASE = 10000.0
EPS = 1e-6

LANES = 128
CHUNK = 128
TILE_N = 512
TM = 256
MIXB_KEYS = 1024
MIXB_QUERIES = 512
EXPERT_ROWS = 512
ROW_GROUP = 16
SORT_ROWS = TOP_K * TM + N_EXPERTS * ROW_GROUP
SORT_CHUNK = 128
COMBINE_CHUNK = 512
NEG = -1e30
VMEM_LIMIT = 56 * 1024 * 1024

T_AQ, T_BQ, T_BK, T_KVA, T_BV, T_CO, T_GT, T_CQ, T_CV, N_TILES = 0, 1, 2, 3, 4, 5, 6, 12, 13, 14

NT_DIMS = (((1,), (1,)), ((), ()))


def _params(sem):
    return pltpu.CompilerParams(dimension_semantics=sem, vmem_limit_bytes=VMEM_LIMIT)


def _rms(x, g):
    return x * lax.rsqrt(jnp.mean(x * x, axis=-1, keepdims=True) + EPS) * g


def _sigmoid(x):
    return jax.nn.sigmoid(x)


def _mod_kernel(c_ref, w_ref, b_ref, o_ref):
    c = c_ref[...]
    s = c * _sigmoid(c)
    o_ref[...] = jnp.dot(s, w_ref[...], precision=HIGHEST, preferred_element_type=F32) + b_ref[...]


def _mod_vectors(cs, w_mod, b_mod):
    depth, d, n = w_mod.shape
    r = cs.shape[0]
    tn = 3 * LANES
    return pl.pallas_call(
        _mod_kernel,
        grid=(depth, n // tn),
        in_specs=[pl.BlockSpec((r, d), lambda l, j: (0, 0)),
                  pl.BlockSpec((None, d, tn), lambda l, j: (l, 0, j)),
                  pl.BlockSpec((None, 1, tn), lambda l, j: (l, 0, j))],
        out_specs=pl.BlockSpec((None, r, tn), lambda l, j: (l, 0, j)),
        out_shape=jax.ShapeDtypeStruct((depth, r, n), F32),
        compiler_params=_params(("parallel", "parallel")),
        name="mod_vectors",
    )(cs, w_mod, b_mod.reshape(depth, 1, n))


def _stream_specs(xs, n_ctx_tiles, t0=0):
    if not isinstance(xs, tuple):
        return [pl.BlockSpec((None, TM, xs.shape[-1]), lambda bi, ti: (bi, ti + t0, 0))], [xs]
    d = xs[0].shape[-1]
    return ([pl.BlockSpec((None, TM, d), lambda bi, ti: (bi, jnp.minimum(ti + t0, n_ctx_tiles - 1), 0)),
             pl.BlockSpec((None, TM, d), lambda bi, ti: (bi, jnp.maximum(ti + t0 - n_ctx_tiles, 0), 0))], list(xs))


def _read_stream(x_refs, is_ctx_tile):
    if len(x_refs) == 1:
        return x_refs[0][...]
    return jnp.where(is_ctx_tile, x_refs[0][...], x_refs[1][...])


def _inproj_kernel(*refs, n_ctx_tiles):
    x_refs, (mod_ref, g_ref, w_ref, wkt_ref, cos_ref, sa_ref, sb_ref, p_ref, gate_ref, kt_ref) = refs[:-10], refs[-10:]
    x = _read_stream(x_refs, pl.program_id(1) < n_ctx_tiles)
    h = _rms(x, g_ref[...]) * (1.0 + mod_ref[1:2, :]) + mod_ref[0:1, :]
    hb = h.astype(BF16)
    cos, sa, sb = cos_ref[...], sa_ref[...], sb_ref[...]

    def rope(t):
        return t * cos + pltpu.roll(t, 16, 1) * sa + pltpu.roll(t, LANES - 16, 1) * sb

    q_scale = HEAD_DIM ** -0.5
    for j in range(N_TILES):
        acc = jnp.dot(hb, w_ref[:, j * TILE_N:(j + 1) * TILE_N], preferred_element_type=F32)
        parts = [acc[:, s * LANES:(s + 1) * LANES] for s in range(TILE_N // LANES)]
        if j == T_AQ:
            parts = [rope(t) * q_scale for t in parts]
        elif j == T_BQ:
            parts = [rope(t) * (q_scale * math.log2(math.e)) for t in parts]
        elif j == T_BK:
            parts = [rope(t) for t in parts]
        elif j == T_KVA:
            gate_ref[...] = parts[2]
            parts[0] = rope(parts[0])
        for s, t in enumerate(parts):
            p_ref[:, j * TILE_N + s * LANES:j * TILE_N + (s + 1) * LANES] = t.astype(BF16)
    kt = lax.dot_general(wkt_ref[...], hb, NT_DIMS, preferred_element_type=F32)
    kt_ref[...] = (kt * (HEAD_DIM_C ** -0.5)).astype(BF16)


def _inproj(xs, mods, g_mix, w_big, w_kt, cos, sa, sb, n_ctx_tiles):
    x_specs, x_args = _stream_specs(xs, n_ctx_tiles)
    b, d = x_args[0].shape[0], x_args[0].shape[-1]
    l = sum(a.shape[1] for a in x_args)
    npad = w_big.shape[1]
    ck = w_kt.shape[0]
    grid = (b, l // TM)
    tok = lambda bi, ti: (bi, ti, 0)
    return pl.pallas_call(
        functools.partial(_inproj_kernel, n_ctx_tiles=n_ctx_tiles),
        grid=grid,
        in_specs=x_specs + [
                  pl.BlockSpec((None, None, 8, d), lambda bi, ti: (bi, jnp.where(ti >= n_ctx_tiles, 1, 0), 0, 0)),
                  pl.BlockSpec((1, d), lambda bi, ti: (0, 0)),
                  pl.BlockSpec((d, npad), lambda bi, ti: (0, 0), pipeline_mode=pl.Buffered(1)),
                  pl.BlockSpec((ck, d), lambda bi, ti: (0, 0), pipeline_mode=pl.Buffered(1)),
                  pl.BlockSpec((TM, LANES), lambda bi, ti: (ti, 0)),
                  pl.BlockSpec((TM, LANES), lambda bi, ti: (ti, 0)),
                  pl.BlockSpec((TM, LANES), lambda bi, ti: (ti, 0))],
        out_specs=[pl.BlockSpec((None, TM, npad), tok),
                   pl.BlockSpec((None, TM, LANES), tok),
                   pl.BlockSpec((None, ck, TM), lambda bi, ti: (bi, 0, ti))],
        out_shape=[jax.ShapeDtypeStruct((b, l, npad), BF16),
                   jax.ShapeDtypeStruct((b, l, LANES), F32),
                   jax.ShapeDtypeStruct((b, ck, l), BF16)],
        compiler_params=_params(("parallel", "parallel")),
        name="inproj",
    )(*x_args, mods, g_mix, w_big, w_kt, cos, sa, sb)


def _mixa_kernel(sink_ref, q_ref, kp_ref, kc_ref, kn_ref, kx_ref, o_ref, *, n_ctx_blocks, n_blocks):
    i = pl.program_id(1)
    lat = i >= n_ctx_blocks
    has_prev = jnp.logical_and(lat, i > n_ctx_blocks)
    has_next = jnp.logical_and(lat, i < n_blocks - 1)
    r = lax.broadcasted_iota(jnp.int32, (CHUNK, CHUNK), 0)
    c = lax.broadcasted_iota(jnp.int32, (CHUNK, CHUNK), 1)
    n_ctx = kx_ref.shape[0]
    valid = jnp.concatenate([
        jnp.logical_and(c >= r, has_prev),
        jnp.broadcast_to(lat, (CHUNK, CHUNK)),
        jnp.logical_and(c <= r, has_next),
        jnp.ones((CHUNK, n_ctx), jnp.bool_)], axis=1)
    kcat = jnp.concatenate([kp_ref[:, :LANES], kc_ref[:, :LANES], kn_ref[:, :LANES], kx_ref[:, :LANES]], axis=0)
    vcat = jnp.concatenate([kp_ref[:, LANES:], kc_ref[:, LANES:], kn_ref[:, LANES:], kx_ref[:, LANES:]], axis=0)
    lane = lax.broadcasted_iota(jnp.int32, (CHUNK, LANES), 1)
    low = lane < HEAD_DIM
    n_pairs = N_HEADS_A // KV_HEADS_A
    outs = []
    for gk in range(KV_HEADS_A):
        keep = low if gk == 0 else jnp.logical_not(low)
        zero = jnp.zeros((CHUNK, LANES), BF16)
        lhs = jnp.concatenate([jnp.where(keep, q_ref[:, t * LANES:(t + 1) * LANES], zero) for t in range(n_pairs)],
                              axis=0)
        s = lax.dot_general(lhs, kcat, NT_DIMS, preferred_element_type=F32)
        o_g = []
        for t in range(n_pairs):
            st = jnp.where(valid, s[t * CHUNK:(t + 1) * CHUNK], NEG)
            sk = sink_ref[gk * n_pairs + t]
            m = jnp.maximum(jnp.max(st, axis=-1, keepdims=True), sk)
            p = jnp.exp(st - m)
            den = jnp.sum(p, axis=-1, keepdims=True) + jnp.exp(sk - m)
            o_g.append(jnp.dot(p.astype(BF16), vcat, preferred_element_type=F32) / den)
        outs.append(o_g)
    for t in range(n_pairs):
        o_ref[:, t * LANES:(t + 1) * LANES] = jnp.where(low, outs[0][t], outs[1][t]).astype(BF16)


def _mixer_a(p, sink, n_ctx):
    b, l, _ = p.shape
    nb = l // CHUNK
    ncb = n_ctx // CHUNK
    kvw = 2 * LANES
    kv_col = T_KVA * TILE_N // kvw
    aq_w = N_HEADS_A * HEAD_DIM
    kern = functools.partial(_mixa_kernel, n_ctx_blocks=ncb, n_blocks=nb)
    return pl.pallas_call(
        kern,
        grid=(b, nb),
        in_specs=[pl.BlockSpec(memory_space=pltpu.SMEM),
                  pl.BlockSpec((None, CHUNK, aq_w), lambda bi, i: (bi, i, T_AQ)),
                  pl.BlockSpec((None, CHUNK, kvw), lambda bi, i: (bi, jnp.maximum(i - 1, 0), kv_col)),
                  pl.BlockSpec((None, CHUNK, kvw), lambda bi, i: (bi, i, kv_col)),
                  pl.BlockSpec((None, CHUNK, kvw), lambda bi, i: (bi, jnp.minimum(i + 1, nb - 1), kv_col)),
                  pl.BlockSpec((None, n_ctx, kvw), lambda bi, i: (bi, 0, kv_col))],
        out_specs=pl.BlockSpec((None, CHUNK, aq_w), lambda bi, i: (bi, i, 0)),
        out_shape=jax.ShapeDtypeStruct((b, l, aq_w), BF16),
        compiler_params=_params(("parallel", "parallel")),
        name="mixer_a",
    )(sink, p, p, p, p, p)


def _fold_lanes(op, acc, s):
    for t in range(s.shape[1] // LANES):
        acc = op(acc, s[:, t * LANES:(t + 1) * LANES])
    return acc


def _mixb_kernel(lam_ref, gd_ref, k_ref, v_ref, *rest, lam_init, chunks):
    q_refs, (o_ref, s_scr, va_scr) = rest[:-3], rest[-3:]

    @pl.when(pl.program_id(2) == 0)
    def _():
        n_keys = v_ref.shape[0]
        va_scr[:, :LANES] = v_ref[...]
        va_scr[:, LANES:] = (lax.broadcasted_iota(jnp.int32, (n_keys, LANES), 1) == 0).astype(BF16)

    lp = lam_ref[...]
    lam = (jnp.exp(jnp.sum(lp[0:1] * lp[1:2], axis=-1, keepdims=True))
           - jnp.exp(jnp.sum(lp[2:3] * lp[3:4], axis=-1, keepdims=True)) + lam_init)
    q = jnp.concatenate([qr[...] for qr in q_refs], axis=0)
    tq = q.shape[0]
    lane = lax.broadcasted_iota(jnp.int32, (tq, LANES), 1)
    zero = jnp.zeros_like(q)
    qs = (jnp.where(lane < HEAD_DIM, q, zero), jnp.where(lane >= HEAD_DIM, q, zero))
    rows = [slice(mi * tq, (mi + 1) * tq) for mi in range(2)]
    mrun = [jnp.full((tq, LANES), NEG, F32) for _ in range(2)]
    for off, sz in chunks:
        for mi in range(2):
            s_scr[rows[mi], off:off + sz] = lax.dot_general(qs[mi], k_ref[off:off + sz, :], NT_DIMS,
                                                            preferred_element_type=F32)
            mrun[mi] = _fold_lanes(jnp.maximum, mrun[mi], s_scr[rows[mi], off:off + sz])
    m = [jnp.max(mr, axis=-1, keepdims=True) for mr in mrun]
    acc = [jnp.zeros((tq, 2 * LANES), F32) for _ in range(2)]
    for off, sz in chunks:
        for mi in range(2):
            pr = jnp.exp2(s_scr[rows[mi], off:off + sz] - m[mi])
            acc[mi] = acc[mi] + jnp.dot(pr.astype(BF16), va_scr[off:off + sz, :], preferred_element_type=F32)
    outs = [a[:, :LANES] / a[:, LANES:LANES + 1] for a in acc]
    o = outs[0] - lam * outs[1]
    o_ref[...] = (_rms(o, gd_ref[...]) * (1.0 - lam_init)).astype(BF16)


def _mixer_b(p, lam_params, g_diff, lam_init, n_ctx):
    b, l, _ = p.shape
    kl = min(MIXB_KEYS, l - n_ctx)
    tq_lat = min(MIXB_QUERIES, l - n_ctx)
    assert (l - n_ctx) % kl == 0 and (l - n_ctx) % tq_lat == 0 and tq_lat % TM == 0 and n_ctx % TM == 0
    q0 = T_BQ * TILE_N // LANES
    k0 = T_BK * TILE_N // LANES
    v0 = T_BV * TILE_N // LANES
    ctx_chunks = ((0, n_ctx),)
    all_chunks = ctx_chunks + tuple((n_ctx + c * kl, kl) for c in range((l - n_ctx) // kl))

    def call(chunks, n_keys, tq, q_tiles, first_row):
        nq = tq // TM
        q_spec = lambda part: pl.BlockSpec((None, TM, LANES),
                                           lambda bi, h, qi: (bi, first_row // TM + qi * nq + part, q0 + h))
        kern = functools.partial(_mixb_kernel, lam_init=lam_init, chunks=chunks)
        return pl.pallas_call(
            kern,
            grid=(b, N_HEADS_B, q_tiles),
            in_specs=[pl.BlockSpec((4, HEAD_DIM), lambda bi, h, qi: (0, 0)),
                      pl.BlockSpec((1, LANES), lambda bi, h, qi: (0, 0)),
                      pl.BlockSpec((None, n_keys, LANES), lambda bi, h, qi: (bi, 0, k0 + h)),
                      pl.BlockSpec((None, n_keys, LANES), lambda bi, h, qi: (bi, 0, v0 + h))]
                     + [q_spec(part) for part in range(nq)],
            out_specs=pl.BlockSpec((None, tq, LANES), lambda bi, h, qi: (bi, qi, h)),
            out_shape=jax.ShapeDtypeStruct((b, q_tiles * tq, N_HEADS_B * LANES), BF16),
            scratch_shapes=[pltpu.VMEM((2 * tq, n_keys), F32), pltpu.VMEM((n_keys, 2 * LANES), BF16)],
            compiler_params=_params(("parallel", "parallel", "arbitrary")),
            name="mixer_b",
        )(lam_params, g_diff, p, p, *([p] * nq))

    tq_ctx = min(tq_lat, n_ctx)
    return (call(ctx_chunks, n_ctx, tq_ctx, n_ctx // tq_ctx, 0),
            call(all_chunks, l, tq_lat, (l - n_ctx) // tq_lat, n_ctx))


def _log_sigmoid(x):
    return jnp.minimum(x, 0.0) - jnp.log1p(jnp.exp(-jnp.abs(x)))


def _mlstm_kernel(qf_ref, ktf_ref, vf_ref, gcf_ref, grf_ref, qb_ref, ktb_ref, vb_ref, gcb_ref, grb_ref,
                  bc_ref, br_ref, of_ref, ob_ref, s_scr, m_scr):
    @pl.when(pl.program_id(1) == 0)
    def _():
        s_scr[...] = jnp.zeros_like(s_scr)
        m_scr[...] = jnp.zeros_like(m_scr)

    r = lax.broadcasted_iota(jnp.int32, (CHUNK, CHUNK), 0)
    cc = lax.broadcasted_iota(jnp.int32, (CHUNK, CHUNK), 1)
    lane = lax.broadcasted_iota(jnp.int32, (CHUNK, LANES), 1)
    ones_col = (lane == 0).astype(BF16)
    nh = N_HEADS_C
    tris = (r >= cc, r <= cc)
    refs = ((qf_ref, ktf_ref, vf_ref, gcf_ref, grf_ref, of_ref), (qb_ref, ktb_ref, vb_ref, gcb_ref, grb_ref, ob_ref))
    grow, bcum_col, bcum_row, tot_row = [], [], [], []
    for di in range(2):
        trif = tris[di].astype(F32)
        gcol = refs[di][3][...] + bc_ref[...]
        grow.append(refs[di][4][...] + br_ref[...])
        lf_col = _log_sigmoid(gcol)
        lf_row = _log_sigmoid(grow[di])
        bcum_col.append(jnp.dot(trif, lf_col, precision=HIGHEST, preferred_element_type=F32))
        bcum_row.append(lax.dot_general(lf_row, trif, NT_DIMS, precision=HIGHEST, preferred_element_type=F32))
        tot_row.append(jnp.sum(lf_row, axis=-1, keepdims=True))
    ch = [(di, h) for di in range(2) for h in range(nh)]
    gi = {c: (2 * c[0]) * nh + c[1] for c in ch}
    gf = {c: (2 * c[0] + 1) * nh + c[1] for c in ch}
    ic_row = {c: grow[c[0]][gi[c]:gi[c] + 1, :] for c in ch}
    b_col = {c: bcum_col[c[0]][:, gf[c]:gf[c] + 1] for c in ch}
    b_row = {c: bcum_row[c[0]][gf[c]:gf[c] + 1, :] for c in ch}
    total = {c: tot_row[c[0]][gf[c]:gf[c] + 1, :] for c in ch}
    m_st = {c: m_scr[c[0], c[1], 0:1, 0:1] for c in ch}
    qh = {c: refs[c[0]][0][:, c[1] * LANES:(c[1] + 1) * LANES] for c in ch}
    kth = {c: refs[c[0]][1][c[1] * LANES:(c[1] + 1) * LANES, :] for c in ch}
    vaug = {c: jnp.concatenate([refs[c[0]][2][:, c[1] * LANES:(c[1] + 1) * LANES], ones_col], axis=1) for c in ch}
    st = {c: s_scr[c[0], c[1]] for c in ch}
    qk = {c: jnp.dot(qh[c], kth[c], preferred_element_type=F32) for c in ch}
    cross = {c: jnp.dot(qh[c], st[c].astype(BF16), preferred_element_type=F32) for c in ch}
    gs_row = {c: total[c] - b_row[c] + ic_row[c] for c in ch}
    m_new = {c: jnp.maximum(total[c] + m_st[c], jnp.max(gs_row[c], axis=-1, keepdims=True)) for c in ch}
    wkt = {c: (kth[c].astype(F32) * jnp.exp(gs_row[c] - m_new[c])).astype(BF16) for c in ch}
    upd = {c: jnp.dot(wkt[c], vaug[c], preferred_element_type=F32) for c in ch}
    dm = {c: jnp.where(tris[c[0]], b_col[c] - b_row[c] + ic_row[c], NEG) for c in ch}
    inter = {c: b_col[c] + m_st[c] for c in ch}
    m_t = {c: jnp.maximum(inter[c], jnp.max(dm[c], axis=-1, keepdims=True)) for c in ch}
    sc = {c: (qk[c] * jnp.exp(dm[c] - m_t[c])).astype(BF16) for c in ch}
    intra = {c: jnp.dot(sc[c], vaug[c], preferred_element_type=F32) for c in ch}
    for c in ch:
        di, h = c
        nd = intra[c] + jnp.exp(inter[c] - m_t[c]) * cross[c]
        den = nd[:, LANES:LANES + 1]
        refs[di][5][:, h * LANES:(h + 1) * LANES] = nd[:, :LANES] / jnp.maximum(jnp.abs(den), jnp.exp(-m_t[c]))
        s_scr[di, h] = jnp.exp(total[c] + m_st[c] - m_new[c]) * st[c] + upd[c]
        m_scr[di, h] = jnp.broadcast_to(m_new[c], m_scr.shape[2:])


def _mlstm(p, kt, gates, gates_t, bias_row, bias_col, n_ctx):
    b, l, _ = p.shape
    nc = l // CHUNK
    ncc = n_ctx // CHUNK
    cw = N_HEADS_C * HEAD_DIM_C

    def rev(c):
        return jnp.where(c < ncc, ncc - 1 - c, nc + ncc - 1 - c)

    def specs(ch):
        return [pl.BlockSpec((None, CHUNK, cw), lambda bi, c: (bi, ch(c), T_CQ)),
                pl.BlockSpec((None, cw, CHUNK), lambda bi, c: (bi, 0, ch(c))),
                pl.BlockSpec((None, CHUNK, cw), lambda bi, c: (bi, ch(c), T_CV)),
                pl.BlockSpec((None, CHUNK, LANES), lambda bi, c: (bi, ch(c), 0)),
                pl.BlockSpec((None, 16, CHUNK), lambda bi, c: (bi, 0, ch(c)))]

    fwd = lambda c: c
    out = jax.ShapeDtypeStruct((b, l, cw), F32)
    return pl.pallas_call(
        _mlstm_kernel,
        grid=(b, nc),
        in_specs=specs(fwd) + specs(rev) + [pl.BlockSpec((1, LANES), lambda bi, c: (0, 0)),
                                            pl.BlockSpec((16, LANES), lambda bi, c: (0, 0))],
        out_specs=[pl.BlockSpec((None, CHUNK, cw), lambda bi, c: (bi, c, 0)),
                   pl.BlockSpec((None, CHUNK, cw), lambda bi, c: (bi, rev(c), 0))],
        out_shape=[out, out],
        scratch_shapes=[pltpu.VMEM((2, N_HEADS_C, HEAD_DIM_C, 2 * LANES), F32),
                        pltpu.VMEM((2, N_HEADS_C, 8, LANES), F32)],
        compiler_params=_params(("parallel", "arbitrary")),
        name="mlstm",
    )(p, kt, p, gates, gates_t, p, kt, p, gates, gates_t, bias_row, bias_col)


def _merge_kernel(*refs, n_ctx_tiles, t0):
    x_refs, (mod_ref, oa_ref, obc_ref, obl_ref, hf_ref, hb_ref, co_ref, gt_ref, gm_ref,
             wa_ref, wb_ref, wc_ref, wo_ref, xo_ref) = refs[:-14], refs[-14:]
    d = xo_ref.shape[-1]
    is_ctx = pl.program_id(1) + t0 < n_ctx_tiles
    ob = jnp.where(is_ctx, obc_ref[...], obl_ref[...])
    hs = hf_ref[...] + hb_ref[...]
    co = co_ref[...].astype(F32)
    gm = gm_ref[...]
    oc = []
    for h in range(N_HEADS_C):
        sl = slice(h * LANES, (h + 1) * LANES)
        oc.append((_rms(hs[:, sl], gm[:, sl]) * _sigmoid(co[:, sl])).astype(BF16))
    oc = jnp.concatenate(oc, axis=1)
    y = (_sigmoid(gt_ref[:, 0:d].astype(F32)) * jnp.dot(oa_ref[...], wa_ref[...], preferred_element_type=F32)
         + _sigmoid(gt_ref[:, d:2 * d].astype(F32)) * jnp.dot(ob, wb_ref[...], preferred_element_type=F32)
         + _sigmoid(gt_ref[:, 2 * d:3 * d].astype(F32)) * jnp.dot(oc, wc_ref[...], preferred_element_type=F32))
    out = jnp.dot(y.astype(BF16), wo_ref[...], preferred_element_type=F32)
    xo_ref[...] = _read_stream(x_refs, is_ctx) + mod_ref[2:3, :] * out


def _merge(xs, mods, oa, ob_ctx, ob_lat, hf, hb, p, g_mlstm, wa, wb, wc, wo, n_ctx_tiles, latent_only):
    t0 = n_ctx_tiles if latent_only else 0
    x_specs, x_args = _stream_specs(xs, n_ctx_tiles, t0)
    b, d = x_args[0].shape[0], x_args[0].shape[-1]
    n_tiles = oa.shape[1] // TM - t0
    tok = lambda bi, ti: (bi, ti + t0, 0)
    cw = N_HEADS_C * HEAD_DIM_C
    const = lambda bi, ti: (0, 0)
    return pl.pallas_call(
        functools.partial(_merge_kernel, n_ctx_tiles=n_ctx_tiles, t0=t0),
        grid=(b, n_tiles),
        in_specs=x_specs + [
                  pl.BlockSpec((None, None, 8, d), lambda bi, ti: (bi, jnp.where(ti + t0 >= n_ctx_tiles, 1, 0), 0, 0)),
                  pl.BlockSpec((None, TM, oa.shape[-1]), tok),
                  pl.BlockSpec((None, TM, ob_ctx.shape[-1]),
                               lambda bi, ti: (bi, jnp.minimum(ti + t0, n_ctx_tiles - 1), 0)),
                  pl.BlockSpec((None, TM, ob_lat.shape[-1]),
                               lambda bi, ti: (bi, jnp.maximum(ti + t0 - n_ctx_tiles, 0), 0)),
                  pl.BlockSpec((None, TM, cw), tok),
                  pl.BlockSpec((None, TM, cw), tok),
                  pl.BlockSpec((None, TM, cw), lambda bi, ti: (bi, ti + t0, T_CO)),
                  pl.BlockSpec((None, TM, 3 * d), lambda bi, ti: (bi, ti + t0, T_GT * TILE_N // (3 * d))),
                  pl.BlockSpec((1, cw), const),
                  pl.BlockSpec(wa.shape, const), pl.BlockSpec(wb.shape, const),
                  pl.BlockSpec(wc.shape, const), pl.BlockSpec(wo.shape, const)],
        out_specs=pl.BlockSpec((None, TM, d), lambda bi, ti: (bi, ti, 0)),
        out_shape=jax.ShapeDtypeStruct((b, n_tiles * TM, d), F32),
        compiler_params=_params(("parallel", "parallel")),
        name="merge",
    )(*x_args, mods, oa, ob_ctx, ob_lat, hf, hb, p, p, g_mlstm, wa, wb, wc, wo)


def _router_kernel(x_ref, mod_ref, g_ref, wrt_ref, br_ref, h_ref, idx_ref, wt_ref, cnt_ref):
    h = _rms(x_ref[...], g_ref[...]) * (1.0 + mod_ref[4:5, :]) + mod_ref[3:4, :]
    h_ref[...] = h.astype(BF16)
    tm = h.shape[0]
    per = N_EXPERTS // N_GROUPS
    lt = lax.dot_general(wrt_ref[...], h, NT_DIMS, precision=HIGHEST, preferred_element_type=F32)
    s = _sigmoid(lt)
    sel = s + br_ref[...]
    ninf = -jnp.inf
    sel3 = sel.reshape(N_GROUPS, per, tm)
    eidx = lax.broadcasted_iota(jnp.int32, (N_GROUPS, per, tm), 1)
    m1 = jnp.max(sel3, axis=1, keepdims=True)
    first = jnp.min(jnp.where(sel3 == m1, eidx, per), axis=1, keepdims=True)
    m2 = jnp.max(jnp.where(eidx == first, ninf, sel3), axis=1, keepdims=True)
    gscore = (m1 + m2).reshape(N_GROUPS, tm)
    gidx = lax.broadcasted_iota(jnp.int32, (N_GROUPS, tm), 0)
    gmask = jnp.zeros((N_GROUPS, tm), jnp.bool_)
    cur = gscore
    for _ in range(TOPK_GROUPS):
        mx = jnp.max(cur, axis=0, keepdims=True)
        hit = gidx == jnp.min(jnp.where(cur == mx, gidx, N_GROUPS), axis=0, keepdims=True)
        gmask = jnp.logical_or(gmask, hit)
        cur = jnp.where(hit, ninf, cur)
    cur = jnp.where(gmask.reshape(N_GROUPS, 1, tm), sel3, ninf).reshape(N_EXPERTS, tm)
    eid = lax.broadcasted_iota(jnp.int32, (N_EXPERTS, tm), 0)
    ids, ws = [], []
    chosen = jnp.zeros((N_EXPERTS, tm), F32)
    for _ in range(TOP_K):
        mx = jnp.max(cur, axis=0, keepdims=True)
        pick = jnp.min(jnp.where(cur == mx, eid, N_EXPERTS), axis=0, keepdims=True)
        hit = eid == pick
        ids.append(pick)
        ws.append(jnp.sum(jnp.where(hit, s, 0.0), axis=0, keepdims=True))
        cur = jnp.where(hit, ninf, cur)
        chosen = chosen + hit.astype(F32)
    wsum = ws[0]
    for w in ws[1:]:
        wsum = wsum + w
    idx_ref[...] = jnp.concatenate(ids, axis=0)
    wt_ref[...] = jnp.concatenate([w / wsum * ROUTED_SCALE for w in ws], axis=0)
    cnt_ref[...] = jnp.sum(chosen, axis=1, keepdims=True).astype(jnp.int32)


def _router(xs, mods, g_ffn, w_router_t, b_router, n_ctx_tiles):
    b, l, d = xs.shape
    tok = lambda bi, ti: (bi, ti, 0)
    const = lambda bi, ti: (0, 0)
    return pl.pallas_call(
        _router_kernel,
        grid=(b, l // TM),
        in_specs=[pl.BlockSpec((None, TM, d), tok),
                  pl.BlockSpec((None, None, 8, d), lambda bi, ti: (bi, jnp.where(ti >= n_ctx_tiles, 1, 0), 0, 0)),
                  pl.BlockSpec((1, d), const),
                  pl.BlockSpec((N_EXPERTS, d), const),
                  pl.BlockSpec((N_EXPERTS, 1), const)],
        out_specs=[pl.BlockSpec((None, TM, d), tok),
                   pl.BlockSpec((None, TOP_K, TM), lambda bi, ti: (bi, 0, ti)),
                   pl.BlockSpec((None, TOP_K, TM), lambda bi, ti: (bi, 0, ti)),
                   pl.BlockSpec((None, None, N_EXPERTS, 1), lambda bi, ti: (bi, ti, 0, 0))],
        out_shape=[jax.ShapeDtypeStruct((b, l, d), BF16),
                   jax.ShapeDtypeStruct((b, TOP_K, l), jnp.int32),
                   jax.ShapeDtypeStruct((b, TOP_K, l), F32),
                   jax.ShapeDtypeStruct((b, l // TM, N_EXPERTS, 1), jnp.int32)],
        compiler_params=_params(("parallel", "parallel")),
        name="router",
    )(xs, mods, g_ffn, w_router_t, b_router)


def _sort_kernel(idx_ref, off_ref, h_ref, posl_ref, ts_ref):
    tm = h_ref.shape[0]
    idx = idx_ref[...]
    eid = lax.broadcasted_iota(jnp.int32, (N_EXPERTS, tm), 0)
    hits = [eid == idx[k:k + 1, :] for k in range(TOP_K)]
    chosen = hits[0].astype(BF16)
    for hk in hits[1:]:
        chosen = chosen + hk.astype(BF16)
    r = lax.broadcasted_iota(jnp.int32, (tm, tm), 0)
    c = lax.broadcasted_iota(jnp.int32, (tm, tm), 1)
    before = (r < c).astype(BF16)
    rank = jnp.dot(chosen, before, preferred_element_type=F32)
    slot = rank.astype(jnp.int32) + off_ref[...]
    posl = jnp.concatenate([jnp.sum(jnp.where(hk, slot, 0), axis=0, keepdims=True) for hk in hits], axis=0)
    posl_ref[...] = posl
    hb = h_ref[...]
    for rb in range(ts_ref.shape[0] // SORT_CHUNK):
        rows = lax.broadcasted_iota(jnp.int32, (SORT_CHUNK, tm), 0) + rb * SORT_CHUNK
        sel = rows == posl[0:1, :]
        for k in range(1, TOP_K):
            sel = jnp.logical_or(sel, rows == posl[k:k + 1, :])
        onehot = jnp.where(sel, 1.0, 0.0).astype(BF16)
        ts = jnp.dot(onehot, hb, preferred_element_type=F32)
        ts_ref[rb * SORT_CHUNK:(rb + 1) * SORT_CHUNK, :] = ts.astype(BF16)


def _sort_rows(idx_t, off, h_flat):
    b, k, l = idx_t.shape
    n, d = h_flat.shape
    nt = l // TM
    return pl.pallas_call(
        _sort_kernel,
        grid=(b * nt,),
        in_specs=[pl.BlockSpec((None, k, TM), lambda i: (i // nt, 0, i % nt)),
                  pl.BlockSpec((None, N_EXPERTS, 1), lambda i: (i, 0, 0)),
                  pl.BlockSpec((TM, d), lambda i: (i, 0))],
        out_specs=[pl.BlockSpec((None, k, TM), lambda i: (i // nt, 0, i % nt)),
                   pl.BlockSpec((None, SORT_ROWS, d), lambda i: (i, 0, 0))],
        out_shape=[jax.ShapeDtypeStruct((b, k, l), jnp.int32),
                   jax.ShapeDtypeStruct((b * nt, SORT_ROWS, d), BF16)],
        compiler_params=_params(("parallel",)),
        name="sort_rows",
    )(idx_t, off, h_flat)


SLAB_SIZES = tuple(1 << s for s in range((EXPERT_ROWS // ROW_GROUP).bit_length()))


def _expert_kernel(be_ref, nu_ref, cov_ref, grp_ref, ts_in, wg_ref, wu_ref, wd_ref, ts_out, xbuf, ybuf, sem_g, sem_s):
    j = pl.program_id(0)
    nu = nu_ref[0]
    n_grp = xbuf.shape[1]
    blk = n_grp * ROW_GROUP
    slot = j % 2

    def for_groups(bj, fn):
        for i in range(n_grp):
            sg = grp_ref[bj * n_grp + i]

            @pl.when(sg >= 0)
            def _():
                fn(sg, i)

    def gather(bj, s):
        xbuf[s] = jnp.zeros(xbuf.shape[1:], xbuf.dtype)
        for_groups(bj, lambda src, i: pltpu.make_async_copy(ts_in.at[src], xbuf.at[s, i], sem_g.at[s]).start())

    def scatter(bj, s):
        for_groups(bj, lambda dst, i: pltpu.make_async_copy(ybuf.at[s, i], ts_out.at[dst], sem_s.at[s]).start())

    def wait_rows(bj, buf, sem, s):
        groups = cov_ref[bj] // ROW_GROUP
        for sz in SLAB_SIZES:
            @pl.when((groups & sz) != 0)
            def _():
                pltpu.make_async_copy(ts_in.at[pl.ds(0, sz)], buf.at[s, pl.ds(0, sz)], sem.at[s]).wait()

    @pl.when(j == 0)
    def _():
        gather(0, 0)

    @pl.when(j + 1 < nu)
    def _():
        gather(j + 1, 1 - slot)

    @pl.when(jnp.logical_and(j >= 2, j - 2 < nu))
    def _():
        wait_rows(j - 2, ybuf, sem_s, slot)

    @pl.when(j < nu)
    def _():
        wait_rows(j, xbuf, sem_g, slot)
        x = xbuf[slot].reshape(blk, xbuf.shape[-1])
        g = jnp.dot(x, wg_ref[...], preferred_element_type=F32)
        u = jnp.dot(x, wu_ref[...], preferred_element_type=F32)
        a = (g * _sigmoid(g) * u).astype(BF16)
        y = jnp.dot(a, wd_ref[...], preferred_element_type=F32)
        ybuf[slot] = y.astype(BF16).reshape(ybuf.shape[1:])
        scatter(j, slot)


def _experts(plan, tiles, wg, wu, wd, layer):
    nt, rows, w = tiles.shape
    blk = EXPERT_ROWS
    d, de = wg.shape[2:]
    n_blocks = plan["block_e"].shape[0]
    tables = (plan["block_e"], plan["n_used"], plan["blk_rows"], plan["blk_groups"])
    wspec = lambda shape: pl.BlockSpec((None, None) + shape, lambda i, be, *_: (layer, be[i], 0, 0))
    grid_spec = pltpu.PrefetchScalarGridSpec(
        num_scalar_prefetch=len(tables),
        grid=(n_blocks,),
        in_specs=[pl.BlockSpec(memory_space=pl.ANY), wspec((d, de)), wspec((d, de)), wspec((de, d))],
        out_specs=pl.BlockSpec(memory_space=pl.ANY),
        scratch_shapes=[pltpu.VMEM((2, blk // ROW_GROUP, ROW_GROUP, w), tiles.dtype)] * 2 + [
                        pltpu.SemaphoreType.DMA((2,)), pltpu.SemaphoreType.DMA((2,))])
    return pl.pallas_call(
        _expert_kernel,
        grid_spec=grid_spec,
        out_shape=jax.ShapeDtypeStruct((nt * rows // ROW_GROUP, ROW_GROUP, w), tiles.dtype),
        input_output_aliases={len(tables): 0},
        compiler_params=_params(("arbitrary",)),
        name="experts",
    )(*tables, tiles.reshape(nt * rows // ROW_GROUP, ROW_GROUP, w), wg, wu, wd).reshape(tiles.shape)


def _combine_kernel(ts_ref, posl_ref, w_ref, x_ref, h_ref, mod_ref, wsg_ref, wsu_ref, wsd_ref, *rest):
    tm = x_ref.shape[0]
    hb = h_ref[...]
    g = jnp.dot(hb, wsg_ref[...], preferred_element_type=F32)
    u = jnp.dot(hb, wsu_ref[...], preferred_element_type=F32)
    acc = jnp.dot((g * _sigmoid(g) * u).astype(BF16), wsd_ref[...], preferred_element_type=F32)
    posl = posl_ref[...]
    w = w_ref[...]
    for rb in range(ts_ref.shape[0] // COMBINE_CHUNK):
        cols = lax.broadcasted_iota(jnp.int32, (tm, COMBINE_CHUNK), 1) + rb * COMBINE_CHUNK
        wm = jnp.zeros((tm, COMBINE_CHUNK), F32)
        for k in range(TOP_K):
            wm = jnp.where(cols == posl[:, k:k + 1], w[:, k:k + 1], wm)
        acc = acc + jnp.dot(wm.astype(BF16), ts_ref[rb * COMBINE_CHUNK:(rb + 1) * COMBINE_CHUNK, :],
                            preferred_element_type=F32)
    out = x_ref[...] + mod_ref[5:6, :] * acc
    rest[-1][...] = _rms(out, rest[0][...]) if len(rest) == 2 else out


def _combine(tiles, posl_tm, wts, x_flat, h_flat, mods, wsg, wsu, wsd, tiles_per_sample, n_ctx_tiles, g_final=None):
    n, d = x_flat.shape
    ds_ = wsg.shape[-1]
    tok = lambda i: (i, 0)
    const = lambda i: (0, 0)

    def mod_idx(i):
        return (i // tiles_per_sample, jnp.where(i % tiles_per_sample >= n_ctx_tiles, 1, 0), 0, 0)

    in_extra, args_extra, out_rows, out_idx = [], [], n, tok
    if g_final is not None:
        lat_tiles = tiles_per_sample - n_ctx_tiles
        in_extra, args_extra = [pl.BlockSpec((1, d), const)], [g_final]
        out_rows = n // TM // tiles_per_sample * lat_tiles * TM
        out_idx = lambda i: (i // tiles_per_sample * lat_tiles + jnp.maximum(i % tiles_per_sample - n_ctx_tiles, 0), 0)
    return pl.pallas_call(
        _combine_kernel,
        grid=(n // TM,),
        in_specs=[pl.BlockSpec((None,) + tiles.shape[1:], lambda i: (i, 0, 0)),
                  pl.BlockSpec((TM, TOP_K), tok),
                  pl.BlockSpec((TM, TOP_K), tok),
                  pl.BlockSpec((TM, d), tok),
                  pl.BlockSpec((TM, d), tok),
                  pl.BlockSpec((None, None, 8, d), mod_idx),
                  pl.BlockSpec((d, ds_), const), pl.BlockSpec((d, ds_), const), pl.BlockSpec((ds_, d), const)]
                 + in_extra,
        out_specs=pl.BlockSpec((TM, d), out_idx),
        out_shape=jax.ShapeDtypeStruct((out_rows, d), F32),
        compiler_params=_params(("arbitrary",)),
        name="combine",
    )(tiles, posl_tm, wts, x_flat, h_flat, mods, wsg, wsu, wsd, *args_extra)


def _moe_plan(cnt, n_assign, blk):
    nt = cnt.shape[0]
    run = (cnt + ROW_GROUP - 1) // ROW_GROUP * ROW_GROUP
    tile_off = jnp.cumsum(run, axis=1) - run
    tot = jnp.sum(run, axis=0)
    padded = (tot + blk - 1) // blk * blk
    pad_end = jnp.cumsum(padded)
    pad_start = pad_end - padded
    n_blocks = -(-(n_assign + nt * N_EXPERTS * (ROW_GROUP - 1)) // blk) + N_EXPERTS + 2
    first_row = jnp.arange(n_blocks, dtype=jnp.int32) * blk
    count = lambda m: jnp.sum(m.astype(jnp.int32), axis=1)
    block_e = jnp.minimum(count(pad_end[None, :] <= first_row[:, None]), N_EXPERTS - 1)
    i32 = lambda a: a.astype(jnp.int32)
    run_end = jnp.cumsum(run, axis=0).T[block_e][:, None, :]
    run_start = run_end - run.T[block_e][:, None, :]
    run_src = (jnp.arange(nt) * SORT_ROWS)[None, :] + tile_off.T[block_e]
    local = (first_row - pad_start[block_e])[:, None] + jnp.arange(0, blk, ROW_GROUP)[None, :]
    hit = jnp.logical_and(run_start <= local[:, :, None], local[:, :, None] < run_end)
    src_row = jnp.sum(jnp.where(hit, run_src[:, None, :] + local[:, :, None] - run_start, 0), axis=2)
    blk_groups = jnp.where(jnp.any(hit, axis=2), src_row // ROW_GROUP, -1).reshape(-1)
    return dict(
        tile_off=i32(tile_off)[:, :, None],
        blk_groups=i32(blk_groups),
        blk_rows=i32(jnp.clip((pad_start + tot)[block_e] - first_row, 0, blk)),
        block_e=i32(block_e), n_used=i32(pad_end[-1] // blk).reshape(1))


def _rope_tables(s_len, n_ctx):
    rows = s_len // GRID_W
    row = jnp.repeat(jnp.arange(rows), GRID_W).astype(F32)
    col = jnp.tile(jnp.arange(GRID_W), rows).astype(F32)
    quarter = HEAD_DIM // 4
    inv = 1.0 / (ROPE_BASE ** (jnp.arange(quarter, dtype=F32) / quarter))
    ar, ac = row[:, None] * inv, col[:, None] * inv
    cr, sr, cc, sc = jnp.cos(ar), jnp.sin(ar), jnp.cos(ac), jnp.sin(ac)
    z = jnp.zeros_like(sr)
    cos = jnp.concatenate([cr, cr, cc, cc], axis=1)
    sa = jnp.concatenate([z, sr, z, sc], axis=1)
    sb = jnp.concatenate([-sr, z, -sc, z], axis=1)
    rep = LANES // HEAD_DIM

    def full(t, fill):
        t = jnp.tile(t, (1, rep))
        return jnp.concatenate([jnp.full((n_ctx, LANES), fill, F32), t], axis=0)

    return full(cos, 1.0), full(sa, 0.0), full(sb, 0.0)


def _pair_perm():
    g = N_HEADS_A // KV_HEADS_A
    heads = [h for t in range(g) for h in (t, t + g)]
    return jnp.concatenate([jnp.arange(h * HEAD_DIM, (h + 1) * HEAD_DIM) for h in heads])


def _split_w_in(w):
    a_q, a_kv = N_HEADS_A * HEAD_DIM, KV_HEADS_A * HEAD_DIM
    b_w = N_HEADS_B * 2 * HEAD_DIM
    c_w = N_HEADS_C * HEAD_DIM_C
    sizes = (a_q, a_kv, a_kv, b_w, b_w, b_w, c_w, c_w, c_w, c_w, 4 * N_HEADS_C, w.shape[1])
    parts, start = [], 0
    for sz in sizes[:-1]:
        parts.append(w[:, start:start + sz])
        start += sz
    parts.append(w[:, start:])
    return parts


def _pack_w_in(w):
    d = w.shape[0]
    aq, ak, av, bq, bk, bv, cq, ck, cv, co, cg, gt = _split_w_in(w)
    pad = lambda n: jnp.zeros((d, n), w.dtype)
    kva = jnp.concatenate([ak, av, cg, pad(TILE_N - ak.shape[1] - av.shape[1] - cg.shape[1])], axis=1)
    big = jnp.concatenate([aq[:, _pair_perm()], bq, bk, kva, bv, co, gt, cq, cv], axis=1)
    return big.astype(BF16), ck.T.astype(BF16)


def kernel(x, c, ctx, c_ctx, w_mod, b_mod, g_mix, g_ffn, w_in, b_gate, sink, lam_q1, lam_k1, lam_q2, lam_k2,
           g_diff, g_mlstm, w_a, w_b, w_c, w_out, w_router, b_router, w_exp_gate, w_exp_up, w_exp_down,
           w_sh_gate, w_sh_up, w_sh_down, g_final):
    b, s_len, d = x.shape
    n_ctx = ctx.shape[1]
    l = n_ctx + s_len
    depth = w_mod.shape[0]
    n_ctx_tiles = n_ctx // TM
    assert n_ctx % TM == 0 and s_len % TM == 0 and d % LANES == 0 and s_len % GRID_W == 0

    xs = (ctx, x)
    cos, sa, sb = _rope_tables(s_len, n_ctx)

    rows_c = 16
    cs = jnp.concatenate([c, c_ctx[None], jnp.zeros((rows_c - b - 1, d), F32)], axis=0)
    mod_all = _mod_vectors(cs, w_mod, b_mod).reshape(depth, rows_c, N_MOD, d)
    mod_all = jnp.pad(mod_all, ((0, 0), (0, 0), (0, 8 - N_MOD), (0, 0)))

    perm = _pair_perm()
    expert_w = [w.astype(BF16) for w in (w_exp_gate, w_exp_up, w_exp_down)]
    for layer in range(depth):
        lam_init = 0.8 - 0.6 * math.exp(-0.3 * layer)
        mods = jnp.stack([jnp.broadcast_to(mod_all[layer, b], (b, 8, d)), mod_all[layer, :b]], axis=1)
        w_big, w_kt = _pack_w_in(w_in[layer])
        p, gates, kt = _inproj(xs, mods, g_mix[layer][None], w_big, w_kt, cos, sa, sb, n_ctx_tiles)

        oa = _mixer_a(p, sink[layer], n_ctx)
        lam_params = jnp.stack([lam_q1[layer], lam_k1[layer], lam_q2[layer], lam_k2[layer]])
        ob_ctx, ob_lat = _mixer_b(p, lam_params, g_diff[layer][None], lam_init, n_ctx)

        bias = b_gate[layer].reshape(-1)
        bias_row = jnp.pad(bias, (0, LANES - bias.shape[0]))[None]
        bias_col = jnp.broadcast_to(bias[:, None], (bias.shape[0], LANES))
        gates_t = jnp.transpose(gates[:, :, :bias.shape[0]], (0, 2, 1))
        hf, hb = _mlstm(p, kt, gates, gates_t, bias_row, bias_col, n_ctx)

        last = layer == depth - 1
        xs = _merge(xs, mods, oa, ob_ctx, ob_lat, hf, hb, p, g_mlstm[layer][None],
                    w_a[layer][perm].astype(BF16), w_b[layer].astype(BF16), w_c[layer].astype(BF16),
                    w_out[layer].astype(BF16), n_ctx_tiles, last)

        lf, ctx_tiles = (s_len, 0) if last else (l, n_ctx_tiles)
        h, idx_t, wt_t, cnt = _router(xs, mods, g_ffn[layer][None], w_router[layer].T, b_router[layer][:, None],
                                      ctx_tiles)
        plan = _moe_plan(cnt.reshape(-1, N_EXPERTS), b * lf * TOP_K, EXPERT_ROWS)
        h_flat = h.reshape(b * lf, d)
        posl, tiles = _sort_rows(idx_t, plan["tile_off"], h_flat)
        tiles = _experts(plan, tiles, *expert_w, layer)
        to_rows = lambda a: jnp.transpose(a, (0, 2, 1)).reshape(b * lf, TOP_K)
        xs = _combine(tiles, to_rows(posl), to_rows(wt_t), xs.reshape(b * lf, d), h_flat, mods,
                      w_sh_gate[layer].astype(BF16), w_sh_up[layer].astype(BF16), w_sh_down[layer].astype(BF16),
                      lf // TM, ctx_tiles, g_final[None] if last else None).reshape(b, lf, d)
    return xs
```

```python
import functools
import math

import jax
import jax.numpy as jnp
from jax import lax
from jax.experimental import pallas as pl
from jax.experimental.pallas import tpu as pltpu

F32 = jnp.float32
BF16 = jnp.bfloat16
HIGHEST = lax.Precision.HIGHEST

GRID_W = 64
N_MOD = 6
HEAD_DIM = 64
N_HEADS_A = 8
KV_HEADS_A = 2
WINDOW = 128
N_HEADS_B = 4
N_HEADS_C = 4
HEAD_DIM_C = 128
N_EXPERTS = 64
N_GROUPS = 8
TOPK_GROUPS = 4
TOP_K = 8
ROUTED_SCALE = 2.5
ROPE_BASE = 10000.0
EPS = 1e-6

LANES = 128
CHUNK = 128
TILE_N = 512
TM = 256
MIXB_KEYS = 1024
MIXB_QUERIES = 512
EXPERT_ROWS = 512
ROW_GROUP = 16
SORT_ROWS = TOP_K * TM + N_EXPERTS * ROW_GROUP
SORT_CHUNK = 128
COMBINE_CHUNK = 512
NEG = -1e30
VMEM_LIMIT = 56 * 1024 * 1024

T_AQ, T_BQ, T_BK, T_KVA, T_BV, T_CO, T_GT, T_CQ, T_CV, N_TILES = 0, 1, 2, 3, 4, 5, 6, 12, 13, 14

NT_DIMS = (((1,), (1,)), ((), ()))


def _params(sem):
    return pltpu.CompilerParams(dimension_semantics=sem, vmem_limit_bytes=VMEM_LIMIT)


def _rms(x, g):
    return x * lax.rsqrt(jnp.mean(x * x, axis=-1, keepdims=True) + EPS) * g


def _sigmoid(x):
    return jax.nn.sigmoid(x)


def _mod_kernel(c_ref, w_ref, b_ref, o_ref):
    c = c_ref[...]
    s = c * _sigmoid(c)
    o_ref[...] = jnp.dot(s, w_ref[...], precision=HIGHEST, preferred_element_type=F32) + b_ref[...]


def _mod_vectors(cs, w_mod, b_mod):
    depth, d, n = w_mod.shape
    r = cs.shape[0]
    tn = 3 * LANES
    return pl.pallas_call(
        _mod_kernel,
        grid=(depth, n // tn),
        in_specs=[pl.BlockSpec((r, d), lambda l, j: (0, 0)),
                  pl.BlockSpec((None, d, tn), lambda l, j: (l, 0, j)),
                  pl.BlockSpec((None, 1, tn), lambda l, j: (l, 0, j))],
        out_specs=pl.BlockSpec((None, r, tn), lambda l, j: (l, 0, j)),
        out_shape=jax.ShapeDtypeStruct((depth, r, n), F32),
        compiler_params=_params(("parallel", "parallel")),
        name="mod_vectors",
    )(cs, w_mod, b_mod.reshape(depth, 1, n))


def _stream_specs(xs, n_ctx_tiles, t0=0):
    if not isinstance(xs, tuple):
        return [pl.BlockSpec((None, TM, xs.shape[-1]), lambda bi, ti: (bi, ti + t0, 0))], [xs]
    d = xs[0].shape[-1]
    return ([pl.BlockSpec((None, TM, d), lambda bi, ti: (bi, jnp.minimum(ti + t0, n_ctx_tiles - 1), 0)),
             pl.BlockSpec((None, TM, d), lambda bi, ti: (bi, jnp.maximum(ti + t0 - n_ctx_tiles, 0), 0))], list(xs))


def _read_stream(x_refs, is_ctx_tile):
    if len(x_refs) == 1:
        return x_refs[0][...]
    return jnp.where(is_ctx_tile, x_refs[0][...], x_refs[1][...])


def _inproj_kernel(*refs, n_ctx_tiles):
    x_refs, (mod_ref, g_ref, w_ref, wkt_ref, cos_ref, sa_ref, sb_ref, p_ref, gate_ref, kt_ref) = refs[:-10], refs[-10:]
    x = _read_stream(x_refs, pl.program_id(1) < n_ctx_tiles)
    h = _rms(x, g_ref[...]) * (1.0 + mod_ref[1:2, :]) + mod_ref[0:1, :]
    hb = h.astype(BF16)
    cos, sa, sb = cos_ref[...], sa_ref[...], sb_ref[...]

    def rope(t):
        return t * cos + pltpu.roll(t, 16, 1) * sa + pltpu.roll(t, LANES - 16, 1) * sb

    q_scale = HEAD_DIM ** -0.5
    for j in range(N_TILES):
        acc = jnp.dot(hb, w_ref[:, j * TILE_N:(j + 1) * TILE_N], preferred_element_type=F32)
        parts = [acc[:, s * LANES:(s + 1) * LANES] for s in range(TILE_N // LANES)]
        if j == T_AQ:
            parts = [rope(t) * q_scale for t in parts]
        elif j == T_BQ:
            parts = [rope(t) * (q_scale * math.log2(math.e)) for t in parts]
        elif j == T_BK:
            parts = [rope(t) for t in parts]
        elif j == T_KVA:
            gate_ref[...] = parts[2]
            parts[0] = rope(parts[0])
        for s, t in enumerate(parts):
            p_ref[:, j * TILE_N + s * LANES:j * TILE_N + (s + 1) * LANES] = t.astype(BF16)
    kt = lax.dot_general(wkt_ref[...], hb, NT_DIMS, preferred_element_type=F32)
    kt_ref[...] = (kt * (HEAD_DIM_C ** -0.5)).astype(BF16)


def _inproj(xs, mods, g_mix, w_big, w_kt, cos, sa, sb, n_ctx_tiles):
    x_specs, x_args = _stream_specs(xs, n_ctx_tiles)
    b, d = x_args[0].shape[0], x_args[0].shape[-1]
    l = sum(a.shape[1] for a in x_args)
    npad = w_big.shape[1]
    ck = w_kt.shape[0]
    grid = (b, l // TM)
    tok = lambda bi, ti: (bi, ti, 0)
    return pl.pallas_call(
        functools.partial(_inproj_kernel, n_ctx_tiles=n_ctx_tiles),
        grid=grid,
        in_specs=x_specs + [
                  pl.BlockSpec((None, None, 8, d), lambda bi, ti: (bi, jnp.where(ti >= n_ctx_tiles, 1, 0), 0, 0)),
                  pl.BlockSpec((1, d), lambda bi, ti: (0, 0)),
                  pl.BlockSpec((d, npad), lambda bi, ti: (0, 0), pipeline_mode=pl.Buffered(1)),
                  pl.BlockSpec((ck, d), lambda bi, ti: (0, 0), pipeline_mode=pl.Buffered(1)),
                  pl.BlockSpec((TM, LANES), lambda bi, ti: (ti, 0)),
                  pl.BlockSpec((TM, LANES), lambda bi, ti: (ti, 0)),
                  pl.BlockSpec((TM, LANES), lambda bi, ti: (ti, 0))],
        out_specs=[pl.BlockSpec((None, TM, npad), tok),
                   pl.BlockSpec((None, TM, LANES), tok),
                   pl.BlockSpec((None, ck, TM), lambda bi, ti: (bi, 0, ti))],
        out_shape=[jax.ShapeDtypeStruct((b, l, npad), BF16),
                   jax.ShapeDtypeStruct((b, l, LANES), F32),
                   jax.ShapeDtypeStruct((b, ck, l), BF16)],
        compiler_params=_params(("parallel", "parallel")),
        name="inproj",
    )(*x_args, mods, g_mix, w_big, w_kt, cos, sa, sb)


def _mixa_kernel(sink_ref, q_ref, kp_ref, kc_ref, kn_ref, kx_ref, o_ref, *, n_ctx_blocks, n_blocks):
    i = pl.program_id(1)
    lat = i >= n_ctx_blocks
    has_prev = jnp.logical_and(lat, i > n_ctx_blocks)
    has_next = jnp.logical_and(lat, i < n_blocks - 1)
    r = lax.broadcasted_iota(jnp.int32, (CHUNK, CHUNK), 0)
    c = lax.broadcasted_iota(jnp.int32, (CHUNK, CHUNK), 1)
    n_ctx = kx_ref.shape[0]
    valid = jnp.concatenate([
        jnp.logical_and(c >= r, has_prev),
        jnp.broadcast_to(lat, (CHUNK, CHUNK)),
        jnp.logical_and(c <= r, has_next),
        jnp.ones((CHUNK, n_ctx), jnp.bool_)], axis=1)
    kcat = jnp.concatenate([kp_ref[:, :LANES], kc_ref[:, :LANES], kn_ref[:, :LANES], kx_ref[:, :LANES]], axis=0)
    vcat = jnp.concatenate([kp_ref[:, LANES:], kc_ref[:, LANES:], kn_ref[:, LANES:], kx_ref[:, LANES:]], axis=0)
    lane = lax.broadcasted_iota(jnp.int32, (CHUNK, LANES), 1)
    low = lane < HEAD_DIM
    n_pairs = N_HEADS_A // KV_HEADS_A
    outs = []
    for gk in range(KV_HEADS_A):
        keep = low if gk == 0 else jnp.logical_not(low)
        zero = jnp.zeros((CHUNK, LANES), BF16)
        lhs = jnp.concatenate([jnp.where(keep, q_ref[:, t * LANES:(t + 1) * LANES], zero) for t in range(n_pairs)],
                              axis=0)
        s = lax.dot_general(lhs, kcat, NT_DIMS, preferred_element_type=F32)
        o_g = []
        for t in range(n_pairs):
            st = jnp.where(valid, s[t * CHUNK:(t + 1) * CHUNK], NEG)
            sk = sink_ref[gk * n_pairs + t]
            m = jnp.maximum(jnp.max(st, axis=-1, keepdims=True), sk)
            p = jnp.exp(st - m)
            den = jnp.sum(p, axis=-1, keepdims=True) + jnp.exp(sk - m)
            o_g.append(jnp.dot(p.astype(BF16), vcat, preferred_element_type=F32) / den)
        outs.append(o_g)
    for t in range(n_pairs):
        o_ref[:, t * LANES:(t + 1) * LANES] = jnp.where(low, outs[0][t], outs[1][t]).astype(BF16)


def _mixer_a(p, sink, n_ctx):
    b, l, _ = p.shape
    nb = l // CHUNK
    ncb = n_ctx // CHUNK
    kvw = 2 * LANES
    kv_col = T_KVA * TILE_N // kvw
    aq_w = N_HEADS_A * HEAD_DIM
    kern = functools.partial(_mixa_kernel, n_ctx_blocks=ncb, n_blocks=nb)
    return pl.pallas_call(
        kern,
        grid=(b, nb),
        in_specs=[pl.BlockSpec(memory_space=pltpu.SMEM),
                  pl.BlockSpec((None, CHUNK, aq_w), lambda bi, i: (bi, i, T_AQ)),
                  pl.BlockSpec((None, CHUNK, kvw), lambda bi, i: (bi, jnp.maximum(i - 1, 0), kv_col)),
                  pl.BlockSpec((None, CHUNK, kvw), lambda bi, i: (bi, i, kv_col)),
                  pl.BlockSpec((None, CHUNK, kvw), lambda bi, i: (bi, jnp.minimum(i + 1, nb - 1), kv_col)),
                  pl.BlockSpec((None, n_ctx, kvw), lambda bi, i: (bi, 0, kv_col))],
        out_specs=pl.BlockSpec((None, CHUNK, aq_w), lambda bi, i: (bi, i, 0)),
        out_shape=jax.ShapeDtypeStruct((b, l, aq_w), BF16),
        compiler_params=_params(("parallel", "parallel")),
        name="mixer_a",
    )(sink, p, p, p, p, p)


def _fold_lanes(op, acc, s):
    for t in range(s.shape[1] // LANES):
        acc = op(acc, s[:, t * LANES:(t + 1) * LANES])
    return acc


def _mixb_kernel(lam_ref, gd_ref, k_ref, v_ref, *rest, lam_init, chunks):
    q_refs, (o_ref, s_scr, va_scr) = rest[:-3], rest[-3:]

    @pl.when(pl.program_id(2) == 0)
    def _():
        n_keys = v_ref.shape[0]
        va_scr[:, :LANES] = v_ref[...]
        va_scr[:, LANES:] = (lax.broadcasted_iota(jnp.int32, (n_keys, LANES), 1) == 0).astype(BF16)

    lp = lam_ref[...]
    lam = (jnp.exp(jnp.sum(lp[0:1] * lp[1:2], axis=-1, keepdims=True))
           - jnp.exp(jnp.sum(lp[2:3] * lp[3:4], axis=-1, keepdims=True)) + lam_init)
    q = jnp.concatenate([qr[...] for qr in q_refs], axis=0)
    tq = q.shape[0]
    lane = lax.broadcasted_iota(jnp.int32, (tq, LANES), 1)
    zero = jnp.zeros_like(q)
    qs = (jnp.where(lane < HEAD_DIM, q, zero), jnp.where(lane >= HEAD_DIM, q, zero))
    rows = [slice(mi * tq, (mi + 1) * tq) for mi in range(2)]
    mrun = [jnp.full((tq, LANES), NEG, F32) for _ in range(2)]
    for off, sz in chunks:
        for mi in range(2):
            s_scr[rows[mi], off:off + sz] = lax.dot_general(qs[mi], k_ref[off:off + sz, :], NT_DIMS,
                                                            preferred_element_type=F32)
            mrun[mi] = _fold_lanes(jnp.maximum, mrun[mi], s_scr[rows[mi], off:off + sz])
    m = [jnp.max(mr, axis=-1, keepdims=True) for mr in mrun]
    acc = [jnp.zeros((tq, 2 * LANES), F32) for _ in range(2)]
    for off, sz in chunks:
        for mi in range(2):
            pr = jnp.exp2(s_scr[rows[mi], off:off + sz] - m[mi])
            acc[mi] = acc[mi] + jnp.dot(pr.astype(BF16), va_scr[off:off + sz, :], preferred_element_type=F32)
    outs = [a[:, :LANES] / a[:, LANES:LANES + 1] for a in acc]
    o = outs[0] - lam * outs[1]
    o_ref[...] = (_rms(o, gd_ref[...]) * (1.0 - lam_init)).astype(BF16)


def _mixer_b(p, lam_params, g_diff, lam_init, n_ctx):
    b, l, _ = p.shape
    kl = min(MIXB_KEYS, l - n_ctx)
    tq_lat = min(MIXB_QUERIES, l - n_ctx)
    assert (l - n_ctx) % kl == 0 and (l - n_ctx) % tq_lat == 0 and tq_lat % TM == 0 and n_ctx % TM == 0
    q0 = T_BQ * TILE_N // LANES
    k0 = T_BK * TILE_N // LANES
    v0 = T_BV * TILE_N // LANES
    ctx_chunks = ((0, n_ctx),)
    all_chunks = ctx_chunks + tuple((n_ctx + c * kl, kl) for c in range((l - n_ctx) // kl))

    def call(chunks, n_keys, tq, q_tiles, first_row):
        nq = tq // TM
        q_spec = lambda part: pl.BlockSpec((None, TM, LANES),
                                           lambda bi, h, qi: (bi, first_row // TM + qi * nq + part, q0 + h))
        kern = functools.partial(_mixb_kernel, lam_init=lam_init, chunks=chunks)
        return pl.pallas_call(
            kern,
            grid=(b, N_HEADS_B, q_tiles),
            in_specs=[pl.BlockSpec((4, HEAD_DIM), lambda bi, h, qi: (0, 0)),
                      pl.BlockSpec((1, LANES), lambda bi, h, qi: (0, 0)),
                      pl.BlockSpec((None, n_keys, LANES), lambda bi, h, qi: (bi, 0, k0 + h)),
                      pl.BlockSpec((None, n_keys, LANES), lambda bi, h, qi: (bi, 0, v0 + h))]
                     + [q_spec(part) for part in range(nq)],
            out_specs=pl.BlockSpec((None, tq, LANES), lambda bi, h, qi: (bi, qi, h)),
            out_shape=jax.ShapeDtypeStruct((b, q_tiles * tq, N_HEADS_B * LANES), BF16),
            scratch_shapes=[pltpu.VMEM((2 * tq, n_keys), F32), pltpu.VMEM((n_keys, 2 * LANES), BF16)],
            compiler_params=_params(("parallel", "parallel", "arbitrary")),
            name="mixer_b",
        )(lam_params, g_diff, p, p, *([p] * nq))

    tq_ctx = min(tq_lat, n_ctx)
    return (call(ctx_chunks, n_ctx, tq_ctx, n_ctx // tq_ctx, 0),
            call(all_chunks, l, tq_lat, (l - n_ctx) // tq_lat, n_ctx))


def _log_sigmoid(x):
    return jnp.minimum(x, 0.0) - jnp.log1p(jnp.exp(-jnp.abs(x)))


def _mlstm_kernel(qf_ref, ktf_ref, vf_ref, gcf_ref, grf_ref, qb_ref, ktb_ref, vb_ref, gcb_ref, grb_ref,
                  bc_ref, br_ref, of_ref, ob_ref, s_scr, m_scr):
    @pl.when(pl.program_id(1) == 0)
    def _():
        s_scr[...] = jnp.zeros_like(s_scr)
        m_scr[...] = jnp.zeros_like(m_scr)

    r = lax.broadcasted_iota(jnp.int32, (CHUNK, CHUNK), 0)
    cc = lax.broadcasted_iota(jnp.int32, (CHUNK, CHUNK), 1)
    lane = lax.broadcasted_iota(jnp.int32, (CHUNK, LANES), 1)
    ones_col = (lane == 0).astype(BF16)
    nh = N_HEADS_C
    tris = (r >= cc, r <= cc)
    refs = ((qf_ref, ktf_ref, vf_ref, gcf_ref, grf_ref, of_ref), (qb_ref, ktb_ref, vb_ref, gcb_ref, grb_ref, ob_ref))
    grow, bcum_col, bcum_row, tot_row = [], [], [], []
    for di in range(2):
        trif = tris[di].astype(F32)
        gcol = refs[di][3][...] + bc_ref[...]
        grow.append(refs[di][4][...] + br_ref[...])
        lf_col = _log_sigmoid(gcol)
        lf_row = _log_sigmoid(grow[di])
        bcum_col.append(jnp.dot(trif, lf_col, precision=HIGHEST, preferred_element_type=F32))
        bcum_row.append(lax.dot_general(lf_row, trif, NT_DIMS, precision=HIGHEST, preferred_element_type=F32))
        tot_row.append(jnp.sum(lf_row, axis=-1, keepdims=True))
    ch = [(di, h) for di in range(2) for h in range(nh)]
    gi = {c: (2 * c[0]) * nh + c[1] for c in ch}
    gf = {c: (2 * c[0] + 1) * nh + c[1] for c in ch}
    ic_row = {c: grow[c[0]][gi[c]:gi[c] + 1, :] for c in ch}
    b_col = {c: bcum_col[c[0]][:, gf[c]:gf[c] + 1] for c in ch}
    b_row = {c: bcum_row[c[0]][gf[c]:gf[c] + 1, :] for c in ch}
    total = {c: tot_row[c[0]][gf[c]:gf[c] + 1, :] for c in ch}
    m_st = {c: m_scr[c[0], c[1], 0:1, 0:1] for c in ch}
    qh = {c: refs[c[0]][0][:, c[1] * LANES:(c[1] + 1) * LANES] for c in ch}
    kth = {c: refs[c[0]][1][c[1] * LANES:(c[1] + 1) * LANES, :] for c in ch}
    vaug = {c: jnp.concatenate([refs[c[0]][2][:, c[1] * LANES:(c[1] + 1) * LANES], ones_col], axis=1) for c in ch}
    st = {c: s_scr[c[0], c[1]] for c in ch}
    qk = {c: jnp.dot(qh[c], kth[c], preferred_element_type=F32) for c in ch}
    cross = {c: jnp.dot(qh[c], st[c].astype(BF16), preferred_element_type=F32) for c in ch}
    gs_row = {c: total[c] - b_row[c] + ic_row[c] for c in ch}
    m_new = {c: jnp.maximum(total[c] + m_st[c], jnp.max(gs_row[c], axis=-1, keepdims=True)) for c in ch}
    wkt = {c: (kth[c].astype(F32) * jnp.exp(gs_row[c] - m_new[c])).astype(BF16) for c in ch}
    upd = {c: jnp.dot(wkt[c], vaug[c], preferred_element_type=F32) for c in ch}
    dm = {c: jnp.where(tris[c[0]], b_col[c] - b_row[c] + ic_row[c], NEG) for c in ch}
    inter = {c: b_col[c] + m_st[c] for c in ch}
    m_t = {c: jnp.maximum(inter[c], jnp.max(dm[c], axis=-1, keepdims=True)) for c in ch}
    sc = {c: (qk[c] * jnp.exp(dm[c] - m_t[c])).astype(BF16) for c in ch}
    intra = {c: jnp.dot(sc[c], vaug[c], preferred_element_type=F32) for c in ch}
    for c in ch:
        di, h = c
        nd = intra[c] + jnp.exp(inter[c] - m_t[c]) * cross[c]
        den = nd[:, LANES:LANES + 1]
        h_out = nd[:, :LANES] / jnp.maximum(jnp.abs(den), jnp.exp(-m_t[c]))
        refs[di][5][:, h * LANES:(h + 1) * LANES] = h_out.astype(BF16)
        s_scr[di, h] = jnp.exp(total[c] + m_st[c] - m_new[c]) * st[c] + upd[c]
        m_scr[di, h] = jnp.broadcast_to(m_new[c], m_scr.shape[2:])


def _mlstm(p, kt, gates, gates_t, bias_row, bias_col, n_ctx):
    b, l, _ = p.shape
    nc = l // CHUNK
    ncc = n_ctx // CHUNK
    cw = N_HEADS_C * HEAD_DIM_C

    def rev(c):
        return jnp.where(c < ncc, ncc - 1 - c, nc + ncc - 1 - c)

    def specs(ch):
        return [pl.BlockSpec((None, CHUNK, cw), lambda bi, c: (bi, ch(c), T_CQ)),
                pl.BlockSpec((None, cw, CHUNK), lambda bi, c: (bi, 0, ch(c))),
                pl.BlockSpec((None, CHUNK, cw), lambda bi, c: (bi, ch(c), T_CV)),
                pl.BlockSpec((None, CHUNK, LANES), lambda bi, c: (bi, ch(c), 0)),
                pl.BlockSpec((None, 16, CHUNK), lambda bi, c: (bi, 0, ch(c)))]

    fwd = lambda c: c
    out = jax.ShapeDtypeStruct((b, l, cw), BF16)
    return pl.pallas_call(
        _mlstm_kernel,
        grid=(b, nc),
        in_specs=specs(fwd) + specs(rev) + [pl.BlockSpec((1, LANES), lambda bi, c: (0, 0)),
                                            pl.BlockSpec((16, LANES), lambda bi, c: (0, 0))],
        out_specs=[pl.BlockSpec((None, CHUNK, cw), lambda bi, c: (bi, c, 0)),
                   pl.BlockSpec((None, CHUNK, cw), lambda bi, c: (bi, rev(c), 0))],
        out_shape=[out, out],
        scratch_shapes=[pltpu.VMEM((2, N_HEADS_C, HEAD_DIM_C, 2 * LANES), F32),
                        pltpu.VMEM((2, N_HEADS_C, 8, LANES), F32)],
        compiler_params=_params(("parallel", "arbitrary")),
        name="mlstm",
    )(p, kt, p, gates, gates_t, p, kt, p, gates, gates_t, bias_row, bias_col)


def _merge_kernel(*refs, n_ctx_tiles, t0):
    x_refs, (mod_ref, oa_ref, obc_ref, obl_ref, hf_ref, hb_ref, co_ref, gt_ref, gm_ref,
             wa_ref, wb_ref, wc_ref, wo_ref, xo_ref) = refs[:-14], refs[-14:]
    d = xo_ref.shape[-1]
    is_ctx = pl.program_id(1) + t0 < n_ctx_tiles
    ob = jnp.where(is_ctx, obc_ref[...], obl_ref[...])
    hs = hf_ref[...].astype(F32) + hb_ref[...].astype(F32)
    co = co_ref[...].astype(F32)
    gm = gm_ref[...]
    oc = []
    for h in range(N_HEADS_C):
        sl = slice(h * LANES, (h + 1) * LANES)
        oc.append((_rms(hs[:, sl], gm[:, sl]) * _sigmoid(co[:, sl])).astype(BF16))
    oc = jnp.concatenate(oc, axis=1)
    y = (_sigmoid(gt_ref[:, 0:d].astype(F32)) * jnp.dot(oa_ref[...], wa_ref[...], preferred_element_type=F32)
         + _sigmoid(gt_ref[:, d:2 * d].astype(F32)) * jnp.dot(ob, wb_ref[...], preferred_element_type=F32)
         + _sigmoid(gt_ref[:, 2 * d:3 * d].astype(F32)) * jnp.dot(oc, wc_ref[...], preferred_element_type=F32))
    out = jnp.dot(y.astype(BF16), wo_ref[...], preferred_element_type=F32)
    xo_ref[...] = _read_stream(x_refs, is_ctx) + mod_ref[2:3, :] * out


def _merge(xs, mods, oa, ob_ctx, ob_lat, hf, hb, p, g_mlstm, wa, wb, wc, wo, n_ctx_tiles, latent_only):
    t0 = n_ctx_tiles if latent_only else 0
    x_specs, x_args = _stream_specs(xs, n_ctx_tiles, t0)
    b, d = x_args[0].shape[0], x_args[0].shape[-1]
    n_tiles = oa.shape[1] // TM - t0
    tok = lambda bi, ti: (bi, ti + t0, 0)
    cw = N_HEADS_C * HEAD_DIM_C
    const = lambda bi, ti: (0, 0)
    return pl.pallas_call(
        functools.partial(_merge_kernel, n_ctx_tiles=n_ctx_tiles, t0=t0),
        grid=(b, n_tiles),
        in_specs=x_specs + [
                  pl.BlockSpec((None, None, 8, d), lambda bi, ti: (bi, jnp.where(ti + t0 >= n_ctx_tiles, 1, 0), 0, 0)),
                  pl.BlockSpec((None, TM, oa.shape[-1]), tok),
                  pl.BlockSpec((None, TM, ob_ctx.shape[-1]),
                               lambda bi, ti: (bi, jnp.minimum(ti + t0, n_ctx_tiles - 1), 0)),
                  pl.BlockSpec((None, TM, ob_lat.shape[-1]),
                               lambda bi, ti: (bi, jnp.maximum(ti + t0 - n_ctx_tiles, 0), 0)),
                  pl.BlockSpec((None, TM, cw), tok),
                  pl.BlockSpec((None, TM, cw), tok),
                  pl.BlockSpec((None, TM, cw), lambda bi, ti: (bi, ti + t0, T_CO)),
                  pl.BlockSpec((None, TM, 3 * d), lambda bi, ti: (bi, ti + t0, T_GT * TILE_N // (3 * d))),
                  pl.BlockSpec((1, cw), const),
                  pl.BlockSpec(wa.shape, const), pl.BlockSpec(wb.shape, const),
                  pl.BlockSpec(wc.shape, const), pl.BlockSpec(wo.shape, const)],
        out_specs=pl.BlockSpec((None, TM, d), lambda bi, ti: (bi, ti, 0)),
        out_shape=jax.ShapeDtypeStruct((b, n_tiles * TM, d), F32),
        compiler_params=_params(("parallel", "parallel")),
        name="merge",
    )(*x_args, mods, oa, ob_ctx, ob_lat, hf, hb, p, p, g_mlstm, wa, wb, wc, wo)


def _router_kernel(x_ref, mod_ref, g_ref, wrt_ref, br_ref, h_ref, idx_ref, wt_ref, cnt_ref):
    h = _rms(x_ref[...], g_ref[...]) * (1.0 + mod_ref[4:5, :]) + mod_ref[3:4, :]
    h_ref[...] = h.astype(BF16)
    tm = h.shape[0]
    per = N_EXPERTS // N_GROUPS
    lt = lax.dot_general(wrt_ref[...], h, NT_DIMS, precision=HIGHEST, preferred_element_type=F32)
    s = _sigmoid(lt)
    sel = s + br_ref[...]
    ninf = -jnp.inf
    sel3 = sel.reshape(N_GROUPS, per, tm)
    eidx = lax.broadcasted_iota(jnp.int32, (N_GROUPS, per, tm), 1)
    m1 = jnp.max(sel3, axis=1, keepdims=True)
    first = jnp.min(jnp.where(sel3 == m1, eidx, per), axis=1, keepdims=True)
    m2 = jnp.max(jnp.where(eidx == first, ninf, sel3), axis=1, keepdims=True)
    gscore = (m1 + m2).reshape(N_GROUPS, tm)
    gidx = lax.broadcasted_iota(jnp.int32, (N_GROUPS, tm), 0)
    gmask = jnp.zeros((N_GROUPS, tm), jnp.bool_)
    cur = gscore
    for _ in range(TOPK_GROUPS):
        mx = jnp.max(cur, axis=0, keepdims=True)
        hit = gidx == jnp.min(jnp.where(cur == mx, gidx, N_GROUPS), axis=0, keepdims=True)
        gmask = jnp.logical_or(gmask, hit)
        cur = jnp.where(hit, ninf, cur)
    cur = jnp.where(gmask.reshape(N_GROUPS, 1, tm), sel3, ninf).reshape(N_EXPERTS, tm)
    eid = lax.broadcasted_iota(jnp.int32, (N_EXPERTS, tm), 0)
    ids, ws = [], []
    chosen = jnp.zeros((N_EXPERTS, tm), F32)
    for _ in range(TOP_K):
        mx = jnp.max(cur, axis=0, keepdims=True)
        pick = jnp.min(jnp.where(cur == mx, eid, N_EXPERTS), axis=0, keepdims=True)
        hit = eid == pick
        ids.append(pick)
        ws.append(jnp.sum(jnp.where(hit, s, 0.0), axis=0, keepdims=True))
        cur = jnp.where(hit, ninf, cur)
        chosen = chosen + hit.astype(F32)
    wsum = ws[0]
    for w in ws[1:]:
        wsum = wsum + w
    idx_ref[...] = jnp.concatenate(ids, axis=0)
    wt_ref[...] = jnp.concatenate([w / wsum * ROUTED_SCALE for w in ws], axis=0)
    cnt_ref[...] = jnp.sum(chosen, axis=1, keepdims=True).astype(jnp.int32)


def _router(xs, mods, g_ffn, w_router_t, b_router, n_ctx_tiles):
    b, l, d = xs.shape
    tok = lambda bi, ti: (bi, ti, 0)
    const = lambda bi, ti: (0, 0)
    return pl.pallas_call(
        _router_kernel,
        grid=(b, l // TM),
        in_specs=[pl.BlockSpec((None, TM, d), tok),
                  pl.BlockSpec((None, None, 8, d), lambda bi, ti: (bi, jnp.where(ti >= n_ctx_tiles, 1, 0), 0, 0)),
                  pl.BlockSpec((1, d), const),
                  pl.BlockSpec((N_EXPERTS, d), const),
                  pl.BlockSpec((N_EXPERTS, 1), const)],
        out_specs=[pl.BlockSpec((None, TM, d), tok),
                   pl.BlockSpec((None, TOP_K, TM), lambda bi, ti: (bi, 0, ti)),
                   pl.BlockSpec((None, TOP_K, TM), lambda bi, ti: (bi, 0, ti)),
                   pl.BlockSpec((None, None, N_EXPERTS, 1), lambda bi, ti: (bi, ti, 0, 0))],
        out_shape=[jax.ShapeDtypeStruct((b, l, d), BF16),
                   jax.ShapeDtypeStruct((b, TOP_K, l), jnp.int32),
                   jax.ShapeDtypeStruct((b, TOP_K, l), F32),
                   jax.ShapeDtypeStruct((b, l // TM, N_EXPERTS, 1), jnp.int32)],
        compiler_params=_params(("parallel", "parallel")),
        name="router",
    )(xs, mods, g_ffn, w_router_t, b_router)


def _sort_kernel(idx_ref, off_ref, h_ref, posl_ref, ts_ref):
    tm = h_ref.shape[0]
    idx = idx_ref[...]
    eid = lax.broadcasted_iota(jnp.int32, (N_EXPERTS, tm), 0)
    hits = [eid == idx[k:k + 1, :] for k in range(TOP_K)]
    chosen = hits[0].astype(BF16)
    for hk in hits[1:]:
        chosen = chosen + hk.astype(BF16)
    r = lax.broadcasted_iota(jnp.int32, (tm, tm), 0)
    c = lax.broadcasted_iota(jnp.int32, (tm, tm), 1)
    before = (r < c).astype(BF16)
    rank = jnp.dot(chosen, before, preferred_element_type=F32)
    slot = rank.astype(jnp.int32) + off_ref[...]
    posl = jnp.concatenate([jnp.sum(jnp.where(hk, slot, 0), axis=0, keepdims=True) for hk in hits], axis=0)
    posl_ref[...] = posl
    hb = h_ref[...]
    for rb in range(ts_ref.shape[0] // SORT_CHUNK):
        rows = lax.broadcasted_iota(jnp.int32, (SORT_CHUNK, tm), 0) + rb * SORT_CHUNK
        sel = rows == posl[0:1, :]
        for k in range(1, TOP_K):
            sel = jnp.logical_or(sel, rows == posl[k:k + 1, :])
        onehot = jnp.where(sel, 1.0, 0.0).astype(BF16)
        ts = jnp.dot(onehot, hb, preferred_element_type=F32)
        ts_ref[rb * SORT_CHUNK:(rb + 1) * SORT_CHUNK, :] = ts.astype(BF16)


def _sort_rows(idx_t, off, h_flat):
    b, k, l = idx_t.shape
    n, d = h_flat.shape
    nt = l // TM
    return pl.pallas_call(
        _sort_kernel,
        grid=(b * nt,),
        in_specs=[pl.BlockSpec((None, k, TM), lambda i: (i // nt, 0, i % nt)),
                  pl.BlockSpec((None, N_EXPERTS, 1), lambda i: (i, 0, 0)),
                  pl.BlockSpec((TM, d), lambda i: (i, 0))],
        out_specs=[pl.BlockSpec((None, k, TM), lambda i: (i // nt, 0, i % nt)),
                   pl.BlockSpec((None, SORT_ROWS, d), lambda i: (i, 0, 0))],
        out_shape=[jax.ShapeDtypeStruct((b, k, l), jnp.int32),
                   jax.ShapeDtypeStruct((b * nt, SORT_ROWS, d), BF16)],
        compiler_params=_params(("parallel",)),
        name="sort_rows",
    )(idx_t, off, h_flat)


SLAB_SIZES = tuple(1 << s for s in range((EXPERT_ROWS // ROW_GROUP).bit_length()))


def _expert_kernel(be_ref, nu_ref, cov_ref, grp_ref, ts_in, wg_ref, wu_ref, wd_ref, ts_out, xbuf, ybuf, sem_g, sem_s):
    j = pl.program_id(0)
    nu = nu_ref[0]
    n_grp = xbuf.shape[1]
    blk = n_grp * ROW_GROUP
    slot = j % 2

    def for_groups(bj, fn):
        for i in range(n_grp):
            sg = grp_ref[bj * n_grp + i]

            @pl.when(sg >= 0)
            def _():
                fn(sg, i)

    def gather(bj, s):
        xbuf[s] = jnp.zeros(xbuf.shape[1:], xbuf.dtype)
        for_groups(bj, lambda src, i: pltpu.make_async_copy(ts_in.at[src], xbuf.at[s, i], sem_g.at[s]).start())

    def scatter(bj, s):
        for_groups(bj, lambda dst, i: pltpu.make_async_copy(ybuf.at[s, i], ts_out.at[dst], sem_s.at[s]).start())

    def wait_rows(bj, buf, sem, s):
        groups = cov_ref[bj] // ROW_GROUP
        for sz in SLAB_SIZES:
            @pl.when((groups & sz) != 0)
            def _():
                pltpu.make_async_copy(ts_in.at[pl.ds(0, sz)], buf.at[s, pl.ds(0, sz)], sem.at[s]).wait()

    @pl.when(j == 0)
    def _():
        gather(0, 0)

    @pl.when(j + 1 < nu)
    def _():
        gather(j + 1, 1 - slot)

    @pl.when(jnp.logical_and(j >= 2, j - 2 < nu))
    def _():
        wait_rows(j - 2, ybuf, sem_s, slot)

    @pl.when(j < nu)
    def _():
        wait_rows(j, xbuf, sem_g, slot)
        x = xbuf[slot].reshape(blk, xbuf.shape[-1])
        g = jnp.dot(x, wg_ref[...], preferred_element_type=F32)
        u = jnp.dot(x, wu_ref[...], preferred_element_type=F32)
        a = (g * _sigmoid(g) * u).astype(BF16)
        y = jnp.dot(a, wd_ref[...], preferred_element_type=F32)
        ybuf[slot] = y.astype(BF16).reshape(ybuf.shape[1:])
        scatter(j, slot)


def _experts(plan, tiles, wg, wu, wd, layer):
    nt, rows, w = tiles.shape
    blk = EXPERT_ROWS
    d, de = wg.shape[2:]
    n_blocks = plan["block_e"].shape[0]
    tables = (plan["block_e"], plan["n_used"], plan["blk_rows"], plan["blk_groups"])
    wspec = lambda shape: pl.BlockSpec((None, None) + shape, lambda i, be, *_: (layer, be[i], 0, 0))
    grid_spec = pltpu.PrefetchScalarGridSpec(
        num_scalar_prefetch=len(tables),
        grid=(n_blocks,),
        in_specs=[pl.BlockSpec(memory_space=pl.ANY), wspec((d, de)), wspec((d, de)), wspec((de, d))],
        out_specs=pl.BlockSpec(memory_space=pl.ANY),
        scratch_shapes=[pltpu.VMEM((2, blk // ROW_GROUP, ROW_GROUP, w), tiles.dtype)] * 2 + [
                        pltpu.SemaphoreType.DMA((2,)), pltpu.SemaphoreType.DMA((2,))])
    return pl.pallas_call(
        _expert_kernel,
        grid_spec=grid_spec,
        out_shape=jax.ShapeDtypeStruct((nt * rows // ROW_GROUP, ROW_GROUP, w), tiles.dtype),
        input_output_aliases={len(tables): 0},
        compiler_params=_params(("arbitrary",)),
        name="experts",
    )(*tables, tiles.reshape(nt * rows // ROW_GROUP, ROW_GROUP, w), wg, wu, wd).reshape(tiles.shape)


def _combine_kernel(ts_ref, posl_ref, w_ref, x_ref, h_ref, mod_ref, wsg_ref, wsu_ref, wsd_ref, *rest):
    tm = x_ref.shape[0]
    hb = h_ref[...]
    g = jnp.dot(hb, wsg_ref[...], preferred_element_type=F32)
    u = jnp.dot(hb, wsu_ref[...], preferred_element_type=F32)
    acc = jnp.dot((g * _sigmoid(g) * u).astype(BF16), wsd_ref[...], preferred_element_type=F32)
    posl = posl_ref[...]
    w = w_ref[...]
    for rb in range(ts_ref.shape[0] // COMBINE_CHUNK):
        cols = lax.broadcasted_iota(jnp.int32, (tm, COMBINE_CHUNK), 1) + rb * COMBINE_CHUNK
        wm = jnp.zeros((tm, COMBINE_CHUNK), F32)
        for k in range(TOP_K):
            wm = jnp.where(cols == posl[:, k:k + 1], w[:, k:k + 1], wm)
        acc = acc + jnp.dot(wm.astype(BF16), ts_ref[rb * COMBINE_CHUNK:(rb + 1) * COMBINE_CHUNK, :],
                            preferred_element_type=F32)
    out = x_ref[...] + mod_ref[5:6, :] * acc
    rest[-1][...] = _rms(out, rest[0][...]) if len(rest) == 2 else out


def _combine(tiles, posl_tm, wts, x_flat, h_flat, mods, wsg, wsu, wsd, tiles_per_sample, n_ctx_tiles, g_final=None):
    n, d = x_flat.shape
    ds_ = wsg.shape[-1]
    tok = lambda i: (i, 0)
    const = lambda i: (0, 0)

    def mod_idx(i):
        return (i // tiles_per_sample, jnp.where(i % tiles_per_sample >= n_ctx_tiles, 1, 0), 0, 0)

    in_extra, args_extra, out_rows, out_idx = [], [], n, tok
    if g_final is not None:
        lat_tiles = tiles_per_sample - n_ctx_tiles
        in_extra, args_extra = [pl.BlockSpec((1, d), const)], [g_final]
        out_rows = n // TM // tiles_per_sample * lat_tiles * TM
        out_idx = lambda i: (i // tiles_per_sample * lat_tiles + jnp.maximum(i % tiles_per_sample - n_ctx_tiles, 0), 0)
    return pl.pallas_call(
        _combine_kernel,
        grid=(n // TM,),
        in_specs=[pl.BlockSpec((None,) + tiles.shape[1:], lambda i: (i, 0, 0)),
                  pl.BlockSpec((TM, TOP_K), tok),
                  pl.BlockSpec((TM, TOP_K), tok),
                  pl.BlockSpec((TM, d), tok),
                  pl.BlockSpec((TM, d), tok),
                  pl.BlockSpec((None, None, 8, d), mod_idx),
                  pl.BlockSpec((d, ds_), const), pl.BlockSpec((d, ds_), const), pl.BlockSpec((ds_, d), const)]
                 + in_extra,
        out_specs=pl.BlockSpec((TM, d), out_idx),
        out_shape=jax.ShapeDtypeStruct((out_rows, d), F32),
        compiler_params=_params(("arbitrary",)),
        name="combine",
    )(tiles, posl_tm, wts, x_flat, h_flat, mods, wsg, wsu, wsd, *args_extra)


def _moe_plan(cnt, n_assign, blk):
    nt = cnt.shape[0]
    run = (cnt + ROW_GROUP - 1) // ROW_GROUP * ROW_GROUP
    tile_off = jnp.cumsum(run, axis=1) - run
    tot = jnp.sum(run, axis=0)
    padded = (tot + blk - 1) // blk * blk
    pad_end = jnp.cumsum(padded)
    pad_start = pad_end - padded
    n_blocks = -(-(n_assign + nt * N_EXPERTS * (ROW_GROUP - 1)) // blk) + N_EXPERTS + 2
    first_row = jnp.arange(n_blocks, dtype=jnp.int32) * blk
    count = lambda m: jnp.sum(m.astype(jnp.int32), axis=1)
    block_e = jnp.minimum(count(pad_end[None, :] <= first_row[:, None]), N_EXPERTS - 1)
    i32 = lambda a: a.astype(jnp.int32)
    run_end = jnp.cumsum(run, axis=0).T[block_e][:, None, :]
    run_start = run_end - run.T[block_e][:, None, :]
    run_src = (jnp.arange(nt) * SORT_ROWS)[None, :] + tile_off.T[block_e]
    local = (first_row - pad_start[block_e])[:, None] + jnp.arange(0, blk, ROW_GROUP)[None, :]
    hit = jnp.logical_and(run_start <= local[:, :, None], local[:, :, None] < run_end)
    src_row = jnp.sum(jnp.where(hit, run_src[:, None, :] + local[:, :, None] - run_start, 0), axis=2)
    blk_groups = jnp.where(jnp.any(hit, axis=2), src_row // ROW_GROUP, -1).reshape(-1)
    return dict(
        tile_off=i32(tile_off)[:, :, None],
        blk_groups=i32(blk_groups),
        blk_rows=i32(jnp.clip((pad_start + tot)[block_e] - first_row, 0, blk)),
        block_e=i32(block_e), n_used=i32(pad_end[-1] // blk).reshape(1))


def _rope_tables(s_len, n_ctx):
    rows = s_len // GRID_W
    row = jnp.repeat(jnp.arange(rows), GRID_W).astype(F32)
    col = jnp.tile(jnp.arange(GRID_W), rows).astype(F32)
    quarter = HEAD_DIM // 4
    inv = 1.0 / (ROPE_BASE ** (jnp.arange(quarter, dtype=F32) / quarter))
    ar, ac = row[:, None] * inv, col[:, None] * inv
    cr, sr, cc, sc = jnp.cos(ar), jnp.sin(ar), jnp.cos(ac), jnp.sin(ac)
    z = jnp.zeros_like(sr)
    cos = jnp.concatenate([cr, cr, cc, cc], axis=1)
    sa = jnp.concatenate([z, sr, z, sc], axis=1)
    sb = jnp.concatenate([-sr, z, -sc, z], axis=1)
    rep = LANES // HEAD_DIM

    def full(t, fill):
        t = jnp.tile(t, (1, rep))
        return jnp.concatenate([jnp.full((n_ctx, LANES), fill, F32), t], axis=0)

    return full(cos, 1.0), full(sa, 0.0), full(sb, 0.0)


def _pair_perm():
    g = N_HEADS_A // KV_HEADS_A
    heads = [h for t in range(g) for h in (t, t + g)]
    return jnp.concatenate([jnp.arange(h * HEAD_DIM, (h + 1) * HEAD_DIM) for h in heads])


def _split_w_in(w):
    a_q, a_kv = N_HEADS_A * HEAD_DIM, KV_HEADS_A * HEAD_DIM
    b_w = N_HEADS_B * 2 * HEAD_DIM
    c_w = N_HEADS_C * HEAD_DIM_C
    sizes = (a_q, a_kv, a_kv, b_w, b_w, b_w, c_w, c_w, c_w, c_w, 4 * N_HEADS_C, w.shape[1])
    parts, start = [], 0
    for sz in sizes[:-1]:
        parts.append(w[:, start:start + sz])
        start += sz
    parts.append(w[:, start:])
    return parts


def _pack_w_in(w):
    d = w.shape[0]
    aq, ak, av, bq, bk, bv, cq, ck, cv, co, cg, gt = _split_w_in(w)
    pad = lambda n: jnp.zeros((d, n), w.dtype)
    kva = jnp.concatenate([ak, av, cg, pad(TILE_N - ak.shape[1] - av.shape[1] - cg.shape[1])], axis=1)
    big = jnp.concatenate([aq[:, _pair_perm()], bq, bk, kva, bv, co, gt, cq, cv], axis=1)
    return big.astype(BF16), ck.T.astype(BF16)


def kernel(x, c, ctx, c_ctx, w_mod, b_mod, g_mix, g_ffn, w_in, b_gate, sink, lam_q1, lam_k1, lam_q2, lam_k2,
           g_diff, g_mlstm, w_a, w_b, w_c, w_out, w_router, b_router, w_exp_gate, w_exp_up, w_exp_down,
           w_sh_gate, w_sh_up, w_sh_down, g_final):
    b, s_len, d = x.shape
    n_ctx = ctx.shape[1]
    l = n_ctx + s_len
    depth = w_mod.shape[0]
    n_ctx_tiles = n_ctx // TM
    assert n_ctx % TM == 0 and s_len % TM == 0 and d % LANES == 0 and s_len % GRID_W == 0

    xs = (ctx, x)
    cos, sa, sb = _rope_tables(s_len, n_ctx)

    rows_c = 16
    cs = jnp.concatenate([c, c_ctx[None], jnp.zeros((rows_c - b - 1, d), F32)], axis=0)
    mod_all = _mod_vectors(cs, w_mod, b_mod).reshape(depth, rows_c, N_MOD, d)
    mod_all = jnp.pad(mod_all, ((0, 0), (0, 0), (0, 8 - N_MOD), (0, 0)))

    perm = _pair_perm()
    expert_w = [w.astype(BF16) for w in (w_exp_gate, w_exp_up, w_exp_down)]
    for layer in range(depth):
        lam_init = 0.8 - 0.6 * math.exp(-0.3 * layer)
        mods = jnp.stack([jnp.broadcast_to(mod_all[layer, b], (b, 8, d)), mod_all[layer, :b]], axis=1)
        w_big, w_kt = _pack_w_in(w_in[layer])
        p, gates, kt = _inproj(xs, mods, g_mix[layer][None], w_big, w_kt, cos, sa, sb, n_ctx_tiles)

        oa = _mixer_a(p, sink[layer], n_ctx)
        lam_params = jnp.stack([lam_q1[layer], lam_k1[layer], lam_q2[layer], lam_k2[layer]])
        ob_ctx, ob_lat = _mixer_b(p, lam_params, g_diff[layer][None], lam_init, n_ctx)

        bias = b_gate[layer].reshape(-1)
        bias_row = jnp.pad(bias, (0, LANES - bias.shape[0]))[None]
        bias_col = jnp.broadcast_to(bias[:, None], (bias.shape[0], LANES))
        gates_t = jnp.transpose(gates[:, :, :bias.shape[0]], (0, 2, 1))
        hf, hb = _mlstm(p, kt, gates, gates_t, bias_row, bias_col, n_ctx)

        last = layer == depth - 1
        xs = _merge(xs, mods, oa, ob_ctx, ob_lat, hf, hb, p, g_mlstm[layer][None],
                    w_a[layer][perm].astype(BF16), w_b[layer].astype(BF16), w_c[layer].astype(BF16),
                    w_out[layer].astype(BF16), n_ctx_tiles, last)

        lf, ctx_tiles = (s_len, 0) if last else (l, n_ctx_tiles)
        h, idx_t, wt_t, cnt = _router(xs, mods, g_ffn[layer][None], w_router[layer].T, b_router[layer][:, None],
                                      ctx_tiles)
        plan = _moe_plan(cnt.reshape(-1, N_EXPERTS), b * lf * TOP_K, EXPERT_ROWS)
        h_flat = h.reshape(b * lf, d)
        posl, tiles = _sort_rows(idx_t, plan["tile_off"], h_flat)
        tiles = _experts(plan, tiles, *expert_w, layer)
        to_rows = lambda a: jnp.transpose(a, (0, 2, 1)).reshape(b * lf, TOP_K)
        xs = _combine(tiles, to_rows(posl), to_rows(wt_t), xs.reshape(b * lf, d), h_flat, mods,
                      w_sh_gate[layer].astype(BF16), w_sh_up[layer].astype(BF16), w_sh_down[layer].astype(BF16),
                      lf // TM, ctx_tiles, g_final[None] if last else None).reshape(b, lf, d)
    return xs
```

```python
import functools
import math

import jax
import jax.numpy as jnp
from jax import lax
from jax.experimental import pallas as pl
from jax.experimental.pallas import tpu as pltpu

F32 = jnp.float32
BF16 = jnp.bfloat16
HIGHEST = lax.Precision.HIGHEST

GRID_W = 64
N_MOD = 6
HEAD_DIM = 64
N_HEADS_A = 8
KV_HEADS_A = 2
WINDOW = 128
N_HEADS_B = 4
N_HEADS_C = 4
HEAD_DIM_C = 128
N_EXPERTS = 64
N_GROUPS = 8
TOPK_GROUPS = 4
TOP_K = 8
ROUTED_SCALE = 2.5
ROPE_BASE = 10000.0
EPS = 1e-6

LANES = 128
CHUNK = 128
TILE_N = 512
TM = 256
MIXB_KEYS = 1024
MIXB_QUERIES = 512
EXPERT_ROWS = 512
ROW_GROUP = 16
SORT_ROWS = TOP_K * TM + N_EXPERTS * ROW_GROUP
SORT_CHUNK = 128
COMBINE_CHUNK = 512
NEG = -1e30
VMEM_LIMIT = 56 * 1024 * 1024

T_AQ, T_BQ, T_BK, T_KVA, T_BV, T_CO, T_GT, T_CQ, T_CV, N_TILES = 0, 1, 2, 3, 4, 5, 6, 12, 13, 14

NT_DIMS = (((1,), (1,)), ((), ()))


def _params(sem):
    return pltpu.CompilerParams(dimension_semantics=sem, vmem_limit_bytes=VMEM_LIMIT)


def _rms(x, g):
    return x * lax.rsqrt(jnp.mean(x * x, axis=-1, keepdims=True) + EPS) * g


def _sigmoid(x):
    return jax.nn.sigmoid(x)


def _mod_kernel(c_ref, w_ref, b_ref, o_ref):
    c = c_ref[...]
    s = c * _sigmoid(c)
    o_ref[...] = jnp.dot(s, w_ref[...], precision=HIGHEST, preferred_element_type=F32) + b_ref[...]


def _mod_vectors(cs, w_mod, b_mod):
    depth, d, n = w_mod.shape
    r = cs.shape[0]
    tn = 3 * LANES
    return pl.pallas_call(
        _mod_kernel,
        grid=(depth, n // tn),
        in_specs=[pl.BlockSpec((r, d), lambda l, j: (0, 0)),
                  pl.BlockSpec((None, d, tn), lambda l, j: (l, 0, j)),
                  pl.BlockSpec((None, 1, tn), lambda l, j: (l, 0, j))],
        out_specs=pl.BlockSpec((None, r, tn), lambda l, j: (l, 0, j)),
        out_shape=jax.ShapeDtypeStruct((depth, r, n), F32),
        compiler_params=_params(("parallel", "parallel")),
        name="mod_vectors",
    )(cs, w_mod, b_mod.reshape(depth, 1, n))


def _stream_specs(xs, n_ctx_tiles, t0=0):
    if not isinstance(xs, tuple):
        return [pl.BlockSpec((None, TM, xs.shape[-1]), lambda bi, ti: (bi, ti + t0, 0))], [xs]
    d = xs[0].shape[-1]
    return ([pl.BlockSpec((None, TM, d), lambda bi, ti: (bi, jnp.minimum(ti + t0, n_ctx_tiles - 1), 0)),
             pl.BlockSpec((None, TM, d), lambda bi, ti: (bi, jnp.maximum(ti + t0 - n_ctx_tiles, 0), 0))], list(xs))


def _read_stream(x_refs, is_ctx_tile):
    if len(x_refs) == 1:
        return x_refs[0][...]
    return jnp.where(is_ctx_tile, x_refs[0][...], x_refs[1][...])


def _inproj_kernel(*refs, n_ctx_tiles):
    x_refs, (mod_ref, g_ref, w_ref, wkt_ref, cos_ref, sa_ref, sb_ref, p_ref, gate_ref, kt_ref) = refs[:-10], refs[-10:]
    x = _read_stream(x_refs, pl.program_id(1) < n_ctx_tiles)
    h = _rms(x, g_ref[...]) * (1.0 + mod_ref[1:2, :]) + mod_ref[0:1, :]
    hb = h.astype(BF16)
    cos, sa, sb = cos_ref[...], sa_ref[...], sb_ref[...]

    def rope(t):
        return t * cos + pltpu.roll(t, 16, 1) * sa + pltpu.roll(t, LANES - 16, 1) * sb

    q_scale = HEAD_DIM ** -0.5
    for j in range(N_TILES):
        acc = jnp.dot(hb, w_ref[:, j * TILE_N:(j + 1) * TILE_N], preferred_element_type=F32)
        parts = [acc[:, s * LANES:(s + 1) * LANES] for s in range(TILE_N // LANES)]
        if j == T_AQ:
            parts = [rope(t) * q_scale for t in parts]
        elif j == T_BQ:
            parts = [rope(t) * (q_scale * math.log2(math.e)) for t in parts]
        elif j == T_BK:
            parts = [rope(t) for t in parts]
        elif j == T_KVA:
            gate_ref[...] = parts[2]
            parts[0] = rope(parts[0])
        for s, t in enumerate(parts):
            p_ref[:, j * TILE_N + s * LANES:j * TILE_N + (s + 1) * LANES] = t.astype(BF16)
    kt = lax.dot_general(wkt_ref[...], hb, NT_DIMS, preferred_element_type=F32)
    kt_ref[...] = (kt * (HEAD_DIM_C ** -0.5)).astype(BF16)


def _inproj(xs, mods, g_mix, w_big, w_kt, cos, sa, sb, n_ctx_tiles):
    x_specs, x_args = _stream_specs(xs, n_ctx_tiles)
    b, d = x_args[0].shape[0], x_args[0].shape[-1]
    l = sum(a.shape[1] for a in x_args)
    npad = w_big.shape[1]
    ck = w_kt.shape[0]
    grid = (b, l // TM)
    tok = lambda bi, ti: (bi, ti, 0)
    return pl.pallas_call(
        functools.partial(_inproj_kernel, n_ctx_tiles=n_ctx_tiles),
        grid=grid,
        in_specs=x_specs + [
                  pl.BlockSpec((None, None, 8, d), lambda bi, ti: (bi, jnp.where(ti >= n_ctx_tiles, 1, 0), 0, 0)),
                  pl.BlockSpec((1, d), lambda bi, ti: (0, 0)),
                  pl.BlockSpec((d, npad), lambda bi, ti: (0, 0), pipeline_mode=pl.Buffered(1)),
                  pl.BlockSpec((ck, d), lambda bi, ti: (0, 0), pipeline_mode=pl.Buffered(1)),
                  pl.BlockSpec((TM, LANES), lambda bi, ti: (ti, 0)),
                  pl.BlockSpec((TM, LANES), lambda bi, ti: (ti, 0)),
                  pl.BlockSpec((TM, LANES), lambda bi, ti: (ti, 0))],
        out_specs=[pl.BlockSpec((None, TM, npad), tok),
                   pl.BlockSpec((None, TM, LANES), tok),
                   pl.BlockSpec((None, ck, TM), lambda bi, ti: (bi, 0, ti))],
        out_shape=[jax.ShapeDtypeStruct((b, l, npad), BF16),
                   jax.ShapeDtypeStruct((b, l, LANES), F32),
                   jax.ShapeDtypeStruct((b, ck, l), BF16)],
        compiler_params=_params(("parallel", "parallel")),
        name="inproj",
    )(*x_args, mods, g_mix, w_big, w_kt, cos, sa, sb)


def _mixa_kernel(sink_ref, q_ref, kp_ref, kc_ref, kn_ref, kx_ref, o_ref, *, n_ctx_blocks, n_blocks):
    i = pl.program_id(1)
    lat = i >= n_ctx_blocks
    has_prev = jnp.logical_and(lat, i > n_ctx_blocks)
    has_next = jnp.logical_and(lat, i < n_blocks - 1)
    r = lax.broadcasted_iota(jnp.int32, (CHUNK, CHUNK), 0)
    c = lax.broadcasted_iota(jnp.int32, (CHUNK, CHUNK), 1)
    n_ctx = kx_ref.shape[0]
    valid = jnp.concatenate([
        jnp.logical_and(c >= r, has_prev),
        jnp.broadcast_to(lat, (CHUNK, CHUNK)),
        jnp.logical_and(c <= r, has_next),
        jnp.ones((CHUNK, n_ctx), jnp.bool_)], axis=1)
    kcat = jnp.concatenate([kp_ref[:, :LANES], kc_ref[:, :LANES], kn_ref[:, :LANES], kx_ref[:, :LANES]], axis=0)
    vcat = jnp.concatenate([kp_ref[:, LANES:], kc_ref[:, LANES:], kn_ref[:, LANES:], kx_ref[:, LANES:]], axis=0)
    lane = lax.broadcasted_iota(jnp.int32, (CHUNK, LANES), 1)
    low = lane < HEAD_DIM
    n_pairs = N_HEADS_A // KV_HEADS_A
    outs = []
    for gk in range(KV_HEADS_A):
        keep = low if gk == 0 else jnp.logical_not(low)
        zero = jnp.zeros((CHUNK, LANES), BF16)
        lhs = jnp.concatenate([jnp.where(keep, q_ref[:, t * LANES:(t + 1) * LANES], zero) for t in range(n_pairs)],
                              axis=0)
        s = lax.dot_general(lhs, kcat, NT_DIMS, preferred_element_type=F32)
        o_g = []
        for t in range(n_pairs):
            st = jnp.where(valid, s[t * CHUNK:(t + 1) * CHUNK], NEG)
            sk = sink_ref[gk * n_pairs + t]
            m = jnp.maximum(jnp.max(st, axis=-1, keepdims=True), sk)
            p = jnp.exp(st - m)
            den = jnp.sum(p, axis=-1, keepdims=True) + jnp.exp(sk - m)
            o_g.append(jnp.dot(p.astype(BF16), vcat, preferred_element_type=F32) / den)
        outs.append(o_g)
    for t in range(n_pairs):
        o_ref[:, t * LANES:(t + 1) * LANES] = jnp.where(low, outs[0][t], outs[1][t]).astype(BF16)


def _mixer_a(p, sink, n_ctx):
    b, l, _ = p.shape
    nb = l // CHUNK
    ncb = n_ctx // CHUNK
    kvw = 2 * LANES
    kv_col = T_KVA * TILE_N // kvw
    aq_w = N_HEADS_A * HEAD_DIM
    kern = functools.partial(_mixa_kernel, n_ctx_blocks=ncb, n_blocks=nb)
    return pl.pallas_call(
        kern,
        grid=(b, nb),
        in_specs=[pl.BlockSpec(memory_space=pltpu.SMEM),
                  pl.BlockSpec((None, CHUNK, aq_w), lambda bi, i: (bi, i, T_AQ)),
                  pl.BlockSpec((None, CHUNK, kvw), lambda bi, i: (bi, jnp.maximum(i - 1, 0), kv_col)),
                  pl.BlockSpec((None, CHUNK, kvw), lambda bi, i: (bi, i, kv_col)),
                  pl.BlockSpec((None, CHUNK, kvw), lambda bi, i: (bi, jnp.minimum(i + 1, nb - 1), kv_col)),
                  pl.BlockSpec((None, n_ctx, kvw), lambda bi, i: (bi, 0, kv_col))],
        out_specs=pl.BlockSpec((None, CHUNK, aq_w), lambda bi, i: (bi, i, 0)),
        out_shape=jax.ShapeDtypeStruct((b, l, aq_w), BF16),
        compiler_params=_params(("parallel", "parallel")),
        name="mixer_a",
    )(sink, p, p, p, p, p)


def _fold_lanes(op, acc, s):
    for t in range(s.shape[1] // LANES):
        acc = op(acc, s[:, t * LANES:(t + 1) * LANES])
    return acc


def _mixb_kernel(lam_ref, gd_ref, k_ref, v_ref, *rest, lam_init, chunks):
    q_refs, (o_ref, s_scr, va_scr) = rest[:-3], rest[-3:]

    @pl.when(pl.program_id(2) == 0)
    def _():
        n_keys = v_ref.shape[0]
        va_scr[:, :LANES] = v_ref[...]
        va_scr[:, LANES:] = (lax.broadcasted_iota(jnp.int32, (n_keys, LANES), 1) == 0).astype(BF16)

    lp = lam_ref[...]
    lam = (jnp.exp(jnp.sum(lp[0:1] * lp[1:2], axis=-1, keepdims=True))
           - jnp.exp(jnp.sum(lp[2:3] * lp[3:4], axis=-1, keepdims=True)) + lam_init)
    q = jnp.concatenate([qr[...] for qr in q_refs], axis=0)
    tq = q.shape[0]
    lane = lax.broadcasted_iota(jnp.int32, (tq, LANES), 1)
    zero = jnp.zeros_like(q)
    qs = (jnp.where(lane < HEAD_DIM, q, zero), jnp.where(lane >= HEAD_DIM, q, zero))
    rows = [slice(mi * tq, (mi + 1) * tq) for mi in range(2)]
    mrun = [jnp.full((tq, LANES), NEG, F32) for _ in range(2)]
    for off, sz in chunks:
        for mi in range(2):
            s_scr[rows[mi], off:off + sz] = lax.dot_general(qs[mi], k_ref[off:off + sz, :], NT_DIMS,
                                                            preferred_element_type=F32)
            mrun[mi] = _fold_lanes(jnp.maximum, mrun[mi], s_scr[rows[mi], off:off + sz])
    m = [jnp.max(mr, axis=-1, keepdims=True) for mr in mrun]
    acc = [jnp.zeros((tq, 2 * LANES), F32) for _ in range(2)]
    for off, sz in chunks:
        for mi in range(2):
            pr = jnp.exp2(s_scr[rows[mi], off:off + sz] - m[mi])
            acc[mi] = acc[mi] + jnp.dot(pr.astype(BF16), va_scr[off:off + sz, :], preferred_element_type=F32)
    outs = [a[:, :LANES] / a[:, LANES:LANES + 1] for a in acc]
    o = outs[0] - lam * outs[1]
    o_ref[...] = (_rms(o, gd_ref[...]) * (1.0 - lam_init)).astype(BF16)


def _mixer_b(p, lam_params, g_diff, lam_init, n_ctx):
    b, l, _ = p.shape
    kl = min(MIXB_KEYS, l - n_ctx)
    tq_lat = min(MIXB_QUERIES, l - n_ctx)
    assert (l - n_ctx) % kl == 0 and (l - n_ctx) % tq_lat == 0 and tq_lat % TM == 0 and n_ctx % TM == 0
    q0 = T_BQ * TILE_N // LANES
    k0 = T_BK * TILE_N // LANES
    v0 = T_BV * TILE_N // LANES
    ctx_chunks = ((0, n_ctx),)
    all_chunks = ctx_chunks + tuple((n_ctx + c * kl, kl) for c in range((l - n_ctx) // kl))

    def call(chunks, n_keys, tq, q_tiles, first_row):
        nq = tq // TM
        q_spec = lambda part: pl.BlockSpec((None, TM, LANES),
                                           lambda bi, h, qi: (bi, first_row // TM + qi * nq + part, q0 + h))
        kern = functools.partial(_mixb_kernel, lam_init=lam_init, chunks=chunks)
        return pl.pallas_call(
            kern,
            grid=(b, N_HEADS_B, q_tiles),
            in_specs=[pl.BlockSpec((4, HEAD_DIM), lambda bi, h, qi: (0, 0)),
                      pl.BlockSpec((1, LANES), lambda bi, h, qi: (0, 0)),
                      pl.BlockSpec((None, n_keys, LANES), lambda bi, h, qi: (bi, 0, k0 + h)),
                      pl.BlockSpec((None, n_keys, LANES), lambda bi, h, qi: (bi, 0, v0 + h))]
                     + [q_spec(part) for part in range(nq)],
            out_specs=pl.BlockSpec((None, tq, LANES), lambda bi, h, qi: (bi, qi, h)),
            out_shape=jax.ShapeDtypeStruct((b, q_tiles * tq, N_HEADS_B * LANES), BF16),
            scratch_shapes=[pltpu.VMEM((2 * tq, n_keys), F32), pltpu.VMEM((n_keys, 2 * LANES), BF16)],
            compiler_params=_params(("parallel", "parallel", "arbitrary")),
            name="mixer_b",
        )(lam_params, g_diff, p, p, *([p] * nq))

    tq_ctx = min(tq_lat, n_ctx)
    return (call(ctx_chunks, n_ctx, tq_ctx, n_ctx // tq_ctx, 0),
            call(all_chunks, l, tq_lat, (l - n_ctx) // tq_lat, n_ctx))


def _log_sigmoid(x):
    return jnp.minimum(x, 0.0) - jnp.log1p(jnp.exp(-jnp.abs(x)))


def _mlstm_kernel(qf_ref, ktf_ref, vf_ref, gcf_ref, grf_ref, qb_ref, ktb_ref, vb_ref, gcb_ref, grb_ref,
                  bc_ref, br_ref, of_ref, ob_ref, s_scr, m_scr):
    @pl.when(pl.program_id(1) == 0)
    def _():
        s_scr[...] = jnp.zeros_like(s_scr)
        m_scr[...] = jnp.zeros_like(m_scr)

    r = lax.broadcasted_iota(jnp.int32, (CHUNK, CHUNK), 0)
    cc = lax.broadcasted_iota(jnp.int32, (CHUNK, CHUNK), 1)
    lane = lax.broadcasted_iota(jnp.int32, (CHUNK, LANES), 1)
    ones_col = (lane == 0).astype(BF16)
    nh = N_HEADS_C
    tris = (r >= cc, r <= cc)
    refs = ((qf_ref, ktf_ref, vf_ref, gcf_ref, grf_ref, of_ref), (qb_ref, ktb_ref, vb_ref, gcb_ref, grb_ref, ob_ref))
    grow, bcum_col, bcum_row, tot_row = [], [], [], []
    for di in range(2):
        trif = tris[di].astype(F32)
        gcol = refs[di][3][...] + bc_ref[...]
        grow.append(refs[di][4][...] + br_ref[...])
        lf_col = _log_sigmoid(gcol)
        lf_row = _log_sigmoid(grow[di])
        bcum_col.append(jnp.dot(trif, lf_col, precision=HIGHEST, preferred_element_type=F32))
        bcum_row.append(lax.dot_general(lf_row, trif, NT_DIMS, precision=HIGHEST, preferred_element_type=F32))
        tot_row.append(jnp.sum(lf_row, axis=-1, keepdims=True))
    ch = [(di, h) for di in range(2) for h in range(nh)]
    gi = {c: (2 * c[0]) * nh + c[1] for c in ch}
    gf = {c: (2 * c[0] + 1) * nh + c[1] for c in ch}
    ic_row = {c: grow[c[0]][gi[c]:gi[c] + 1, :] for c in ch}
    b_col = {c: bcum_col[c[0]][:, gf[c]:gf[c] + 1] for c in ch}
    b_row = {c: bcum_row[c[0]][gf[c]:gf[c] + 1, :] for c in ch}
    total = {c: tot_row[c[0]][gf[c]:gf[c] + 1, :] for c in ch}
    m_st = {c: m_scr[c[0], c[1], 0:1, 0:1] for c in ch}
    qh = {c: refs[c[0]][0][:, c[1] * LANES:(c[1] + 1) * LANES] for c in ch}
    kth = {c: refs[c[0]][1][c[1] * LANES:(c[1] + 1) * LANES, :] for c in ch}
    vaug = {c: jnp.concatenate([refs[c[0]][2][:, c[1] * LANES:(c[1] + 1) * LANES], ones_col], axis=1) for c in ch}
    st = {c: s_scr[c[0], c[1]] for c in ch}
    qk = {c: jnp.dot(qh[c], kth[c], preferred_element_type=F32) for c in ch}
    cross = {c: jnp.dot(qh[c], st[c].astype(BF16), preferred_element_type=F32) for c in ch}
    gs_row = {c: total[c] - b_row[c] + ic_row[c] for c in ch}
    m_new = {c: jnp.maximum(total[c] + m_st[c], jnp.max(gs_row[c], axis=-1, keepdims=True)) for c in ch}
    wkt = {c: (kth[c].astype(F32) * jnp.exp(gs_row[c] - m_new[c])).astype(BF16) for c in ch}
    upd = {c: jnp.dot(wkt[c], vaug[c], preferred_element_type=F32) for c in ch}
    dm = {c: jnp.where(tris[c[0]], b_col[c] - b_row[c] + ic_row[c], NEG) for c in ch}
    inter = {c: b_col[c] + m_st[c] for c in ch}
    m_t = {c: jnp.maximum(inter[c], jnp.max(dm[c], axis=-1, keepdims=True)) for c in ch}
    sc = {c: (qk[c] * jnp.exp(dm[c] - m_t[c])).astype(BF16) for c in ch}
    intra = {c: jnp.dot(sc[c], vaug[c], preferred_element_type=F32) for c in ch}
    for c in ch:
        di, h = c
        nd = intra[c] + jnp.exp(inter[c] - m_t[c]) * cross[c]
        den = nd[:, LANES:LANES + 1]
        h_out = nd[:, :LANES] / jnp.maximum(jnp.abs(den), jnp.exp(-m_t[c]))
        refs[di][5][:, h * LANES:(h + 1) * LANES] = h_out.astype(BF16)
        s_scr[di, h] = jnp.exp(total[c] + m_st[c] - m_new[c]) * st[c] + upd[c]
        m_scr[di, h] = jnp.broadcast_to(m_new[c], m_scr.shape[2:])


def _mlstm(p, kt, gates, gates_t, bias_row, bias_col, n_ctx):
    b, l, _ = p.shape
    nc = l // CHUNK
    ncc = n_ctx // CHUNK
    cw = N_HEADS_C * HEAD_DIM_C

    def rev(c):
        return jnp.where(c < ncc, ncc - 1 - c, nc + ncc - 1 - c)

    def specs(ch):
        return [pl.BlockSpec((None, CHUNK, cw), lambda bi, c: (bi, ch(c), T_CQ)),
                pl.BlockSpec((None, cw, CHUNK), lambda bi, c: (bi, 0, ch(c))),
                pl.BlockSpec((None, CHUNK, cw), lambda bi, c: (bi, ch(c), T_CV)),
                pl.BlockSpec((None, CHUNK, LANES), lambda bi, c: (bi, ch(c), 0)),
                pl.BlockSpec((None, 16, CHUNK), lambda bi, c: (bi, 0, ch(c)))]

    fwd = lambda c: c
    out = jax.ShapeDtypeStruct((b, l, cw), BF16)
    return pl.pallas_call(
        _mlstm_kernel,
        grid=(b, nc),
        in_specs=specs(fwd) + specs(rev) + [pl.BlockSpec((1, LANES), lambda bi, c: (0, 0)),
                                            pl.BlockSpec((16, LANES), lambda bi, c: (0, 0))],
        out_specs=[pl.BlockSpec((None, CHUNK, cw), lambda bi, c: (bi, c, 0)),
                   pl.BlockSpec((None, CHUNK, cw), lambda bi, c: (bi, rev(c), 0))],
        out_shape=[out, out],
        scratch_shapes=[pltpu.VMEM((2, N_HEADS_C, HEAD_DIM_C, 2 * LANES), F32),
                        pltpu.VMEM((2, N_HEADS_C, 8, LANES), F32)],
        compiler_params=_params(("parallel", "arbitrary")),
        name="mlstm",
    )(p, kt, p, gates, gates_t, p, kt, p, gates, gates_t, bias_row, bias_col)


def _merge_kernel(*refs, n_ctx_tiles, t0):
    x_refs, (mod_ref, oa_ref, obc_ref, obl_ref, hf_ref, hb_ref, co_ref, gt_ref, gm_ref,
             wa_ref, wb_ref, wc_ref, wo_ref, xo_ref) = refs[:-14], refs[-14:]
    d = xo_ref.shape[-1]
    is_ctx = pl.program_id(1) + t0 < n_ctx_tiles
    ob = jnp.where(is_ctx, obc_ref[...], obl_ref[...])
    hs = hf_ref[...].astype(F32) + hb_ref[...].astype(F32)
    co = co_ref[...].astype(F32)
    gm = gm_ref[...]
    oc = []
    for h in range(N_HEADS_C):
        sl = slice(h * LANES, (h + 1) * LANES)
        oc.append((_rms(hs[:, sl], gm[:, sl]) * _sigmoid(co[:, sl])).astype(BF16))
    oc = jnp.concatenate(oc, axis=1)
    y = (_sigmoid(gt_ref[:, 0:d].astype(F32)) * jnp.dot(oa_ref[...], wa_ref[...], preferred_element_type=F32)
         + _sigmoid(gt_ref[:, d:2 * d].astype(F32)) * jnp.dot(ob, wb_ref[...], preferred_element_type=F32)
         + _sigmoid(gt_ref[:, 2 * d:3 * d].astype(F32)) * jnp.dot(oc, wc_ref[...], preferred_element_type=F32))
    out = jnp.dot(y.astype(BF16), wo_ref[...], preferred_element_type=F32)
    xo_ref[...] = _read_stream(x_refs, is_ctx) + mod_ref[2:3, :] * out


def _merge(xs, mods, oa, ob_ctx, ob_lat, hf, hb, p, g_mlstm, wa, wb, wc, wo, n_ctx_tiles, latent_only):
    t0 = n_ctx_tiles if latent_only else 0
    x_specs, x_args = _stream_specs(xs, n_ctx_tiles, t0)
    b, d = x_args[0].shape[0], x_args[0].shape[-1]
    n_tiles = oa.shape[1] // TM - t0
    tok = lambda bi, ti: (bi, ti + t0, 0)
    cw = N_HEADS_C * HEAD_DIM_C
    const = lambda bi, ti: (0, 0)
    return pl.pallas_call(
        functools.partial(_merge_kernel, n_ctx_tiles=n_ctx_tiles, t0=t0),
        grid=(b, n_tiles),
        in_specs=x_specs + [
                  pl.BlockSpec((None, None, 8, d), lambda bi, ti: (bi, jnp.where(ti + t0 >= n_ctx_tiles, 1, 0), 0, 0)),
                  pl.BlockSpec((None, TM, oa.shape[-1]), tok),
                  pl.BlockSpec((None, TM, ob_ctx.shape[-1]),
                               lambda bi, ti: (bi, jnp.minimum(ti + t0, n_ctx_tiles - 1), 0)),
                  pl.BlockSpec((None, TM, ob_lat.shape[-1]),
                               lambda bi, ti: (bi, jnp.maximum(ti + t0 - n_ctx_tiles, 0), 0)),
                  pl.BlockSpec((None, TM, cw), tok),
                  pl.BlockSpec((None, TM, cw), tok),
                  pl.BlockSpec((None, TM, cw), lambda bi, ti: (bi, ti + t0, T_CO)),
                  pl.BlockSpec((None, TM, 3 * d), lambda bi, ti: (bi, ti + t0, T_GT * TILE_N // (3 * d))),
                  pl.BlockSpec((1, cw), const),
                  pl.BlockSpec(wa.shape, const), pl.BlockSpec(wb.shape, const),
                  pl.BlockSpec(wc.shape, const), pl.BlockSpec(wo.shape, const)],
        out_specs=pl.BlockSpec((None, TM, d), lambda bi, ti: (bi, ti, 0)),
        out_shape=jax.ShapeDtypeStruct((b, n_tiles * TM, d), F32),
        compiler_params=_params(("parallel", "parallel")),
        name="merge",
    )(*x_args, mods, oa, ob_ctx, ob_lat, hf, hb, p, p, g_mlstm, wa, wb, wc, wo)


def _router_kernel(x_ref, mod_ref, g_ref, wrt_ref, br_ref, h_ref, idx_ref, wt_ref, cnt_ref):
    h = _rms(x_ref[...], g_ref[...]) * (1.0 + mod_ref[4:5, :]) + mod_ref[3:4, :]
    h_ref[...] = h.astype(BF16)
    tm = h.shape[0]
    per = N_EXPERTS // N_GROUPS
    lt = lax.dot_general(wrt_ref[...], h, NT_DIMS, precision=HIGHEST, preferred_element_type=F32)
    s = _sigmoid(lt)
    sel = s + br_ref[...]
    ninf = -jnp.inf
    sel3 = sel.reshape(N_GROUPS, per, tm)
    eidx = lax.broadcasted_iota(jnp.int32, (N_GROUPS, per, tm), 1)
    m1 = jnp.max(sel3, axis=1, keepdims=True)
    first = jnp.min(jnp.where(sel3 == m1, eidx, per), axis=1, keepdims=True)
    m2 = jnp.max(jnp.where(eidx == first, ninf, sel3), axis=1, keepdims=True)
    gscore = (m1 + m2).reshape(N_GROUPS, tm)
    gidx = lax.broadcasted_iota(jnp.int32, (N_GROUPS, tm), 0)
    gmask = jnp.zeros((N_GROUPS, tm), jnp.bool_)
    cur = gscore
    for _ in range(TOPK_GROUPS):
        mx = jnp.max(cur, axis=0, keepdims=True)
        hit = gidx == jnp.min(jnp.where(cur == mx, gidx, N_GROUPS), axis=0, keepdims=True)
        gmask = jnp.logical_or(gmask, hit)
        cur = jnp.where(hit, ninf, cur)
    cur = jnp.where(gmask.reshape(N_GROUPS, 1, tm), sel3, ninf).reshape(N_EXPERTS, tm)
    eid = lax.broadcasted_iota(jnp.int32, (N_EXPERTS, tm), 0)
    ids, ws = [], []
    chosen = jnp.zeros((N_EXPERTS, tm), F32)
    for _ in range(TOP_K):
        mx = jnp.max(cur, axis=0, keepdims=True)
        pick = jnp.min(jnp.where(cur == mx, eid, N_EXPERTS), axis=0, keepdims=True)
        hit = eid == pick
        ids.append(pick)
        ws.append(jnp.sum(jnp.where(hit, s, 0.0), axis=0, keepdims=True))
        cur = jnp.where(hit, ninf, cur)
        chosen = chosen + hit.astype(F32)
    wsum = ws[0]
    for w in ws[1:]:
        wsum = wsum + w
    idx_ref[...] = jnp.concatenate(ids, axis=0)
    wt_ref[...] = jnp.concatenate([w / wsum * ROUTED_SCALE for w in ws], axis=0)
    cnt_ref[...] = jnp.sum(chosen, axis=1, keepdims=True).astype(jnp.int32)


def _router(xs, mods, g_ffn, w_router_t, b_router, n_ctx_tiles):
    b, l, d = xs.shape
    tok = lambda bi, ti: (bi, ti, 0)
    const = lambda bi, ti: (0, 0)
    return pl.pallas_call(
        _router_kernel,
        grid=(b, l // TM),
        in_specs=[pl.BlockSpec((None, TM, d), tok),
                  pl.BlockSpec((None, None, 8, d), lambda bi, ti: (bi, jnp.where(ti >= n_ctx_tiles, 1, 0), 0, 0)),
                  pl.BlockSpec((1, d), const),
                  pl.BlockSpec((N_EXPERTS, d), const),
                  pl.BlockSpec((N_EXPERTS, 1), const)],
        out_specs=[pl.BlockSpec((None, TM, d), tok),
                   pl.BlockSpec((None, TOP_K, TM), lambda bi, ti: (bi, 0, ti)),
                   pl.BlockSpec((None, TOP_K, TM), lambda bi, ti: (bi, 0, ti)),
                   pl.BlockSpec((None, None, N_EXPERTS, 1), lambda bi, ti: (bi, ti, 0, 0))],
        out_shape=[jax.ShapeDtypeStruct((b, l, d), BF16),
                   jax.ShapeDtypeStruct((b, TOP_K, l), jnp.int32),
                   jax.ShapeDtypeStruct((b, TOP_K, l), F32),
                   jax.ShapeDtypeStruct((b, l // TM, N_EXPERTS, 1), jnp.int32)],
        compiler_params=_params(("parallel", "parallel")),
        name="router",
    )(xs, mods, g_ffn, w_router_t, b_router)


def _sort_kernel(idx_ref, off_ref, h_ref, posl_ref, ts_ref):
    tm = h_ref.shape[0]
    idx = idx_ref[...]
    eid = lax.broadcasted_iota(jnp.int32, (N_EXPERTS, tm), 0)
    hits = [eid == idx[k:k + 1, :] for k in range(TOP_K)]
    chosen = hits[0].astype(BF16)
    for hk in hits[1:]:
        chosen = chosen + hk.astype(BF16)
    r = lax.broadcasted_iota(jnp.int32, (tm, tm), 0)
    c = lax.broadcasted_iota(jnp.int32, (tm, tm), 1)
    before = (r < c).astype(BF16)
    rank = jnp.dot(chosen, before, preferred_element_type=F32)
    slot = rank.astype(jnp.int32) + off_ref[...]
    posl = jnp.concatenate([jnp.sum(jnp.where(hk, slot, 0), axis=0, keepdims=True) for hk in hits], axis=0)
    posl_ref[...] = posl
    hb = h_ref[...]
    for rb in range(ts_ref.shape[0] // SORT_CHUNK):
        rows = lax.broadcasted_iota(jnp.int32, (SORT_CHUNK, tm), 0) + rb * SORT_CHUNK
        sel = rows == posl[0:1, :]
        for k in range(1, TOP_K):
            sel = jnp.logical_or(sel, rows == posl[k:k + 1, :])
        onehot = jnp.where(sel, 1.0, 0.0).astype(BF16)
        ts = jnp.dot(onehot, hb, preferred_element_type=F32)
        ts_ref[rb * SORT_CHUNK:(rb + 1) * SORT_CHUNK, :] = ts.astype(BF16)


def _sort_rows(idx_t, off, h_flat):
    b, k, l = idx_t.shape
    n, d = h_flat.shape
    nt = l // TM
    return pl.pallas_call(
        _sort_kernel,
        grid=(b * nt,),
        in_specs=[pl.BlockSpec((None, k, TM), lambda i: (i // nt, 0, i % nt)),
                  pl.BlockSpec((None, N_EXPERTS, 1), lambda i: (i, 0, 0)),
                  pl.BlockSpec((TM, d), lambda i: (i, 0))],
        out_specs=[pl.BlockSpec((None, k, TM), lambda i: (i // nt, 0, i % nt)),
                   pl.BlockSpec((None, SORT_ROWS, d), lambda i: (i, 0, 0))],
        out_shape=[jax.ShapeDtypeStruct((b, k, l), jnp.int32),
                   jax.ShapeDtypeStruct((b * nt, SORT_ROWS, d), BF16)],
        compiler_params=_params(("parallel",)),
        name="sort_rows",
    )(idx_t, off, h_flat)


SLAB_SIZES = tuple(1 << s for s in range((EXPERT_ROWS // ROW_GROUP).bit_length()))


def _expert_kernel(be_ref, nu_ref, cov_ref, grp_ref, ts_in, wg_ref, wu_ref, wd_ref, ts_out,
                   xbuf, ybuf, wg_scr, wu_scr, wd_scr, sem_g, sem_s):
    j = pl.program_id(0)
    nu = nu_ref[0]
    n_grp = xbuf.shape[1]
    blk = n_grp * ROW_GROUP
    slot = j % 2

    def for_groups(bj, fn):
        for i in range(n_grp):
            sg = grp_ref[bj * n_grp + i]

            @pl.when(sg >= 0)
            def _():
                fn(sg, i)

    def gather(bj, s):
        xbuf[s] = jnp.zeros(xbuf.shape[1:], xbuf.dtype)
        for_groups(bj, lambda src, i: pltpu.make_async_copy(ts_in.at[src], xbuf.at[s, i], sem_g.at[s]).start())

    def scatter(bj, s):
        for_groups(bj, lambda dst, i: pltpu.make_async_copy(ybuf.at[s, i], ts_out.at[dst], sem_s.at[s]).start())

    def wait_rows(bj, buf, sem, s):
        groups = cov_ref[bj] // ROW_GROUP
        for sz in SLAB_SIZES:
            @pl.when((groups & sz) != 0)
            def _():
                pltpu.make_async_copy(ts_in.at[pl.ds(0, sz)], buf.at[s, pl.ds(0, sz)], sem.at[s]).wait()

    @pl.when(j == 0)
    def _():
        gather(0, 0)

    @pl.when(j + 1 < nu)
    def _():
        gather(j + 1, 1 - slot)

    @pl.when(jnp.logical_and(j >= 2, j - 2 < nu))
    def _():
        wait_rows(j - 2, ybuf, sem_s, slot)

    @pl.when(jnp.logical_and(j < nu, jnp.logical_or(j == 0, be_ref[j] != be_ref[jnp.maximum(j - 1, 0)])))
    def _():
        wg_scr[...] = wg_ref[...].astype(BF16)
        wu_scr[...] = wu_ref[...].astype(BF16)
        wd_scr[...] = wd_ref[...].astype(BF16)

    @pl.when(j < nu)
    def _():
        wait_rows(j, xbuf, sem_g, slot)
        x = xbuf[slot].reshape(blk, xbuf.shape[-1])
        g = jnp.dot(x, wg_scr[...], preferred_element_type=F32)
        u = jnp.dot(x, wu_scr[...], preferred_element_type=F32)
        a = (g * _sigmoid(g) * u).astype(BF16)
        y = jnp.dot(a, wd_scr[...], preferred_element_type=F32)
        ybuf[slot] = y.astype(BF16).reshape(ybuf.shape[1:])
        scatter(j, slot)


def _experts(plan, tiles, wg, wu, wd, layer):
    nt, rows, w = tiles.shape
    blk = EXPERT_ROWS
    d, de = wg.shape[2:]
    n_blocks = plan["block_e"].shape[0]
    tables = (plan["block_e"], plan["n_used"], plan["blk_rows"], plan["blk_groups"])
    wspec = lambda shape: pl.BlockSpec((None, None) + shape, lambda i, be, *_: (layer, be[i], 0, 0))
    grid_spec = pltpu.PrefetchScalarGridSpec(
        num_scalar_prefetch=len(tables),
        grid=(n_blocks,),
        in_specs=[pl.BlockSpec(memory_space=pl.ANY), wspec((d, de)), wspec((d, de)), wspec((de, d))],
        out_specs=pl.BlockSpec(memory_space=pl.ANY),
        scratch_shapes=[pltpu.VMEM((2, blk // ROW_GROUP, ROW_GROUP, w), tiles.dtype)] * 2 + [
                        pltpu.VMEM((d, de), BF16), pltpu.VMEM((d, de), BF16), pltpu.VMEM((de, d), BF16),
                        pltpu.SemaphoreType.DMA((2,)), pltpu.SemaphoreType.DMA((2,))])
    return pl.pallas_call(
        _expert_kernel,
        grid_spec=grid_spec,
        out_shape=jax.ShapeDtypeStruct((nt * rows // ROW_GROUP, ROW_GROUP, w), tiles.dtype),
        input_output_aliases={len(tables): 0},
        compiler_params=_params(("arbitrary",)),
        name="experts",
    )(*tables, tiles.reshape(nt * rows // ROW_GROUP, ROW_GROUP, w), wg, wu, wd).reshape(tiles.shape)


def _combine_kernel(ts_ref, posl_ref, w_ref, x_ref, h_ref, mod_ref, wsg_ref, wsu_ref, wsd_ref, *rest):
    tm = x_ref.shape[0]
    hb = h_ref[...]
    g = jnp.dot(hb, wsg_ref[...], preferred_element_type=F32)
    u = jnp.dot(hb, wsu_ref[...], preferred_element_type=F32)
    acc = jnp.dot((g * _sigmoid(g) * u).astype(BF16), wsd_ref[...], preferred_element_type=F32)
    posl = posl_ref[...]
    w = w_ref[...]
    for rb in range(ts_ref.shape[0] // COMBINE_CHUNK):
        cols = lax.broadcasted_iota(jnp.int32, (tm, COMBINE_CHUNK), 1) + rb * COMBINE_CHUNK
        wm = jnp.zeros((tm, COMBINE_CHUNK), F32)
        for k in range(TOP_K):
            wm = jnp.where(cols == posl[:, k:k + 1], w[:, k:k + 1], wm)
        acc = acc + jnp.dot(wm.astype(BF16), ts_ref[rb * COMBINE_CHUNK:(rb + 1) * COMBINE_CHUNK, :],
                            preferred_element_type=F32)
    out = x_ref[...] + mod_ref[5:6, :] * acc
    rest[-1][...] = _rms(out, rest[0][...]) if len(rest) == 2 else out


def _combine(tiles, posl_tm, wts, x_flat, h_flat, mods, wsg, wsu, wsd, tiles_per_sample, n_ctx_tiles, g_final=None):
    n, d = x_flat.shape
    ds_ = wsg.shape[-1]
    tok = lambda i: (i, 0)
    const = lambda i: (0, 0)

    def mod_idx(i):
        return (i // tiles_per_sample, jnp.where(i % tiles_per_sample >= n_ctx_tiles, 1, 0), 0, 0)

    in_extra, args_extra, out_rows, out_idx = [], [], n, tok
    if g_final is not None:
        lat_tiles = tiles_per_sample - n_ctx_tiles
        in_extra, args_extra = [pl.BlockSpec((1, d), const)], [g_final]
        out_rows = n // TM // tiles_per_sample * lat_tiles * TM
        out_idx = lambda i: (i // tiles_per_sample * lat_tiles + jnp.maximum(i % tiles_per_sample - n_ctx_tiles, 0), 0)
    return pl.pallas_call(
        _combine_kernel,
        grid=(n // TM,),
        in_specs=[pl.BlockSpec((None,) + tiles.shape[1:], lambda i: (i, 0, 0)),
                  pl.BlockSpec((TM, TOP_K), tok),
                  pl.BlockSpec((TM, TOP_K), tok),
                  pl.BlockSpec((TM, d), tok),
                  pl.BlockSpec((TM, d), tok),
                  pl.BlockSpec((None, None, 8, d), mod_idx),
                  pl.BlockSpec((d, ds_), const), pl.BlockSpec((d, ds_), const), pl.BlockSpec((ds_, d), const)]
                 + in_extra,
        out_specs=pl.BlockSpec((TM, d), out_idx),
        out_shape=jax.ShapeDtypeStruct((out_rows, d), F32),
        compiler_params=_params(("arbitrary",)),
        name="combine",
    )(tiles, posl_tm, wts, x_flat, h_flat, mods, wsg, wsu, wsd, *args_extra)


def _moe_plan(cnt, n_assign, blk):
    nt = cnt.shape[0]
    run = (cnt + ROW_GROUP - 1) // ROW_GROUP * ROW_GROUP
    tile_off = jnp.cumsum(run, axis=1) - run
    tot = jnp.sum(run, axis=0)
    padded = (tot + blk - 1) // blk * blk
    pad_end = jnp.cumsum(padded)
    pad_start = pad_end - padded
    n_blocks = -(-(n_assign + nt * N_EXPERTS * (ROW_GROUP - 1)) // blk) + N_EXPERTS + 2
    first_row = jnp.arange(n_blocks, dtype=jnp.int32) * blk
    count = lambda m: jnp.sum(m.astype(jnp.int32), axis=1)
    block_e = jnp.minimum(count(pad_end[None, :] <= first_row[:, None]), N_EXPERTS - 1)
    i32 = lambda a: a.astype(jnp.int32)
    run_end = jnp.cumsum(run, axis=0).T[block_e][:, None, :]
    run_start = run_end - run.T[block_e][:, None, :]
    run_src = (jnp.arange(nt) * SORT_ROWS)[None, :] + tile_off.T[block_e]
    local = (first_row - pad_start[block_e])[:, None] + jnp.arange(0, blk, ROW_GROUP)[None, :]
    hit = jnp.logical_and(run_start <= local[:, :, None], local[:, :, None] < run_end)
    src_row = jnp.sum(jnp.where(hit, run_src[:, None, :] + local[:, :, None] - run_start, 0), axis=2)
    blk_groups = jnp.where(jnp.any(hit, axis=2), src_row // ROW_GROUP, -1).reshape(-1)
    return dict(
        tile_off=i32(tile_off)[:, :, None],
        blk_groups=i32(blk_groups),
        blk_rows=i32(jnp.clip((pad_start + tot)[block_e] - first_row, 0, blk)),
        block_e=i32(block_e), n_used=i32(pad_end[-1] // blk).reshape(1))


def _rope_tables(s_len, n_ctx):
    rows = s_len // GRID_W
    row = jnp.repeat(jnp.arange(rows), GRID_W).astype(F32)
    col = jnp.tile(jnp.arange(GRID_W), rows).astype(F32)
    quarter = HEAD_DIM // 4
    inv = 1.0 / (ROPE_BASE ** (jnp.arange(quarter, dtype=F32) / quarter))
    ar, ac = row[:, None] * inv, col[:, None] * inv
    cr, sr, cc, sc = jnp.cos(ar), jnp.sin(ar), jnp.cos(ac), jnp.sin(ac)
    z = jnp.zeros_like(sr)
    cos = jnp.concatenate([cr, cr, cc, cc], axis=1)
    sa = jnp.concatenate([z, sr, z, sc], axis=1)
    sb = jnp.concatenate([-sr, z, -sc, z], axis=1)
    rep = LANES // HEAD_DIM

    def full(t, fill):
        t = jnp.tile(t, (1, rep))
        return jnp.concatenate([jnp.full((n_ctx, LANES), fill, F32), t], axis=0)

    return full(cos, 1.0), full(sa, 0.0), full(sb, 0.0)


def _pair_perm():
    g = N_HEADS_A // KV_HEADS_A
    heads = [h for t in range(g) for h in (t, t + g)]
    return jnp.concatenate([jnp.arange(h * HEAD_DIM, (h + 1) * HEAD_DIM) for h in heads])


def _split_w_in(w):
    a_q, a_kv = N_HEADS_A * HEAD_DIM, KV_HEADS_A * HEAD_DIM
    b_w = N_HEADS_B * 2 * HEAD_DIM
    c_w = N_HEADS_C * HEAD_DIM_C
    sizes = (a_q, a_kv, a_kv, b_w, b_w, b_w, c_w, c_w, c_w, c_w, 4 * N_HEADS_C, w.shape[1])
    parts, start = [], 0
    for sz in sizes[:-1]:
        parts.append(w[:, start:start + sz])
        start += sz
    parts.append(w[:, start:])
    return parts


def _pack_w_in(w):
    d = w.shape[0]
    aq, ak, av, bq, bk, bv, cq, ck, cv, co, cg, gt = _split_w_in(w)
    pad = lambda n: jnp.zeros((d, n), w.dtype)
    kva = jnp.concatenate([ak, av, cg, pad(TILE_N - ak.shape[1] - av.shape[1] - cg.shape[1])], axis=1)
    big = jnp.concatenate([aq[:, _pair_perm()], bq, bk, kva, bv, co, gt, cq, cv], axis=1)
    return big.astype(BF16), ck.T.astype(BF16)


def kernel(x, c, ctx, c_ctx, w_mod, b_mod, g_mix, g_ffn, w_in, b_gate, sink, lam_q1, lam_k1, lam_q2, lam_k2,
           g_diff, g_mlstm, w_a, w_b, w_c, w_out, w_router, b_router, w_exp_gate, w_exp_up, w_exp_down,
           w_sh_gate, w_sh_up, w_sh_down, g_final):
    b, s_len, d = x.shape
    n_ctx = ctx.shape[1]
    l = n_ctx + s_len
    depth = w_mod.shape[0]
    n_ctx_tiles = n_ctx // TM
    assert n_ctx % TM == 0 and s_len % TM == 0 and d % LANES == 0 and s_len % GRID_W == 0

    xs = (ctx, x)
    cos, sa, sb = _rope_tables(s_len, n_ctx)

    rows_c = 16
    cs = jnp.concatenate([c, c_ctx[None], jnp.zeros((rows_c - b - 1, d), F32)], axis=0)
    mod_all = _mod_vectors(cs, w_mod, b_mod).reshape(depth, rows_c, N_MOD, d)
    mod_all = jnp.pad(mod_all, ((0, 0), (0, 0), (0, 8 - N_MOD), (0, 0)))

    perm = _pair_perm()
    expert_w = (w_exp_gate, w_exp_up, w_exp_down)
    for layer in range(depth):
        lam_init = 0.8 - 0.6 * math.exp(-0.3 * layer)
        mods = jnp.stack([jnp.broadcast_to(mod_all[layer, b], (b, 8, d)), mod_all[layer, :b]], axis=1)
        w_big, w_kt = _pack_w_in(w_in[layer])
        p, gates, kt = _inproj(xs, mods, g_mix[layer][None], w_big, w_kt, cos, sa, sb, n_ctx_tiles)

        oa = _mixer_a(p, sink[layer], n_ctx)
        lam_params = jnp.stack([lam_q1[layer], lam_k1[layer], lam_q2[layer], lam_k2[layer]])
        ob_ctx, ob_lat = _mixer_b(p, lam_params, g_diff[layer][None], lam_init, n_ctx)

        bias = b_gate[layer].reshape(-1)
        bias_row = jnp.pad(bias, (0, LANES - bias.shape[0]))[None]
        bias_col = jnp.broadcast_to(bias[:, None], (bias.shape[0], LANES))
        gates_t = jnp.transpose(gates[:, :, :bias.shape[0]], (0, 2, 1))
        hf, hb = _mlstm(p, kt, gates, gates_t, bias_row, bias_col, n_ctx)

        last = layer == depth - 1
        xs = _merge(xs, mods, oa, ob_ctx, ob_lat, hf, hb, p, g_mlstm[layer][None],
                    w_a[layer][perm].astype(BF16), w_b[layer].astype(BF16), w_c[layer].astype(BF16),
                    w_out[layer].astype(BF16), n_ctx_tiles, last)

        lf, ctx_tiles = (s_len, 0) if last else (l, n_ctx_tiles)
        h, idx_t, wt_t, cnt = _router(xs, mods, g_ffn[layer][None], w_router[layer].T, b_router[layer][:, None],
                                      ctx_tiles)
        plan = _moe_plan(cnt.reshape(-1, N_EXPERTS), b * lf * TOP_K, EXPERT_ROWS)
        h_flat = h.reshape(b * lf, d)
        posl, tiles = _sort_rows(idx_t, plan["tile_off"], h_flat)
        tiles = _experts(plan, tiles, *expert_w, layer)
        to_rows = lambda a: jnp.transpose(a, (0, 2, 1)).reshape(b * lf, TOP_K)
        xs = _combine(tiles, to_rows(posl), to_rows(wt_t), xs.reshape(b * lf, d), h_flat, mods,
                      w_sh_gate[layer].astype(BF16), w_sh_up[layer].astype(BF16), w_sh_down[layer].astype(BF16),
                      lf // TM, ctx_tiles, g_final[None] if last else None).reshape(b, lf, d)
    return xs
```

```python
import functools
import math

import jax
import jax.numpy as jnp
from jax import lax
from jax.experimental import pallas as pl
from jax.experimental.pallas import tpu as pltpu

F32 = jnp.float32
BF16 = jnp.bfloat16
HIGHEST = lax.Precision.HIGHEST

GRID_W = 64
N_MOD = 6
HEAD_DIM = 64
N_HEADS_A = 8
KV_HEADS_A = 2
WINDOW = 128
N_HEADS_B = 4
N_HEADS_C = 4
HEAD_DIM_C = 128
N_EXPERTS = 64
N_GROUPS = 8
TOPK_GROUPS = 4
TOP_K = 8
ROUTED_SCALE = 2.5
ROPE_BASE = 10000.0
EPS = 1e-6

LANES = 128
CHUNK = 128
TILE_N = 512
TM = 256
MIXB_KEYS = 1024
MIXB_QUERIES = 512
EXPERT_ROWS = 512
ROW_GROUP = 16
SORT_ROWS = TOP_K * TM + N_EXPERTS * ROW_GROUP
SORT_CHUNK = 128
COMBINE_CHUNK = 512
NEG = -1e30
VMEM_LIMIT = 56 * 1024 * 1024

T_AQ, T_BQ, T_BK, T_KVA, T_BV, T_CO, T_GT, T_CQ, T_CV, N_TILES = 0, 1, 2, 3, 4, 5, 6, 12, 13, 14

NT_DIMS = (((1,), (1,)), ((), ()))


def _params(sem):
    return pltpu.CompilerParams(dimension_semantics=sem, vmem_limit_bytes=VMEM_LIMIT)


def _rms(x, g):
    return x * lax.rsqrt(jnp.mean(x * x, axis=-1, keepdims=True) + EPS) * g


def _sigmoid(x):
    return jax.nn.sigmoid(x)


def _mod_kernel(c_ref, w_ref, b_ref, o_ref):
    c = c_ref[...]
    s = c * _sigmoid(c)
    o_ref[...] = jnp.dot(s, w_ref[...], precision=HIGHEST, preferred_element_type=F32) + b_ref[...]


def _mod_vectors(cs, w_mod, b_mod):
    depth, d, n = w_mod.shape
    r = cs.shape[0]
    tn = 3 * LANES
    return pl.pallas_call(
        _mod_kernel,
        grid=(depth, n // tn),
        in_specs=[pl.BlockSpec((r, d), lambda l, j: (0, 0)),
                  pl.BlockSpec((None, d, tn), lambda l, j: (l, 0, j)),
                  pl.BlockSpec((None, 1, tn), lambda l, j: (l, 0, j))],
        out_specs=pl.BlockSpec((None, r, tn), lambda l, j: (l, 0, j)),
        out_shape=jax.ShapeDtypeStruct((depth, r, n), F32),
        compiler_params=_params(("parallel", "parallel")),
        name="mod_vectors",
    )(cs, w_mod, b_mod.reshape(depth, 1, n))


def _stream_specs(xs, n_ctx_tiles, t0=0):
    if not isinstance(xs, tuple):
        return [pl.BlockSpec((None, TM, xs.shape[-1]), lambda bi, ti: (bi, ti + t0, 0))], [xs]
    d = xs[0].shape[-1]
    return ([pl.BlockSpec((None, TM, d), lambda bi, ti: (bi, jnp.minimum(ti + t0, n_ctx_tiles - 1), 0)),
             pl.BlockSpec((None, TM, d), lambda bi, ti: (bi, jnp.maximum(ti + t0 - n_ctx_tiles, 0), 0))], list(xs))


def _read_stream(x_refs, is_ctx_tile):
    if len(x_refs) == 1:
        return x_refs[0][...]
    return jnp.where(is_ctx_tile, x_refs[0][...], x_refs[1][...])


def _inproj_kernel(*refs, n_ctx_tiles):
    x_refs, (mod_ref, g_ref, w_ref, wkt_ref, cos_ref, sa_ref, sb_ref, p_ref, gate_ref, kt_ref) = refs[:-10], refs[-10:]
    x = _read_stream(x_refs, pl.program_id(1) < n_ctx_tiles)
    h = _rms(x, g_ref[...]) * (1.0 + mod_ref[1:2, :]) + mod_ref[0:1, :]
    hb = h.astype(BF16)
    cos, sa, sb = cos_ref[...], sa_ref[...], sb_ref[...]

    def rope(t):
        return t * cos + pltpu.roll(t, 16, 1) * sa + pltpu.roll(t, LANES - 16, 1) * sb

    q_scale = HEAD_DIM ** -0.5
    for j in range(N_TILES):
        acc = jnp.dot(hb, w_ref[:, j * TILE_N:(j + 1) * TILE_N], preferred_element_type=F32)
        parts = [acc[:, s * LANES:(s + 1) * LANES] for s in range(TILE_N // LANES)]
        if j == T_AQ:
            parts = [rope(t) * q_scale for t in parts]
        elif j == T_BQ:
            parts = [rope(t) * (q_scale * math.log2(math.e)) for t in parts]
        elif j == T_BK:
            parts = [rope(t) for t in parts]
        elif j == T_KVA:
            gate_ref[...] = parts[2]
            parts[0] = rope(parts[0])
        for s, t in enumerate(parts):
            p_ref[:, j * TILE_N + s * LANES:j * TILE_N + (s + 1) * LANES] = t.astype(BF16)
    kt = lax.dot_general(wkt_ref[...], hb, NT_DIMS, preferred_element_type=F32)
    kt_ref[...] = (kt * (HEAD_DIM_C ** -0.5)).astype(BF16)


def _inproj(xs, mods, g_mix, w_big, w_kt, cos, sa, sb, n_ctx_tiles):
    x_specs, x_args = _stream_specs(xs, n_ctx_tiles)
    b, d = x_args[0].shape[0], x_args[0].shape[-1]
    l = sum(a.shape[1] for a in x_args)
    npad = w_big.shape[1]
    ck = w_kt.shape[0]
    grid = (b, l // TM)
    tok = lambda bi, ti: (bi, ti, 0)
    return pl.pallas_call(
        functools.partial(_inproj_kernel, n_ctx_tiles=n_ctx_tiles),
        grid=grid,
        in_specs=x_specs + [
                  pl.BlockSpec((None, None, 8, d), lambda bi, ti: (bi, jnp.where(ti >= n_ctx_tiles, 1, 0), 0, 0)),
                  pl.BlockSpec((1, d), lambda bi, ti: (0, 0)),
                  pl.BlockSpec((d, npad), lambda bi, ti: (0, 0), pipeline_mode=pl.Buffered(1)),
                  pl.BlockSpec((ck, d), lambda bi, ti: (0, 0), pipeline_mode=pl.Buffered(1)),
                  pl.BlockSpec((TM, LANES), lambda bi, ti: (ti, 0)),
                  pl.BlockSpec((TM, LANES), lambda bi, ti: (ti, 0)),
                  pl.BlockSpec((TM, LANES), lambda bi, ti: (ti, 0))],
        out_specs=[pl.BlockSpec((None, TM, npad), tok),
                   pl.BlockSpec((None, TM, LANES), tok),
                   pl.BlockSpec((None, ck, TM), lambda bi, ti: (bi, 0, ti))],
        out_shape=[jax.ShapeDtypeStruct((b, l, npad), BF16),
                   jax.ShapeDtypeStruct((b, l, LANES), F32),
                   jax.ShapeDtypeStruct((b, ck, l), BF16)],
        compiler_params=_params(("parallel", "parallel")),
        name="inproj",
    )(*x_args, mods, g_mix, w_big, w_kt, cos, sa, sb)


def _mixa_kernel(sink_ref, q_ref, kp_ref, kc_ref, kn_ref, kx_ref, o_ref, *, n_ctx_blocks, n_blocks):
    i = pl.program_id(1)
    lat = i >= n_ctx_blocks
    has_prev = jnp.logical_and(lat, i > n_ctx_blocks)
    has_next = jnp.logical_and(lat, i < n_blocks - 1)
    r = lax.broadcasted_iota(jnp.int32, (CHUNK, CHUNK), 0)
    c = lax.broadcasted_iota(jnp.int32, (CHUNK, CHUNK), 1)
    n_ctx = kx_ref.shape[0]
    valid = jnp.concatenate([
        jnp.logical_and(c >= r, has_prev),
        jnp.broadcast_to(lat, (CHUNK, CHUNK)),
        jnp.logical_and(c <= r, has_next),
        jnp.ones((CHUNK, n_ctx), jnp.bool_)], axis=1)
    kcat = jnp.concatenate([kp_ref[:, :LANES], kc_ref[:, :LANES], kn_ref[:, :LANES], kx_ref[:, :LANES]], axis=0)
    vcat = jnp.concatenate([kp_ref[:, LANES:], kc_ref[:, LANES:], kn_ref[:, LANES:], kx_ref[:, LANES:]], axis=0)
    lane = lax.broadcasted_iota(jnp.int32, (CHUNK, LANES), 1)
    low = lane < HEAD_DIM
    n_pairs = N_HEADS_A // KV_HEADS_A
    outs = []
    for gk in range(KV_HEADS_A):
        keep = low if gk == 0 else jnp.logical_not(low)
        zero = jnp.zeros((CHUNK, LANES), BF16)
        lhs = jnp.concatenate([jnp.where(keep, q_ref[:, t * LANES:(t + 1) * LANES], zero) for t in range(n_pairs)],
                              axis=0)
        s = lax.dot_general(lhs, kcat, NT_DIMS, preferred_element_type=F32)
        o_g = []
        for t in range(n_pairs):
            st = jnp.where(valid, s[t * CHUNK:(t + 1) * CHUNK], NEG)
            sk = sink_ref[gk * n_pairs + t]
            m = jnp.maximum(jnp.max(st, axis=-1, keepdims=True), sk)
            p = jnp.exp(st - m)
            den = jnp.sum(p, axis=-1, keepdims=True) + jnp.exp(sk - m)
            o_g.append(jnp.dot(p.astype(BF16), vcat, preferred_element_type=F32) / den)
        outs.append(o_g)
    for t in range(n_pairs):
        o_ref[:, t * LANES:(t + 1) * LANES] = jnp.where(low, outs[0][t], outs[1][t]).astype(BF16)


def _mixer_a(p, sink, n_ctx):
    b, l, _ = p.shape
    nb = l // CHUNK
    ncb = n_ctx // CHUNK
    kvw = 2 * LANES
    kv_col = T_KVA * TILE_N // kvw
    aq_w = N_HEADS_A * HEAD_DIM
    kern = functools.partial(_mixa_kernel, n_ctx_blocks=ncb, n_blocks=nb)
    return pl.pallas_call(
        kern,
        grid=(b, nb),
        in_specs=[pl.BlockSpec(memory_space=pltpu.SMEM),
                  pl.BlockSpec((None, CHUNK, aq_w), lambda bi, i: (bi, i, T_AQ)),
                  pl.BlockSpec((None, CHUNK, kvw), lambda bi, i: (bi, jnp.maximum(i - 1, 0), kv_col)),
                  pl.BlockSpec((None, CHUNK, kvw), lambda bi, i: (bi, i, kv_col)),
                  pl.BlockSpec((None, CHUNK, kvw), lambda bi, i: (bi, jnp.minimum(i + 1, nb - 1), kv_col)),
                  pl.BlockSpec((None, n_ctx, kvw), lambda bi, i: (bi, 0, kv_col))],
        out_specs=pl.BlockSpec((None, CHUNK, aq_w), lambda bi, i: (bi, i, 0)),
        out_shape=jax.ShapeDtypeStruct((b, l, aq_w), BF16),
        compiler_params=_params(("parallel", "parallel")),
        name="mixer_a",
    )(sink, p, p, p, p, p)


def _fold_lanes(op, acc, s):
    for t in range(s.shape[1] // LANES):
        acc = op(acc, s[:, t * LANES:(t + 1) * LANES])
    return acc


def _mixb_kernel(lam_ref, gd_ref, k_ref, v_ref, *rest, lam_init, chunks):
    q_refs, (o_ref, s_scr, va_scr) = rest[:-3], rest[-3:]

    @pl.when(pl.program_id(2) == 0)
    def _():
        n_keys = v_ref.shape[0]
        va_scr[:, :LANES] = v_ref[...]
        va_scr[:, LANES:] = (lax.broadcasted_iota(jnp.int32, (n_keys, LANES), 1) == 0).astype(BF16)

    lp = lam_ref[...]
    lam = (jnp.exp(jnp.sum(lp[0:1] * lp[1:2], axis=-1, keepdims=True))
           - jnp.exp(jnp.sum(lp[2:3] * lp[3:4], axis=-1, keepdims=True)) + lam_init)
    q = jnp.concatenate([qr[...] for qr in q_refs], axis=0)
    tq = q.shape[0]
    lane = lax.broadcasted_iota(jnp.int32, (tq, LANES), 1)
    zero = jnp.zeros_like(q)
    qs = (jnp.where(lane < HEAD_DIM, q, zero), jnp.where(lane >= HEAD_DIM, q, zero))
    rows = [slice(mi * tq, (mi + 1) * tq) for mi in range(2)]
    mrun = [jnp.full((tq, LANES), NEG, F32) for _ in range(2)]
    for off, sz in chunks:
        for mi in range(2):
            s_scr[rows[mi], off:off + sz] = lax.dot_general(qs[mi], k_ref[off:off + sz, :], NT_DIMS,
                                                            preferred_element_type=F32)
            mrun[mi] = _fold_lanes(jnp.maximum, mrun[mi], s_scr[rows[mi], off:off + sz])
    m = [jnp.max(mr, axis=-1, keepdims=True) for mr in mrun]
    acc = [jnp.zeros((tq, 2 * LANES), F32) for _ in range(2)]
    for off, sz in chunks:
        for mi in range(2):
            pr = jnp.exp2(s_scr[rows[mi], off:off + sz] - m[mi])
            acc[mi] = acc[mi] + jnp.dot(pr.astype(BF16), va_scr[off:off + sz, :], preferred_element_type=F32)
    outs = [a[:, :LANES] / a[:, LANES:LANES + 1] for a in acc]
    o = outs[0] - lam * outs[1]
    o_ref[...] = (_rms(o, gd_ref[...]) * (1.0 - lam_init)).astype(BF16)


def _mixer_b(p, lam_params, g_diff, lam_init, n_ctx):
    b, l, _ = p.shape
    kl = min(MIXB_KEYS, l - n_ctx)
    tq_lat = min(MIXB_QUERIES, l - n_ctx)
    assert (l - n_ctx) % kl == 0 and (l - n_ctx) % tq_lat == 0 and tq_lat % TM == 0 and n_ctx % TM == 0
    q0 = T_BQ * TILE_N // LANES
    k0 = T_BK * TILE_N // LANES
    v0 = T_BV * TILE_N // LANES
    ctx_chunks = ((0, n_ctx),)
    all_chunks = ctx_chunks + tuple((n_ctx + c * kl, kl) for c in range((l - n_ctx) // kl))

    def call(chunks, n_keys, tq, q_tiles, first_row):
        nq = tq // TM
        q_spec = lambda part: pl.BlockSpec((None, TM, LANES),
                                           lambda bi, h, qi: (bi, first_row // TM + qi * nq + part, q0 + h))
        kern = functools.partial(_mixb_kernel, lam_init=lam_init, chunks=chunks)
        return pl.pallas_call(
            kern,
            grid=(b, N_HEADS_B, q_tiles),
            in_specs=[pl.BlockSpec((4, HEAD_DIM), lambda bi, h, qi: (0, 0)),
                      pl.BlockSpec((1, LANES), lambda bi, h, qi: (0, 0)),
                      pl.BlockSpec((None, n_keys, LANES), lambda bi, h, qi: (bi, 0, k0 + h)),
                      pl.BlockSpec((None, n_keys, LANES), lambda bi, h, qi: (bi, 0, v0 + h))]
                     + [q_spec(part) for part in range(nq)],
            out_specs=pl.BlockSpec((None, tq, LANES), lambda bi, h, qi: (bi, qi, h)),
            out_shape=jax.ShapeDtypeStruct((b, q_tiles * tq, N_HEADS_B * LANES), BF16),
            scratch_shapes=[pltpu.VMEM((2 * tq, n_keys), F32), pltpu.VMEM((n_keys, 2 * LANES), BF16)],
            compiler_params=_params(("parallel", "parallel", "arbitrary")),
            name="mixer_b",
        )(lam_params, g_diff, p, p, *([p] * nq))

    tq_ctx = min(tq_lat, n_ctx)
    return (call(ctx_chunks, n_ctx, tq_ctx, n_ctx // tq_ctx, 0),
            call(all_chunks, l, tq_lat, (l - n_ctx) // tq_lat, n_ctx))


def _log_sigmoid(x):
    return jnp.minimum(x, 0.0) - jnp.log1p(jnp.exp(-jnp.abs(x)))


def _mlstm_kernel(qf_ref, ktf_ref, vf_ref, gcf_ref, grf_ref, qb_ref, ktb_ref, vb_ref, gcb_ref, grb_ref,
                  bc_ref, br_ref, of_ref, ob_ref, s_scr, m_scr):
    @pl.when(pl.program_id(1) == 0)
    def _():
        s_scr[...] = jnp.zeros_like(s_scr)
        m_scr[...] = jnp.zeros_like(m_scr)

    r = lax.broadcasted_iota(jnp.int32, (CHUNK, CHUNK), 0)
    cc = lax.broadcasted_iota(jnp.int32, (CHUNK, CHUNK), 1)
    lane = lax.broadcasted_iota(jnp.int32, (CHUNK, LANES), 1)
    ones_col = (lane == 0).astype(BF16)
    nh = N_HEADS_C
    tris = (r >= cc, r <= cc)
    refs = ((qf_ref, ktf_ref, vf_ref, gcf_ref, grf_ref, of_ref), (qb_ref, ktb_ref, vb_ref, gcb_ref, grb_ref, ob_ref))
    grow, bcum_col, bcum_row, tot_row = [], [], [], []
    for di in range(2):
        trif = tris[di].astype(F32)
        gcol = refs[di][3][...] + bc_ref[...]
        grow.append(refs[di][4][...] + br_ref[...])
        lf_col = _log_sigmoid(gcol)
        lf_row = _log_sigmoid(grow[di])
        bcum_col.append(jnp.dot(trif, lf_col, precision=HIGHEST, preferred_element_type=F32))
        bcum_row.append(lax.dot_general(lf_row, trif, NT_DIMS, precision=HIGHEST, preferred_element_type=F32))
        tot_row.append(jnp.sum(lf_row, axis=-1, keepdims=True))
    ch = [(di, h) for di in range(2) for h in range(nh)]
    gi = {c: (2 * c[0]) * nh + c[1] for c in ch}
    gf = {c: (2 * c[0] + 1) * nh + c[1] for c in ch}
    ic_row = {c: grow[c[0]][gi[c]:gi[c] + 1, :] for c in ch}
    b_col = {c: bcum_col[c[0]][:, gf[c]:gf[c] + 1] for c in ch}
    b_row = {c: bcum_row[c[0]][gf[c]:gf[c] + 1, :] for c in ch}
    total = {c: tot_row[c[0]][gf[c]:gf[c] + 1, :] for c in ch}
    m_st = {c: m_scr[c[0], c[1], 0:1, 0:1] for c in ch}
    qh = {c: refs[c[0]][0][:, c[1] * LANES:(c[1] + 1) * LANES] for c in ch}
    kth = {c: refs[c[0]][1][c[1] * LANES:(c[1] + 1) * LANES, :] for c in ch}
    vaug = {c: jnp.concatenate([refs[c[0]][2][:, c[1] * LANES:(c[1] + 1) * LANES], ones_col], axis=1) for c in ch}
    st = {c: s_scr[c[0], c[1]] for c in ch}
    qk = {c: jnp.dot(qh[c], kth[c], preferred_element_type=F32) for c in ch}
    cross = {c: jnp.dot(qh[c], st[c].astype(BF16), preferred_element_type=F32) for c in ch}
    gs_row = {c: total[c] - b_row[c] + ic_row[c] for c in ch}
    m_new = {c: jnp.maximum(total[c] + m_st[c], jnp.max(gs_row[c], axis=-1, keepdims=True)) for c in ch}
    wkt = {c: (kth[c].astype(F32) * jnp.exp(gs_row[c] - m_new[c])).astype(BF16) for c in ch}
    upd = {c: jnp.dot(wkt[c], vaug[c], preferred_element_type=F32) for c in ch}
    dm = {c: jnp.where(tris[c[0]], b_col[c] - b_row[c] + ic_row[c], NEG) for c in ch}
    inter = {c: b_col[c] + m_st[c] for c in ch}
    m_t = {c: jnp.maximum(inter[c], jnp.max(dm[c], axis=-1, keepdims=True)) for c in ch}
    sc = {c: (qk[c] * jnp.exp(dm[c] - m_t[c])).astype(BF16) for c in ch}
    intra = {c: jnp.dot(sc[c], vaug[c], preferred_element_type=F32) for c in ch}
    for c in ch:
        di, h = c
        nd = intra[c] + jnp.exp(inter[c] - m_t[c]) * cross[c]
        den = nd[:, LANES:LANES + 1]
        h_out = nd[:, :LANES] / jnp.maximum(jnp.abs(den), jnp.exp(-m_t[c]))
        refs[di][5][:, h * LANES:(h + 1) * LANES] = h_out.astype(BF16)
        s_scr[di, h] = jnp.exp(total[c] + m_st[c] - m_new[c]) * st[c] + upd[c]
        m_scr[di, h] = jnp.broadcast_to(m_new[c], m_scr.shape[2:])


def _mlstm(p, kt, gates, gates_t, bias_row, bias_col, n_ctx):
    b, l, _ = p.shape
    nc = l // CHUNK
    ncc = n_ctx // CHUNK
    cw = N_HEADS_C * HEAD_DIM_C

    def rev(c):
        return jnp.where(c < ncc, ncc - 1 - c, nc + ncc - 1 - c)

    def specs(ch):
        return [pl.BlockSpec((None, CHUNK, cw), lambda bi, c: (bi, ch(c), T_CQ)),
                pl.BlockSpec((None, cw, CHUNK), lambda bi, c: (bi, 0, ch(c))),
                pl.BlockSpec((None, CHUNK, cw), lambda bi, c: (bi, ch(c), T_CV)),
                pl.BlockSpec((None, CHUNK, LANES), lambda bi, c: (bi, ch(c), 0)),
                pl.BlockSpec((None, 16, CHUNK), lambda bi, c: (bi, 0, ch(c)))]

    fwd = lambda c: c
    out = jax.ShapeDtypeStruct((b, l, cw), BF16)
    return pl.pallas_call(
        _mlstm_kernel,
        grid=(b, nc),
        in_specs=specs(fwd) + specs(rev) + [pl.BlockSpec((1, LANES), lambda bi, c: (0, 0)),
                                            pl.BlockSpec((16, LANES), lambda bi, c: (0, 0))],
        out_specs=[pl.BlockSpec((None, CHUNK, cw), lambda bi, c: (bi, c, 0)),
                   pl.BlockSpec((None, CHUNK, cw), lambda bi, c: (bi, rev(c), 0))],
        out_shape=[out, out],
        scratch_shapes=[pltpu.VMEM((2, N_HEADS_C, HEAD_DIM_C, 2 * LANES), F32),
                        pltpu.VMEM((2, N_HEADS_C, 8, LANES), F32)],
        compiler_params=_params(("parallel", "arbitrary")),
        name="mlstm",
    )(p, kt, p, gates, gates_t, p, kt, p, gates, gates_t, bias_row, bias_col)


def _merge_kernel(*refs, n_ctx_tiles, t0):
    x_refs, (mod_ref, oa_ref, obc_ref, obl_ref, hf_ref, hb_ref, co_ref, gt_ref, gm_ref,
             wa_ref, wb_ref, wc_ref, wo_ref, xo_ref) = refs[:-14], refs[-14:]
    d = xo_ref.shape[-1]
    is_ctx = pl.program_id(1) + t0 < n_ctx_tiles
    ob = jnp.where(is_ctx, obc_ref[...], obl_ref[...])
    hs = hf_ref[...].astype(F32) + hb_ref[...].astype(F32)
    co = co_ref[...].astype(F32)
    gm = gm_ref[...]
    oc = []
    for h in range(N_HEADS_C):
        sl = slice(h * LANES, (h + 1) * LANES)
        oc.append((_rms(hs[:, sl], gm[:, sl]) * _sigmoid(co[:, sl])).astype(BF16))
    oc = jnp.concatenate(oc, axis=1)
    y = (_sigmoid(gt_ref[:, 0:d].astype(F32)) * jnp.dot(oa_ref[...], wa_ref[...], preferred_element_type=F32)
         + _sigmoid(gt_ref[:, d:2 * d].astype(F32)) * jnp.dot(ob, wb_ref[...], preferred_element_type=F32)
         + _sigmoid(gt_ref[:, 2 * d:3 * d].astype(F32)) * jnp.dot(oc, wc_ref[...], preferred_element_type=F32))
    out = jnp.dot(y.astype(BF16), wo_ref[...], preferred_element_type=F32)
    xo_ref[...] = _read_stream(x_refs, is_ctx) + mod_ref[2:3, :] * out


def _merge(xs, mods, oa, ob_ctx, ob_lat, hf, hb, p, g_mlstm, wa, wb, wc, wo, n_ctx_tiles, latent_only):
    t0 = n_ctx_tiles if latent_only else 0
    x_specs, x_args = _stream_specs(xs, n_ctx_tiles, t0)
    b, d = x_args[0].shape[0], x_args[0].shape[-1]
    n_tiles = oa.shape[1] // TM - t0
    tok = lambda bi, ti: (bi, ti + t0, 0)
    cw = N_HEADS_C * HEAD_DIM_C
    const = lambda bi, ti: (0, 0)
    return pl.pallas_call(
        functools.partial(_merge_kernel, n_ctx_tiles=n_ctx_tiles, t0=t0),
        grid=(b, n_tiles),
        in_specs=x_specs + [
                  pl.BlockSpec((None, None, 8, d), lambda bi, ti: (bi, jnp.where(ti + t0 >= n_ctx_tiles, 1, 0), 0, 0)),
                  pl.BlockSpec((None, TM, oa.shape[-1]), tok),
                  pl.BlockSpec((None, TM, ob_ctx.shape[-1]),
                               lambda bi, ti: (bi, jnp.minimum(ti + t0, n_ctx_tiles - 1), 0)),
                  pl.BlockSpec((None, TM, ob_lat.shape[-1]),
                               lambda bi, ti: (bi, jnp.maximum(ti + t0 - n_ctx_tiles, 0), 0)),
                  pl.BlockSpec((None, TM, cw), tok),
                  pl.BlockSpec((None, TM, cw), tok),
                  pl.BlockSpec((None, TM, cw), lambda bi, ti: (bi, ti + t0, T_CO)),
                  pl.BlockSpec((None, TM, 3 * d), lambda bi, ti: (bi, ti + t0, T_GT * TILE_N // (3 * d))),
                  pl.BlockSpec((1, cw), const),
                  pl.BlockSpec(wa.shape, const), pl.BlockSpec(wb.shape, const),
                  pl.BlockSpec(wc.shape, const), pl.BlockSpec(wo.shape, const)],
        out_specs=pl.BlockSpec((None, TM, d), lambda bi, ti: (bi, ti, 0)),
        out_shape=jax.ShapeDtypeStruct((b, n_tiles * TM, d), F32),
        compiler_params=_params(("parallel", "parallel")),
        name="merge",
    )(*x_args, mods, oa, ob_ctx, ob_lat, hf, hb, p, p, g_mlstm, wa, wb, wc, wo)


def _router_kernel(x_ref, mod_ref, g_ref, wrt_ref, br_ref, h_ref, idx_ref, wt_ref, cnt_ref):
    h = _rms(x_ref[...], g_ref[...]) * (1.0 + mod_ref[4:5, :]) + mod_ref[3:4, :]
    h_ref[...] = h.astype(BF16)
    tm = h.shape[0]
    per = N_EXPERTS // N_GROUPS
    lt = lax.dot_general(wrt_ref[...], h, NT_DIMS, precision=HIGHEST, preferred_element_type=F32)
    s = _sigmoid(lt)
    sel = s + br_ref[...]
    ninf = -jnp.inf
    sel3 = sel.reshape(N_GROUPS, per, tm)
    eidx = lax.broadcasted_iota(jnp.int32, (N_GROUPS, per, tm), 1)
    m1 = jnp.max(sel3, axis=1, keepdims=True)
    first = jnp.min(jnp.where(sel3 == m1, eidx, per), axis=1, keepdims=True)
    m2 = jnp.max(jnp.where(eidx == first, ninf, sel3), axis=1, keepdims=True)
    gscore = (m1 + m2).reshape(N_GROUPS, tm)
    gidx = lax.broadcasted_iota(jnp.int32, (N_GROUPS, tm), 0)
    gmask = jnp.zeros((N_GROUPS, tm), jnp.bool_)
    cur = gscore
    for _ in range(TOPK_GROUPS):
        mx = jnp.max(cur, axis=0, keepdims=True)
        hit = gidx == jnp.min(jnp.where(cur == mx, gidx, N_GROUPS), axis=0, keepdims=True)
        gmask = jnp.logical_or(gmask, hit)
        cur = jnp.where(hit, ninf, cur)
    cur = jnp.where(gmask.reshape(N_GROUPS, 1, tm), sel3, ninf).reshape(N_EXPERTS, tm)
    eid = lax.broadcasted_iota(jnp.int32, (N_EXPERTS, tm), 0)
    ids, ws = [], []
    chosen = jnp.zeros((N_EXPERTS, tm), F32)
    for _ in range(TOP_K):
        mx = jnp.max(cur, axis=0, keepdims=True)
        pick = jnp.min(jnp.where(cur == mx, eid, N_EXPERTS), axis=0, keepdims=True)
        hit = eid == pick
        ids.append(pick)
        ws.append(jnp.sum(jnp.where(hit, s, 0.0), axis=0, keepdims=True))
        cur = jnp.where(hit, ninf, cur)
        chosen = chosen + hit.astype(F32)
    wsum = ws[0]
    for w in ws[1:]:
        wsum = wsum + w
    idx_ref[...] = jnp.concatenate(ids, axis=0)
    wt_ref[...] = jnp.concatenate([w / wsum * ROUTED_SCALE for w in ws], axis=0)
    cnt_ref[...] = jnp.sum(chosen, axis=1, keepdims=True).astype(jnp.int32)


def _router(xs, mods, g_ffn, w_router_t, b_router, n_ctx_tiles):
    b, l, d = xs.shape
    tok = lambda bi, ti: (bi, ti, 0)
    const = lambda bi, ti: (0, 0)
    return pl.pallas_call(
        _router_kernel,
        grid=(b, l // TM),
        in_specs=[pl.BlockSpec((None, TM, d), tok),
                  pl.BlockSpec((None, None, 8, d), lambda bi, ti: (bi, jnp.where(ti >= n_ctx_tiles, 1, 0), 0, 0)),
                  pl.BlockSpec((1, d), const),
                  pl.BlockSpec((N_EXPERTS, d), const),
                  pl.BlockSpec((N_EXPERTS, 1), const)],
        out_specs=[pl.BlockSpec((None, TM, d), tok),
                   pl.BlockSpec((None, TOP_K, TM), lambda bi, ti: (bi, 0, ti)),
                   pl.BlockSpec((None, TOP_K, TM), lambda bi, ti: (bi, 0, ti)),
                   pl.BlockSpec((None, None, N_EXPERTS, 1), lambda bi, ti: (bi, ti, 0, 0))],
        out_shape=[jax.ShapeDtypeStruct((b, l, d), BF16),
                   jax.ShapeDtypeStruct((b, TOP_K, l), jnp.int32),
                   jax.ShapeDtypeStruct((b, TOP_K, l), F32),
                   jax.ShapeDtypeStruct((b, l // TM, N_EXPERTS, 1), jnp.int32)],
        compiler_params=_params(("parallel", "parallel")),
        name="router",
    )(xs, mods, g_ffn, w_router_t, b_router)


def _sort_kernel(idx_ref, off_ref, h_ref, posl_ref, ts_ref):
    tm = h_ref.shape[0]
    idx = idx_ref[...]
    eid = lax.broadcasted_iota(jnp.int32, (N_EXPERTS, tm), 0)
    hits = [eid == idx[k:k + 1, :] for k in range(TOP_K)]
    chosen = hits[0].astype(BF16)
    for hk in hits[1:]:
        chosen = chosen + hk.astype(BF16)
    r = lax.broadcasted_iota(jnp.int32, (tm, tm), 0)
    c = lax.broadcasted_iota(jnp.int32, (tm, tm), 1)
    before = (r < c).astype(BF16)
    rank = jnp.dot(chosen, before, preferred_element_type=F32)
    slot = rank.astype(jnp.int32) + off_ref[...]
    posl = jnp.concatenate([jnp.sum(jnp.where(hk, slot, 0), axis=0, keepdims=True) for hk in hits], axis=0)
    posl_ref[...] = posl
    hb = h_ref[...]
    for rb in range(ts_ref.shape[0] // SORT_CHUNK):
        rows = lax.broadcasted_iota(jnp.int32, (SORT_CHUNK, tm), 0) + rb * SORT_CHUNK
        sel = rows == posl[0:1, :]
        for k in range(1, TOP_K):
            sel = jnp.logical_or(sel, rows == posl[k:k + 1, :])
        onehot = jnp.where(sel, 1.0, 0.0).astype(BF16)
        ts = jnp.dot(onehot, hb, preferred_element_type=F32)
        ts_ref[rb * SORT_CHUNK:(rb + 1) * SORT_CHUNK, :] = ts.astype(BF16)


def _sort_rows(idx_t, off, h_flat):
    b, k, l = idx_t.shape
    n, d = h_flat.shape
    nt = l // TM
    return pl.pallas_call(
        _sort_kernel,
        grid=(b * nt,),
        in_specs=[pl.BlockSpec((None, k, TM), lambda i: (i // nt, 0, i % nt)),
                  pl.BlockSpec((None, N_EXPERTS, 1), lambda i: (i, 0, 0)),
                  pl.BlockSpec((TM, d), lambda i: (i, 0))],
        out_specs=[pl.BlockSpec((None, k, TM), lambda i: (i // nt, 0, i % nt)),
                   pl.BlockSpec((None, SORT_ROWS, d), lambda i: (i, 0, 0))],
        out_shape=[jax.ShapeDtypeStruct((b, k, l), jnp.int32),
                   jax.ShapeDtypeStruct((b * nt, SORT_ROWS, d), BF16)],
        compiler_params=_params(("parallel",)),
        name="sort_rows",
    )(idx_t, off, h_flat)


SLAB_SIZES = tuple(1 << s for s in range((EXPERT_ROWS // ROW_GROUP).bit_length()))


def _expert_kernel(be_ref, nu_ref, cov_ref, grp_ref, ts_in, wg_ref, wu_ref, wd_ref, ts_out,
                   xbuf, ybuf, wg_scr, wu_scr, wd_scr, sem_g, sem_s):
    j = pl.program_id(0)
    nu = nu_ref[0]
    n_grp = xbuf.shape[1]
    blk = n_grp * ROW_GROUP
    slot = j % 2

    def for_groups(bj, fn):
        for i in range(n_grp):
            sg = grp_ref[bj * n_grp + i]

            @pl.when(sg >= 0)
            def _():
                fn(sg, i)

    def gather(bj, s):
        xbuf[s] = jnp.zeros(xbuf.shape[1:], xbuf.dtype)
        for_groups(bj, lambda src, i: pltpu.make_async_copy(
            ts_in.at[src], xbuf.at[s, i], sem_g.at[s]).start(priority=i % 2))

    def scatter(bj, s):
        for_groups(bj, lambda dst, i: pltpu.make_async_copy(
            ybuf.at[s, i], ts_out.at[dst], sem_s.at[s]).start(priority=i % 2))

    def wait_rows(bj, buf, sem, s):
        groups = cov_ref[bj] // ROW_GROUP
        for sz in SLAB_SIZES:
            @pl.when((groups & sz) != 0)
            def _():
                pltpu.make_async_copy(ts_in.at[pl.ds(0, sz)], buf.at[s, pl.ds(0, sz)], sem.at[s]).wait()

    @pl.when(j == 0)
    def _():
        gather(0, 0)

    @pl.when(j + 1 < nu)
    def _():
        gather(j + 1, 1 - slot)

    @pl.when(jnp.logical_and(j >= 2, j - 2 < nu))
    def _():
        wait_rows(j - 2, ybuf, sem_s, slot)

    @pl.when(jnp.logical_and(j < nu, jnp.logical_or(j == 0, be_ref[j] != be_ref[jnp.maximum(j - 1, 0)])))
    def _():
        wg_scr[...] = wg_ref[...].astype(BF16)
        wu_scr[...] = wu_ref[...].astype(BF16)
        wd_scr[...] = wd_ref[...].astype(BF16)

    @pl.when(j < nu)
    def _():
        wait_rows(j, xbuf, sem_g, slot)
        x = xbuf[slot].reshape(blk, xbuf.shape[-1])
        g = jnp.dot(x, wg_scr[...], preferred_element_type=F32)
        u = jnp.dot(x, wu_scr[...], preferred_element_type=F32)
        a = (g * _sigmoid(g) * u).astype(BF16)
        y = jnp.dot(a, wd_scr[...], preferred_element_type=F32)
        ybuf[slot] = y.astype(BF16).reshape(ybuf.shape[1:])
        scatter(j, slot)


def _experts(plan, tiles, wg, wu, wd, layer):
    nt, rows, w = tiles.shape
    blk = EXPERT_ROWS
    d, de = wg.shape[2:]
    n_blocks = plan["block_e"].shape[0]
    tables = (plan["block_e"], plan["n_used"], plan["blk_rows"], plan["blk_groups"])
    wspec = lambda shape: pl.BlockSpec((None, None) + shape, lambda i, be, *_: (layer, be[i], 0, 0))
    grid_spec = pltpu.PrefetchScalarGridSpec(
        num_scalar_prefetch=len(tables),
        grid=(n_blocks,),
        in_specs=[pl.BlockSpec(memory_space=pl.ANY), wspec((d, de)), wspec((d, de)), wspec((de, d))],
        out_specs=pl.BlockSpec(memory_space=pl.ANY),
        scratch_shapes=[pltpu.VMEM((2, blk // ROW_GROUP, ROW_GROUP, w), tiles.dtype)] * 2 + [
                        pltpu.VMEM((d, de), BF16), pltpu.VMEM((d, de), BF16), pltpu.VMEM((de, d), BF16),
                        pltpu.SemaphoreType.DMA((2,)), pltpu.SemaphoreType.DMA((2,))])
    return pl.pallas_call(
        _expert_kernel,
        grid_spec=grid_spec,
        out_shape=jax.ShapeDtypeStruct((nt * rows // ROW_GROUP, ROW_GROUP, w), tiles.dtype),
        input_output_aliases={len(tables): 0},
        compiler_params=_params(("arbitrary",)),
        name="experts",
    )(*tables, tiles.reshape(nt * rows // ROW_GROUP, ROW_GROUP, w), wg, wu, wd).reshape(tiles.shape)


def _combine_kernel(ts_ref, posl_ref, w_ref, x_ref, h_ref, mod_ref, wsg_ref, wsu_ref, wsd_ref, *rest):
    tm = x_ref.shape[0]
    hb = h_ref[...]
    g = jnp.dot(hb, wsg_ref[...], preferred_element_type=F32)
    u = jnp.dot(hb, wsu_ref[...], preferred_element_type=F32)
    acc = jnp.dot((g * _sigmoid(g) * u).astype(BF16), wsd_ref[...], preferred_element_type=F32)
    posl = posl_ref[...]
    w = w_ref[...]
    for rb in range(ts_ref.shape[0] // COMBINE_CHUNK):
        cols = lax.broadcasted_iota(jnp.int32, (tm, COMBINE_CHUNK), 1) + rb * COMBINE_CHUNK
        wm = jnp.zeros((tm, COMBINE_CHUNK), F32)
        for k in range(TOP_K):
            wm = jnp.where(cols == posl[:, k:k + 1], w[:, k:k + 1], wm)
        acc = acc + jnp.dot(wm.astype(BF16), ts_ref[rb * COMBINE_CHUNK:(rb + 1) * COMBINE_CHUNK, :],
                            preferred_element_type=F32)
    out = x_ref[...] + mod_ref[5:6, :] * acc
    rest[-1][...] = _rms(out, rest[0][...]) if len(rest) == 2 else out


def _combine(tiles, posl_tm, wts, x_flat, h_flat, mods, wsg, wsu, wsd, tiles_per_sample, n_ctx_tiles, g_final=None):
    n, d = x_flat.shape
    ds_ = wsg.shape[-1]
    tok = lambda i: (i, 0)
    const = lambda i: (0, 0)

    def mod_idx(i):
        return (i // tiles_per_sample, jnp.where(i % tiles_per_sample >= n_ctx_tiles, 1, 0), 0, 0)

    in_extra, args_extra, out_rows, out_idx = [], [], n, tok
    if g_final is not None:
        lat_tiles = tiles_per_sample - n_ctx_tiles
        in_extra, args_extra = [pl.BlockSpec((1, d), const)], [g_final]
        out_rows = n // TM // tiles_per_sample * lat_tiles * TM
        out_idx = lambda i: (i // tiles_per_sample * lat_tiles + jnp.maximum(i % tiles_per_sample - n_ctx_tiles, 0), 0)
    return pl.pallas_call(
        _combine_kernel,
        grid=(n // TM,),
        in_specs=[pl.BlockSpec((None,) + tiles.shape[1:], lambda i: (i, 0, 0)),
                  pl.BlockSpec((TM, TOP_K), tok),
                  pl.BlockSpec((TM, TOP_K), tok),
                  pl.BlockSpec((TM, d), tok),
                  pl.BlockSpec((TM, d), tok),
                  pl.BlockSpec((None, None, 8, d), mod_idx),
                  pl.BlockSpec((d, ds_), const), pl.BlockSpec((d, ds_), const), pl.BlockSpec((ds_, d), const)]
                 + in_extra,
        out_specs=pl.BlockSpec((TM, d), out_idx),
        out_shape=jax.ShapeDtypeStruct((out_rows, d), F32),
        compiler_params=_params(("arbitrary",)),
        name="combine",
    )(tiles, posl_tm, wts, x_flat, h_flat, mods, wsg, wsu, wsd, *args_extra)


def _moe_plan(cnt, n_assign, blk):
    nt = cnt.shape[0]
    run = (cnt + ROW_GROUP - 1) // ROW_GROUP * ROW_GROUP
    tile_off = jnp.cumsum(run, axis=1) - run
    tot = jnp.sum(run, axis=0)
    padded = (tot + blk - 1) // blk * blk
    pad_end = jnp.cumsum(padded)
    pad_start = pad_end - padded
    n_blocks = -(-(n_assign + nt * N_EXPERTS * (ROW_GROUP - 1)) // blk) + N_EXPERTS + 2
    first_row = jnp.arange(n_blocks, dtype=jnp.int32) * blk
    count = lambda m: jnp.sum(m.astype(jnp.int32), axis=1)
    block_e = jnp.minimum(count(pad_end[None, :] <= first_row[:, None]), N_EXPERTS - 1)
    i32 = lambda a: a.astype(jnp.int32)
    run_end = jnp.cumsum(run, axis=0).T[block_e][:, None, :]
    run_start = run_end - run.T[block_e][:, None, :]
    run_src = (jnp.arange(nt) * SORT_ROWS)[None, :] + tile_off.T[block_e]
    local = (first_row - pad_start[block_e])[:, None] + jnp.arange(0, blk, ROW_GROUP)[None, :]
    hit = jnp.logical_and(run_start <= local[:, :, None], local[:, :, None] < run_end)
    src_row = jnp.sum(jnp.where(hit, run_src[:, None, :] + local[:, :, None] - run_start, 0), axis=2)
    blk_groups = jnp.where(jnp.any(hit, axis=2), src_row // ROW_GROUP, -1).reshape(-1)
    return dict(
        tile_off=i32(tile_off)[:, :, None],
        blk_groups=i32(blk_groups),
        blk_rows=i32(jnp.clip((pad_start + tot)[block_e] - first_row, 0, blk)),
        block_e=i32(block_e), n_used=i32(pad_end[-1] // blk).reshape(1))


def _rope_tables(s_len, n_ctx):
    rows = s_len // GRID_W
    row = jnp.repeat(jnp.arange(rows), GRID_W).astype(F32)
    col = jnp.tile(jnp.arange(GRID_W), rows).astype(F32)
    quarter = HEAD_DIM // 4
    inv = 1.0 / (ROPE_BASE ** (jnp.arange(quarter, dtype=F32) / quarter))
    ar, ac = row[:, None] * inv, col[:, None] * inv
    cr, sr, cc, sc = jnp.cos(ar), jnp.sin(ar), jnp.cos(ac), jnp.sin(ac)
    z = jnp.zeros_like(sr)
    cos = jnp.concatenate([cr, cr, cc, cc], axis=1)
    sa = jnp.concatenate([z, sr, z, sc], axis=1)
    sb = jnp.concatenate([-sr, z, -sc, z], axis=1)
    rep = LANES // HEAD_DIM

    def full(t, fill):
        t = jnp.tile(t, (1, rep))
        return jnp.concatenate([jnp.full((n_ctx, LANES), fill, F32), t], axis=0)

    return full(cos, 1.0), full(sa, 0.0), full(sb, 0.0)


def _pair_perm():
    g = N_HEADS_A // KV_HEADS_A
    heads = [h for t in range(g) for h in (t, t + g)]
    return jnp.concatenate([jnp.arange(h * HEAD_DIM, (h + 1) * HEAD_DIM) for h in heads])


def _split_w_in(w):
    a_q, a_kv = N_HEADS_A * HEAD_DIM, KV_HEADS_A * HEAD_DIM
    b_w = N_HEADS_B * 2 * HEAD_DIM
    c_w = N_HEADS_C * HEAD_DIM_C
    sizes = (a_q, a_kv, a_kv, b_w, b_w, b_w, c_w, c_w, c_w, c_w, 4 * N_HEADS_C, w.shape[1])
    parts, start = [], 0
    for sz in sizes[:-1]:
        parts.append(w[:, start:start + sz])
        start += sz
    parts.append(w[:, start:])
    return parts


def _pack_w_in(w):
    d = w.shape[0]
    aq, ak, av, bq, bk, bv, cq, ck, cv, co, cg, gt = _split_w_in(w)
    pad = lambda n: jnp.zeros((d, n), w.dtype)
    kva = jnp.concatenate([ak, av, cg, pad(TILE_N - ak.shape[1] - av.shape[1] - cg.shape[1])], axis=1)
    big = jnp.concatenate([aq[:, _pair_perm()], bq, bk, kva, bv, co, gt, cq, cv], axis=1)
    return big.astype(BF16), ck.T.astype(BF16)


def kernel(x, c, ctx, c_ctx, w_mod, b_mod, g_mix, g_ffn, w_in, b_gate, sink, lam_q1, lam_k1, lam_q2, lam_k2,
           g_diff, g_mlstm, w_a, w_b, w_c, w_out, w_router, b_router, w_exp_gate, w_exp_up, w_exp_down,
           w_sh_gate, w_sh_up, w_sh_down, g_final):
    b, s_len, d = x.shape
    n_ctx = ctx.shape[1]
    l = n_ctx + s_len
    depth = w_mod.shape[0]
    n_ctx_tiles = n_ctx // TM
    assert n_ctx % TM == 0 and s_len % TM == 0 and d % LANES == 0 and s_len % GRID_W == 0

    xs = (ctx, x)
    cos, sa, sb = _rope_tables(s_len, n_ctx)

    rows_c = 16
    cs = jnp.concatenate([c, c_ctx[None], jnp.zeros((rows_c - b - 1, d), F32)], axis=0)
    mod_all = _mod_vectors(cs, w_mod, b_mod).reshape(depth, rows_c, N_MOD, d)
    mod_all = jnp.pad(mod_all, ((0, 0), (0, 0), (0, 8 - N_MOD), (0, 0)))

    perm = _pair_perm()
    expert_w = (w_exp_gate, w_exp_up, w_exp_down)
    for layer in range(depth):
        lam_init = 0.8 - 0.6 * math.exp(-0.3 * layer)
        mods = jnp.stack([jnp.broadcast_to(mod_all[layer, b], (b, 8, d)), mod_all[layer, :b]], axis=1)
        w_big, w_kt = _pack_w_in(w_in[layer])
        p, gates, kt = _inproj(xs, mods, g_mix[layer][None], w_big, w_kt, cos, sa, sb, n_ctx_tiles)

        oa = _mixer_a(p, sink[layer], n_ctx)
        lam_params = jnp.stack([lam_q1[layer], lam_k1[layer], lam_q2[layer], lam_k2[layer]])
        ob_ctx, ob_lat = _mixer_b(p, lam_params, g_diff[layer][None], lam_init, n_ctx)

        bias = b_gate[layer].reshape(-1)
        bias_row = jnp.pad(bias, (0, LANES - bias.shape[0]))[None]
        bias_col = jnp.broadcast_to(bias[:, None], (bias.shape[0], LANES))
        gates_t = jnp.transpose(gates[:, :, :bias.shape[0]], (0, 2, 1))
        hf, hb = _mlstm(p, kt, gates, gates_t, bias_row, bias_col, n_ctx)

        last = layer == depth - 1
        xs = _merge(xs, mods, oa, ob_ctx, ob_lat, hf, hb, p, g_mlstm[layer][None],
                    w_a[layer][perm].astype(BF16), w_b[layer].astype(BF16), w_c[layer].astype(BF16),
                    w_out[layer].astype(BF16), n_ctx_tiles, last)

        lf, ctx_tiles = (s_len, 0) if last else (l, n_ctx_tiles)
        h, idx_t, wt_t, cnt = _router(xs, mods, g_ffn[layer][None], w_router[layer].T, b_router[layer][:, None],
                                      ctx_tiles)
        plan = _moe_plan(cnt.reshape(-1, N_EXPERTS), b * lf * TOP_K, EXPERT_ROWS)
        h_flat = h.reshape(b * lf, d)
        posl, tiles = _sort_rows(idx_t, plan["tile_off"], h_flat)
        tiles = _experts(plan, tiles, *expert_w, layer)
        to_rows = lambda a: jnp.transpose(a, (0, 2, 1)).reshape(b * lf, TOP_K)
        xs = _combine(tiles, to_rows(posl), to_rows(wt_t), xs.reshape(b * lf, d), h_flat, mods,
                      w_sh_gate[layer].astype(BF16), w_sh_up[layer].astype(BF16), w_sh_down[layer].astype(BF16),
                      lf // TM, ctx_tiles, g_final[None] if last else None).reshape(b, lf, d)
    return xs
```
